```python
import jax, jax.numpy as jnp
from jax import lax
import numpy as np

D_MODEL = 1024
BATCH = 8
SEQ = 2048
DEPTH = 1

N_MEM = 256
D_MIX = D_MODEL
HEAD_DIM = 64
RET_WIDTH = D_MIX // 2
RWKV_WIDTH = D_MIX - RET_WIDTH
RET_HEADS = RET_WIDTH // HEAD_DIM
RWKV_HEADS = RWKV_WIDTH // HEAD_DIM
RET_CHUNK = 128
ROPE_BASE = 10000.0
DECAY_LORA = 64
AAA_LORA = 64
GATE_LORA = 160
RET_PROJ = 4 * RET_WIDTH
RWKV_PROJ = 3 * RWKV_WIDTH + DECAY_LORA + AAA_LORA + GATE_LORA
D_IN_PROJ = RET_PROJ + RWKV_PROJ
XATTN_HEADS = 4
XATTN_HEAD_DIM = D_MODEL // XATTN_HEADS
D_FF = 4 * D_MODEL
RMS_EPS = 1e-6
GN_EPS_RET = 1e-5
GN_EPS_RWKV = 64e-5

kernel_name = "hybrid_retention_rwkv7_memxattn_block"


def rmsnorm(x, g):
    xf = x.astype(jnp.float32)
    y = xf * lax.rsqrt(jnp.mean(jnp.square(xf), axis=-1, keepdims=True) + RMS_EPS)
    return (y * g.astype(jnp.float32)).astype(x.dtype)


def head_group_norm(y, w, b, eps):
    B, S, H, d = y.shape
    yf = y.astype(jnp.float32)
    mu = jnp.mean(yf, axis=-1, keepdims=True)
    var = jnp.mean(jnp.square(yf - mu), axis=-1, keepdims=True)
    yn = ((yf - mu) * lax.rsqrt(var + eps)).reshape(B, S, H * d)
    return yn * w.astype(jnp.float32) + b.astype(jnp.float32)


def rope(x, positions):
    d = x.shape[-1]
    half = d // 2
    inv_freq = ROPE_BASE ** (-jnp.arange(half, dtype=jnp.float32) / half)
    ang = positions.astype(jnp.float32)[..., None] * inv_freq
    cos = jnp.cos(ang)[:, :, None, :]
    sin = jnp.sin(ang)[:, :, None, :]
    xf = x.astype(jnp.float32)
    x1, x2 = xf[..., :half], xf[..., half:]
    return jnp.concatenate([x1 * cos - x2 * sin, x2 * cos + x1 * sin], axis=-1)


def retention_chunkwise(q, k, v):
    B, S, H, d = q.shape
    C = RET_CHUNK
    N = S // C
    log_g = jnp.log(1.0 - 2.0 ** (-5.0 - jnp.arange(H, dtype=jnp.float32)))
    idx = jnp.arange(C, dtype=jnp.float32)
    diff = idx[:, None] - idx[None, :]
    causal = diff >= 0
    dmask = jnp.where(causal[None], jnp.exp(log_g[:, None, None] * jnp.where(causal, diff, 0.0)[None]), 0.0)
    xi = jnp.exp(log_g[:, None] * (idx + 1.0)[None])
    zeta = jnp.exp(log_g[:, None] * (C - 1.0 - idx)[None])
    g_chunk = jnp.exp(log_g * C)
    qc = q.reshape(B, N, C, H, d)
    kc = k.reshape(B, N, C, H, d)
    vc = v.reshape(B, N, C, H, d)
    s = jnp.einsum('bnihd,bnjhd->bnhij', qc, kc) * dmask[None, None]
    intra = jnp.einsum('bnhij,bnjhe->bnihe', s, vc)
    kv = jnp.einsum('bnjhd,hj,bnjhe->bnhde', kc, zeta, vc)

    def step(R, kv_n):
        return R * g_chunk[None, :, None, None] + kv_n, R

    _, R_prev = lax.scan(step, jnp.zeros((B, H, d, d), jnp.float32), jnp.moveaxis(kv, 1, 0))
    R_prev = jnp.moveaxis(R_prev, 0, 1)
    inter = jnp.einsum('bnihd,hi,bnhde->bnihe', qc, xi, R_prev)
    return (intra + inter).reshape(B, S, H, d)


def retention_mixer(p, positions, gn_w, gn_b):
    B, S, _ = p.shape
    q, k, v, g = jnp.split(p, 4, axis=-1)
    q = rope(q.reshape(B, S, RET_HEADS, HEAD_DIM), positions)
    k = rope(k.reshape(B, S, RET_HEADS, HEAD_DIM), positions) * (HEAD_DIM ** -0.5)
    v = v.reshape(B, S, RET_HEADS, HEAD_DIM).astype(jnp.float32)
    y = retention_chunkwise(q, k, v)
    return jax.nn.silu(g.astype(jnp.float32)) * head_group_norm(y, gn_w, gn_b, GN_EPS_RET)


def token_shift(p, mu):
    prev = jnp.pad(p, ((0, 0), (1, 0), (0, 0)))[:, :-1]
    return p + (prev - p) * mu


def wkv7_scan(r, w, k, v, a_vec, b_vec):
    B, S, H, d = r.shape

    def step(state, inp):
        r_t, w_t, k_t, v_t, a_t, b_t = inp
        sa = jnp.einsum('bhvk,bhk->bhv', state, a_t)
        state = (state * w_t[:, :, None, :] + sa[..., None] * b_t[:, :, None, :]
                 + v_t[..., None] * k_t[:, :, None, :])
        return state, jnp.einsum('bhvk,bhk->bhv', state, r_t)

    xs = (jnp.moveaxis(r, 1, 0), jnp.moveaxis(w, 1, 0), jnp.moveaxis(k, 1, 0),
          jnp.moveaxis(v, 1, 0), jnp.moveaxis(a_vec, 1, 0), jnp.moveaxis(b_vec, 1, 0))
    _, y = lax.scan(step, jnp.zeros((B, H, d, d), jnp.float32), xs)
    return jnp.moveaxis(y, 0, 1)


def rwkv7_mixer(p, mu, w0, w_up, a0, a_up, g_up, k_k, k_a, r_k, gn_w, gn_b):
    B, S, _ = p.shape
    H, d, W = RWKV_HEADS, HEAD_DIM, RWKV_WIDTH
    p = token_shift(p.astype(jnp.float32), mu.astype(jnp.float32))
    r, k, v, w_lr, a_lr, g_lr = jnp.split(
        p, [W, 2 * W, 3 * W, 3 * W + DECAY_LORA, 3 * W + DECAY_LORA + AAA_LORA], axis=-1)
    w_log = -jax.nn.softplus(-(w0 + jnp.tanh(w_lr) @ w_up)) - 0.5
    decay = jnp.exp(-jnp.exp(w_log))
    a = jax.nn.sigmoid(a0 + a_lr @ a_up)
    g = jax.nn.sigmoid(g_lr) @ g_up
    kk = (k * k_k).reshape(B, S, H, d)
    kk = kk / jnp.maximum(jnp.sqrt(jnp.sum(jnp.square(kk), axis=-1, keepdims=True)), 1e-12)
    k = k * (1.0 + (a - 1.0) * k_a)
    rh = r.reshape(B, S, H, d)
    kh = k.reshape(B, S, H, d)
    vh = v.reshape(B, S, H, d)
    ah = a.reshape(B, S, H, d)
    y = wkv7_scan(rh, decay.reshape(B, S, H, d), kh, vh, -kk, kk * ah)
    y = head_group_norm(y, gn_w, gn_b, GN_EPS_RWKV)
    bonus = jnp.sum(rh * kh * r_k.astype(jnp.float32), axis=-1, keepdims=True) * vh
    return (y + bonus.reshape(B, S, W)) * g


def memory_cross_attention(h, mem_n, w_q, w_kv, w_o):
    B, S, _ = h.shape
    M = mem_n.shape[1]
    q = (h @ w_q).reshape(B, S, XATTN_HEADS, XATTN_HEAD_DIM)
    k, v = jnp.split(mem_n @ w_kv, 2, axis=-1)
    k = k.reshape(B, M, XATTN_HEADS, XATTN_HEAD_DIM)
    v = v.reshape(B, M, XATTN_HEADS, XATTN_HEAD_DIM)
    s = jnp.einsum('bshd,bmhd->bhsm', q.astype(jnp.float32), k.astype(jnp.float32)) * (XATTN_HEAD_DIM ** -0.5)
    prob = jax.nn.softmax(s, axis=-1).astype(v.dtype)
    o = jnp.einsum('bhsm,bmhd->bshd', prob, v).reshape(B, S, D_MODEL)
    return o @ w_o


def setup_inputs(seed: int = 0) -> dict:
    key = jax.random.key(seed)
    ks = jax.random.split(key, 32)
    f32 = jnp.float32
    L = DEPTH

    def nrm(k, shape, scale):
        return jax.random.normal(k, shape, f32) * scale

    def gain(k, shape):
        return 1.0 + 0.05 * jax.random.normal(k, shape, f32)

    x = jax.random.normal(ks[0], (BATCH, SEQ, D_MODEL), f32)
    mem = jax.random.normal(ks[1], (BATCH, N_MEM, D_MODEL), f32)
    positions = jnp.broadcast_to(jnp.arange(SEQ, dtype=jnp.int32)[None], (BATCH, SEQ))
    return {
        "x": x,
        "mem": mem,
        "positions": positions,
        "norm_mix": gain(ks[2], (L, D_MODEL)),
        "w_in": nrm(ks[3], (L, D_MODEL, D_IN_PROJ), D_MODEL ** -0.5),
        "ret_gn_w": gain(ks[4], (L, RET_WIDTH)),
        "ret_gn_b": nrm(ks[5], (L, RET_WIDTH), 0.02),
        "rwkv_mu": jax.random.uniform(ks[6], (L, RWKV_PROJ), f32),
        "rwkv_w0": jax.random.uniform(ks[7], (L, RWKV_WIDTH), f32, -6.5, -1.5),
        "rwkv_w_up": nrm(ks[8], (L, DECAY_LORA, RWKV_WIDTH), 0.1 * DECAY_LORA ** -0.5),
        "rwkv_a0": nrm(ks[9], (L, RWKV_WIDTH), 0.1),
        "rwkv_a_up": nrm(ks[10], (L, AAA_LORA, RWKV_WIDTH), 0.1 * AAA_LORA ** -0.5),
        "rwkv_g_up": nrm(ks[11], (L, GATE_LORA, RWKV_WIDTH), GATE_LORA ** -0.5),
        "rwkv_k_k": 0.85 + 0.05 * jax.random.normal(ks[12], (L, RWKV_WIDTH), f32),
        "rwkv_k_a": gain(ks[13], (L, RWKV_WIDTH)),
        "rwkv_r_k": nrm(ks[14], (L, RWKV_HEADS, HEAD_DIM), 0.1),
        "rwkv_gn_w": gain(ks[15], (L, RWKV_WIDTH)),
        "rwkv_gn_b": nrm(ks[16], (L, RWKV_WIDTH), 0.02),
        "w_out": nrm(ks[17], (L, D_MIX, D_MODEL), D_MIX ** -0.5),
        "norm_xattn": gain(ks[18], (L, D_MODEL)),
        "norm_mem": gain(ks[19], (L, D_MODEL)),
        "xattn_w_q": nrm(ks[20], (L, D_MODEL, D_MODEL), D_MODEL ** -0.5),
        "xattn_w_kv": nrm(ks[21], (L, D_MODEL, 2 * D_MODEL), D_MODEL ** -0.5),
        "xattn_w_o": nrm(ks[22], (L, D_MODEL, D_MODEL), D_MODEL ** -0.5),
        "norm_mlp": gain(ks[23], (L, D_MODEL)),
        "mlp_w_up": nrm(ks[24], (L, D_MODEL, D_FF), D_MODEL ** -0.5),
        "mlp_w_down": nrm(ks[25], (L, D_FF, D_MODEL), D_FF ** -0.5),
        "norm_final": gain(ks[26], (D_MODEL,)),
    }


def reference(x, mem, positions, norm_mix, w_in, ret_gn_w, ret_gn_b, rwkv_mu, rwkv_w0,
              rwkv_w_up, rwkv_a0, rwkv_a_up, rwkv_g_up, rwkv_k_k, rwkv_k_a, rwkv_r_k,
              rwkv_gn_w, rwkv_gn_b, w_out, norm_xattn, norm_mem, xattn_w_q, xattn_w_kv,
              xattn_w_o, norm_mlp, mlp_w_up, mlp_w_down, norm_final):
    dt = x.dtype
    for l in range(DEPTH):
        h = rmsnorm(x, norm_mix[l])
        p = h @ w_in[l]
        y_ret = retention_mixer(p[..., :RET_PROJ], positions, ret_gn_w[l], ret_gn_b[l])
        y_rwkv = rwkv7_mixer(p[..., RET_PROJ:], rwkv_mu[l], rwkv_w0[l], rwkv_w_up[l], rwkv_a0[l],
                             rwkv_a_up[l], rwkv_g_up[l], rwkv_k_k[l], rwkv_k_a[l], rwkv_r_k[l],
                             rwkv_gn_w[l], rwkv_gn_b[l])
        y = jnp.concatenate([y_ret, y_rwkv], axis=-1).astype(dt)
        x = x + y @ w_out[l]
        x = x + memory_cross_attention(rmsnorm(x, norm_xattn[l]), rmsnorm(mem, norm_mem[l]),
                                       xattn_w_q[l], xattn_w_kv[l], xattn_w_o[l])
        h = rmsnorm(x, norm_mlp[l])
        x = x + jnp.square(jax.nn.relu(h @ mlp_w_up[l])) @ mlp_w_down[l]
    return rmsnorm(x, norm_final)
```

```python
import functools

import jax
import jax.numpy as jnp
from jax import lax
from jax.experimental import pallas as pl
from jax.experimental.pallas import tpu as pltpu

D_MODEL = 1024
HEAD_DIM = 64
RET_WIDTH = 512
RWKV_WIDTH = 512
N_HEADS = 8
RET_CHUNK = 128
ROPE_BASE = 10000.0
DECAY_LORA = 64
AAA_LORA = 64
GATE_LORA = 160
LORA_WIDTH = DECAY_LORA + AAA_LORA + GATE_LORA
RET_PROJ = 4 * RET_WIDTH
XATTN_HEADS = 4
XATTN_HEAD_DIM = D_MODEL // XATTN_HEADS
D_FF = 4 * D_MODEL
RMS_EPS = 1e-6
GN_EPS_RET = 1e-5
GN_EPS_RWKV = 64e-5

WKV_CHUNK = 64
WKV_BLOCK = 128
VMEM_LIMIT_BYTES = 56 * 1024 * 1024

BF16 = jnp.bfloat16
F32 = jnp.float32


def _dot(a, b):
    return jnp.dot(a, b, preferred_element_type=F32)


def _dot_nt(a, b):
    return lax.dot_general(a, b, (((1,), (1,)), ((), ())), preferred_element_type=F32)


def _dot_tn(a, b):
    return lax.dot_general(a, b, (((0,), (0,)), ((), ())), preferred_element_type=F32)


def _rms(x, g):
    return x * lax.rsqrt(jnp.mean(x * x, axis=-1, keepdims=True) + RMS_EPS) * g


def _params(n_axes):
    return pltpu.CompilerParams(dimension_semantics=("arbitrary",) * n_axes,
                                vmem_limit_bytes=VMEM_LIMIT_BYTES)


def _full(shape):
    zeros = (0,) * len(shape)
    return pl.BlockSpec(shape, lambda *_: zeros)


def _in_proj_kernel(x_ref, g_ref, w_ret_ref, w_rkv_ref, w_lora_ref, p_ret_ref, p_rkv_ref, p_lora_ref):
    h = _rms(x_ref[...], g_ref[...]).astype(BF16)
    p_ret_ref[...] = _dot(h, w_ret_ref[...])
    p_rkv_ref[...] = _dot(h, w_rkv_ref[...])
    p_lora_ref[...] = _dot(h, w_lora_ref[...])


def _in_proj(x2d, g, w_ret, w_rkv, w_lora, tm=256):
    t = x2d.shape[0]
    row = lambda i: (i, 0)
    return pl.pallas_call(
        _in_proj_kernel,
        grid=(t // tm,),
        in_specs=[pl.BlockSpec((tm, D_MODEL), row), _full(g.shape), _full(w_ret.shape),
                  _full(w_rkv.shape), _full(w_lora.shape)],
        out_specs=[pl.BlockSpec((tm, w_ret.shape[1]), row), pl.BlockSpec((tm, w_rkv.shape[1]), row),
                   pl.BlockSpec((tm, w_lora.shape[1]), row)],
        out_shape=[jax.ShapeDtypeStruct((t, w_ret.shape[1]), F32),
                   jax.ShapeDtypeStruct((t, w_rkv.shape[1]), F32),
                   jax.ShapeDtypeStruct((t, w_lora.shape[1]), F32)],
        compiler_params=_params(1),
        name="in_proj",
    )(x2d, g, w_ret, w_rkv, w_lora)


def _head_norm(y, eps):
    mu = jnp.mean(y, axis=-1, keepdims=True)
    d = y - mu
    var = jnp.mean(d * d, axis=-1, keepdims=True)
    return d * lax.rsqrt(var + eps)


def _retention_kernel(p_ref, pos_ref, freq_ref, dmask_ref, xi_ref, zeta_ref, gc_ref, gnw_ref, gnb_ref,
                      o_ref, state_ref):
    c = RET_CHUNK
    w = RET_WIDTH

    @pl.when(pl.program_id(1) == 0)
    def _():
        state_ref[...] = jnp.zeros_like(state_ref)

    p = p_ref[0]
    q, k, v, gate = p[:, :w], p[:, w:2 * w], p[:, 2 * w:3 * w], p[:, 3 * w:]

    ang = pos_ref[0].astype(F32) * freq_ref[...]
    cos = jnp.concatenate([jnp.cos(ang)] * (w // 128), axis=1)
    sin = jnp.concatenate([jnp.sin(ang)] * (w // 128), axis=1)
    lane = lax.broadcasted_iota(jnp.int32, (c, w), 1)
    first_half = (lane % HEAD_DIM) < (HEAD_DIM // 2)

    def rope(t):
        rot = jnp.where(first_half, -pltpu.roll(t, w - HEAD_DIM // 2, 1), pltpu.roll(t, HEAD_DIM // 2, 1))
        return t * cos + rot * sin

    q = rope(q)
    k = rope(k) * (HEAD_DIM ** -0.5)
    q_in = (q * xi_ref[...]).astype(BF16)
    k_out = (k * zeta_ref[...]).astype(BF16)
    q = q.astype(BF16)
    k = k.astype(BF16)
    vb = v.astype(BF16)

    ys = []
    for h in range(N_HEADS):
        sl = slice(h * HEAD_DIM, (h + 1) * HEAD_DIM)
        s = _dot_nt(q[:, sl], k[:, sl]) * dmask_ref[h]
        r_prev = state_ref[h]
        y = _dot(s.astype(BF16), vb[:, sl]) + _dot(q_in[:, sl], r_prev.astype(BF16))
        state_ref[h] = r_prev * gc_ref[h] + _dot_tn(k_out[:, sl], vb[:, sl])
        ys.append(_head_norm(y, GN_EPS_RET))
    yn = jnp.concatenate(ys, axis=1) * gnw_ref[...] + gnb_ref[...]
    o_ref[0] = (gate * jax.nn.sigmoid(gate) * yn).astype(o_ref.dtype)


def _retention(p_ret, positions, gn_w, gn_b):
    b, s, _ = p_ret.shape
    c = RET_CHUNK
    half = HEAD_DIM // 2
    inv_freq = ROPE_BASE ** (-jnp.arange(half, dtype=F32) / half)
    freq = jnp.tile(inv_freq, 128 // half)[None, :]
    log_g = jnp.log(1.0 - 2.0 ** (-5.0 - jnp.arange(N_HEADS, dtype=F32)))
    idx = jnp.arange(c, dtype=F32)
    diff = idx[:, None] - idx[None, :]
    causal = diff >= 0
    dmask = jnp.where(causal[None], jnp.exp(log_g[:, None, None] * jnp.where(causal, diff, 0.0)[None]), 0.0)
    xi = jnp.exp(log_g[:, None] * (idx + 1.0)[None])
    zeta = jnp.exp(log_g[:, None] * (c - 1.0 - idx)[None])
    g_chunk = jnp.exp(log_g * c)
    xi_w = jnp.repeat(xi.T, HEAD_DIM, axis=1)
    zeta_w = jnp.repeat(zeta.T, HEAD_DIM, axis=1)
    gc_w = jnp.broadcast_to(g_chunk[:, None, None], (N_HEADS, 1, HEAD_DIM))
    blk = lambda bi, ci: (bi, ci, 0)
    return pl.pallas_call(
        _retention_kernel,
        grid=(b, s // c),
        in_specs=[pl.BlockSpec((1, c, RET_PROJ), blk), pl.BlockSpec((1, c, 1), blk), _full(freq.shape),
                  _full(dmask.shape), _full(xi_w.shape), _full(zeta_w.shape), _full(gc_w.shape),
                  _full(gn_w.shape), _full(gn_b.shape)],
        out_specs=pl.BlockSpec((1, c, RET_WIDTH), blk),
        out_shape=jax.ShapeDtypeStruct((b, s, RET_WIDTH), BF16),
        scratch_shapes=[pltpu.VMEM((N_HEADS, HEAD_DIM, HEAD_DIM), F32)],
        compiler_params=_params(2),
        name="retention",
    )(p_ret, positions[..., None], freq, dmask, xi_w, zeta_w, gc_w, gn_w, gn_b)


def _shift(p, carry_ref, mu):
    rows = lax.broadcasted_iota(jnp.int32, p.shape, 0)
    prev = jnp.where(rows == 0, carry_ref[...], pltpu.roll(p, 1, 0))
    carry_ref[...] = p[p.shape[0] - 1:, :]
    return p + (prev - p) * mu


def _wkv_kernel(prkv_ref, plora_ref, mu_rkv_ref, mu_lora_ref, w0_ref, wup_ref, a0_ref, aup_ref, gup_ref,
                kk_ref, ka_ref, rk_ref, gnw_ref, gnb_ref, tri_ref, mask_ref, eye_ref,
                o_ref, carry_rkv_ref, carry_lora_ref, state_ref):
    tb = WKV_BLOCK
    c = WKV_CHUNK
    w = RWKV_WIDTH
    d = HEAD_DIM

    @pl.when(pl.program_id(1) == 0)
    def _():
        carry_rkv_ref[...] = jnp.zeros_like(carry_rkv_ref)
        carry_lora_ref[...] = jnp.zeros_like(carry_lora_ref)
        state_ref[...] = jnp.zeros_like(state_ref)

    prkv = _shift(prkv_ref[0], carry_rkv_ref, mu_rkv_ref[...])
    plora = _shift(plora_ref[0], carry_lora_ref, mu_lora_ref[...])
    r, k, v = prkv[:, :w], prkv[:, w:2 * w], prkv[:, 2 * w:]
    w_lr = plora[:, :DECAY_LORA]
    a_lr = plora[:, DECAY_LORA:DECAY_LORA + AAA_LORA]
    g_lr = plora[:, DECAY_LORA + AAA_LORA:]

    z = -(w0_ref[...] + _dot(jnp.tanh(w_lr).astype(BF16), wup_ref[...]))
    w_log = -(jnp.maximum(z, 0.0) + jnp.log1p(jnp.exp(-jnp.abs(z)))) - 0.5
    lw = -jnp.exp(w_log)
    a_sig = jax.nn.sigmoid(a0_ref[...] + _dot(a_lr.astype(BF16), aup_ref[...]))
    gate = _dot(jax.nn.sigmoid(g_lr).astype(BF16), gup_ref[...])
    kk = k * kk_ref[...]
    k2 = k * (1.0 + (a_sig - 1.0) * ka_ref[...])
    rk = r * k2 * rk_ref[...]

    inv_norms, bonus = [], []
    for h in range(N_HEADS):
        sl = slice(h * d, (h + 1) * d)
        kh = kk[:, sl]
        n = jnp.sqrt(jnp.sum(kh * kh, axis=-1, keepdims=True))
        inv_norms.append(jnp.broadcast_to(1.0 / jnp.maximum(n, 1e-12), (tb, d)))
        bonus.append(jnp.broadcast_to(jnp.sum(rk[:, sl], axis=-1, keepdims=True), (tb, d)))
    kk = kk * jnp.concatenate(inv_norms, axis=1)
    bonus = jnp.concatenate(bonus, axis=1) * v
    a_vec = -kk
    b_vec = kk * a_sig

    n_chunks = tb // c
    at, rt, bt, kt, bc, kc, pc = [], [], [], [], [], [], []
    for ci in range(n_chunks):
        rows = slice(ci * c, (ci + 1) * c)
        lw_c = lw[rows]
        cum = jnp.dot(tri_ref[...], lw_c, precision=lax.Precision.HIGHEST, preferred_element_type=F32)
        cum_end = cum[c - 1:, :]
        p_in = jnp.exp(cum)
        p_inv = jnp.exp(-cum)
        p_out = jnp.exp(cum_end - cum)
        at.append((a_vec[rows] * jnp.exp(cum - lw_c)).astype(BF16))
        rt.append((r[rows] * p_in).astype(BF16))
        bt.append((b_vec[rows] * p_inv).astype(BF16))
        kt.append((k2[rows] * p_inv).astype(BF16))
        bc.append((b_vec[rows] * p_out).astype(BF16))
        kc.append((k2[rows] * p_out).astype(BF16))
        pc.append(jnp.exp(cum_end))
    vb = v.astype(BF16)
    eye = eye_ref[...]
    mask = mask_ref[...]

    y_heads = []
    for h in range(N_HEADS):
        sl = slice(h * d, (h + 1) * d)
        s_h = state_ref[h]
        y_chunks = []
        for ci in range(n_chunks):
            rows = slice(ci * c, (ci + 1) * c)
            at_h, rt_h, bt_h, kt_h = at[ci][:, sl], rt[ci][:, sl], bt[ci][:, sl], kt[ci][:, sl]
            v_h = vb[rows, sl]
            g = _dot_nt(jnp.concatenate([at_h, rt_h], axis=0), jnp.concatenate([bt_h, kt_h], axis=0))
            g = jnp.where(mask > 0.5, g, 0.0)
            a_ab, a_ak, a_rb, a_rk = g[:c, :c], g[:c, c:], g[c:, :c], g[c:, c:]
            t_inv = eye + a_ab
            m = a_ab
            steps = 1
            while 2 * steps < c:
                mb = m.astype(BF16)
                m = _dot(mb, mb)
                t_inv = _dot(t_inv.astype(BF16), (eye + m).astype(BF16))
                steps *= 2
            tb16 = t_inv.astype(BF16)
            w_mat = _dot(tb16, at_h).astype(BF16)
            u0 = _dot(tb16, _dot(a_ak.astype(BF16), v_h).astype(BF16))
            y0 = _dot(a_rk.astype(BF16), v_h)
            s_b = s_h.astype(BF16)
            u = _dot_nt(w_mat, s_b) + u0
            ub = u.astype(BF16)
            y_chunks.append(_dot_nt(rt_h, s_b) + _dot(a_rb.astype(BF16), ub) + y0)
            s_h = (s_h * pc[ci][:, sl]
                   + _dot_tn(jnp.concatenate([ub, v_h], axis=0),
                             jnp.concatenate([bc[ci][:, sl], kc[ci][:, sl]], axis=0)))
        state_ref[h] = s_h
        y_heads.append(_head_norm(jnp.concatenate(y_chunks, axis=0), GN_EPS_RWKV))
    yn = jnp.concatenate(y_heads, axis=1) * gnw_ref[...] + gnb_ref[...]
    o_ref[0] = ((yn + bonus) * gate).astype(o_ref.dtype)


def _wkv(p_rkv, p_lora, mu, w0, w_up, a0, a_up, g_up, k_k, k_a, r_k, gn_w, gn_b):
    b, s, _ = p_rkv.shape
    tb, c = WKV_BLOCK, WKV_CHUNK
    mu_rkv, mu_lora = mu[:, :3 * RWKV_WIDTH], mu[:, 3 * RWKV_WIDTH:]
    idx = jnp.arange(c)
    incl = (idx[:, None] >= idx[None, :]).astype(F32)
    strict = (idx[:, None] > idx[None, :]).astype(F32)
    mask = jnp.concatenate([jnp.concatenate([strict, strict], axis=1),
                            jnp.concatenate([incl, incl], axis=1)], axis=0)
    eye = jnp.eye(c, dtype=F32)
    blk = lambda bi, ti: (bi, ti, 0)
    consts = [mu_rkv, mu_lora, w0, w_up.astype(BF16), a0, a_up.astype(BF16), g_up.astype(BF16),
              k_k, k_a, r_k, gn_w, gn_b, incl, mask, eye]
    return pl.pallas_call(
        _wkv_kernel,
        grid=(b, s // tb),
        in_specs=[pl.BlockSpec((1, tb, 3 * RWKV_WIDTH), blk), pl.BlockSpec((1, tb, LORA_WIDTH), blk)]
                 + [_full(a.shape) for a in consts],
        out_specs=pl.BlockSpec((1, tb, RWKV_WIDTH), blk),
        out_shape=jax.ShapeDtypeStruct((b, s, RWKV_WIDTH), BF16),
        scratch_shapes=[pltpu.VMEM((1, 3 * RWKV_WIDTH), F32), pltpu.VMEM((1, LORA_WIDTH), F32),
                        pltpu.VMEM((N_HEADS, HEAD_DIM, HEAD_DIM), F32)],
        compiler_params=_params(2),
        name="wkv",
    )(p_rkv, p_lora, *consts)


def _mem_kv_kernel(m_ref, g_ref, w_ref, o_ref):
    h = _rms(m_ref[...], g_ref[...]).astype(BF16)
    o_ref[...] = _dot(h, w_ref[...]).astype(o_ref.dtype)


def _mem_kv(mem2d, g, w_kv, tm=256):
    t = mem2d.shape[0]
    n = w_kv.shape[1]
    return pl.pallas_call(
        _mem_kv_kernel,
        grid=(t // tm,),
        in_specs=[pl.BlockSpec((tm, D_MODEL), lambda i: (i, 0)), _full(g.shape), _full(w_kv.shape)],
        out_specs=pl.BlockSpec((tm, n), lambda i: (i, 0)),
        out_shape=jax.ShapeDtypeStruct((t, n), BF16),
        compiler_params=_params(1),
        name="mem_kv",
    )(mem2d, g, w_kv)


def _post_mix_kernel(x_ref, yr_ref, yw_ref, wout_r_ref, wout_w_ref, gx_ref, wq_ref, kv_ref, wo_ref, o_ref):
    x1 = x_ref[0] + _dot(yr_ref[0], wout_r_ref[...]) + _dot(yw_ref[0], wout_w_ref[...])
    q = _dot(_rms(x1, gx_ref[...]).astype(BF16), wq_ref[...]).astype(BF16)
    kv = kv_ref[0]
    heads = []
    for h in range(XATTN_HEADS):
        sl = slice(h * XATTN_HEAD_DIM, (h + 1) * XATTN_HEAD_DIM)
        s = _dot_nt(q[:, sl], kv[:, sl]) * (XATTN_HEAD_DIM ** -0.5)
        e = jnp.exp(s - jnp.max(s, axis=-1, keepdims=True))
        prob = e / jnp.sum(e, axis=-1, keepdims=True)
        heads.append(_dot(prob.astype(BF16), kv[:, D_MODEL + h * XATTN_HEAD_DIM:D_MODEL + (h + 1) * XATTN_HEAD_DIM]))
    o = jnp.concatenate(heads, axis=1).astype(BF16)
    o_ref[0] = x1 + _dot(o, wo_ref[...])


def _post_mix(x, y_ret, y_rwkv, w_out_r, w_out_w, g_x, w_q, kv, w_o, tm=256):
    b, s, _ = x.shape
    m = kv.shape[1]
    blk = lambda bi, ti: (bi, ti, 0)
    return pl.pallas_call(
        _post_mix_kernel,
        grid=(b, s // tm),
        in_specs=[pl.BlockSpec((1, tm, D_MODEL), blk), pl.BlockSpec((1, tm, RET_WIDTH), blk),
                  pl.BlockSpec((1, tm, RWKV_WIDTH), blk), _full(w_out_r.shape), _full(w_out_w.shape),
                  _full(g_x.shape), _full(w_q.shape), pl.BlockSpec((1, m, 2 * D_MODEL), lambda bi, ti: (bi, 0, 0)),
                  _full(w_o.shape)],
        out_specs=pl.BlockSpec((1, tm, D_MODEL), blk),
        out_shape=jax.ShapeDtypeStruct((b, s, D_MODEL), F32),
        compiler_params=_params(2),
        name="post_mix",
    )(x, y_ret, y_rwkv, w_out_r, w_out_w, g_x, w_q, kv, w_o)


def _mlp_kernel(x_ref, g_ref, wup_ref, wdown_ref, gf_ref, o_ref, *, tf, final_norm):
    x = x_ref[...]
    h = _rms(x, g_ref[...]).astype(BF16)
    acc = x
    for j in range(D_FF // tf):
        u = jnp.maximum(_dot(h, wup_ref[:, j * tf:(j + 1) * tf]), 0.0)
        acc = acc + _dot((u * u).astype(BF16), wdown_ref[j * tf:(j + 1) * tf, :])
    o_ref[...] = _rms(acc, gf_ref[...]) if final_norm else acc


def _mlp(x2d, g, w_up, w_down, g_final, final_norm, tm=256, tf=512):
    t = x2d.shape[0]
    row = lambda i: (i, 0)
    return pl.pallas_call(
        functools.partial(_mlp_kernel, tf=tf, final_norm=final_norm),
        grid=(t // tm,),
        in_specs=[pl.BlockSpec((tm, D_MODEL), row), _full(g.shape), _full(w_up.shape), _full(w_down.shape),
                  _full(g_final.shape)],
        out_specs=pl.BlockSpec((tm, D_MODEL), row),
        out_shape=jax.ShapeDtypeStruct((t, D_MODEL), F32),
        compiler_params=_params(1),
        name="mlp",
    )(x2d, g, w_up, w_down, g_final)


def kernel(x, mem, positions, norm_mix, w_in, ret_gn_w, ret_gn_b, rwkv_mu, rwkv_w0, rwkv_w_up, rwkv_a0,
           rwkv_a_up, rwkv_g_up, rwkv_k_k, rwkv_k_a, rwkv_r_k, rwkv_gn_w, rwkv_gn_b, w_out, norm_xattn,
           norm_mem, xattn_w_q, xattn_w_kv, xattn_w_o, norm_mlp, mlp_w_up, mlp_w_down, norm_final):
    b, s, dm = x.shape
    n_layers = w_in.shape[0]
    rkv_end = RET_PROJ + 3 * RWKV_WIDTH
    for l in range(n_layers):
        w_in_l = w_in[l].astype(BF16)
        p_ret, p_rkv, p_lora = _in_proj(x.reshape(b * s, dm), norm_mix[l][None, :], w_in_l[:, :RET_PROJ],
                                        w_in_l[:, RET_PROJ:rkv_end], w_in_l[:, rkv_end:])
        y_ret = _retention(p_ret.reshape(b, s, -1), positions, ret_gn_w[l][None, :], ret_gn_b[l][None, :])
        y_rwkv = _wkv(p_rkv.reshape(b, s, -1), p_lora.reshape(b, s, -1), rwkv_mu[l][None, :],
                      rwkv_w0[l][None, :], rwkv_w_up[l], rwkv_a0[l][None, :], rwkv_a_up[l], rwkv_g_up[l],
                      rwkv_k_k[l][None, :], rwkv_k_a[l][None, :], rwkv_r_k[l].reshape(1, -1),
                      rwkv_gn_w[l][None, :], rwkv_gn_b[l][None, :])
        kv = _mem_kv(mem.reshape(-1, dm), norm_mem[l][None, :], xattn_w_kv[l].astype(BF16))
        w_out_l = w_out[l].astype(BF16)
        x = _post_mix(x, y_ret, y_rwkv, w_out_l[:RET_WIDTH], w_out_l[RET_WIDTH:], norm_xattn[l][None, :],
                      xattn_w_q[l].astype(BF16), kv.reshape(b, -1, 2 * dm), xattn_w_o[l].astype(BF16))
        x = _mlp(x.reshape(b * s, dm), norm_mlp[l][None, :], mlp_w_up[l].astype(BF16),
                 mlp_w_down[l].astype(BF16), norm_final[None, :], l == n_layers - 1).reshape(b, s, dm)
    return x
```

```python
import functools

import jax
import jax.numpy as jnp
from jax import lax
from jax.experimental import pallas as pl
from jax.experimental.pallas import tpu as pltpu

D_MODEL = 1024
HEAD_DIM = 64
RET_WIDTH = 512
RWKV_WIDTH = 512
N_HEADS = 8
RET_CHUNK = 128
ROPE_BASE = 10000.0
DECAY_LORA = 64
AAA_LORA = 64
GATE_LORA = 160
LORA_WIDTH = DECAY_LORA + AAA_LORA + GATE_LORA
RET_PROJ = 4 * RET_WIDTH
XATTN_HEADS = 4
XATTN_HEAD_DIM = D_MODEL // XATTN_HEADS
D_FF = 4 * D_MODEL
RMS_EPS = 1e-6
GN_EPS_RET = 1e-5
GN_EPS_RWKV = 64e-5

WKV_CHUNK = 64
WKV_BLOCK = 128
VMEM_LIMIT_BYTES = 56 * 1024 * 1024

BF16 = jnp.bfloat16
F32 = jnp.float32


def _dot(a, b):
    return jnp.dot(a, b, preferred_element_type=F32)


def _dot_nt(a, b):
    return lax.dot_general(a, b, (((1,), (1,)), ((), ())), preferred_element_type=F32)


def _dot_tn(a, b):
    return lax.dot_general(a, b, (((0,), (0,)), ((), ())), preferred_element_type=F32)


def _rms(x, g):
    return x * lax.rsqrt(jnp.mean(x * x, axis=-1, keepdims=True) + RMS_EPS) * g


def _params(n_axes):
    return pltpu.CompilerParams(dimension_semantics=("arbitrary",) * n_axes,
                                vmem_limit_bytes=VMEM_LIMIT_BYTES)


def _full(shape):
    zeros = (0,) * len(shape)
    return pl.BlockSpec(shape, lambda *_: zeros)


def _in_proj_kernel(x_ref, g_ref, w_ret_ref, w_rkv_ref, w_lora_ref, p_ret_ref, p_rkv_ref, p_lora_ref):
    h = _rms(x_ref[...], g_ref[...]).astype(BF16)
    p_ret_ref[...] = _dot(h, w_ret_ref[...])
    p_rkv_ref[...] = _dot(h, w_rkv_ref[...])
    p_lora_ref[...] = _dot(h, w_lora_ref[...])


def _in_proj(x2d, g, w_ret, w_rkv, w_lora, tm=256):
    t = x2d.shape[0]
    row = lambda i: (i, 0)
    return pl.pallas_call(
        _in_proj_kernel,
        grid=(t // tm,),
        in_specs=[pl.BlockSpec((tm, D_MODEL), row), _full(g.shape), _full(w_ret.shape),
                  _full(w_rkv.shape), _full(w_lora.shape)],
        out_specs=[pl.BlockSpec((tm, w_ret.shape[1]), row), pl.BlockSpec((tm, w_rkv.shape[1]), row),
                   pl.BlockSpec((tm, w_lora.shape[1]), row)],
        out_shape=[jax.ShapeDtypeStruct((t, w_ret.shape[1]), F32),
                   jax.ShapeDtypeStruct((t, w_rkv.shape[1]), F32),
                   jax.ShapeDtypeStruct((t, w_lora.shape[1]), F32)],
        compiler_params=_params(1),
        name="in_proj",
    )(x2d, g, w_ret, w_rkv, w_lora)


def _head_norm(y, eps):
    mu = jnp.mean(y, axis=-1, keepdims=True)
    d = y - mu
    var = jnp.mean(d * d, axis=-1, keepdims=True)
    return d * lax.rsqrt(var + eps)


def _retention_kernel(p_ref, pos_ref, freq_ref, dmask_ref, xi_ref, zeta_ref, gc_ref, gnw_ref, gnb_ref,
                      o_ref, state_ref):
    c = RET_CHUNK
    w = RET_WIDTH

    @pl.when(pl.program_id(1) == 0)
    def _():
        state_ref[...] = jnp.zeros_like(state_ref)

    p = p_ref[0]
    q, k, v, gate = p[:, :w], p[:, w:2 * w], p[:, 2 * w:3 * w], p[:, 3 * w:]

    ang = pos_ref[0].astype(F32) * freq_ref[...]
    cos = jnp.concatenate([jnp.cos(ang)] * (w // 128), axis=1)
    sin = jnp.concatenate([jnp.sin(ang)] * (w // 128), axis=1)
    lane = lax.broadcasted_iota(jnp.int32, (c, w), 1)
    first_half = (lane % HEAD_DIM) < (HEAD_DIM // 2)

    def rope(t):
        rot = jnp.where(first_half, -pltpu.roll(t, w - HEAD_DIM // 2, 1), pltpu.roll(t, HEAD_DIM // 2, 1))
        return t * cos + rot * sin

    q = rope(q)
    k = rope(k) * (HEAD_DIM ** -0.5)
    q_in = (q * xi_ref[...]).astype(BF16)
    k_out = (k * zeta_ref[...]).astype(BF16)
    q = q.astype(BF16)
    k = k.astype(BF16)
    vb = v.astype(BF16)

    ys = []
    for h in range(N_HEADS):
        sl = slice(h * HEAD_DIM, (h + 1) * HEAD_DIM)
        s = _dot_nt(q[:, sl], k[:, sl]) * dmask_ref[h]
        r_prev = state_ref[h]
        y = _dot(s.astype(BF16), vb[:, sl]) + _dot(q_in[:, sl], r_prev.astype(BF16))
        state_ref[h] = r_prev * gc_ref[h] + _dot_tn(k_out[:, sl], vb[:, sl])
        ys.append(_head_norm(y, GN_EPS_RET))
    yn = jnp.concatenate(ys, axis=1) * gnw_ref[...] + gnb_ref[...]
    o_ref[0] = (gate * jax.nn.sigmoid(gate) * yn).astype(o_ref.dtype)


def _retention(p_ret, positions, gn_w, gn_b):
    b, s, _ = p_ret.shape
    c = RET_CHUNK
    half = HEAD_DIM // 2
    inv_freq = ROPE_BASE ** (-jnp.arange(half, dtype=F32) / half)
    freq = jnp.tile(inv_freq, 128 // half)[None, :]
    log_g = jnp.log(1.0 - 2.0 ** (-5.0 - jnp.arange(N_HEADS, dtype=F32)))
    idx = jnp.arange(c, dtype=F32)
    diff = idx[:, None] - idx[None, :]
    causal = diff >= 0
    dmask = jnp.where(causal[None], jnp.exp(log_g[:, None, None] * jnp.where(causal, diff, 0.0)[None]), 0.0)
    xi = jnp.exp(log_g[:, None] * (idx + 1.0)[None])
    zeta = jnp.exp(log_g[:, None] * (c - 1.0 - idx)[None])
    g_chunk = jnp.exp(log_g * c)
    xi_w = jnp.repeat(xi.T, HEAD_DIM, axis=1)
    zeta_w = jnp.repeat(zeta.T, HEAD_DIM, axis=1)
    gc_w = jnp.broadcast_to(g_chunk[:, None, None], (N_HEADS, 1, HEAD_DIM))
    blk = lambda bi, ci: (bi, ci, 0)
    return pl.pallas_call(
        _retention_kernel,
        grid=(b, s // c),
        in_specs=[pl.BlockSpec((1, c, RET_PROJ), blk), pl.BlockSpec((1, c, 1), blk), _full(freq.shape),
                  _full(dmask.shape), _full(xi_w.shape), _full(zeta_w.shape), _full(gc_w.shape),
                  _full(gn_w.shape), _full(gn_b.shape)],
        out_specs=pl.BlockSpec((1, c, RET_WIDTH), blk),
        out_shape=jax.ShapeDtypeStruct((b, s, RET_WIDTH), BF16),
        scratch_shapes=[pltpu.VMEM((N_HEADS, HEAD_DIM, HEAD_DIM), F32)],
        compiler_params=_params(2),
        name="retention",
    )(p_ret, positions[..., None], freq, dmask, xi_w, zeta_w, gc_w, gn_w, gn_b)


def _shift(p, carry_ref, mu):
    rows = lax.broadcasted_iota(jnp.int32, p.shape, 0)
    prev = jnp.where(rows == 0, carry_ref[...], pltpu.roll(p, 1, 0))
    carry_ref[...] = p[p.shape[0] - 1:, :]
    return p + (prev - p) * mu


def _wkv_kernel(prkv_ref, plora_ref, mu_rkv_ref, mu_lora_ref, w0_ref, wup_ref, a0_ref, aup_ref, gup_ref,
                kk_ref, ka_ref, rk_ref, gnw_ref, gnb_ref, tri_ref, mask_ref, eye_ref,
                o_ref, carry_rkv_ref, carry_lora_ref, state_ref):
    tb = WKV_BLOCK
    c = WKV_CHUNK
    w = RWKV_WIDTH
    d = HEAD_DIM

    @pl.when(pl.program_id(1) == 0)
    def _():
        carry_rkv_ref[...] = jnp.zeros_like(carry_rkv_ref)
        carry_lora_ref[...] = jnp.zeros_like(carry_lora_ref)
        state_ref[...] = jnp.zeros_like(state_ref)

    prkv = _shift(prkv_ref[0], carry_rkv_ref, mu_rkv_ref[...])
    plora = _shift(plora_ref[0], carry_lora_ref, mu_lora_ref[...])
    r, k, v = prkv[:, :w], prkv[:, w:2 * w], prkv[:, 2 * w:]
    w_lr = plora[:, :DECAY_LORA]
    a_lr = plora[:, DECAY_LORA:DECAY_LORA + AAA_LORA]
    g_lr = plora[:, DECAY_LORA + AAA_LORA:]

    z = -(w0_ref[...] + _dot(jnp.tanh(w_lr).astype(BF16), wup_ref[...]))
    w_log = -(jnp.maximum(z, 0.0) + jnp.log1p(jnp.exp(-jnp.abs(z)))) - 0.5
    lw = -jnp.exp(w_log)
    a_sig = jax.nn.sigmoid(a0_ref[...] + _dot(a_lr.astype(BF16), aup_ref[...]))
    gate = _dot(jax.nn.sigmoid(g_lr).astype(BF16), gup_ref[...])
    kk = k * kk_ref[...]
    k2 = k * (1.0 + (a_sig - 1.0) * ka_ref[...])
    rk = r * k2 * rk_ref[...]

    inv_norms, bonus = [], []
    for h in range(N_HEADS):
        sl = slice(h * d, (h + 1) * d)
        kh = kk[:, sl]
        n = jnp.sqrt(jnp.sum(kh * kh, axis=-1, keepdims=True))
        inv_norms.append(jnp.broadcast_to(1.0 / jnp.maximum(n, 1e-12), (tb, d)))
        bonus.append(jnp.broadcast_to(jnp.sum(rk[:, sl], axis=-1, keepdims=True), (tb, d)))
    kk = kk * jnp.concatenate(inv_norms, axis=1)
    bonus = jnp.concatenate(bonus, axis=1) * v
    a_vec = -kk
    b_vec = kk * a_sig

    n_chunks = tb // c
    at, rt, bt, kt, bc, kc, pc = [], [], [], [], [], [], []
    for ci in range(n_chunks):
        rows = slice(ci * c, (ci + 1) * c)
        lw_c = lw[rows]
        cum = jnp.dot(tri_ref[...], lw_c, precision=lax.Precision.HIGHEST, preferred_element_type=F32)
        cum_end = cum[c - 1:, :]
        p_in = jnp.exp(cum)
        p_inv = jnp.exp(-cum)
        p_out = jnp.exp(cum_end - cum)
        at.append((a_vec[rows] * jnp.exp(cum - lw_c)).astype(BF16))
        rt.append((r[rows] * p_in).astype(BF16))
        bt.append((b_vec[rows] * p_inv).astype(BF16))
        kt.append((k2[rows] * p_inv).astype(BF16))
        bc.append((b_vec[rows] * p_out).astype(BF16))
        kc.append((k2[rows] * p_out).astype(BF16))
        pc.append(jnp.exp(cum_end))
    vb = v.astype(BF16)
    eye = eye_ref[...]
    mask = mask_ref[...]

    pairs = [(ci, h) for ci in range(n_chunks) for h in range(N_HEADS)]
    hsl = lambda h: slice(h * d, (h + 1) * d)
    csl = lambda ci: slice(ci * c, (ci + 1) * c)
    v_p = {(ci, h): vb[csl(ci), hsl(h)] for ci, h in pairs}
    g_p = {}
    for ci, h in pairs:
        g = _dot_nt(jnp.concatenate([at[ci][:, hsl(h)], rt[ci][:, hsl(h)]], axis=0),
                    jnp.concatenate([bt[ci][:, hsl(h)], kt[ci][:, hsl(h)]], axis=0))
        g_p[ci, h] = jnp.where(mask > 0.5, g, 0.0)
    m_p = {p: g_p[p][:c, :c] for p in pairs}
    t_p = {p: eye + m_p[p] for p in pairs}
    steps = 1
    while 2 * steps < c:
        for p in pairs:
            mb = m_p[p].astype(BF16)
            m_p[p] = _dot(mb, mb)
        for p in pairs:
            t_p[p] = _dot(t_p[p].astype(BF16), (eye + m_p[p]).astype(BF16))
        steps *= 2
    t_p = {p: t_p[p].astype(BF16) for p in pairs}
    w_p = {(ci, h): _dot(t_p[ci, h], at[ci][:, hsl(h)]).astype(BF16) for ci, h in pairs}
    akv_p = {p: _dot(g_p[p][:c, c:].astype(BF16), v_p[p]).astype(BF16) for p in pairs}
    u0_p = {p: _dot(t_p[p], akv_p[p]) for p in pairs}
    y0_p = {p: _dot(g_p[p][c:, c:].astype(BF16), v_p[p]) for p in pairs}

    s_h = [state_ref[h] for h in range(N_HEADS)]
    y_p = {}
    for ci in range(n_chunks):
        s_b = [s.astype(BF16) for s in s_h]
        u_b = [(_dot_nt(w_p[ci, h], s_b[h]) + u0_p[ci, h]).astype(BF16) for h in range(N_HEADS)]
        for h in range(N_HEADS):
            y_p[ci, h] = (_dot_nt(rt[ci][:, hsl(h)], s_b[h])
                          + _dot(g_p[ci, h][c:, :c].astype(BF16), u_b[h]) + y0_p[ci, h])
        s_h = [s_h[h] * pc[ci][:, hsl(h)]
               + _dot_tn(jnp.concatenate([u_b[h], v_p[ci, h]], axis=0),
                         jnp.concatenate([bc[ci][:, hsl(h)], kc[ci][:, hsl(h)]], axis=0))
               for h in range(N_HEADS)]
    y_heads = []
    for h in range(N_HEADS):
        state_ref[h] = s_h[h]
        y_heads.append(_head_norm(jnp.concatenate([y_p[ci, h] for ci in range(n_chunks)], axis=0), GN_EPS_RWKV))
    yn = jnp.concatenate(y_heads, axis=1) * gnw_ref[...] + gnb_ref[...]
    o_ref[0] = ((yn + bonus) * gate).astype(o_ref.dtype)


def _wkv(p_rkv, p_lora, mu, w0, w_up, a0, a_up, g_up, k_k, k_a, r_k, gn_w, gn_b):
    b, s, _ = p_rkv.shape
    tb, c = WKV_BLOCK, WKV_CHUNK
    mu_rkv, mu_lora = mu[:, :3 * RWKV_WIDTH], mu[:, 3 * RWKV_WIDTH:]
    idx = jnp.arange(c)
    incl = (idx[:, None] >= idx[None, :]).astype(F32)
    strict = (idx[:, None] > idx[None, :]).astype(F32)
    mask = jnp.concatenate([jnp.concatenate([strict, strict], axis=1),
                            jnp.concatenate([incl, incl], axis=1)], axis=0)
    eye = jnp.eye(c, dtype=F32)
    blk = lambda bi, ti: (bi, ti, 0)
    consts = [mu_rkv, mu_lora, w0, w_up.astype(BF16), a0, a_up.astype(BF16), g_up.astype(BF16),
              k_k, k_a, r_k, gn_w, gn_b, incl, mask, eye]
    return pl.pallas_call(
        _wkv_kernel,
        grid=(b, s // tb),
        in_specs=[pl.BlockSpec((1, tb, 3 * RWKV_WIDTH), blk), pl.BlockSpec((1, tb, LORA_WIDTH), blk)]
                 + [_full(a.shape) for a in consts],
        out_specs=pl.BlockSpec((1, tb, RWKV_WIDTH), blk),
        out_shape=jax.ShapeDtypeStruct((b, s, RWKV_WIDTH), BF16),
        scratch_shapes=[pltpu.VMEM((1, 3 * RWKV_WIDTH), F32), pltpu.VMEM((1, LORA_WIDTH), F32),
                        pltpu.VMEM((N_HEADS, HEAD_DIM, HEAD_DIM), F32)],
        compiler_params=_params(2),
        name="wkv",
    )(p_rkv, p_lora, *consts)


def _mem_kv_kernel(m_ref, g_ref, w_ref, o_ref):
    h = _rms(m_ref[...], g_ref[...]).astype(BF16)
    o_ref[...] = _dot(h, w_ref[...]).astype(o_ref.dtype)


def _mem_kv(mem2d, g, w_kv, tm=256):
    t = mem2d.shape[0]
    n = w_kv.shape[1]
    return pl.pallas_call(
        _mem_kv_kernel,
        grid=(t // tm,),
        in_specs=[pl.BlockSpec((tm, D_MODEL), lambda i: (i, 0)), _full(g.shape), _full(w_kv.shape)],
        out_specs=pl.BlockSpec((tm, n), lambda i: (i, 0)),
        out_shape=jax.ShapeDtypeStruct((t, n), BF16),
        compiler_params=_params(1),
        name="mem_kv",
    )(mem2d, g, w_kv)


def _post_mix_kernel(x_ref, yr_ref, yw_ref, wout_r_ref, wout_w_ref, gx_ref, wq_ref, kv_ref, wo_ref, o_ref):
    x1 = x_ref[0] + _dot(yr_ref[0], wout_r_ref[...]) + _dot(yw_ref[0], wout_w_ref[...])
    q = _dot(_rms(x1, gx_ref[...]).astype(BF16), wq_ref[...]).astype(BF16)
    kv = kv_ref[0]
    heads = []
    for h in range(XATTN_HEADS):
        sl = slice(h * XATTN_HEAD_DIM, (h + 1) * XATTN_HEAD_DIM)
        s = _dot_nt(q[:, sl], kv[:, sl]) * (XATTN_HEAD_DIM ** -0.5)
        e = jnp.exp(s - jnp.max(s, axis=-1, keepdims=True))
        prob = e / jnp.sum(e, axis=-1, keepdims=True)
        heads.append(_dot(prob.astype(BF16), kv[:, D_MODEL + h * XATTN_HEAD_DIM:D_MODEL + (h + 1) * XATTN_HEAD_DIM]))
    o = jnp.concatenate(heads, axis=1).astype(BF16)
    o_ref[0] = x1 + _dot(o, wo_ref[...])


def _post_mix(x, y_ret, y_rwkv, w_out_r, w_out_w, g_x, w_q, kv, w_o, tm=256):
    b, s, _ = x.shape
    m = kv.shape[1]
    blk = lambda bi, ti: (bi, ti, 0)
    return pl.pallas_call(
        _post_mix_kernel,
        grid=(b, s // tm),
        in_specs=[pl.BlockSpec((1, tm, D_MODEL), blk), pl.BlockSpec((1, tm, RET_WIDTH), blk),
                  pl.BlockSpec((1, tm, RWKV_WIDTH), blk), _full(w_out_r.shape), _full(w_out_w.shape),
                  _full(g_x.shape), _full(w_q.shape), pl.BlockSpec((1, m, 2 * D_MODEL), lambda bi, ti: (bi, 0, 0)),
                  _full(w_o.shape)],
        out_specs=pl.BlockSpec((1, tm, D_MODEL), blk),
        out_shape=jax.ShapeDtypeStruct((b, s, D_MODEL), F32),
        compiler_params=_params(2),
        name="post_mix",
    )(x, y_ret, y_rwkv, w_out_r, w_out_w, g_x, w_q, kv, w_o)


def _mlp_kernel(x_ref, g_ref, wup_ref, wdown_ref, gf_ref, o_ref, *, tf, final_norm):
    x = x_ref[...]
    h = _rms(x, g_ref[...]).astype(BF16)
    acc = x
    for j in range(D_FF // tf):
        u = jnp.maximum(_dot(h, wup_ref[:, j * tf:(j + 1) * tf]), 0.0)
        acc = acc + _dot((u * u).astype(BF16), wdown_ref[j * tf:(j + 1) * tf, :])
    o_ref[...] = _rms(acc, gf_ref[...]) if final_norm else acc


def _mlp(x2d, g, w_up, w_down, g_final, final_norm, tm=256, tf=512):
    t = x2d.shape[0]
    row = lambda i: (i, 0)
    return pl.pallas_call(
        functools.partial(_mlp_kernel, tf=tf, final_norm=final_norm),
        grid=(t // tm,),
        in_specs=[pl.BlockSpec((tm, D_MODEL), row), _full(g.shape), _full(w_up.shape), _full(w_down.shape),
                  _full(g_final.shape)],
        out_specs=pl.BlockSpec((tm, D_MODEL), row),
        out_shape=jax.ShapeDtypeStruct((t, D_MODEL), F32),
        compiler_params=_params(1),
        name="mlp",
    )(x2d, g, w_up, w_down, g_final)


def kernel(x, mem, positions, norm_mix, w_in, ret_gn_w, ret_gn_b, rwkv_mu, rwkv_w0, rwkv_w_up, rwkv_a0,
           rwkv_a_up, rwkv_g_up, rwkv_k_k, rwkv_k_a, rwkv_r_k, rwkv_gn_w, rwkv_gn_b, w_out, norm_xattn,
           norm_mem, xattn_w_q, xattn_w_kv, xattn_w_o, norm_mlp, mlp_w_up, mlp_w_down, norm_final):
    b, s, dm = x.shape
    n_layers = w_in.shape[0]
    rkv_end = RET_PROJ + 3 * RWKV_WIDTH
    for l in range(n_layers):
        w_in_l = w_in[l].astype(BF16)
        p_ret, p_rkv, p_lora = _in_proj(x.reshape(b * s, dm), norm_mix[l][None, :], w_in_l[:, :RET_PROJ],
                                        w_in_l[:, RET_PROJ:rkv_end], w_in_l[:, rkv_end:])
        y_ret = _retention(p_ret.reshape(b, s, -1), positions, ret_gn_w[l][None, :], ret_gn_b[l][None, :])
        y_rwkv = _wkv(p_rkv.reshape(b, s, -1), p_lora.reshape(b, s, -1), rwkv_mu[l][None, :],
                      rwkv_w0[l][None, :], rwkv_w_up[l], rwkv_a0[l][None, :], rwkv_a_up[l], rwkv_g_up[l],
                      rwkv_k_k[l][None, :], rwkv_k_a[l][None, :], rwkv_r_k[l].reshape(1, -1),
                      rwkv_gn_w[l][None, :], rwkv_gn_b[l][None, :])
        kv = _mem_kv(mem.reshape(-1, dm), norm_mem[l][None, :], xattn_w_kv[l].astype(BF16))
        w_out_l = w_out[l].astype(BF16)
        x = _post_mix(x, y_ret, y_rwkv, w_out_l[:RET_WIDTH], w_out_l[RET_WIDTH:], norm_xattn[l][None, :],
                      xattn_w_q[l].astype(BF16), kv.reshape(b, -1, 2 * dm), xattn_w_o[l].astype(BF16))
        x = _mlp(x.reshape(b * s, dm), norm_mlp[l][None, :], mlp_w_up[l].astype(BF16),
                 mlp_w_down[l].astype(BF16), norm_final[None, :], l == n_layers - 1).reshape(b, s, dm)
    return x
```

```python
import functools

import jax
import jax.numpy as jnp
from jax import lax
from jax.experimental import pallas as pl
from jax.experimental.pallas import tpu as pltpu

D_MODEL = 1024
HEAD_DIM = 64
RET_WIDTH = 512
RWKV_WIDTH = 512
N_HEADS = 8
RET_CHUNK = 128
ROPE_BASE = 10000.0
DECAY_LORA = 64
AAA_LORA = 64
GATE_LORA = 160
LORA_WIDTH = DECAY_LORA + AAA_LORA + GATE_LORA
RET_PROJ = 4 * RET_WIDTH
XATTN_HEADS = 4
XATTN_HEAD_DIM = D_MODEL // XATTN_HEADS
D_FF = 4 * D_MODEL
RMS_EPS = 1e-6
GN_EPS_RET = 1e-5
GN_EPS_RWKV = 64e-5

WKV_CHUNK = 64
WKV_BLOCK = 512
VMEM_LIMIT_BYTES = 56 * 1024 * 1024

BF16 = jnp.bfloat16
F32 = jnp.float32


def _dot(a, b):
    return jnp.dot(a, b, preferred_element_type=F32)


def _dot_nt(a, b):
    return lax.dot_general(a, b, (((1,), (1,)), ((), ())), preferred_element_type=F32)


def _dot_tn(a, b):
    return lax.dot_general(a, b, (((0,), (0,)), ((), ())), preferred_element_type=F32)


def _rms(x, g):
    return x * lax.rsqrt(jnp.mean(x * x, axis=-1, keepdims=True) + RMS_EPS) * g


def _params(n_axes):
    return pltpu.CompilerParams(dimension_semantics=("arbitrary",) * n_axes,
                                vmem_limit_bytes=VMEM_LIMIT_BYTES)


def _full(shape):
    zeros = (0,) * len(shape)
    return pl.BlockSpec(shape, lambda *_: zeros)


def _in_proj_kernel(x_ref, g_ref, w_ret_ref, w_rkv_ref, w_lora_ref, p_ret_ref, p_rkv_ref, p_lora_ref):
    h = _rms(x_ref[...], g_ref[...]).astype(BF16)
    p_ret_ref[...] = _dot(h, w_ret_ref[...])
    p_rkv_ref[...] = _dot(h, w_rkv_ref[...])
    p_lora_ref[...] = _dot(h, w_lora_ref[...])


def _in_proj(x2d, g, w_ret, w_rkv, w_lora, tm=256):
    t = x2d.shape[0]
    row = lambda i: (i, 0)
    return pl.pallas_call(
        _in_proj_kernel,
        grid=(t // tm,),
        in_specs=[pl.BlockSpec((tm, D_MODEL), row), _full(g.shape), _full(w_ret.shape),
                  _full(w_rkv.shape), _full(w_lora.shape)],
        out_specs=[pl.BlockSpec((tm, w_ret.shape[1]), row), pl.BlockSpec((tm, w_rkv.shape[1]), row),
                   pl.BlockSpec((tm, w_lora.shape[1]), row)],
        out_shape=[jax.ShapeDtypeStruct((t, w_ret.shape[1]), F32),
                   jax.ShapeDtypeStruct((t, w_rkv.shape[1]), F32),
                   jax.ShapeDtypeStruct((t, w_lora.shape[1]), F32)],
        compiler_params=_params(1),
        name="in_proj",
    )(x2d, g, w_ret, w_rkv, w_lora)


def _head_norm(y, eps):
    mu = jnp.mean(y, axis=-1, keepdims=True)
    d = y - mu
    var = jnp.mean(d * d, axis=-1, keepdims=True)
    return d * lax.rsqrt(var + eps)


def _pair_diag(x):
    in_h0 = lax.broadcasted_iota(jnp.int32, x.shape, 1) < HEAD_DIM
    zero = jnp.zeros_like(x)
    return jnp.concatenate([jnp.where(in_h0, x, zero), jnp.where(in_h0, zero, x)], axis=0)


def _retention_body(p_ref, pos_ref, freq_ref, dmask_ref, xi_ref, zeta_ref, gc_ref, same_head_ref, gnw_ref, gnb_ref,
                    state_ref):
    c = RET_CHUNK
    w = RET_WIDTH
    lanes = 2 * HEAD_DIM
    n_groups = w // lanes

    p = p_ref[0]
    tb = p.shape[0]
    q, k, v, gate = p[:, :w], p[:, w:2 * w], p[:, 2 * w:3 * w], p[:, 3 * w:]

    ang = pos_ref[0].astype(F32) * freq_ref[...]
    cos = jnp.concatenate([jnp.cos(ang)] * n_groups, axis=1)
    sin = jnp.concatenate([jnp.sin(ang)] * n_groups, axis=1)
    lane = lax.broadcasted_iota(jnp.int32, (tb, w), 1)
    first_half = (lane % HEAD_DIM) < (HEAD_DIM // 2)

    def rope(t):
        rot = jnp.where(first_half, -pltpu.roll(t, w - HEAD_DIM // 2, 1), pltpu.roll(t, HEAD_DIM // 2, 1))
        return t * cos + rot * sin

    q = rope(q)
    k = rope(k) * (HEAD_DIM ** -0.5)
    q_in = (q * xi_ref[...]).astype(BF16)
    k_out = (k * zeta_ref[...]).astype(BF16)
    q = q.astype(BF16)
    k = k.astype(BF16)
    vb = v.astype(BF16)

    units = [(ci, g) for ci in range(tb // c) for g in range(n_groups)]
    tile = lambda x, ci, g: x[ci * c:(ci + 1) * c, g * lanes:(g + 1) * lanes]
    s_u = {(ci, g): (_dot_nt(tile(q, ci, g), _pair_diag(tile(k, ci, g))) * dmask_ref[g]).astype(BF16)
           for ci, g in units}
    y_u = {(ci, g): _dot(s_u[ci, g], _pair_diag(tile(vb, ci, g))) for ci, g in units}
    kv_u = {(ci, g): _dot_tn(tile(k_out, ci, g), tile(vb, ci, g)) for ci, g in units}
    same_head = same_head_ref[...] > 0.5
    r_g = [state_ref[g] for g in range(n_groups)]
    for ci in range(tb // c):
        for g in range(n_groups):
            y_u[ci, g] = y_u[ci, g] + _dot(tile(q_in, ci, g), r_g[g].astype(BF16))
        r_g = [r_g[g] * gc_ref[g] + jnp.where(same_head, kv_u[ci, g], 0.0) for g in range(n_groups)]
    for g in range(n_groups):
        state_ref[g] = r_g[g]
    y_all = jnp.concatenate([jnp.concatenate([y_u[ci, g] for g in range(n_groups)], axis=1)
                             for ci in range(tb // c)], axis=0)
    yn = jnp.concatenate([_head_norm(y_all[:, h * HEAD_DIM:(h + 1) * HEAD_DIM], GN_EPS_RET)
                          for h in range(N_HEADS)], axis=1) * gnw_ref[...] + gnb_ref[...]
    return gate * jax.nn.sigmoid(gate) * yn


def _retention_consts(tb):
    c = RET_CHUNK
    half = HEAD_DIM // 2
    inv_freq = ROPE_BASE ** (-jnp.arange(half, dtype=F32) / half)
    freq = jnp.tile(inv_freq, 128 // half)[None, :]
    log_g = jnp.log(1.0 - 2.0 ** (-5.0 - jnp.arange(N_HEADS, dtype=F32)))
    idx = jnp.arange(c, dtype=F32)
    diff = idx[:, None] - idx[None, :]
    causal = diff >= 0
    dmask = jnp.where(causal[None], jnp.exp(log_g[:, None, None] * jnp.where(causal, diff, 0.0)[None]), 0.0)
    xi = jnp.exp(log_g[:, None] * (idx + 1.0)[None])
    zeta = jnp.exp(log_g[:, None] * (c - 1.0 - idx)[None])
    g_chunk = jnp.exp(log_g * c)
    xi_w = jnp.tile(jnp.repeat(xi.T, HEAD_DIM, axis=1), (tb // c, 1))
    zeta_w = jnp.tile(jnp.repeat(zeta.T, HEAD_DIM, axis=1), (tb // c, 1))
    dmask_pairs = jnp.concatenate([dmask[0::2], dmask[1::2]], axis=2)
    gc_rows = jnp.repeat(g_chunk, HEAD_DIM).reshape(N_HEADS // 2, 2 * HEAD_DIM, 1)
    gc_w = jnp.broadcast_to(gc_rows, (N_HEADS // 2, 2 * HEAD_DIM, 2 * HEAD_DIM))
    return [freq, dmask_pairs, xi_w, zeta_w, gc_w]


def _shift(p, carry_ref, mu):
    rows = lax.broadcasted_iota(jnp.int32, p.shape, 0)
    prev = jnp.where(rows == 0, carry_ref[...], pltpu.roll(p, 1, 0))
    carry_ref[...] = p[p.shape[0] - 1:, :]
    return p + (prev - p) * mu


def _wkv_body(prkv_ref, plora_ref, mu_rkv_ref, mu_lora_ref, w0_ref, wup_ref, a0_ref, aup_ref, gup_ref,
              kk_ref, ka_ref, rk_ref, gnw_ref, gnb_ref, tri_ref, mask_ref, eye_ref, state_mask_ref,
              carry_rkv_ref, carry_lora_ref, state_ref):
    tb = WKV_BLOCK
    c = WKV_CHUNK
    w = RWKV_WIDTH
    d = HEAD_DIM

    prkv = _shift(prkv_ref[0], carry_rkv_ref, mu_rkv_ref[...])
    plora = _shift(plora_ref[0], carry_lora_ref, mu_lora_ref[...])
    r, k, v = prkv[:, :w], prkv[:, w:2 * w], prkv[:, 2 * w:]
    w_lr = plora[:, :DECAY_LORA]
    a_lr = plora[:, DECAY_LORA:DECAY_LORA + AAA_LORA]
    g_lr = plora[:, DECAY_LORA + AAA_LORA:]

    z = -(w0_ref[...] + _dot(jnp.tanh(w_lr).astype(BF16), wup_ref[...]))
    w_log = -(jnp.maximum(z, 0.0) + jnp.log1p(jnp.exp(-jnp.abs(z)))) - 0.5
    lw = -jnp.exp(w_log)
    a_sig = jax.nn.sigmoid(a0_ref[...] + _dot(a_lr.astype(BF16), aup_ref[...]))
    gate = _dot(jax.nn.sigmoid(g_lr).astype(BF16), gup_ref[...])
    kk = k * kk_ref[...]
    k2 = k * (1.0 + (a_sig - 1.0) * ka_ref[...])
    rk = r * k2 * rk_ref[...]

    inv_norms, bonus = [], []
    for h in range(N_HEADS):
        sl = slice(h * d, (h + 1) * d)
        kh = kk[:, sl]
        n = jnp.sqrt(jnp.sum(kh * kh, axis=-1, keepdims=True))
        inv_norms.append(jnp.broadcast_to(1.0 / jnp.maximum(n, 1e-12), (tb, d)))
        bonus.append(jnp.broadcast_to(jnp.sum(rk[:, sl], axis=-1, keepdims=True), (tb, d)))
    kk = kk * jnp.concatenate(inv_norms, axis=1)
    bonus = jnp.concatenate(bonus, axis=1) * v
    a_vec = -kk
    b_vec = kk * a_sig

    n_chunks = tb // c
    at, rt, bt, kt, bc, kc, pc = [], [], [], [], [], [], []
    for ci in range(n_chunks):
        rows = slice(ci * c, (ci + 1) * c)
        lw_c = lw[rows]
        cum = jnp.dot(tri_ref[...], lw_c, precision=lax.Precision.HIGHEST, preferred_element_type=F32)
        cum_end = cum[c - 1:, :]
        p_in = jnp.exp(cum)
        p_inv = jnp.exp(-cum)
        p_out = jnp.exp(cum_end - cum)
        at.append((a_vec[rows] * jnp.exp(cum - lw_c)).astype(BF16))
        rt.append((r[rows] * p_in).astype(BF16))
        bt.append((b_vec[rows] * p_inv).astype(BF16))
        kt.append((k2[rows] * p_inv).astype(BF16))
        bc.append((b_vec[rows] * p_out).astype(BF16))
        kc.append((k2[rows] * p_out).astype(BF16))
        pc.append(jnp.exp(cum_end))
    vb = v.astype(BF16)
    eye = eye_ref[...]
    mask = mask_ref[...]

    lanes = 2 * d
    n_groups = w // lanes
    pair_diag = _pair_diag
    units = [(ci, g) for ci in range(n_chunks) for g in range(n_groups)]
    gsl = lambda g: slice(g * lanes, (g + 1) * lanes)
    csl = lambda ci: slice(ci * c, (ci + 1) * c)
    tril = mask > 0.5
    vd_u = {(ci, g): pair_diag(vb[csl(ci), gsl(g)]) for ci, g in units}
    g_u = {}
    for ci, g in units:
        lhs = jnp.concatenate([at[ci][:, gsl(g)], rt[ci][:, gsl(g)]], axis=0)
        rhs = jnp.concatenate([pair_diag(bt[ci][:, gsl(g)]), pair_diag(kt[ci][:, gsl(g)])], axis=0)
        g_u[ci, g] = jnp.where(tril, _dot_nt(lhs, rhs), 0.0)
    m_u = {u: g_u[u][:c, :lanes] for u in units}
    t_u = {u: eye + m_u[u] for u in units}
    mb_u = {u: m_u[u].astype(BF16) for u in units}
    steps = 1
    while 2 * steps < c:
        mb_u = {u: _dot(mb_u[u], pair_diag(mb_u[u])).astype(BF16) for u in units}
        t_u = {u: t_u[u] + _dot(t_u[u].astype(BF16), pair_diag(mb_u[u])) for u in units}
        steps *= 2
    t_u = {u: t_u[u].astype(BF16) for u in units}
    w_u = {(ci, g): _dot(t_u[ci, g], pair_diag(at[ci][:, gsl(g)])).astype(BF16) for ci, g in units}
    akv_u = {u: _dot(g_u[u][:c, lanes:].astype(BF16), vd_u[u]).astype(BF16) for u in units}
    u0_u = {u: _dot(t_u[u], pair_diag(akv_u[u])) for u in units}
    y0_u = {u: _dot(g_u[u][c:, lanes:].astype(BF16), vd_u[u]) for u in units}

    same_head = state_mask_ref[...] > 0.5
    s_g = [state_ref[g] for g in range(n_groups)]
    y_u = {}
    for ci in range(n_chunks):
        s_b = [s.astype(BF16) for s in s_g]
        u_b = [(_dot_nt(w_u[ci, g], s_b[g]) + u0_u[ci, g]).astype(BF16) for g in range(n_groups)]
        for g in range(n_groups):
            y_u[ci, g] = (_dot_nt(rt[ci][:, gsl(g)], s_b[g])
                          + _dot(g_u[ci, g][c:, :lanes].astype(BF16), pair_diag(u_b[g])) + y0_u[ci, g])
        s_g = [s_g[g] * pc[ci][:, gsl(g)]
               + jnp.where(same_head,
                           _dot_tn(jnp.concatenate([u_b[g], vb[csl(ci), gsl(g)]], axis=0),
                                   jnp.concatenate([bc[ci][:, gsl(g)], kc[ci][:, gsl(g)]], axis=0)), 0.0)
               for g in range(n_groups)]
    for g in range(n_groups):
        state_ref[g] = s_g[g]
    y_all = jnp.concatenate([jnp.concatenate([y_u[ci, g] for g in range(n_groups)], axis=1)
                             for ci in range(n_chunks)], axis=0)
    y_heads = [_head_norm(y_all[:, h * d:(h + 1) * d], GN_EPS_RWKV) for h in range(N_HEADS)]
    yn = jnp.concatenate(y_heads, axis=1) * gnw_ref[...] + gnb_ref[...]
    return (yn + bonus) * gate


N_RET_IN = 10
N_WKV_IN = 18


def _mixers_kernel(*refs):
    ret_in, refs = refs[:N_RET_IN], refs[N_RET_IN:]
    wkv_in, refs = refs[:N_WKV_IN], refs[N_WKV_IN:]
    o_ref, ret_state_ref, carry_rkv_ref, carry_lora_ref, wkv_state_ref = refs

    @pl.when(pl.program_id(1) == 0)
    def _():
        for ref in (ret_state_ref, carry_rkv_ref, carry_lora_ref, wkv_state_ref):
            ref[...] = jnp.zeros_like(ref)

    y_ret = _retention_body(*ret_in, ret_state_ref)
    y_rwkv = _wkv_body(*wkv_in, carry_rkv_ref, carry_lora_ref, wkv_state_ref)
    o_ref[0] = jnp.concatenate([y_ret, y_rwkv], axis=1).astype(o_ref.dtype)


def _mixers(p_ret, p_rkv, p_lora, positions, ret_gn_w, ret_gn_b, mu, w0, w_up, a0, a_up, g_up, k_k, k_a, r_k,
            gn_w, gn_b):
    b, s, _ = p_ret.shape
    tb, c = WKV_BLOCK, WKV_CHUNK
    assert tb % RET_CHUNK == 0
    mu_rkv, mu_lora = mu[:, :3 * RWKV_WIDTH], mu[:, 3 * RWKV_WIDTH:]
    idx = jnp.arange(c)
    incl = (idx[:, None] >= idx[None, :]).astype(F32)
    strict = (idx[:, None] > idx[None, :]).astype(F32)
    mask = jnp.concatenate([jnp.tile(strict, (1, 4)), jnp.tile(incl, (1, 4))], axis=0)
    eye = jnp.tile(jnp.eye(c, dtype=F32), (1, 2))
    head_of_lane = jnp.arange(2 * HEAD_DIM) // HEAD_DIM
    state_mask = (head_of_lane[:, None] == head_of_lane[None, :]).astype(F32)
    ret_consts = _retention_consts(tb) + [state_mask, ret_gn_w, ret_gn_b]
    wkv_consts = [mu_rkv, mu_lora, w0, w_up.astype(BF16), a0, a_up.astype(BF16), g_up.astype(BF16),
                  k_k, k_a, r_k, gn_w, gn_b, incl, mask, eye, state_mask]
    assert 2 + len(ret_consts) == N_RET_IN and 2 + len(wkv_consts) == N_WKV_IN
    blk = lambda bi, ti: (bi, ti, 0)
    return pl.pallas_call(
        _mixers_kernel,
        grid=(b, s // tb),
        in_specs=[pl.BlockSpec((1, tb, RET_PROJ), blk), pl.BlockSpec((1, tb, 1), blk)]
                 + [_full(a.shape) for a in ret_consts]
                 + [pl.BlockSpec((1, tb, 3 * RWKV_WIDTH), blk), pl.BlockSpec((1, tb, LORA_WIDTH), blk)]
                 + [_full(a.shape) for a in wkv_consts],
        out_specs=pl.BlockSpec((1, tb, RET_WIDTH + RWKV_WIDTH), blk),
        out_shape=jax.ShapeDtypeStruct((b, s, RET_WIDTH + RWKV_WIDTH), BF16),
        scratch_shapes=[pltpu.VMEM((N_HEADS // 2, 2 * HEAD_DIM, 2 * HEAD_DIM), F32),
                        pltpu.VMEM((1, 3 * RWKV_WIDTH), F32), pltpu.VMEM((1, LORA_WIDTH), F32),
                        pltpu.VMEM((N_HEADS // 2, 2 * HEAD_DIM, 2 * HEAD_DIM), F32)],
        compiler_params=_params(2),
        name="mixers",
    )(p_ret, positions[..., None], *ret_consts, p_rkv, p_lora, *wkv_consts)


def _mem_kv_kernel(m_ref, g_ref, w_ref, o_ref):
    h = _rms(m_ref[...], g_ref[...]).astype(BF16)
    o_ref[...] = _dot(h, w_ref[...]).astype(o_ref.dtype)


def _mem_kv(mem2d, g, w_kv, tm=256):
    t = mem2d.shape[0]
    n = w_kv.shape[1]
    return pl.pallas_call(
        _mem_kv_kernel,
        grid=(t // tm,),
        in_specs=[pl.BlockSpec((tm, D_MODEL), lambda i: (i, 0)), _full(g.shape), _full(w_kv.shape)],
        out_specs=pl.BlockSpec((tm, n), lambda i: (i, 0)),
        out_shape=jax.ShapeDtypeStruct((t, n), BF16),
        compiler_params=_params(1),
        name="mem_kv",
    )(mem2d, g, w_kv)


def _post_mix_kernel(x_ref, y_ref, wout_ref, gx_ref, wq_ref, kv_ref, wo_ref, o_ref):
    x1 = x_ref[0] + _dot(y_ref[0], wout_ref[...])
    q = _dot(_rms(x1, gx_ref[...]).astype(BF16), wq_ref[...]).astype(BF16)
    kv = kv_ref[0]
    heads = []
    for h in range(XATTN_HEADS):
        sl = slice(h * XATTN_HEAD_DIM, (h + 1) * XATTN_HEAD_DIM)
        s = _dot_nt(q[:, sl], kv[:, sl]) * (XATTN_HEAD_DIM ** -0.5)
        e = jnp.exp(s - jnp.max(s, axis=-1, keepdims=True))
        prob = e / jnp.sum(e, axis=-1, keepdims=True)
        heads.append(_dot(prob.astype(BF16), kv[:, D_MODEL + h * XATTN_HEAD_DIM:D_MODEL + (h + 1) * XATTN_HEAD_DIM]))
    o = jnp.concatenate(heads, axis=1).astype(BF16)
    o_ref[0] = x1 + _dot(o, wo_ref[...])


def _post_mix(x, y, w_out, g_x, w_q, kv, w_o, tm=256):
    b, s, _ = x.shape
    m = kv.shape[1]
    blk = lambda bi, ti: (bi, ti, 0)
    return pl.pallas_call(
        _post_mix_kernel,
        grid=(b, s // tm),
        in_specs=[pl.BlockSpec((1, tm, D_MODEL), blk), pl.BlockSpec((1, tm, y.shape[2]), blk),
                  _full(w_out.shape), _full(g_x.shape), _full(w_q.shape), pl.BlockSpec((1, m, 2 * D_MODEL), lambda bi, ti: (bi, 0, 0)),
                  _full(w_o.shape)],
        out_specs=pl.BlockSpec((1, tm, D_MODEL), blk),
        out_shape=jax.ShapeDtypeStruct((b, s, D_MODEL), F32),
        compiler_params=_params(2),
        name="post_mix",
    )(x, y, w_out, g_x, w_q, kv, w_o)


def _mlp_kernel(x_ref, g_ref, wup_ref, wdown_ref, gf_ref, o_ref, *, tf, final_norm):
    x = x_ref[...]
    h = _rms(x, g_ref[...]).astype(BF16)
    acc = x
    for j in range(D_FF // tf):
        u = jnp.maximum(_dot(h, wup_ref[:, j * tf:(j + 1) * tf]), 0.0)
        acc = acc + _dot((u * u).astype(BF16), wdown_ref[j * tf:(j + 1) * tf, :])
    o_ref[...] = _rms(acc, gf_ref[...]) if final_norm else acc


def _mlp(x2d, g, w_up, w_down, g_final, final_norm, tm=256, tf=512):
    t = x2d.shape[0]
    row = lambda i: (i, 0)
    return pl.pallas_call(
        functools.partial(_mlp_kernel, tf=tf, final_norm=final_norm),
        grid=(t // tm,),
        in_specs=[pl.BlockSpec((tm, D_MODEL), row), _full(g.shape), _full(w_up.shape), _full(w_down.shape),
                  _full(g_final.shape)],
        out_specs=pl.BlockSpec((tm, D_MODEL), row),
        out_shape=jax.ShapeDtypeStruct((t, D_MODEL), F32),
        compiler_params=_params(1),
        name="mlp",
    )(x2d, g, w_up, w_down, g_final)


def kernel(x, mem, positions, norm_mix, w_in, ret_gn_w, ret_gn_b, rwkv_mu, rwkv_w0, rwkv_w_up, rwkv_a0,
           rwkv_a_up, rwkv_g_up, rwkv_k_k, rwkv_k_a, rwkv_r_k, rwkv_gn_w, rwkv_gn_b, w_out, norm_xattn,
           norm_mem, xattn_w_q, xattn_w_kv, xattn_w_o, norm_mlp, mlp_w_up, mlp_w_down, norm_final):
    b, s, dm = x.shape
    n_layers = w_in.shape[0]
    rkv_end = RET_PROJ + 3 * RWKV_WIDTH
    for l in range(n_layers):
        w_in_l = w_in[l].astype(BF16)
        p_ret, p_rkv, p_lora = _in_proj(x.reshape(b * s, dm), norm_mix[l][None, :], w_in_l[:, :RET_PROJ],
                                        w_in_l[:, RET_PROJ:rkv_end], w_in_l[:, rkv_end:])
        y = _mixers(p_ret.reshape(b, s, -1), p_rkv.reshape(b, s, -1), p_lora.reshape(b, s, -1), positions,
                    ret_gn_w[l][None, :], ret_gn_b[l][None, :], rwkv_mu[l][None, :],
                    rwkv_w0[l][None, :], rwkv_w_up[l], rwkv_a0[l][None, :], rwkv_a_up[l], rwkv_g_up[l],
                    rwkv_k_k[l][None, :], rwkv_k_a[l][None, :], rwkv_r_k[l].reshape(1, -1),
                    rwkv_gn_w[l][None, :], rwkv_gn_b[l][None, :])
        kv = _mem_kv(mem.reshape(-1, dm), norm_mem[l][None, :], xattn_w_kv[l].astype(BF16))
        x = _post_mix(x, y, w_out[l].astype(BF16), norm_xattn[l][None, :],
                      xattn_w_q[l].astype(BF16), kv.reshape(b, -1, 2 * dm), xattn_w_o[l].astype(BF16))
        x = _mlp(x.reshape(b * s, dm), norm_mlp[l][None, :], mlp_w_up[l].astype(BF16),
                 mlp_w_down[l].astype(BF16), norm_final[None, :], l == n_layers - 1).reshape(b, s, dm)
    return x
```

```python
import functools

import jax
import jax.numpy as jnp
from jax import lax
from jax.experimental import pallas as pl
from jax.experimental.pallas import tpu as pltpu

D_MODEL = 1024
HEAD_DIM = 64
RET_WIDTH = 512
RWKV_WIDTH = 512
N_HEADS = 8
RET_CHUNK = 128
ROPE_BASE = 10000.0
DECAY_LORA = 64
AAA_LORA = 64
GATE_LORA = 160
LORA_WIDTH = DECAY_LORA + AAA_LORA + GATE_LORA
RET_PROJ = 4 * RET_WIDTH
XATTN_HEADS = 4
XATTN_HEAD_DIM = D_MODEL // XATTN_HEADS
D_FF = 4 * D_MODEL
RMS_EPS = 1e-6
GN_EPS_RET = 1e-5
GN_EPS_RWKV = 64e-5

WKV_CHUNK = 64
WKV_BLOCK = 512
VMEM_LIMIT_BYTES = 56 * 1024 * 1024

BF16 = jnp.bfloat16
F32 = jnp.float32


def _dot(a, b):
    return jnp.dot(a, b, preferred_element_type=F32)


def _dot_nt(a, b):
    return lax.dot_general(a, b, (((1,), (1,)), ((), ())), preferred_element_type=F32)


def _dot_tn(a, b):
    return lax.dot_general(a, b, (((0,), (0,)), ((), ())), preferred_element_type=F32)


def _rms(x, g):
    return x * lax.rsqrt(jnp.mean(x * x, axis=-1, keepdims=True) + RMS_EPS) * g


def _params(n_axes):
    return pltpu.CompilerParams(dimension_semantics=("arbitrary",) * n_axes,
                                vmem_limit_bytes=VMEM_LIMIT_BYTES)


def _full(shape):
    zeros = (0,) * len(shape)
    return pl.BlockSpec(shape, lambda *_: zeros)


def _head_norm(y, eps):
    mu = jnp.mean(y, axis=-1, keepdims=True)
    d = y - mu
    var = jnp.mean(d * d, axis=-1, keepdims=True)
    return d * lax.rsqrt(var + eps)


def _pair_diag(x):
    in_h0 = lax.broadcasted_iota(jnp.int32, x.shape, 1) < HEAD_DIM
    zero = jnp.zeros_like(x)
    return jnp.concatenate([jnp.where(in_h0, x, zero), jnp.where(in_h0, zero, x)], axis=0)


def _retention_body(p, pos, freq_ref, dmask_ref, xi_ref, zeta_ref, gc_ref, same_head_ref, gnw_ref, gnb_ref,
                    state_ref):
    c = RET_CHUNK
    w = RET_WIDTH
    lanes = 2 * HEAD_DIM
    n_groups = w // lanes

    tb = p.shape[0]
    q, k, v, gate = p[:, :w], p[:, w:2 * w], p[:, 2 * w:3 * w], p[:, 3 * w:]

    ang = pos.astype(F32) * freq_ref[...]
    cos = jnp.concatenate([jnp.cos(ang)] * n_groups, axis=1)
    sin = jnp.concatenate([jnp.sin(ang)] * n_groups, axis=1)
    lane = lax.broadcasted_iota(jnp.int32, (tb, w), 1)
    first_half = (lane % HEAD_DIM) < (HEAD_DIM // 2)

    def rope(t):
        rot = jnp.where(first_half, -pltpu.roll(t, w - HEAD_DIM // 2, 1), pltpu.roll(t, HEAD_DIM // 2, 1))
        return t * cos + rot * sin

    q = rope(q)
    k = rope(k) * (HEAD_DIM ** -0.5)
    q_in = (q * xi_ref[...]).astype(BF16)
    k_out = (k * zeta_ref[...]).astype(BF16)
    q = q.astype(BF16)
    k = k.astype(BF16)
    vb = v.astype(BF16)

    units = [(ci, g) for ci in range(tb // c) for g in range(n_groups)]
    tile = lambda x, ci, g: x[ci * c:(ci + 1) * c, g * lanes:(g + 1) * lanes]
    s_u = {(ci, g): (_dot_nt(tile(q, ci, g), _pair_diag(tile(k, ci, g))) * dmask_ref[g]).astype(BF16)
           for ci, g in units}
    y_u = {(ci, g): _dot(s_u[ci, g], _pair_diag(tile(vb, ci, g))) for ci, g in units}
    kv_u = {(ci, g): _dot_tn(tile(k_out, ci, g), tile(vb, ci, g)) for ci, g in units}
    same_head = same_head_ref[...] > 0.5
    r_g = [state_ref[g] for g in range(n_groups)]
    for ci in range(tb // c):
        for g in range(n_groups):
            y_u[ci, g] = y_u[ci, g] + _dot(tile(q_in, ci, g), r_g[g].astype(BF16))
        r_g = [r_g[g] * gc_ref[g] + jnp.where(same_head, kv_u[ci, g], 0.0) for g in range(n_groups)]
    for g in range(n_groups):
        state_ref[g] = r_g[g]
    y_all = jnp.concatenate([jnp.concatenate([y_u[ci, g] for g in range(n_groups)], axis=1)
                             for ci in range(tb // c)], axis=0)
    yn = jnp.concatenate([_head_norm(y_all[:, h * HEAD_DIM:(h + 1) * HEAD_DIM], GN_EPS_RET)
                          for h in range(N_HEADS)], axis=1) * gnw_ref[...] + gnb_ref[...]
    return gate * jax.nn.sigmoid(gate) * yn


def _retention_consts(tb):
    c = RET_CHUNK
    half = HEAD_DIM // 2
    inv_freq = ROPE_BASE ** (-jnp.arange(half, dtype=F32) / half)
    freq = jnp.tile(inv_freq, 128 // half)[None, :]
    log_g = jnp.log(1.0 - 2.0 ** (-5.0 - jnp.arange(N_HEADS, dtype=F32)))
    idx = jnp.arange(c, dtype=F32)
    diff = idx[:, None] - idx[None, :]
    causal = diff >= 0
    dmask = jnp.where(causal[None], jnp.exp(log_g[:, None, None] * jnp.where(causal, diff, 0.0)[None]), 0.0)
    xi = jnp.exp(log_g[:, None] * (idx + 1.0)[None])
    zeta = jnp.exp(log_g[:, None] * (c - 1.0 - idx)[None])
    g_chunk = jnp.exp(log_g * c)
    xi_w = jnp.tile(jnp.repeat(xi.T, HEAD_DIM, axis=1), (tb // c, 1))
    zeta_w = jnp.tile(jnp.repeat(zeta.T, HEAD_DIM, axis=1), (tb // c, 1))
    dmask_pairs = jnp.concatenate([dmask[0::2], dmask[1::2]], axis=2)
    gc_rows = jnp.repeat(g_chunk, HEAD_DIM).reshape(N_HEADS // 2, 2 * HEAD_DIM, 1)
    gc_w = jnp.broadcast_to(gc_rows, (N_HEADS // 2, 2 * HEAD_DIM, 2 * HEAD_DIM))
    return [freq, dmask_pairs, xi_w, zeta_w, gc_w]


def _shift(p, carry_ref, mu):
    rows = lax.broadcasted_iota(jnp.int32, p.shape, 0)
    prev = jnp.where(rows == 0, carry_ref[...], pltpu.roll(p, 1, 0))
    carry_ref[...] = p[p.shape[0] - 1:, :]
    return p + (prev - p) * mu


def _wkv_body(prkv_raw, plora_raw, mu_rkv_ref, mu_lora_ref, w0_ref, wup_ref, a0_ref, aup_ref, gup_ref,
              kk_ref, ka_ref, rk_ref, gnw_ref, gnb_ref, tri_ref, mask_ref, eye_ref, state_mask_ref,
              carry_rkv_ref, carry_lora_ref, state_ref):
    tb = prkv_raw.shape[0]
    c = WKV_CHUNK
    w = RWKV_WIDTH
    d = HEAD_DIM

    prkv = _shift(prkv_raw, carry_rkv_ref, mu_rkv_ref[...])
    plora = _shift(plora_raw, carry_lora_ref, mu_lora_ref[...])
    r, k, v = prkv[:, :w], prkv[:, w:2 * w], prkv[:, 2 * w:]
    w_lr = plora[:, :DECAY_LORA]
    a_lr = plora[:, DECAY_LORA:DECAY_LORA + AAA_LORA]
    g_lr = plora[:, DECAY_LORA + AAA_LORA:]

    z = -(w0_ref[...] + _dot(jnp.tanh(w_lr).astype(BF16), wup_ref[...]))
    w_log = -(jnp.maximum(z, 0.0) + jnp.log1p(jnp.exp(-jnp.abs(z)))) - 0.5
    lw = -jnp.exp(w_log)
    a_sig = jax.nn.sigmoid(a0_ref[...] + _dot(a_lr.astype(BF16), aup_ref[...]))
    gate = _dot(jax.nn.sigmoid(g_lr).astype(BF16), gup_ref[...])
    kk = k * kk_ref[...]
    k2 = k * (1.0 + (a_sig - 1.0) * ka_ref[...])
    rk = r * k2 * rk_ref[...]

    inv_norms, bonus = [], []
    for h in range(N_HEADS):
        sl = slice(h * d, (h + 1) * d)
        kh = kk[:, sl]
        n = jnp.sqrt(jnp.sum(kh * kh, axis=-1, keepdims=True))
        inv_norms.append(jnp.broadcast_to(1.0 / jnp.maximum(n, 1e-12), (tb, d)))
        bonus.append(jnp.broadcast_to(jnp.sum(rk[:, sl], axis=-1, keepdims=True), (tb, d)))
    kk = kk * jnp.concatenate(inv_norms, axis=1)
    bonus = jnp.concatenate(bonus, axis=1) * v
    a_vec = -kk
    b_vec = kk * a_sig

    n_chunks = tb // c
    at, rt, bt, kt, bc, kc, pc = [], [], [], [], [], [], []
    for ci in range(n_chunks):
        rows = slice(ci * c, (ci + 1) * c)
        lw_c = lw[rows]
        cum = jnp.dot(tri_ref[...], lw_c, precision=lax.Precision.HIGHEST, preferred_element_type=F32)
        cum_end = cum[c - 1:, :]
        p_in = jnp.exp(cum)
        p_inv = jnp.exp(-cum)
        p_out = jnp.exp(cum_end - cum)
        at.append((a_vec[rows] * jnp.exp(cum - lw_c)).astype(BF16))
        rt.append((r[rows] * p_in).astype(BF16))
        bt.append((b_vec[rows] * p_inv).astype(BF16))
        kt.append((k2[rows] * p_inv).astype(BF16))
        bc.append((b_vec[rows] * p_out).astype(BF16))
        kc.append((k2[rows] * p_out).astype(BF16))
        pc.append(jnp.exp(cum_end))
    vb = v.astype(BF16)
    eye = eye_ref[...]
    mask = mask_ref[...]

    lanes = 2 * d
    n_groups = w // lanes
    pair_diag = _pair_diag
    units = [(ci, g) for ci in range(n_chunks) for g in range(n_groups)]
    gsl = lambda g: slice(g * lanes, (g + 1) * lanes)
    csl = lambda ci: slice(ci * c, (ci + 1) * c)
    tril = mask > 0.5
    vd_u = {(ci, g): pair_diag(vb[csl(ci), gsl(g)]) for ci, g in units}
    g_u = {}
    for ci, g in units:
        lhs = jnp.concatenate([at[ci][:, gsl(g)], rt[ci][:, gsl(g)]], axis=0)
        rhs = jnp.concatenate([pair_diag(bt[ci][:, gsl(g)]), pair_diag(kt[ci][:, gsl(g)])], axis=0)
        g_u[ci, g] = jnp.where(tril, _dot_nt(lhs, rhs), 0.0)
    m_u = {u: g_u[u][:c, :lanes] for u in units}
    t_u = {u: eye + m_u[u] for u in units}
    mb_u = {u: m_u[u].astype(BF16) for u in units}
    steps = 1
    while 2 * steps < c:
        mb_u = {u: _dot(mb_u[u], pair_diag(mb_u[u])).astype(BF16) for u in units}
        t_u = {u: t_u[u] + _dot(t_u[u].astype(BF16), pair_diag(mb_u[u])) for u in units}
        steps *= 2
    t_u = {u: t_u[u].astype(BF16) for u in units}
    w_u = {(ci, g): _dot(t_u[ci, g], pair_diag(at[ci][:, gsl(g)])).astype(BF16) for ci, g in units}
    akv_u = {u: _dot(g_u[u][:c, lanes:].astype(BF16), vd_u[u]).astype(BF16) for u in units}
    u0_u = {u: _dot(t_u[u], pair_diag(akv_u[u])) for u in units}
    y0_u = {u: _dot(g_u[u][c:, lanes:].astype(BF16), vd_u[u]) for u in units}

    same_head = state_mask_ref[...] > 0.5
    s_g = [state_ref[g] for g in range(n_groups)]
    y_u = {}
    for ci in range(n_chunks):
        s_b = [s.astype(BF16) for s in s_g]
        u_b = [(_dot_nt(w_u[ci, g], s_b[g]) + u0_u[ci, g]).astype(BF16) for g in range(n_groups)]
        for g in range(n_groups):
            y_u[ci, g] = (_dot_nt(rt[ci][:, gsl(g)], s_b[g])
                          + _dot(g_u[ci, g][c:, :lanes].astype(BF16), pair_diag(u_b[g])) + y0_u[ci, g])
        s_g = [s_g[g] * pc[ci][:, gsl(g)]
               + jnp.where(same_head,
                           _dot_tn(jnp.concatenate([u_b[g], vb[csl(ci), gsl(g)]], axis=0),
                                   jnp.concatenate([bc[ci][:, gsl(g)], kc[ci][:, gsl(g)]], axis=0)), 0.0)
               for g in range(n_groups)]
    for g in range(n_groups):
        state_ref[g] = s_g[g]
    y_all = jnp.concatenate([jnp.concatenate([y_u[ci, g] for g in range(n_groups)], axis=1)
                             for ci in range(n_chunks)], axis=0)
    y_heads = [_head_norm(y_all[:, h * d:(h + 1) * d], GN_EPS_RWKV) for h in range(N_HEADS)]
    yn = jnp.concatenate(y_heads, axis=1) * gnw_ref[...] + gnb_ref[...]
    return (yn + bonus) * gate


N_PROJ_IN = 7
N_RET_CONST = 8
N_WKV_CONST = 16


def _mixers_kernel(*refs, steps_per_seq):
    (x_first_ref, x_mid_ref, x_next_ref, g_ref, w_ret_ref, w_rkv_ref, w_lora_ref), refs = (
        refs[:N_PROJ_IN], refs[N_PROJ_IN:])
    pos_ref, refs = refs[0], refs[1:]
    ret_consts, refs = refs[:N_RET_CONST], refs[N_RET_CONST:]
    wkv_consts, refs = refs[:N_WKV_CONST], refs[N_WKV_CONST:]
    o_ref, refs = refs[0], refs[1:]
    p_a, p_b, refs = refs[:3], refs[3:6], refs[6:]
    ret_state_ref, carry_rkv_ref, carry_lora_ref, wkv_state_ref = refs
    i = pl.program_id(0)
    half = o_ref.shape[0] // 2

    def project(x_ref, dst):
        h = _rms(x_ref[...], g_ref[...]).astype(BF16)
        for w_ref, p_ref in zip((w_ret_ref, w_rkv_ref, w_lora_ref), dst):
            p_ref[...] = _dot(h, w_ref[...])

    def mix(src, rows):
        p_ret_ref, p_rkv_ref, p_lora_ref = src
        y_ret = _retention_body(p_ret_ref[...], pos_ref[rows, :], *ret_consts, ret_state_ref)
        y_rwkv = _wkv_body(p_rkv_ref[...], p_lora_ref[...], *wkv_consts, carry_rkv_ref, carry_lora_ref,
                           wkv_state_ref)
        o_ref[rows, :] = jnp.concatenate([y_ret, y_rwkv], axis=1).astype(o_ref.dtype)

    @pl.when(i == 0)
    def _():
        project(x_first_ref, p_a)

    @pl.when(i % steps_per_seq == 0)
    def _():
        for ref in (ret_state_ref, carry_rkv_ref, carry_lora_ref, wkv_state_ref):
            ref[...] = jnp.zeros_like(ref)

    mix(p_a, slice(0, half))
    project(x_mid_ref, p_b)
    mix(p_b, slice(half, 2 * half))
    project(x_next_ref, p_a)


def _mixers(x2d, norm_g, w_ret, w_rkv, w_lora, positions, ret_gn_w, ret_gn_b, mu, w0, w_up, a0, a_up, g_up,
            k_k, k_a, r_k, gn_w, gn_b):
    b, s = positions.shape
    tb, c = WKV_BLOCK, WKV_CHUNK
    assert tb % RET_CHUNK == 0 and s % tb == 0
    n_steps = b * s // tb
    mu_rkv, mu_lora = mu[:, :3 * RWKV_WIDTH], mu[:, 3 * RWKV_WIDTH:]
    idx = jnp.arange(c)
    incl = (idx[:, None] >= idx[None, :]).astype(F32)
    strict = (idx[:, None] > idx[None, :]).astype(F32)
    mask = jnp.concatenate([jnp.tile(strict, (1, 4)), jnp.tile(incl, (1, 4))], axis=0)
    eye = jnp.tile(jnp.eye(c, dtype=F32), (1, 2))
    head_of_lane = jnp.arange(2 * HEAD_DIM) // HEAD_DIM
    state_mask = (head_of_lane[:, None] == head_of_lane[None, :]).astype(F32)
    half = tb // 2
    ret_consts = _retention_consts(half) + [state_mask, ret_gn_w, ret_gn_b]
    wkv_consts = [mu_rkv, mu_lora, w0, w_up.astype(BF16), a0, a_up.astype(BF16), g_up.astype(BF16),
                  k_k, k_a, r_k, gn_w, gn_b, incl, mask, eye, state_mask]
    assert len(ret_consts) == N_RET_CONST and len(wkv_consts) == N_WKV_CONST
    n_half = 2 * n_steps
    proj = [norm_g, w_ret, w_rkv, w_lora]
    p_set = [pltpu.VMEM((half, RET_PROJ), F32), pltpu.VMEM((half, 3 * RWKV_WIDTH), F32),
             pltpu.VMEM((half, LORA_WIDTH), F32)]
    return pl.pallas_call(
        functools.partial(_mixers_kernel, steps_per_seq=s // tb),
        grid=(n_steps,),
        in_specs=[pl.BlockSpec((half, D_MODEL), lambda i: (0, 0)),
                  pl.BlockSpec((half, D_MODEL), lambda i: (2 * i + 1, 0)),
                  pl.BlockSpec((half, D_MODEL), lambda i: (jnp.minimum(2 * i + 2, n_half - 1), 0))]
                 + [_full(a.shape) for a in proj]
                 + [pl.BlockSpec((tb, 1), lambda i: (i, 0))] + [_full(a.shape) for a in ret_consts]
                 + [_full(a.shape) for a in wkv_consts],
        out_specs=pl.BlockSpec((tb, RET_WIDTH + RWKV_WIDTH), lambda i: (i, 0)),
        out_shape=jax.ShapeDtypeStruct((b * s, RET_WIDTH + RWKV_WIDTH), BF16),
        scratch_shapes=p_set + p_set + [
            pltpu.VMEM((N_HEADS // 2, 2 * HEAD_DIM, 2 * HEAD_DIM), F32),
            pltpu.VMEM((1, 3 * RWKV_WIDTH), F32), pltpu.VMEM((1, LORA_WIDTH), F32),
            pltpu.VMEM((N_HEADS // 2, 2 * HEAD_DIM, 2 * HEAD_DIM), F32)],
        compiler_params=_params(1),
        name="mixers",
    )(x2d, x2d, x2d, *proj, positions.reshape(b * s, 1), *ret_consts, *wkv_consts)


def _mem_kv_kernel(m_ref, g_ref, w_ref, o_ref):
    h = _rms(m_ref[...], g_ref[...]).astype(BF16)
    o_ref[...] = _dot(h, w_ref[...]).astype(o_ref.dtype)


def _mem_kv(mem2d, g, w_kv, tm=256):
    t = mem2d.shape[0]
    n = w_kv.shape[1]
    return pl.pallas_call(
        _mem_kv_kernel,
        grid=(t // tm,),
        in_specs=[pl.BlockSpec((tm, D_MODEL), lambda i: (i, 0)), _full(g.shape), _full(w_kv.shape)],
        out_specs=pl.BlockSpec((tm, n), lambda i: (i, 0)),
        out_shape=jax.ShapeDtypeStruct((t, n), BF16),
        compiler_params=_params(1),
        name="mem_kv",
    )(mem2d, g, w_kv)


def _post_mix_kernel(x_ref, y_ref, wout_ref, gx_ref, wq_ref, kv_ref, wo_ref, o_ref):
    x1 = x_ref[0] + _dot(y_ref[0], wout_ref[...])
    q = _dot(_rms(x1, gx_ref[...]).astype(BF16), wq_ref[...]).astype(BF16)
    kv = kv_ref[0]
    heads = []
    for h in range(XATTN_HEADS):
        sl = slice(h * XATTN_HEAD_DIM, (h + 1) * XATTN_HEAD_DIM)
        s = _dot_nt(q[:, sl], kv[:, sl]) * (XATTN_HEAD_DIM ** -0.5)
        e = jnp.exp(s - jnp.max(s, axis=-1, keepdims=True))
        prob = e / jnp.sum(e, axis=-1, keepdims=True)
        heads.append(_dot(prob.astype(BF16), kv[:, D_MODEL + h * XATTN_HEAD_DIM:D_MODEL + (h + 1) * XATTN_HEAD_DIM]))
    o = jnp.concatenate(heads, axis=1).astype(BF16)
    o_ref[0] = x1 + _dot(o, wo_ref[...])


def _post_mix(x, y, w_out, g_x, w_q, kv, w_o, tm=256):
    b, s, _ = x.shape
    m = kv.shape[1]
    blk = lambda bi, ti: (bi, ti, 0)
    return pl.pallas_call(
        _post_mix_kernel,
        grid=(b, s // tm),
        in_specs=[pl.BlockSpec((1, tm, D_MODEL), blk), pl.BlockSpec((1, tm, y.shape[2]), blk),
                  _full(w_out.shape), _full(g_x.shape), _full(w_q.shape), pl.BlockSpec((1, m, 2 * D_MODEL), lambda bi, ti: (bi, 0, 0)),
                  _full(w_o.shape)],
        out_specs=pl.BlockSpec((1, tm, D_MODEL), blk),
        out_shape=jax.ShapeDtypeStruct((b, s, D_MODEL), F32),
        compiler_params=_params(2),
        name="post_mix",
    )(x, y, w_out, g_x, w_q, kv, w_o)


def _mlp_kernel(x_ref, g_ref, wup_ref, wdown_ref, gf_ref, o_ref, *, tf, final_norm):
    x = x_ref[...]
    h = _rms(x, g_ref[...]).astype(BF16)
    acc = x
    for j in range(D_FF // tf):
        u = jnp.maximum(_dot(h, wup_ref[:, j * tf:(j + 1) * tf]), 0.0)
        acc = acc + _dot((u * u).astype(BF16), wdown_ref[j * tf:(j + 1) * tf, :])
    o_ref[...] = _rms(acc, gf_ref[...]) if final_norm else acc


def _mlp(x2d, g, w_up, w_down, g_final, final_norm, tm=256, tf=512):
    t = x2d.shape[0]
    row = lambda i: (i, 0)
    return pl.pallas_call(
        functools.partial(_mlp_kernel, tf=tf, final_norm=final_norm),
        grid=(t // tm,),
        in_specs=[pl.BlockSpec((tm, D_MODEL), row), _full(g.shape), _full(w_up.shape), _full(w_down.shape),
                  _full(g_final.shape)],
        out_specs=pl.BlockSpec((tm, D_MODEL), row),
        out_shape=jax.ShapeDtypeStruct((t, D_MODEL), F32),
        compiler_params=_params(1),
        name="mlp",
    )(x2d, g, w_up, w_down, g_final)


def kernel(x, mem, positions, norm_mix, w_in, ret_gn_w, ret_gn_b, rwkv_mu, rwkv_w0, rwkv_w_up, rwkv_a0,
           rwkv_a_up, rwkv_g_up, rwkv_k_k, rwkv_k_a, rwkv_r_k, rwkv_gn_w, rwkv_gn_b, w_out, norm_xattn,
           norm_mem, xattn_w_q, xattn_w_kv, xattn_w_o, norm_mlp, mlp_w_up, mlp_w_down, norm_final):
    b, s, dm = x.shape
    n_layers = w_in.shape[0]
    rkv_end = RET_PROJ + 3 * RWKV_WIDTH
    for l in range(n_layers):
        w_in_l = w_in[l].astype(BF16)
        y = _mixers(x.reshape(b * s, dm), norm_mix[l][None, :], w_in_l[:, :RET_PROJ],
                    w_in_l[:, RET_PROJ:rkv_end], w_in_l[:, rkv_end:], positions,
                    ret_gn_w[l][None, :], ret_gn_b[l][None, :], rwkv_mu[l][None, :],
                    rwkv_w0[l][None, :], rwkv_w_up[l], rwkv_a0[l][None, :], rwkv_a_up[l], rwkv_g_up[l],
                    rwkv_k_k[l][None, :], rwkv_k_a[l][None, :], rwkv_r_k[l].reshape(1, -1),
                    rwkv_gn_w[l][None, :], rwkv_gn_b[l][None, :])
        kv = _mem_kv(mem.reshape(-1, dm), norm_mem[l][None, :], xattn_w_kv[l].astype(BF16))
        x = _post_mix(x, y.reshape(b, s, -1), w_out[l].astype(BF16), norm_xattn[l][None, :],
                      xattn_w_q[l].astype(BF16), kv.reshape(b, -1, 2 * dm), xattn_w_o[l].astype(BF16))
        x = _mlp(x.reshape(b * s, dm), norm_mlp[l][None, :], mlp_w_up[l].astype(BF16),
                 mlp_w_down[l].astype(BF16), norm_final[None, :], l == n_layers - 1).reshape(b, s, dm)
    return x
```

```python
import functools

import jax
import jax.numpy as jnp
from jax import lax
from jax.experimental import pallas as pl
from jax.experimental.pallas import tpu as pltpu

D_MODEL = 1024
HEAD_DIM = 64
RET_WIDTH = 512
RWKV_WIDTH = 512
N_HEADS = 8
RET_CHUNK = 128
ROPE_BASE = 10000.0
DECAY_LORA = 64
AAA_LORA = 64
GATE_LORA = 160
LORA_WIDTH = DECAY_LORA + AAA_LORA + GATE_LORA
RET_PROJ = 4 * RET_WIDTH
XATTN_HEADS = 4
XATTN_HEAD_DIM = D_MODEL // XATTN_HEADS
D_FF = 4 * D_MODEL
RMS_EPS = 1e-6
GN_EPS_RET = 1e-5
GN_EPS_RWKV = 64e-5

WKV_CHUNK = 64
WKV_BLOCK = 512
VMEM_LIMIT_BYTES = 56 * 1024 * 1024

BF16 = jnp.bfloat16
F32 = jnp.float32


def _dot(a, b):
    return jnp.dot(a, b, preferred_element_type=F32)


def _dot_nt(a, b):
    return lax.dot_general(a, b, (((1,), (1,)), ((), ())), preferred_element_type=F32)


def _dot_tn(a, b):
    return lax.dot_general(a, b, (((0,), (0,)), ((), ())), preferred_element_type=F32)


def _rms(x, g):
    return x * lax.rsqrt(jnp.mean(x * x, axis=-1, keepdims=True) + RMS_EPS) * g


def _params(n_axes):
    return pltpu.CompilerParams(dimension_semantics=("arbitrary",) * n_axes,
                                vmem_limit_bytes=VMEM_LIMIT_BYTES)


def _full(shape):
    zeros = (0,) * len(shape)
    return pl.BlockSpec(shape, lambda *_: zeros, pipeline_mode=pl.Buffered(1))


def _head_norm(y, eps):
    mu = jnp.mean(y, axis=-1, keepdims=True)
    d = y - mu
    var = jnp.mean(d * d, axis=-1, keepdims=True)
    return d * lax.rsqrt(var + eps)


def _pair_diag(x):
    in_h0 = lax.broadcasted_iota(jnp.int32, x.shape, 1) < HEAD_DIM
    zero = jnp.zeros_like(x)
    return jnp.concatenate([jnp.where(in_h0, x, zero), jnp.where(in_h0, zero, x)], axis=0)


def _retention_body(p, pos, freq_ref, dmask_ref, xi_ref, zeta_ref, gc_ref, same_head_ref, gnw_ref, gnb_ref,
                    state_ref):
    c = RET_CHUNK
    w = RET_WIDTH
    lanes = 2 * HEAD_DIM
    n_groups = w // lanes

    tb = p.shape[0]
    q, k, v, gate = p[:, :w], p[:, w:2 * w], p[:, 2 * w:3 * w], p[:, 3 * w:]

    ang = pos.astype(F32) * freq_ref[...]
    cos = jnp.concatenate([jnp.cos(ang)] * n_groups, axis=1)
    sin = jnp.concatenate([jnp.sin(ang)] * n_groups, axis=1)
    lane = lax.broadcasted_iota(jnp.int32, (tb, w), 1)
    first_half = (lane % HEAD_DIM) < (HEAD_DIM // 2)

    def rope(t):
        rot = jnp.where(first_half, -pltpu.roll(t, w - HEAD_DIM // 2, 1), pltpu.roll(t, HEAD_DIM // 2, 1))
        return t * cos + rot * sin

    q = rope(q)
    k = rope(k) * (HEAD_DIM ** -0.5)
    q_in = (q * xi_ref[...]).astype(BF16)
    k_out = (k * zeta_ref[...]).astype(BF16)
    q = q.astype(BF16)
    k = k.astype(BF16)
    vb = v.astype(BF16)

    units = [(ci, g) for ci in range(tb // c) for g in range(n_groups)]
    tile = lambda x, ci, g: x[ci * c:(ci + 1) * c, g * lanes:(g + 1) * lanes]
    s_u = {(ci, g): (_dot_nt(tile(q, ci, g), _pair_diag(tile(k, ci, g))) * dmask_ref[g]).astype(BF16)
           for ci, g in units}
    y_u = {(ci, g): _dot(s_u[ci, g], _pair_diag(tile(vb, ci, g))) for ci, g in units}
    kv_u = {(ci, g): _dot_tn(tile(k_out, ci, g), tile(vb, ci, g)) for ci, g in units}
    same_head = same_head_ref[...] > 0.5
    r_g = [state_ref[g] for g in range(n_groups)]
    for ci in range(tb // c):
        for g in range(n_groups):
            y_u[ci, g] = y_u[ci, g] + _dot(tile(q_in, ci, g), r_g[g].astype(BF16))
        r_g = [r_g[g] * gc_ref[g] + jnp.where(same_head, kv_u[ci, g], 0.0) for g in range(n_groups)]
    for g in range(n_groups):
        state_ref[g] = r_g[g]
    y_all = jnp.concatenate([jnp.concatenate([y_u[ci, g] for g in range(n_groups)], axis=1)
                             for ci in range(tb // c)], axis=0)
    yn = jnp.concatenate([_head_norm(y_all[:, h * HEAD_DIM:(h + 1) * HEAD_DIM], GN_EPS_RET)
                          for h in range(N_HEADS)], axis=1) * gnw_ref[...] + gnb_ref[...]
    return gate * jax.nn.sigmoid(gate) * yn


def _retention_consts(tb):
    c = RET_CHUNK
    half = HEAD_DIM // 2
    inv_freq = ROPE_BASE ** (-jnp.arange(half, dtype=F32) / half)
    freq = jnp.tile(inv_freq, 128 // half)[None, :]
    log_g = jnp.log(1.0 - 2.0 ** (-5.0 - jnp.arange(N_HEADS, dtype=F32)))
    idx = jnp.arange(c, dtype=F32)
    diff = idx[:, None] - idx[None, :]
    causal = diff >= 0
    dmask = jnp.where(causal[None], jnp.exp(log_g[:, None, None] * jnp.where(causal, diff, 0.0)[None]), 0.0)
    xi = jnp.exp(log_g[:, None] * (idx + 1.0)[None])
    zeta = jnp.exp(log_g[:, None] * (c - 1.0 - idx)[None])
    g_chunk = jnp.exp(log_g * c)
    xi_w = jnp.tile(jnp.repeat(xi.T, HEAD_DIM, axis=1), (tb // c, 1))
    zeta_w = jnp.tile(jnp.repeat(zeta.T, HEAD_DIM, axis=1), (tb // c, 1))
    dmask_pairs = jnp.concatenate([dmask[0::2], dmask[1::2]], axis=2)
    gc_rows = jnp.repeat(g_chunk, HEAD_DIM).reshape(N_HEADS // 2, 2 * HEAD_DIM, 1)
    gc_w = jnp.broadcast_to(gc_rows, (N_HEADS // 2, 2 * HEAD_DIM, 2 * HEAD_DIM))
    return [freq, dmask_pairs, xi_w, zeta_w, gc_w]


def _shift(p, carry_ref, mu):
    rows = lax.broadcasted_iota(jnp.int32, p.shape, 0)
    prev = jnp.where(rows == 0, carry_ref[...], pltpu.roll(p, 1, 0))
    carry_ref[...] = p[p.shape[0] - 1:, :]
    return p + (prev - p) * mu


def _wkv_body(prkv_raw, plora_raw, mu_rkv_ref, mu_lora_ref, w0_ref, wup_ref, a0_ref, aup_ref, gup_ref,
              kk_ref, ka_ref, rk_ref, gnw_ref, gnb_ref, tri_ref, mask_ref, eye_ref, state_mask_ref,
              carry_rkv_ref, carry_lora_ref, state_ref):
    tb = prkv_raw.shape[0]
    c = WKV_CHUNK
    w = RWKV_WIDTH
    d = HEAD_DIM

    prkv = _shift(prkv_raw, carry_rkv_ref, mu_rkv_ref[...])
    plora = _shift(plora_raw, carry_lora_ref, mu_lora_ref[...])
    r, k, v = prkv[:, :w], prkv[:, w:2 * w], prkv[:, 2 * w:]
    w_lr = plora[:, :DECAY_LORA]
    a_lr = plora[:, DECAY_LORA:DECAY_LORA + AAA_LORA]
    g_lr = plora[:, DECAY_LORA + AAA_LORA:]

    z = -(w0_ref[...] + _dot(jnp.tanh(w_lr).astype(BF16), wup_ref[...]))
    w_log = -(jnp.maximum(z, 0.0) + jnp.log1p(jnp.exp(-jnp.abs(z)))) - 0.5
    lw = -jnp.exp(w_log)
    a_sig = jax.nn.sigmoid(a0_ref[...] + _dot(a_lr.astype(BF16), aup_ref[...]))
    gate = _dot(jax.nn.sigmoid(g_lr).astype(BF16), gup_ref[...])
    kk = k * kk_ref[...]
    k2 = k * (1.0 + (a_sig - 1.0) * ka_ref[...])
    rk = r * k2 * rk_ref[...]

    inv_norms, bonus = [], []
    for h in range(N_HEADS):
        sl = slice(h * d, (h + 1) * d)
        kh = kk[:, sl]
        n = jnp.sqrt(jnp.sum(kh * kh, axis=-1, keepdims=True))
        inv_norms.append(jnp.broadcast_to(1.0 / jnp.maximum(n, 1e-12), (tb, d)))
        bonus.append(jnp.broadcast_to(jnp.sum(rk[:, sl], axis=-1, keepdims=True), (tb, d)))
    kk = kk * jnp.concatenate(inv_norms, axis=1)
    bonus = jnp.concatenate(bonus, axis=1) * v
    a_vec = -kk
    b_vec = kk * a_sig

    n_chunks = tb // c
    at, rt, bt, kt, bc, kc, pc = [], [], [], [], [], [], []
    for ci in range(n_chunks):
        rows = slice(ci * c, (ci + 1) * c)
        lw_c = lw[rows]
        cum = jnp.dot(tri_ref[...], lw_c, precision=lax.Precision.HIGHEST, preferred_element_type=F32)
        cum_end = cum[c - 1:, :]
        p_in = jnp.exp(cum)
        p_inv = jnp.exp(-cum)
        p_out = jnp.exp(cum_end - cum)
        at.append((a_vec[rows] * jnp.exp(cum - lw_c)).astype(BF16))
        rt.append((r[rows] * p_in).astype(BF16))
        bt.append((b_vec[rows] * p_inv).astype(BF16))
        kt.append((k2[rows] * p_inv).astype(BF16))
        bc.append((b_vec[rows] * p_out).astype(BF16))
        kc.append((k2[rows] * p_out).astype(BF16))
        pc.append(jnp.exp(cum_end))
    vb = v.astype(BF16)
    eye = eye_ref[...]
    mask = mask_ref[...]

    lanes = 2 * d
    n_groups = w // lanes
    pair_diag = _pair_diag
    units = [(ci, g) for ci in range(n_chunks) for g in range(n_groups)]
    gsl = lambda g: slice(g * lanes, (g + 1) * lanes)
    csl = lambda ci: slice(ci * c, (ci + 1) * c)
    tril = mask > 0.5
    vd_u = {(ci, g): pair_diag(vb[csl(ci), gsl(g)]) for ci, g in units}
    g_u = {}
    for ci, g in units:
        lhs = jnp.concatenate([at[ci][:, gsl(g)], rt[ci][:, gsl(g)]], axis=0)
        rhs = jnp.concatenate([pair_diag(bt[ci][:, gsl(g)]), pair_diag(kt[ci][:, gsl(g)])], axis=0)
        g_u[ci, g] = jnp.where(tril, _dot_nt(lhs, rhs), 0.0)
    m_u = {u: g_u[u][:c, :lanes] for u in units}
    t_u = {u: eye + m_u[u] for u in units}
    mb_u = {u: m_u[u].astype(BF16) for u in units}
    steps = 1
    while 2 * steps < c:
        mb_u = {u: _dot(mb_u[u], pair_diag(mb_u[u])).astype(BF16) for u in units}
        t_u = {u: t_u[u] + _dot(t_u[u].astype(BF16), pair_diag(mb_u[u])) for u in units}
        steps *= 2
    t_u = {u: t_u[u].astype(BF16) for u in units}
    w_u = {(ci, g): _dot(t_u[ci, g], pair_diag(at[ci][:, gsl(g)])).astype(BF16) for ci, g in units}
    akv_u = {u: _dot(g_u[u][:c, lanes:].astype(BF16), vd_u[u]).astype(BF16) for u in units}
    u0_u = {u: _dot(t_u[u], pair_diag(akv_u[u])) for u in units}
    y0_u = {u: _dot(g_u[u][c:, lanes:].astype(BF16), vd_u[u]) for u in units}

    same_head = state_mask_ref[...] > 0.5
    s_g = [state_ref[g] for g in range(n_groups)]
    y_u = {}
    for ci in range(n_chunks):
        s_b = [s.astype(BF16) for s in s_g]
        u_b = [(_dot_nt(w_u[ci, g], s_b[g]) + u0_u[ci, g]).astype(BF16) for g in range(n_groups)]
        for g in range(n_groups):
            y_u[ci, g] = (_dot_nt(rt[ci][:, gsl(g)], s_b[g])
                          + _dot(g_u[ci, g][c:, :lanes].astype(BF16), pair_diag(u_b[g])) + y0_u[ci, g])
        s_g = [s_g[g] * pc[ci][:, gsl(g)]
               + jnp.where(same_head,
                           _dot_tn(jnp.concatenate([u_b[g], vb[csl(ci), gsl(g)]], axis=0),
                                   jnp.concatenate([bc[ci][:, gsl(g)], kc[ci][:, gsl(g)]], axis=0)), 0.0)
               for g in range(n_groups)]
    for g in range(n_groups):
        state_ref[g] = s_g[g]
    y_all = jnp.concatenate([jnp.concatenate([y_u[ci, g] for g in range(n_groups)], axis=1)
                             for ci in range(n_chunks)], axis=0)
    y_heads = [_head_norm(y_all[:, h * d:(h + 1) * d], GN_EPS_RWKV) for h in range(N_HEADS)]
    yn = jnp.concatenate(y_heads, axis=1) * gnw_ref[...] + gnb_ref[...]
    return (yn + bonus) * gate


N_PROJ_IN = 7
N_RET_CONST = 8
N_WKV_CONST = 16


def _mixers_kernel(*refs, steps_per_seq):
    (x_first_ref, x_mid_ref, x_next_ref, g_ref, w_ret_ref, w_rkv_ref, w_lora_ref), refs = (
        refs[:N_PROJ_IN], refs[N_PROJ_IN:])
    pos_ref, refs = refs[0], refs[1:]
    ret_consts, refs = refs[:N_RET_CONST], refs[N_RET_CONST:]
    wkv_consts, refs = refs[:N_WKV_CONST], refs[N_WKV_CONST:]
    o_ref, refs = refs[0], refs[1:]
    p_a, p_b, refs = refs[:3], refs[3:6], refs[6:]
    ret_state_ref, carry_rkv_ref, carry_lora_ref, wkv_state_ref = refs
    i = pl.program_id(0)
    half = o_ref.shape[0] // 2

    def project(x_ref, dst):
        h = _rms(x_ref[...], g_ref[...]).astype(BF16)
        for w_ref, p_ref in zip((w_ret_ref, w_rkv_ref, w_lora_ref), dst):
            p_ref[...] = _dot(h, w_ref[...])

    def mix(src, rows):
        p_ret_ref, p_rkv_ref, p_lora_ref = src
        y_ret = _retention_body(p_ret_ref[...], pos_ref[rows, :], *ret_consts, ret_state_ref)
        y_rwkv = _wkv_body(p_rkv_ref[...], p_lora_ref[...], *wkv_consts, carry_rkv_ref, carry_lora_ref,
                           wkv_state_ref)
        o_ref[rows, :] = jnp.concatenate([y_ret, y_rwkv], axis=1).astype(o_ref.dtype)

    @pl.when(i == 0)
    def _():
        project(x_first_ref, p_a)

    @pl.when(i % steps_per_seq == 0)
    def _():
        for ref in (ret_state_ref, carry_rkv_ref, carry_lora_ref, wkv_state_ref):
            ref[...] = jnp.zeros_like(ref)

    mix(p_a, slice(0, half))
    project(x_mid_ref, p_b)
    mix(p_b, slice(half, 2 * half))
    project(x_next_ref, p_a)


def _mixers(x2d, norm_g, w_ret, w_rkv, w_lora, positions, ret_gn_w, ret_gn_b, mu, w0, w_up, a0, a_up, g_up,
            k_k, k_a, r_k, gn_w, gn_b):
    b, s = positions.shape
    tb, c = WKV_BLOCK, WKV_CHUNK
    assert tb % RET_CHUNK == 0 and s % tb == 0
    n_steps = b * s // tb
    mu_rkv, mu_lora = mu[:, :3 * RWKV_WIDTH], mu[:, 3 * RWKV_WIDTH:]
    idx = jnp.arange(c)
    incl = (idx[:, None] >= idx[None, :]).astype(F32)
    strict = (idx[:, None] > idx[None, :]).astype(F32)
    mask = jnp.concatenate([jnp.tile(strict, (1, 4)), jnp.tile(incl, (1, 4))], axis=0)
    eye = jnp.tile(jnp.eye(c, dtype=F32), (1, 2))
    head_of_lane = jnp.arange(2 * HEAD_DIM) // HEAD_DIM
    state_mask = (head_of_lane[:, None] == head_of_lane[None, :]).astype(F32)
    half = tb // 2
    ret_consts = _retention_consts(half) + [state_mask, ret_gn_w, ret_gn_b]
    wkv_consts = [mu_rkv, mu_lora, w0, w_up.astype(BF16), a0, a_up.astype(BF16), g_up.astype(BF16),
                  k_k, k_a, r_k, gn_w, gn_b, incl, mask, eye, state_mask]
    assert len(ret_consts) == N_RET_CONST and len(wkv_consts) == N_WKV_CONST
    n_half = 2 * n_steps
    proj = [norm_g, w_ret, w_rkv, w_lora]
    p_set = [pltpu.VMEM((half, RET_PROJ), F32), pltpu.VMEM((half, 3 * RWKV_WIDTH), F32),
             pltpu.VMEM((half, LORA_WIDTH), F32)]
    return pl.pallas_call(
        functools.partial(_mixers_kernel, steps_per_seq=s // tb),
        grid=(n_steps,),
        in_specs=[pl.BlockSpec((half, D_MODEL), lambda i: (0, 0)),
                  pl.BlockSpec((half, D_MODEL), lambda i: (2 * i + 1, 0)),
                  pl.BlockSpec((half, D_MODEL), lambda i: (jnp.minimum(2 * i + 2, n_half - 1), 0))]
                 + [_full(a.shape) for a in proj]
                 + [pl.BlockSpec((tb, 1), lambda i: (i, 0))] + [_full(a.shape) for a in ret_consts]
                 + [_full(a.shape) for a in wkv_consts],
        out_specs=pl.BlockSpec((tb, RET_WIDTH + RWKV_WIDTH), lambda i: (i, 0)),
        out_shape=jax.ShapeDtypeStruct((b * s, RET_WIDTH + RWKV_WIDTH), BF16),
        scratch_shapes=p_set + p_set + [
            pltpu.VMEM((N_HEADS // 2, 2 * HEAD_DIM, 2 * HEAD_DIM), F32),
            pltpu.VMEM((1, 3 * RWKV_WIDTH), F32), pltpu.VMEM((1, LORA_WIDTH), F32),
            pltpu.VMEM((N_HEADS // 2, 2 * HEAD_DIM, 2 * HEAD_DIM), F32)],
        compiler_params=_params(1),
        name="mixers",
    )(x2d, x2d, x2d, *proj, positions.reshape(b * s, 1), *ret_consts, *wkv_consts)


def _mem_kv_kernel(m_ref, g_ref, w_ref, o_ref):
    h = _rms(m_ref[...], g_ref[...]).astype(BF16)
    o_ref[...] = _dot(h, w_ref[...]).astype(o_ref.dtype)


def _mem_kv(mem2d, g, w_kv, tm=256):
    t = mem2d.shape[0]
    n = w_kv.shape[1]
    return pl.pallas_call(
        _mem_kv_kernel,
        grid=(t // tm,),
        in_specs=[pl.BlockSpec((tm, D_MODEL), lambda i: (i, 0)), _full(g.shape), _full(w_kv.shape)],
        out_specs=pl.BlockSpec((tm, n), lambda i: (i, 0)),
        out_shape=jax.ShapeDtypeStruct((t, n), BF16),
        compiler_params=_params(1),
        name="mem_kv",
    )(mem2d, g, w_kv)


def _post_mix_kernel(x_ref, y_ref, wout_ref, gx_ref, wq_ref, kv_ref, wo_ref, o_ref):
    x1 = x_ref[0] + _dot(y_ref[0], wout_ref[...])
    q = _dot(_rms(x1, gx_ref[...]).astype(BF16), wq_ref[...]).astype(BF16)
    kv = kv_ref[0]
    heads = []
    for h in range(XATTN_HEADS):
        sl = slice(h * XATTN_HEAD_DIM, (h + 1) * XATTN_HEAD_DIM)
        s = _dot_nt(q[:, sl], kv[:, sl]) * (XATTN_HEAD_DIM ** -0.5)
        e = jnp.exp(s - jnp.max(s, axis=-1, keepdims=True))
        prob = e / jnp.sum(e, axis=-1, keepdims=True)
        heads.append(_dot(prob.astype(BF16), kv[:, D_MODEL + h * XATTN_HEAD_DIM:D_MODEL + (h + 1) * XATTN_HEAD_DIM]))
    o = jnp.concatenate(heads, axis=1).astype(BF16)
    o_ref[0] = x1 + _dot(o, wo_ref[...])


def _post_mix(x, y, w_out, g_x, w_q, kv, w_o, tm=1024):
    b, s, _ = x.shape
    m = kv.shape[1]
    blk = lambda bi, ti: (bi, ti, 0)
    return pl.pallas_call(
        _post_mix_kernel,
        grid=(b, s // tm),
        in_specs=[pl.BlockSpec((1, tm, D_MODEL), blk), pl.BlockSpec((1, tm, y.shape[2]), blk),
                  _full(w_out.shape), _full(g_x.shape), _full(w_q.shape), pl.BlockSpec((1, m, 2 * D_MODEL), lambda bi, ti: (bi, 0, 0)),
                  _full(w_o.shape)],
        out_specs=pl.BlockSpec((1, tm, D_MODEL), blk),
        out_shape=jax.ShapeDtypeStruct((b, s, D_MODEL), F32),
        compiler_params=_params(2),
        name="post_mix",
    )(x, y, w_out, g_x, w_q, kv, w_o)


def _mlp_kernel(x_ref, g_ref, wup_ref, wdown_ref, gf_ref, o_ref, *, tf, final_norm):
    x = x_ref[...]
    h = _rms(x, g_ref[...]).astype(BF16)
    acc = x
    for j in range(D_FF // tf):
        u = jnp.maximum(_dot(h, wup_ref[:, j * tf:(j + 1) * tf]), 0.0)
        acc = acc + _dot((u * u).astype(BF16), wdown_ref[j * tf:(j + 1) * tf, :])
    o_ref[...] = _rms(acc, gf_ref[...]) if final_norm else acc


def _mlp(x2d, g, w_up, w_down, g_final, final_norm, tm=1024, tf=512):
    t = x2d.shape[0]
    row = lambda i: (i, 0)
    return pl.pallas_call(
        functools.partial(_mlp_kernel, tf=tf, final_norm=final_norm),
        grid=(t // tm,),
        in_specs=[pl.BlockSpec((tm, D_MODEL), row), _full(g.shape), _full(w_up.shape), _full(w_down.shape),
                  _full(g_final.shape)],
        out_specs=pl.BlockSpec((tm, D_MODEL), row),
        out_shape=jax.ShapeDtypeStruct((t, D_MODEL), F32),
        compiler_params=_params(1),
        name="mlp",
    )(x2d, g, w_up, w_down, g_final)


def kernel(x, mem, positions, norm_mix, w_in, ret_gn_w, ret_gn_b, rwkv_mu, rwkv_w0, rwkv_w_up, rwkv_a0,
           rwkv_a_up, rwkv_g_up, rwkv_k_k, rwkv_k_a, rwkv_r_k, rwkv_gn_w, rwkv_gn_b, w_out, norm_xattn,
           norm_mem, xattn_w_q, xattn_w_kv, xattn_w_o, norm_mlp, mlp_w_up, mlp_w_down, norm_final):
    b, s, dm = x.shape
    n_layers = w_in.shape[0]
    rkv_end = RET_PROJ + 3 * RWKV_WIDTH
    for l in range(n_layers):
        w_in_l = w_in[l].astype(BF16)
        y = _mixers(x.reshape(b * s, dm), norm_mix[l][None, :], w_in_l[:, :RET_PROJ],
                    w_in_l[:, RET_PROJ:rkv_end], w_in_l[:, rkv_end:], positions,
                    ret_gn_w[l][None, :], ret_gn_b[l][None, :], rwkv_mu[l][None, :],
                    rwkv_w0[l][None, :], rwkv_w_up[l], rwkv_a0[l][None, :], rwkv_a_up[l], rwkv_g_up[l],
                    rwkv_k_k[l][None, :], rwkv_k_a[l][None, :], rwkv_r_k[l].reshape(1, -1),
                    rwkv_gn_w[l][None, :], rwkv_gn_b[l][None, :])
        kv = _mem_kv(mem.reshape(-1, dm), norm_mem[l][None, :], xattn_w_kv[l].astype(BF16))
        x = _post_mix(x, y.reshape(b, s, -1), w_out[l].astype(BF16), norm_xattn[l][None, :],
                      xattn_w_q[l].astype(BF16), kv.reshape(b, -1, 2 * dm), xattn_w_o[l].astype(BF16))
        x = _mlp(x.reshape(b * s, dm), norm_mlp[l][None, :], mlp_w_up[l].astype(BF16),
                 mlp_w_down[l].astype(BF16), norm_final[None, :], l == n_layers - 1).reshape(b, s, dm)
    return x
```

```python
import functools

import jax
import jax.numpy as jnp
from jax import lax
from jax.experimental import pallas as pl
from jax.experimental.pallas import tpu as pltpu

D_MODEL = 1024
HEAD_DIM = 64
RET_WIDTH = 512
RWKV_WIDTH = 512
N_HEADS = 8
RET_CHUNK = 128
ROPE_BASE = 10000.0
DECAY_LORA = 64
AAA_LORA = 64
GATE_LORA = 160
LORA_WIDTH = DECAY_LORA + AAA_LORA + GATE_LORA
RET_PROJ = 4 * RET_WIDTH
XATTN_HEADS = 4
XATTN_HEAD_DIM = D_MODEL // XATTN_HEADS
D_FF = 4 * D_MODEL
RMS_EPS = 1e-6
GN_EPS_RET = 1e-5
GN_EPS_RWKV = 64e-5

WKV_CHUNK = 64
WKV_BLOCK = 1024
VMEM_LIMIT_BYTES = 62 * 1024 * 1024

BF16 = jnp.bfloat16
F32 = jnp.float32


def _dot(a, b):
    return jnp.dot(a, b, preferred_element_type=F32)


def _dot_nt(a, b):
    return lax.dot_general(a, b, (((1,), (1,)), ((), ())), preferred_element_type=F32)


def _dot_tn(a, b):
    return lax.dot_general(a, b, (((0,), (0,)), ((), ())), preferred_element_type=F32)


def _rms(x, g):
    return x * lax.rsqrt(jnp.mean(x * x, axis=-1, keepdims=True) + RMS_EPS) * g


def _params(n_axes):
    return pltpu.CompilerParams(dimension_semantics=("arbitrary",) * n_axes,
                                vmem_limit_bytes=VMEM_LIMIT_BYTES)


def _full(shape):
    zeros = (0,) * len(shape)
    return pl.BlockSpec(shape, lambda *_: zeros, pipeline_mode=pl.Buffered(1))


def _head_norm(y, eps):
    mu = jnp.mean(y, axis=-1, keepdims=True)
    d = y - mu
    var = jnp.mean(d * d, axis=-1, keepdims=True)
    return d * lax.rsqrt(var + eps)


def _pair_diag(x):
    in_h0 = lax.broadcasted_iota(jnp.int32, x.shape, 1) < HEAD_DIM
    zero = jnp.zeros_like(x)
    return jnp.concatenate([jnp.where(in_h0, x, zero), jnp.where(in_h0, zero, x)], axis=0)


def _retention_body(p, pos, freq_ref, dmask_ref, xi_ref, zeta_ref, gc_ref, same_head_ref, gnw_ref, gnb_ref,
                    state_ref):
    c = RET_CHUNK
    w = RET_WIDTH
    lanes = 2 * HEAD_DIM
    n_groups = w // lanes

    tb = p.shape[0]
    q, k, v, gate = p[:, :w], p[:, w:2 * w], p[:, 2 * w:3 * w], p[:, 3 * w:]

    ang = pos.astype(F32) * freq_ref[...]
    cos = jnp.concatenate([jnp.cos(ang)] * n_groups, axis=1)
    sin = jnp.concatenate([jnp.sin(ang)] * n_groups, axis=1)
    lane = lax.broadcasted_iota(jnp.int32, (tb, w), 1)
    first_half = (lane % HEAD_DIM) < (HEAD_DIM // 2)

    def rope(t):
        rot = jnp.where(first_half, -pltpu.roll(t, w - HEAD_DIM // 2, 1), pltpu.roll(t, HEAD_DIM // 2, 1))
        return t * cos + rot * sin

    q = rope(q)
    k = rope(k) * (HEAD_DIM ** -0.5)
    q_in = (q * xi_ref[...]).astype(BF16)
    k_out = (k * zeta_ref[...]).astype(BF16)
    q = q.astype(BF16)
    k = k.astype(BF16)
    vb = v.astype(BF16)

    units = [(ci, g) for ci in range(tb // c) for g in range(n_groups)]
    tile = lambda x, ci, g: x[ci * c:(ci + 1) * c, g * lanes:(g + 1) * lanes]
    s_u = {(ci, g): (_dot_nt(tile(q, ci, g), _pair_diag(tile(k, ci, g))) * dmask_ref[g]).astype(BF16)
           for ci, g in units}
    y_u = {(ci, g): _dot(s_u[ci, g], _pair_diag(tile(vb, ci, g))) for ci, g in units}
    kv_u = {(ci, g): _dot_tn(tile(k_out, ci, g), tile(vb, ci, g)) for ci, g in units}
    same_head = same_head_ref[...] > 0.5
    r_g = [state_ref[g] for g in range(n_groups)]
    for ci in range(tb // c):
        for g in range(n_groups):
            y_u[ci, g] = y_u[ci, g] + _dot(tile(q_in, ci, g), r_g[g].astype(BF16))
        r_g = [r_g[g] * gc_ref[g] + jnp.where(same_head, kv_u[ci, g], 0.0) for g in range(n_groups)]
    for g in range(n_groups):
        state_ref[g] = r_g[g]
    y_all = jnp.concatenate([jnp.concatenate([y_u[ci, g] for g in range(n_groups)], axis=1)
                             for ci in range(tb // c)], axis=0)
    yn = jnp.concatenate([_head_norm(y_all[:, h * HEAD_DIM:(h + 1) * HEAD_DIM], GN_EPS_RET)
                          for h in range(N_HEADS)], axis=1) * gnw_ref[...] + gnb_ref[...]
    return gate * jax.nn.sigmoid(gate) * yn


def _retention_consts(tb):
    c = RET_CHUNK
    half = HEAD_DIM // 2
    inv_freq = ROPE_BASE ** (-jnp.arange(half, dtype=F32) / half)
    freq = jnp.tile(inv_freq, 128 // half)[None, :]
    log_g = jnp.log(1.0 - 2.0 ** (-5.0 - jnp.arange(N_HEADS, dtype=F32)))
    idx = jnp.arange(c, dtype=F32)
    diff = idx[:, None] - idx[None, :]
    causal = diff >= 0
    dmask = jnp.where(causal[None], jnp.exp(log_g[:, None, None] * jnp.where(causal, diff, 0.0)[None]), 0.0)
    xi = jnp.exp(log_g[:, None] * (idx + 1.0)[None])
    zeta = jnp.exp(log_g[:, None] * (c - 1.0 - idx)[None])
    g_chunk = jnp.exp(log_g * c)
    xi_w = jnp.tile(jnp.repeat(xi.T, HEAD_DIM, axis=1), (tb // c, 1))
    zeta_w = jnp.tile(jnp.repeat(zeta.T, HEAD_DIM, axis=1), (tb // c, 1))
    dmask_pairs = jnp.concatenate([dmask[0::2], dmask[1::2]], axis=2)
    gc_rows = jnp.repeat(g_chunk, HEAD_DIM).reshape(N_HEADS // 2, 2 * HEAD_DIM, 1)
    gc_w = jnp.broadcast_to(gc_rows, (N_HEADS // 2, 2 * HEAD_DIM, 2 * HEAD_DIM))
    return [freq, dmask_pairs, xi_w, zeta_w, gc_w]


def _shift(p, carry_ref, mu):
    rows = lax.broadcasted_iota(jnp.int32, p.shape, 0)
    prev = jnp.where(rows == 0, carry_ref[...], pltpu.roll(p, 1, 0))
    carry_ref[...] = p[p.shape[0] - 1:, :]
    return p + (prev - p) * mu


def _wkv_body(prkv_raw, plora_raw, mu_rkv_ref, mu_lora_ref, w0_ref, wup_ref, a0_ref, aup_ref, gup_ref,
              kk_ref, ka_ref, rk_ref, gnw_ref, gnb_ref, tri_ref, mask_ref, eye_ref, state_mask_ref,
              carry_rkv_ref, carry_lora_ref, state_ref):
    tb = prkv_raw.shape[0]
    c = WKV_CHUNK
    w = RWKV_WIDTH
    d = HEAD_DIM

    prkv = _shift(prkv_raw, carry_rkv_ref, mu_rkv_ref[...])
    plora = _shift(plora_raw, carry_lora_ref, mu_lora_ref[...])
    r, k, v = prkv[:, :w], prkv[:, w:2 * w], prkv[:, 2 * w:]
    w_lr = plora[:, :DECAY_LORA]
    a_lr = plora[:, DECAY_LORA:DECAY_LORA + AAA_LORA]
    g_lr = plora[:, DECAY_LORA + AAA_LORA:]

    z = -(w0_ref[...] + _dot(jnp.tanh(w_lr).astype(BF16), wup_ref[...]))
    w_log = -(jnp.maximum(z, 0.0) + jnp.log1p(jnp.exp(-jnp.abs(z)))) - 0.5
    lw = -jnp.exp(w_log)
    a_sig = jax.nn.sigmoid(a0_ref[...] + _dot(a_lr.astype(BF16), aup_ref[...]))
    gate = _dot(jax.nn.sigmoid(g_lr).astype(BF16), gup_ref[...])
    kk = k * kk_ref[...]
    k2 = k * (1.0 + (a_sig - 1.0) * ka_ref[...])
    rk = r * k2 * rk_ref[...]

    inv_norms, bonus = [], []
    for h in range(N_HEADS):
        sl = slice(h * d, (h + 1) * d)
        kh = kk[:, sl]
        n = jnp.sqrt(jnp.sum(kh * kh, axis=-1, keepdims=True))
        inv_norms.append(jnp.broadcast_to(1.0 / jnp.maximum(n, 1e-12), (tb, d)))
        bonus.append(jnp.broadcast_to(jnp.sum(rk[:, sl], axis=-1, keepdims=True), (tb, d)))
    kk = kk * jnp.concatenate(inv_norms, axis=1)
    bonus = jnp.concatenate(bonus, axis=1) * v
    a_vec = -kk
    b_vec = kk * a_sig

    lw_hi = lw.astype(BF16)
    lw_r1 = lw - lw_hi.astype(F32)
    lw_mid = lw_r1.astype(BF16)
    lw_lo = (lw_r1 - lw_mid.astype(F32)).astype(BF16)
    tri = tri_ref[...]
    cum_all = _dot(tri, lw_hi) + _dot(tri, lw_mid) + _dot(tri, lw_lo)

    n_chunks = tb // c
    at, rt, bt, kt, bc, kc, pc = [], [], [], [], [], [], []
    for ci in range(n_chunks):
        rows = slice(ci * c, (ci + 1) * c)
        lw_c = lw[rows]
        cum = cum_all[rows]
        cum_end = cum[c - 1:, :]
        p_in = jnp.exp(cum)
        p_inv = jnp.exp(-cum)
        p_out = jnp.exp(cum_end - cum)
        at.append((a_vec[rows] * jnp.exp(cum - lw_c)).astype(BF16))
        rt.append((r[rows] * p_in).astype(BF16))
        bt.append((b_vec[rows] * p_inv).astype(BF16))
        kt.append((k2[rows] * p_inv).astype(BF16))
        bc.append((b_vec[rows] * p_out).astype(BF16))
        kc.append((k2[rows] * p_out).astype(BF16))
        pc.append(jnp.exp(cum_end))
    vb = v.astype(BF16)
    eye = eye_ref[...]
    mask = mask_ref[...]

    lanes = 2 * d
    n_groups = w // lanes
    pair_diag = _pair_diag
    units = [(ci, g) for ci in range(n_chunks) for g in range(n_groups)]
    gsl = lambda g: slice(g * lanes, (g + 1) * lanes)
    csl = lambda ci: slice(ci * c, (ci + 1) * c)
    tril = mask > 0.5
    vd_u = {(ci, g): pair_diag(vb[csl(ci), gsl(g)]) for ci, g in units}
    g_u = {}
    for ci, g in units:
        lhs = jnp.concatenate([at[ci][:, gsl(g)], rt[ci][:, gsl(g)]], axis=0)
        rhs = jnp.concatenate([pair_diag(bt[ci][:, gsl(g)]), pair_diag(kt[ci][:, gsl(g)])], axis=0)
        g_u[ci, g] = jnp.where(tril, _dot_nt(lhs, rhs), 0.0)
    m_u = {u: g_u[u][:c, :lanes] for u in units}
    t_u = {u: eye + m_u[u] for u in units}
    mb_u = {u: m_u[u].astype(BF16) for u in units}
    steps = 1
    while 2 * steps < c:
        mb_u = {u: _dot(mb_u[u], pair_diag(mb_u[u])).astype(BF16) for u in units}
        t_u = {u: t_u[u] + _dot(t_u[u].astype(BF16), pair_diag(mb_u[u])) for u in units}
        steps *= 2
    t_u = {u: t_u[u].astype(BF16) for u in units}
    kv_u = {u: _dot(g_u[u][:, lanes:].astype(BF16), vd_u[u]) for u in units}
    wu_u = {(ci, g): _dot(t_u[ci, g], jnp.concatenate([pair_diag(at[ci][:, gsl(g)]),
                                                       pair_diag(kv_u[ci, g][:c].astype(BF16))], axis=1))
            for ci, g in units}
    wr_u = {(ci, g): jnp.concatenate([wu_u[ci, g][:, :lanes].astype(BF16), rt[ci][:, gsl(g)]], axis=0)
            for ci, g in units}

    same_head = state_mask_ref[...] > 0.5
    s_g = [state_ref[g] for g in range(n_groups)]
    y_u = {}
    for ci in range(n_chunks):
        ws = [_dot_nt(wr_u[ci, g], s_g[g].astype(BF16)) for g in range(n_groups)]
        u_b = [(ws[g][:c] + wu_u[ci, g][:, lanes:]).astype(BF16) for g in range(n_groups)]
        for g in range(n_groups):
            y_u[ci, g] = (ws[g][c:] + _dot(g_u[ci, g][c:, :lanes].astype(BF16), pair_diag(u_b[g]))
                          + kv_u[ci, g][c:])
        s_g = [s_g[g] * pc[ci][:, gsl(g)]
               + jnp.where(same_head,
                           _dot_tn(jnp.concatenate([u_b[g], vb[csl(ci), gsl(g)]], axis=0),
                                   jnp.concatenate([bc[ci][:, gsl(g)], kc[ci][:, gsl(g)]], axis=0)), 0.0)
               for g in range(n_groups)]
    for g in range(n_groups):
        state_ref[g] = s_g[g]
    y_all = jnp.concatenate([jnp.concatenate([y_u[ci, g] for g in range(n_groups)], axis=1)
                             for ci in range(n_chunks)], axis=0)
    y_heads = [_head_norm(y_all[:, h * d:(h + 1) * d], GN_EPS_RWKV) for h in range(N_HEADS)]
    yn = jnp.concatenate(y_heads, axis=1) * gnw_ref[...] + gnb_ref[...]
    return (yn + bonus) * gate


N_PROJ_IN = 7
N_RET_CONST = 8
N_WKV_CONST = 16


def _mixers_kernel(*refs, steps_per_seq):
    (x_first_ref, x_mid_ref, x_next_ref, g_ref, w_ret_ref, w_rkv_ref, w_lora_ref), refs = (
        refs[:N_PROJ_IN], refs[N_PROJ_IN:])
    pos_ref, refs = refs[0], refs[1:]
    ret_consts, refs = refs[:N_RET_CONST], refs[N_RET_CONST:]
    wkv_consts, refs = refs[:N_WKV_CONST], refs[N_WKV_CONST:]
    o_ref, refs = refs[0], refs[1:]
    p_a, p_b, refs = refs[:3], refs[3:6], refs[6:]
    ret_state_ref, carry_rkv_ref, carry_lora_ref, wkv_state_ref = refs
    i = pl.program_id(0)
    half = o_ref.shape[0] // 2

    def project(x_ref, dst):
        h = _rms(x_ref[...], g_ref[...]).astype(BF16)
        for w_ref, p_ref in zip((w_ret_ref, w_rkv_ref, w_lora_ref), dst):
            p_ref[...] = _dot(h, w_ref[...])

    def mix(src, rows):
        p_ret_ref, p_rkv_ref, p_lora_ref = src
        y_ret = _retention_body(p_ret_ref[...], pos_ref[rows, :], *ret_consts, ret_state_ref)
        y_rwkv = _wkv_body(p_rkv_ref[...], p_lora_ref[...], *wkv_consts, carry_rkv_ref, carry_lora_ref,
                           wkv_state_ref)
        o_ref[rows, :] = jnp.concatenate([y_ret, y_rwkv], axis=1).astype(o_ref.dtype)

    @pl.when(i == 0)
    def _():
        project(x_first_ref, p_a)

    @pl.when(i % steps_per_seq == 0)
    def _():
        for ref in (ret_state_ref, carry_rkv_ref, carry_lora_ref, wkv_state_ref):
            ref[...] = jnp.zeros_like(ref)

    mix(p_a, slice(0, half))
    project(x_mid_ref, p_b)
    mix(p_b, slice(half, 2 * half))
    project(x_next_ref, p_a)


def _mixers(x2d, norm_g, w_ret, w_rkv, w_lora, positions, ret_gn_w, ret_gn_b, mu, w0, w_up, a0, a_up, g_up,
            k_k, k_a, r_k, gn_w, gn_b):
    b, s = positions.shape
    tb, c = WKV_BLOCK, WKV_CHUNK
    assert tb % RET_CHUNK == 0 and s % tb == 0
    n_steps = b * s // tb
    mu_rkv, mu_lora = mu[:, :3 * RWKV_WIDTH], mu[:, 3 * RWKV_WIDTH:]
    idx = jnp.arange(c)
    incl = (idx[:, None] >= idx[None, :]).astype(F32)
    strict = (idx[:, None] > idx[None, :]).astype(F32)
    mask = jnp.concatenate([jnp.tile(strict, (1, 4)), jnp.tile(incl, (1, 4))], axis=0)
    eye = jnp.tile(jnp.eye(c, dtype=F32), (1, 2))
    head_of_lane = jnp.arange(2 * HEAD_DIM) // HEAD_DIM
    state_mask = (head_of_lane[:, None] == head_of_lane[None, :]).astype(F32)
    half = tb // 2
    t_idx = jnp.arange(half)
    tri = ((t_idx[:, None] >= t_idx[None, :]) & (t_idx[:, None] // c == t_idx[None, :] // c)).astype(BF16)
    ret_consts = _retention_consts(half) + [state_mask, ret_gn_w, ret_gn_b]
    wkv_consts = [mu_rkv, mu_lora, w0, w_up.astype(BF16), a0, a_up.astype(BF16), g_up.astype(BF16),
                  k_k, k_a, r_k, gn_w, gn_b, tri, mask, eye, state_mask]
    assert len(ret_consts) == N_RET_CONST and len(wkv_consts) == N_WKV_CONST
    n_half = 2 * n_steps
    proj = [norm_g, w_ret, w_rkv, w_lora]
    p_set = [pltpu.VMEM((half, RET_PROJ), F32), pltpu.VMEM((half, 3 * RWKV_WIDTH), F32),
             pltpu.VMEM((half, LORA_WIDTH), F32)]
    return pl.pallas_call(
        functools.partial(_mixers_kernel, steps_per_seq=s // tb),
        grid=(n_steps,),
        in_specs=[pl.BlockSpec((half, D_MODEL), lambda i: (0, 0), pipeline_mode=pl.Buffered(1)),
                  pl.BlockSpec((half, D_MODEL), lambda i: (2 * i + 1, 0)),
                  pl.BlockSpec((half, D_MODEL), lambda i: (jnp.minimum(2 * i + 2, n_half - 1), 0))]
                 + [_full(a.shape) for a in proj]
                 + [pl.BlockSpec((tb, 1), lambda i: (i, 0))] + [_full(a.shape) for a in ret_consts]
                 + [_full(a.shape) for a in wkv_consts],
        out_specs=pl.BlockSpec((tb, RET_WIDTH + RWKV_WIDTH), lambda i: (i, 0)),
        out_shape=jax.ShapeDtypeStruct((b * s, RET_WIDTH + RWKV_WIDTH), BF16),
        scratch_shapes=p_set + p_set + [
            pltpu.VMEM((N_HEADS // 2, 2 * HEAD_DIM, 2 * HEAD_DIM), F32),
            pltpu.VMEM((1, 3 * RWKV_WIDTH), F32), pltpu.VMEM((1, LORA_WIDTH), F32),
            pltpu.VMEM((N_HEADS // 2, 2 * HEAD_DIM, 2 * HEAD_DIM), F32)],
        compiler_params=_params(1),
        name="mixers",
    )(x2d, x2d, x2d, *proj, positions.reshape(b * s, 1), *ret_consts, *wkv_consts)


def _mem_kv_kernel(m_ref, g_ref, w_ref, o_ref):
    h = _rms(m_ref[...], g_ref[...]).astype(BF16)
    o_ref[...] = _dot(h, w_ref[...]).astype(o_ref.dtype)


def _mem_kv(mem2d, g, w_kv, tm=1024):
    t = mem2d.shape[0]
    n = w_kv.shape[1]
    tm = min(tm, t)
    assert t % tm == 0
    return pl.pallas_call(
        _mem_kv_kernel,
        grid=(t // tm,),
        in_specs=[pl.BlockSpec((tm, D_MODEL), lambda i: (i, 0)), _full(g.shape), _full(w_kv.shape)],
        out_specs=pl.BlockSpec((tm, n), lambda i: (i, 0)),
        out_shape=jax.ShapeDtypeStruct((t, n), BF16),
        compiler_params=_params(1),
        name="mem_kv",
    )(mem2d, g, w_kv)


def _post_mix_kernel(x_ref, y_ref, wout_ref, gx_ref, wq_ref, kv_ref, wo_ref, o_ref):
    x1 = x_ref[0] + _dot(y_ref[0], wout_ref[...])
    q = _dot(_rms(x1, gx_ref[...]).astype(BF16), wq_ref[...]).astype(BF16)
    kv = kv_ref[0]
    heads = []
    for h in range(XATTN_HEADS):
        sl = slice(h * XATTN_HEAD_DIM, (h + 1) * XATTN_HEAD_DIM)
        s = _dot_nt(q[:, sl], kv[:, sl]) * (XATTN_HEAD_DIM ** -0.5)
        e = jnp.exp(s - jnp.max(s, axis=-1, keepdims=True))
        prob = e / jnp.sum(e, axis=-1, keepdims=True)
        heads.append(_dot(prob.astype(BF16), kv[:, D_MODEL + h * XATTN_HEAD_DIM:D_MODEL + (h + 1) * XATTN_HEAD_DIM]))
    o = jnp.concatenate(heads, axis=1).astype(BF16)
    o_ref[0] = x1 + _dot(o, wo_ref[...])


def _post_mix(x, y, w_out, g_x, w_q, kv, w_o, tm=1024):
    b, s, _ = x.shape
    m = kv.shape[1]
    assert s % tm == 0
    blk = lambda bi, ti: (bi, ti, 0)
    return pl.pallas_call(
        _post_mix_kernel,
        grid=(b, s // tm),
        in_specs=[pl.BlockSpec((1, tm, D_MODEL), blk), pl.BlockSpec((1, tm, y.shape[2]), blk),
                  _full(w_out.shape), _full(g_x.shape), _full(w_q.shape), pl.BlockSpec((1, m, 2 * D_MODEL), lambda bi, ti: (bi, 0, 0)),
                  _full(w_o.shape)],
        out_specs=pl.BlockSpec((1, tm, D_MODEL), blk),
        out_shape=jax.ShapeDtypeStruct((b, s, D_MODEL), F32),
        compiler_params=_params(2),
        name="post_mix",
    )(x, y, w_out, g_x, w_q, kv, w_o)


def _mlp_kernel(x_ref, g_ref, wup_ref, wdown_ref, gf_ref, o_ref, *, tf, final_norm):
    x = x_ref[...]
    h = _rms(x, g_ref[...]).astype(BF16)
    acc = x
    for j in range(D_FF // tf):
        u = jnp.maximum(_dot(h, wup_ref[:, j * tf:(j + 1) * tf]), 0.0)
        acc = acc + _dot((u * u).astype(BF16), wdown_ref[j * tf:(j + 1) * tf, :])
    o_ref[...] = _rms(acc, gf_ref[...]) if final_norm else acc


def _mlp(x2d, g, w_up, w_down, g_final, final_norm, tm=1024, tf=512):
    t = x2d.shape[0]
    assert t % tm == 0 and D_FF % tf == 0
    row = lambda i: (i, 0)
    return pl.pallas_call(
        functools.partial(_mlp_kernel, tf=tf, final_norm=final_norm),
        grid=(t // tm,),
        in_specs=[pl.BlockSpec((tm, D_MODEL), row), _full(g.shape), _full(w_up.shape), _full(w_down.shape),
                  _full(g_final.shape)],
        out_specs=pl.BlockSpec((tm, D_MODEL), row),
        out_shape=jax.ShapeDtypeStruct((t, D_MODEL), F32),
        compiler_params=_params(1),
        name="mlp",
    )(x2d, g, w_up, w_down, g_final)


def kernel(x, mem, positions, norm_mix, w_in, ret_gn_w, ret_gn_b, rwkv_mu, rwkv_w0, rwkv_w_up, rwkv_a0,
           rwkv_a_up, rwkv_g_up, rwkv_k_k, rwkv_k_a, rwkv_r_k, rwkv_gn_w, rwkv_gn_b, w_out, norm_xattn,
           norm_mem, xattn_w_q, xattn_w_kv, xattn_w_o, norm_mlp, mlp_w_up, mlp_w_down, norm_final):
    b, s, dm = x.shape
    n_layers = w_in.shape[0]
    rkv_end = RET_PROJ + 3 * RWKV_WIDTH
    for l in range(n_layers):
        y = _mixers(x.reshape(b * s, dm), norm_mix[l][None, :], w_in[l, :, :RET_PROJ].astype(BF16),
                    w_in[l, :, RET_PROJ:rkv_end].astype(BF16), w_in[l, :, rkv_end:].astype(BF16), positions,
                    ret_gn_w[l][None, :], ret_gn_b[l][None, :], rwkv_mu[l][None, :],
                    rwkv_w0[l][None, :], rwkv_w_up[l], rwkv_a0[l][None, :], rwkv_a_up[l], rwkv_g_up[l],
                    rwkv_k_k[l][None, :], rwkv_k_a[l][None, :], rwkv_r_k[l].reshape(1, -1),
                    rwkv_gn_w[l][None, :], rwkv_gn_b[l][None, :])
        kv = _mem_kv(mem.reshape(-1, dm), norm_mem[l][None, :], xattn_w_kv[l].astype(BF16))
        x = _post_mix(x, y.reshape(b, s, -1), w_out[l].astype(BF16), norm_xattn[l][None, :],
                      xattn_w_q[l].astype(BF16), kv.reshape(b, -1, 2 * dm), xattn_w_o[l].astype(BF16))
        x = _mlp(x.reshape(b * s, dm), norm_mlp[l][None, :], mlp_w_up[l].astype(BF16),
                 mlp_w_down[l].astype(BF16), norm_final[None, :], l == n_layers - 1).reshape(b, s, dm)
    return x
```

```python
import functools

import jax
import jax.numpy as jnp
from jax import lax
from jax.experimental import pallas as pl
from jax.experimental.pallas import tpu as pltpu

D_MODEL = 1024
HEAD_DIM = 64
RET_WIDTH = 512
RWKV_WIDTH = 512
N_HEADS = 8
RET_CHUNK = 128
ROPE_BASE = 10000.0
DECAY_LORA = 64
AAA_LORA = 64
GATE_LORA = 160
LORA_WIDTH = DECAY_LORA + AAA_LORA + GATE_LORA
RET_PROJ = 4 * RET_WIDTH
XATTN_HEADS = 4
XATTN_HEAD_DIM = D_MODEL // XATTN_HEADS
D_FF = 4 * D_MODEL
RMS_EPS = 1e-6
GN_EPS_RET = 1e-5
GN_EPS_RWKV = 64e-5

WKV_CHUNK = 64
WKV_BLOCK = 512
CUM_ROWS = 256
VMEM_LIMIT_BYTES =56 * 1024 * 1024

BF16 = jnp.bfloat16
F32 = jnp.float32


def _dot(a, b):
    return jnp.dot(a, b, preferred_element_type=F32)


def _dot_nt(a, b):
    return lax.dot_general(a, b, (((1,), (1,)), ((), ())), preferred_element_type=F32)


def _dot_tn(a, b):
    return lax.dot_general(a, b, (((0,), (0,)), ((), ())), preferred_element_type=F32)


def _rms(x, g):
    return x * lax.rsqrt(jnp.mean(x * x, axis=-1, keepdims=True) + RMS_EPS) * g


def _params(n_axes):
    return pltpu.CompilerParams(dimension_semantics=("arbitrary",) * n_axes,
                                vmem_limit_bytes=VMEM_LIMIT_BYTES)


def _full(shape):
    zeros = (0,) * len(shape)
    return pl.BlockSpec(shape, lambda *_: zeros, pipeline_mode=pl.Buffered(1))


def _head_norm(y, eps):
    mu = jnp.mean(y, axis=-1, keepdims=True)
    d = y - mu
    var = jnp.mean(d * d, axis=-1, keepdims=True)
    return d * lax.rsqrt(var + eps)


def _pair_diag(x):
    in_h0 = lax.broadcasted_iota(jnp.int32, x.shape, 1) < HEAD_DIM
    zero = jnp.zeros_like(x)
    return jnp.concatenate([jnp.where(in_h0, x, zero), jnp.where(in_h0, zero, x)], axis=0)


def _retention_body(p, pos, freq_ref, dmask_ref, xi_ref, zeta_ref, gc_ref, same_head_ref, gnw_ref, gnb_ref,
                    state_ref):
    c = RET_CHUNK
    w = RET_WIDTH
    lanes = 2 * HEAD_DIM
    n_groups = w // lanes

    tb = p.shape[0]
    q, k, v, gate = p[:, :w], p[:, w:2 * w], p[:, 2 * w:3 * w], p[:, 3 * w:]

    ang = pos.astype(F32) * freq_ref[...]
    cos = jnp.concatenate([jnp.cos(ang)] * n_groups, axis=1)
    sin = jnp.concatenate([jnp.sin(ang)] * n_groups, axis=1)
    lane = lax.broadcasted_iota(jnp.int32, (tb, w), 1)
    first_half = (lane % HEAD_DIM) < (HEAD_DIM // 2)

    def rope(t):
        rot = jnp.where(first_half, -pltpu.roll(t, w - HEAD_DIM // 2, 1), pltpu.roll(t, HEAD_DIM // 2, 1))
        return t * cos + rot * sin

    q = rope(q)
    k = rope(k) * (HEAD_DIM ** -0.5)
    q_in = (q * xi_ref[...]).astype(BF16)
    k_out = (k * zeta_ref[...]).astype(BF16)
    q = q.astype(BF16)
    k = k.astype(BF16)
    vb = v.astype(BF16)

    units = [(ci, g) for ci in range(tb // c) for g in range(n_groups)]
    tile = lambda x, ci, g: x[ci * c:(ci + 1) * c, g * lanes:(g + 1) * lanes]
    s_u = {(ci, g): (_dot_nt(tile(q, ci, g), _pair_diag(tile(k, ci, g))) * dmask_ref[g]).astype(BF16)
           for ci, g in units}
    y_u = {(ci, g): _dot(s_u[ci, g], _pair_diag(tile(vb, ci, g))) for ci, g in units}
    kv_u = {(ci, g): _dot_tn(tile(k_out, ci, g), tile(vb, ci, g)) for ci, g in units}
    same_head = same_head_ref[...] > 0.5
    r_g = [state_ref[g] for g in range(n_groups)]
    for ci in range(tb // c):
        for g in range(n_groups):
            y_u[ci, g] = y_u[ci, g] + _dot(tile(q_in, ci, g), r_g[g].astype(BF16))
        r_g = [r_g[g] * gc_ref[g] + jnp.where(same_head, kv_u[ci, g], 0.0) for g in range(n_groups)]
    for g in range(n_groups):
        state_ref[g] = r_g[g]
    y_all = jnp.concatenate([jnp.concatenate([y_u[ci, g] for g in range(n_groups)], axis=1)
                             for ci in range(tb // c)], axis=0)
    yn = jnp.concatenate([_head_norm(y_all[:, h * HEAD_DIM:(h + 1) * HEAD_DIM], GN_EPS_RET)
                          for h in range(N_HEADS)], axis=1) * gnw_ref[...] + gnb_ref[...]
    return gate * jax.nn.sigmoid(gate) * yn


def _retention_consts(tb):
    c = RET_CHUNK
    half = HEAD_DIM // 2
    inv_freq = ROPE_BASE ** (-jnp.arange(half, dtype=F32) / half)
    freq = jnp.tile(inv_freq, 128 // half)[None, :]
    log_g = jnp.log(1.0 - 2.0 ** (-5.0 - jnp.arange(N_HEADS, dtype=F32)))
    idx = jnp.arange(c, dtype=F32)
    diff = idx[:, None] - idx[None, :]
    causal = diff >= 0
    dmask = jnp.where(causal[None], jnp.exp(log_g[:, None, None] * jnp.where(causal, diff, 0.0)[None]), 0.0)
    xi = jnp.exp(log_g[:, None] * (idx + 1.0)[None])
    zeta = jnp.exp(log_g[:, None] * (c - 1.0 - idx)[None])
    g_chunk = jnp.exp(log_g * c)
    xi_w = jnp.tile(jnp.repeat(xi.T, HEAD_DIM, axis=1), (tb // c, 1))
    zeta_w = jnp.tile(jnp.repeat(zeta.T, HEAD_DIM, axis=1), (tb // c, 1))
    dmask_pairs = jnp.concatenate([dmask[0::2], dmask[1::2]], axis=2)
    gc_rows = jnp.repeat(g_chunk, HEAD_DIM).reshape(N_HEADS // 2, 2 * HEAD_DIM, 1)
    gc_w = jnp.broadcast_to(gc_rows, (N_HEADS // 2, 2 * HEAD_DIM, 2 * HEAD_DIM))
    return [freq, dmask_pairs, xi_w, zeta_w, gc_w]


def _shift(p, carry_ref, mu):
    rows = lax.broadcasted_iota(jnp.int32, p.shape, 0)
    prev = jnp.where(rows == 0, carry_ref[...], pltpu.roll(p, 1, 0))
    carry_ref[...] = p[p.shape[0] - 1:, :]
    return p + (prev - p) * mu


def _wkv_body(prkv_raw, plora_raw, mu_rkv_ref, mu_lora_ref, w0_ref, wup_ref, a0_ref, aup_ref, gup_ref,
              kk_ref, ka_ref, rk_ref, gnw_ref, gnb_ref, tri_ref, mask_ref, eye_ref, state_mask_ref,
              carry_rkv_ref, carry_lora_ref, state_ref):
    tb = prkv_raw.shape[0]
    c = WKV_CHUNK
    w = RWKV_WIDTH
    d = HEAD_DIM

    prkv = _shift(prkv_raw, carry_rkv_ref, mu_rkv_ref[...])
    plora = _shift(plora_raw, carry_lora_ref, mu_lora_ref[...])
    r, k, v = prkv[:, :w], prkv[:, w:2 * w], prkv[:, 2 * w:]
    w_lr = plora[:, :DECAY_LORA]
    a_lr = plora[:, DECAY_LORA:DECAY_LORA + AAA_LORA]
    g_lr = plora[:, DECAY_LORA + AAA_LORA:]

    z = -(w0_ref[...] + _dot(jnp.tanh(w_lr).astype(BF16), wup_ref[...]))
    w_log = -(jnp.maximum(z, 0.0) + jnp.log1p(jnp.exp(-jnp.abs(z)))) - 0.5
    lw = -jnp.exp(w_log)
    a_sig = jax.nn.sigmoid(a0_ref[...] + _dot(a_lr.astype(BF16), aup_ref[...]))
    gate = _dot(jax.nn.sigmoid(g_lr).astype(BF16), gup_ref[...])
    kk = k * kk_ref[...]
    k2 = k * (1.0 + (a_sig - 1.0) * ka_ref[...])
    rk = r * k2 * rk_ref[...]

    inv_norms, bonus = [], []
    for h in range(N_HEADS):
        sl = slice(h * d, (h + 1) * d)
        kh = kk[:, sl]
        n = jnp.sqrt(jnp.sum(kh * kh, axis=-1, keepdims=True))
        inv_norms.append(jnp.broadcast_to(1.0 / jnp.maximum(n, 1e-12), (tb, d)))
        bonus.append(jnp.broadcast_to(jnp.sum(rk[:, sl], axis=-1, keepdims=True), (tb, d)))
    kk = kk * jnp.concatenate(inv_norms, axis=1)
    bonus = jnp.concatenate(bonus, axis=1) * v
    a_vec = -kk
    b_vec = kk * a_sig

    lw_hi = lw.astype(BF16)
    lw_r1 = lw - lw_hi.astype(F32)
    lw_mid = lw_r1.astype(BF16)
    lw_lo = (lw_r1 - lw_mid.astype(F32)).astype(BF16)
    tri = tri_ref[...]
    cum_all = jnp.concatenate(
        [_dot(tri, lw_hi[lo:lo + CUM_ROWS]) + _dot(tri, lw_mid[lo:lo + CUM_ROWS])
         + _dot(tri, lw_lo[lo:lo + CUM_ROWS]) for lo in range(0, tb, CUM_ROWS)], axis=0)

    n_chunks = tb // c
    at, rt, bt, kt, bc, kc, pc = [], [], [], [], [], [], []
    for ci in range(n_chunks):
        rows = slice(ci * c, (ci + 1) * c)
        lw_c = lw[rows]
        cum = cum_all[rows]
        cum_end = cum[c - 1:, :]
        p_in = jnp.exp(cum)
        p_inv = jnp.exp(-cum)
        p_out = jnp.exp(cum_end - cum)
        at.append((a_vec[rows] * jnp.exp(cum - lw_c)).astype(BF16))
        rt.append((r[rows] * p_in).astype(BF16))
        bt.append((b_vec[rows] * p_inv).astype(BF16))
        kt.append((k2[rows] * p_inv).astype(BF16))
        bc.append((b_vec[rows] * p_out).astype(BF16))
        kc.append((k2[rows] * p_out).astype(BF16))
        pc.append(jnp.exp(cum_end))
    vb = v.astype(BF16)
    eye = eye_ref[...]
    mask = mask_ref[...]

    lanes = 2 * d
    n_groups = w // lanes
    pair_diag = _pair_diag
    units = [(ci, g) for ci in range(n_chunks) for g in range(n_groups)]
    gsl = lambda g: slice(g * lanes, (g + 1) * lanes)
    csl = lambda ci: slice(ci * c, (ci + 1) * c)
    tril = mask > 0.5
    vd_u = {(ci, g): pair_diag(vb[csl(ci), gsl(g)]) for ci, g in units}
    g_u = {}
    for ci, g in units:
        lhs = jnp.concatenate([at[ci][:, gsl(g)], rt[ci][:, gsl(g)]], axis=0)
        rhs = jnp.concatenate([pair_diag(bt[ci][:, gsl(g)]), pair_diag(kt[ci][:, gsl(g)])], axis=0)
        g_u[ci, g] = jnp.where(tril, _dot_nt(lhs, rhs), 0.0)
    m_u = {u: g_u[u][:c, :lanes] for u in units}
    t_u = {u: eye + m_u[u] for u in units}
    mb_u = {u: m_u[u].astype(BF16) for u in units}
    steps = 1
    while 2 * steps < c:
        mb_u = {u: _dot(mb_u[u], pair_diag(mb_u[u])).astype(BF16) for u in units}
        t_u = {u: t_u[u] + _dot(t_u[u].astype(BF16), pair_diag(mb_u[u])) for u in units}
        steps *= 2
    t_u = {u: t_u[u].astype(BF16) for u in units}
    w_u = {(ci, g): _dot(t_u[ci, g], pair_diag(at[ci][:, gsl(g)])).astype(BF16) for ci, g in units}
    akv_u = {u: _dot(g_u[u][:c, lanes:].astype(BF16), vd_u[u]).astype(BF16) for u in units}
    u0_u = {u: _dot(t_u[u], pair_diag(akv_u[u])) for u in units}
    y0_u = {u: _dot(g_u[u][c:, lanes:].astype(BF16), vd_u[u]) for u in units}

    same_head = state_mask_ref[...] > 0.5
    s_g = [state_ref[g] for g in range(n_groups)]
    y_u = {}
    for ci in range(n_chunks):
        s_b = [s.astype(BF16) for s in s_g]
        u_b = [(_dot_nt(w_u[ci, g], s_b[g]) + u0_u[ci, g]).astype(BF16) for g in range(n_groups)]
        for g in range(n_groups):
            y_u[ci, g] = (_dot_nt(rt[ci][:, gsl(g)], s_b[g])
                          + _dot(g_u[ci, g][c:, :lanes].astype(BF16), pair_diag(u_b[g])) + y0_u[ci, g])
        s_g = [s_g[g] * pc[ci][:, gsl(g)]
               + jnp.where(same_head,
                           _dot_tn(jnp.concatenate([u_b[g], vb[csl(ci), gsl(g)]], axis=0),
                                   jnp.concatenate([bc[ci][:, gsl(g)], kc[ci][:, gsl(g)]], axis=0)), 0.0)
               for g in range(n_groups)]
    for g in range(n_groups):
        state_ref[g] = s_g[g]
    y_all = jnp.concatenate([jnp.concatenate([y_u[ci, g] for g in range(n_groups)], axis=1)
                             for ci in range(n_chunks)], axis=0)
    y_heads = [_head_norm(y_all[:, h * d:(h + 1) * d], GN_EPS_RWKV) for h in range(N_HEADS)]
    yn = jnp.concatenate(y_heads, axis=1) * gnw_ref[...] + gnb_ref[...]
    return (yn + bonus) * gate


N_RET_CONST = 8
N_WKV_CONST = 16


def _in_proj_kernel(x_ref, g_ref, w_ret_ref, w_rkv_ref, w_lora_ref, p_ret_ref, p_rkv_ref, p_lora_ref):
    h = _rms(x_ref[...], g_ref[...]).astype(BF16)
    p_ret_ref[...] = _dot(h, w_ret_ref[...])
    p_rkv_ref[...] = _dot(h, w_rkv_ref[...])
    p_lora_ref[...] = _dot(h, w_lora_ref[...])


def _in_proj(x2d, g, w_ret, w_rkv, w_lora, tm=1024):
    t = x2d.shape[0]
    assert t % tm == 0
    row = lambda i: (i, 0)
    widths = [w_ret.shape[1], w_rkv.shape[1], w_lora.shape[1]]
    return pl.pallas_call(
        _in_proj_kernel,
        grid=(t // tm,),
        in_specs=[pl.BlockSpec((tm, D_MODEL), row), _full(g.shape), _full(w_ret.shape),
                  _full(w_rkv.shape), _full(w_lora.shape)],
        out_specs=[pl.BlockSpec((tm, n), row) for n in widths],
        out_shape=[jax.ShapeDtypeStruct((t, n), F32) for n in widths],
        compiler_params=_params(1),
        name="in_proj",
    )(x2d, g, w_ret, w_rkv, w_lora)


def _mixers_kernel(*refs, steps_per_seq):
    (p_ret_ref, p_rkv_ref, p_lora_ref, pos_ref), refs = refs[:4], refs[4:]
    ret_consts, refs = refs[:N_RET_CONST], refs[N_RET_CONST:]
    wkv_consts, refs = refs[:N_WKV_CONST], refs[N_WKV_CONST:]
    o_ref, ret_state_ref, carry_rkv_ref, carry_lora_ref, wkv_state_ref = refs

    @pl.when(pl.program_id(0) % steps_per_seq == 0)
    def _():
        for ref in (ret_state_ref, carry_rkv_ref, carry_lora_ref, wkv_state_ref):
            ref[...] = jnp.zeros_like(ref)

    y_ret = _retention_body(p_ret_ref[...], pos_ref[...], *ret_consts, ret_state_ref)
    y_rwkv = _wkv_body(p_rkv_ref[...], p_lora_ref[...], *wkv_consts, carry_rkv_ref, carry_lora_ref,
                       wkv_state_ref)
    o_ref[...] = jnp.concatenate([y_ret, y_rwkv], axis=1).astype(o_ref.dtype)


def _mixers(p_ret, p_rkv, p_lora, positions, ret_gn_w, ret_gn_b, mu, w0, w_up, a0, a_up, g_up,
            k_k, k_a, r_k, gn_w, gn_b):
    b, s = positions.shape
    tb, c = WKV_BLOCK, WKV_CHUNK
    assert tb % RET_CHUNK == 0 and s % tb == 0
    mu_rkv, mu_lora = mu[:, :3 * RWKV_WIDTH], mu[:, 3 * RWKV_WIDTH:]
    idx = jnp.arange(c)
    incl = (idx[:, None] >= idx[None, :]).astype(F32)
    strict = (idx[:, None] > idx[None, :]).astype(F32)
    mask = jnp.concatenate([jnp.tile(strict, (1, 4)), jnp.tile(incl, (1, 4))], axis=0)
    eye = jnp.tile(jnp.eye(c, dtype=F32), (1, 2))
    head_of_lane = jnp.arange(2 * HEAD_DIM) // HEAD_DIM
    state_mask = (head_of_lane[:, None] == head_of_lane[None, :]).astype(F32)
    assert tb % CUM_ROWS == 0 and CUM_ROWS % c == 0
    t_idx = jnp.arange(CUM_ROWS)
    tri = ((t_idx[:, None] >= t_idx[None, :]) & (t_idx[:, None] // c == t_idx[None, :] // c)).astype(BF16)
    ret_consts = _retention_consts(tb) + [state_mask, ret_gn_w, ret_gn_b]
    wkv_consts = [mu_rkv, mu_lora, w0, w_up.astype(BF16), a0, a_up.astype(BF16), g_up.astype(BF16),
                  k_k, k_a, r_k, gn_w, gn_b, tri, mask, eye, state_mask]
    assert len(ret_consts) == N_RET_CONST and len(wkv_consts) == N_WKV_CONST
    row = lambda i: (i, 0)
    return pl.pallas_call(
        functools.partial(_mixers_kernel, steps_per_seq=s // tb),
        grid=(b * s // tb,),
        in_specs=[pl.BlockSpec((tb, p.shape[1]), row) for p in (p_ret, p_rkv, p_lora)]
                 + [pl.BlockSpec((tb, 1), row)] + [_full(a.shape) for a in ret_consts]
                 + [_full(a.shape) for a in wkv_consts],
        out_specs=pl.BlockSpec((tb, RET_WIDTH + RWKV_WIDTH), row),
        out_shape=jax.ShapeDtypeStruct((b * s, RET_WIDTH + RWKV_WIDTH), BF16),
        scratch_shapes=[pltpu.VMEM((N_HEADS // 2, 2 * HEAD_DIM, 2 * HEAD_DIM), F32),
                        pltpu.VMEM((1, 3 * RWKV_WIDTH), F32), pltpu.VMEM((1, LORA_WIDTH), F32),
                        pltpu.VMEM((N_HEADS // 2, 2 * HEAD_DIM, 2 * HEAD_DIM), F32)],
        compiler_params=_params(1),
        name="mixers",
    )(p_ret, p_rkv, p_lora, positions.reshape(b * s, 1), *ret_consts, *wkv_consts)


def _mem_kv_kernel(m_ref, g_ref, w_ref, o_ref):
    h = _rms(m_ref[...], g_ref[...]).astype(BF16)
    o_ref[...] = _dot(h, w_ref[...]).astype(o_ref.dtype)


def _mem_kv(mem2d, g, w_kv, tm=1024):
    t = mem2d.shape[0]
    n = w_kv.shape[1]
    tm = min(tm, t)
    assert t % tm == 0
    return pl.pallas_call(
        _mem_kv_kernel,
        grid=(t // tm,),
        in_specs=[pl.BlockSpec((tm, D_MODEL), lambda i: (i, 0)), _full(g.shape), _full(w_kv.shape)],
        out_specs=pl.BlockSpec((tm, n), lambda i: (i, 0)),
        out_shape=jax.ShapeDtypeStruct((t, n), BF16),
        compiler_params=_params(1),
        name="mem_kv",
    )(mem2d, g, w_kv)


def _post_mix_kernel(x_ref, y_ref, wout_ref, gx_ref, wq_ref, kv_ref, wo_ref, o_ref):
    x1 = x_ref[0] + _dot(y_ref[0], wout_ref[...])
    q = _dot(_rms(x1, gx_ref[...]).astype(BF16), wq_ref[...]).astype(BF16)
    kv = kv_ref[0]
    heads = []
    for h in range(XATTN_HEADS):
        sl = slice(h * XATTN_HEAD_DIM, (h + 1) * XATTN_HEAD_DIM)
        s = _dot_nt(q[:, sl], kv[:, sl]) * (XATTN_HEAD_DIM ** -0.5)
        e = jnp.exp(s - jnp.max(s, axis=-1, keepdims=True))
        prob = e / jnp.sum(e, axis=-1, keepdims=True)
        heads.append(_dot(prob.astype(BF16), kv[:, D_MODEL + h * XATTN_HEAD_DIM:D_MODEL + (h + 1) * XATTN_HEAD_DIM]))
    o = jnp.concatenate(heads, axis=1).astype(BF16)
    o_ref[0] = x1 + _dot(o, wo_ref[...])


def _post_mix(x, y, w_out, g_x, w_q, kv, w_o, tm=1024):
    b, s, _ = x.shape
    m = kv.shape[1]
    assert s % tm == 0
    blk = lambda bi, ti: (bi, ti, 0)
    return pl.pallas_call(
        _post_mix_kernel,
        grid=(b, s // tm),
        in_specs=[pl.BlockSpec((1, tm, D_MODEL), blk), pl.BlockSpec((1, tm, y.shape[2]), blk),
                  _full(w_out.shape), _full(g_x.shape), _full(w_q.shape), pl.BlockSpec((1, m, 2 * D_MODEL), lambda bi, ti: (bi, 0, 0)),
                  _full(w_o.shape)],
        out_specs=pl.BlockSpec((1, tm, D_MODEL), blk),
        out_shape=jax.ShapeDtypeStruct((b, s, D_MODEL), F32),
        compiler_params=_params(2),
        name="post_mix",
    )(x, y, w_out, g_x, w_q, kv, w_o)


def _mlp_kernel(x_ref, g_ref, wup_ref, wdown_ref, gf_ref, o_ref, *, tf, final_norm):
    x = x_ref[...]
    h = _rms(x, g_ref[...]).astype(BF16)
    acc = x
    for j in range(D_FF // tf):
        u = jnp.maximum(_dot(h, wup_ref[:, j * tf:(j + 1) * tf]), 0.0)
        acc = acc + _dot((u * u).astype(BF16), wdown_ref[j * tf:(j + 1) * tf, :])
    o_ref[...] = _rms(acc, gf_ref[...]) if final_norm else acc


def _mlp(x2d, g, w_up, w_down, g_final, final_norm, tm=1024, tf=512):
    t = x2d.shape[0]
    assert t % tm == 0 and D_FF % tf == 0
    row = lambda i: (i, 0)
    return pl.pallas_call(
        functools.partial(_mlp_kernel, tf=tf, final_norm=final_norm),
        grid=(t // tm,),
        in_specs=[pl.BlockSpec((tm, D_MODEL), row), _full(g.shape), _full(w_up.shape), _full(w_down.shape),
                  _full(g_final.shape)],
        out_specs=pl.BlockSpec((tm, D_MODEL), row),
        out_shape=jax.ShapeDtypeStruct((t, D_MODEL), F32),
        compiler_params=_params(1),
        name="mlp",
    )(x2d, g, w_up, w_down, g_final)


def kernel(x, mem, positions, norm_mix, w_in, ret_gn_w, ret_gn_b, rwkv_mu, rwkv_w0, rwkv_w_up, rwkv_a0,
           rwkv_a_up, rwkv_g_up, rwkv_k_k, rwkv_k_a, rwkv_r_k, rwkv_gn_w, rwkv_gn_b, w_out, norm_xattn,
           norm_mem, xattn_w_q, xattn_w_kv, xattn_w_o, norm_mlp, mlp_w_up, mlp_w_down, norm_final):
    b, s, dm = x.shape
    n_layers = w_in.shape[0]
    rkv_end = RET_PROJ + 3 * RWKV_WIDTH
    for l in range(n_layers):
        p_ret, p_rkv, p_lora = _in_proj(x.reshape(b * s, dm), norm_mix[l][None, :],
                                        w_in[l, :, :RET_PROJ].astype(BF16),
                                        w_in[l, :, RET_PROJ:rkv_end].astype(BF16),
                                        w_in[l, :, rkv_end:].astype(BF16))
        y = _mixers(p_ret, p_rkv, p_lora, positions,
                    ret_gn_w[l][None, :], ret_gn_b[l][None, :], rwkv_mu[l][None, :],
                    rwkv_w0[l][None, :], rwkv_w_up[l], rwkv_a0[l][None, :], rwkv_a_up[l], rwkv_g_up[l],
                    rwkv_k_k[l][None, :], rwkv_k_a[l][None, :], rwkv_r_k[l].reshape(1, -1),
                    rwkv_gn_w[l][None, :], rwkv_gn_b[l][None, :])
        kv = _mem_kv(mem.reshape(-1, dm), norm_mem[l][None, :], xattn_w_kv[l].astype(BF16))
        x = _post_mix(x, y.reshape(b, s, -1), w_out[l].astype(BF16), norm_xattn[l][None, :],
                      xattn_w_q[l].astype(BF16), kv.reshape(b, -1, 2 * dm), xattn_w_o[l].astype(BF16))
        x = _mlp(x.reshape(b * s, dm), norm_mlp[l][None, :], mlp_w_up[l].astype(BF16),
                 mlp_w_down[l].astype(BF16), norm_final[None, :], l == n_layers - 1).reshape(b, s, dm)
    return x
```

```python
import functools

import jax
import jax.numpy as jnp
from jax import lax
from jax.experimental import pallas as pl
from jax.experimental.pallas import tpu as pltpu

D_MODEL = 1024
HEAD_DIM = 64
RET_WIDTH = 512
RWKV_WIDTH = 512
N_HEADS = 8
RET_CHUNK = 128
ROPE_BASE = 10000.0
DECAY_LORA = 64
AAA_LORA = 64
GATE_LORA = 160
LORA_WIDTH = DECAY_LORA + AAA_LORA + GATE_LORA
RET_PROJ = 4 * RET_WIDTH
XATTN_HEADS = 4
XATTN_HEAD_DIM = D_MODEL // XATTN_HEADS
D_FF = 4 * D_MODEL
RMS_EPS = 1e-6
GN_EPS_RET = 1e-5
GN_EPS_RWKV = 64e-5

WKV_CHUNK = 64
MIX_SEQS = 2
MIX_BLOCK = 256
VMEM_LIMIT_BYTES = 56 * 1024 * 1024

BF16 = jnp.bfloat16
F32 = jnp.float32


def _dot(a, b):
    return jnp.dot(a, b, preferred_element_type=F32)


def _dot_nt(a, b):
    return lax.dot_general(a, b, (((1,), (1,)), ((), ())), preferred_element_type=F32)


def _dot_tn(a, b):
    return lax.dot_general(a, b, (((0,), (0,)), ((), ())), preferred_element_type=F32)


def _rms(x, g):
    return x * lax.rsqrt(jnp.mean(x * x, axis=-1, keepdims=True) + RMS_EPS) * g


def _params(n_axes):
    return pltpu.CompilerParams(dimension_semantics=("arbitrary",) * n_axes,
                                vmem_limit_bytes=VMEM_LIMIT_BYTES)


def _full(shape):
    zeros = (0,) * len(shape)
    return pl.BlockSpec(shape, lambda *_: zeros, pipeline_mode=pl.Buffered(1))


def _head_sums(x):
    lanes = 2 * HEAD_DIM
    in_h0 = lax.broadcasted_iota(jnp.int32, (x.shape[0], lanes), 1) < HEAD_DIM
    tiles = []
    for lo in range(0, x.shape[1], lanes):
        t = x[:, lo:lo + lanes]
        s0 = jnp.sum(jnp.where(in_h0, t, 0.0), axis=-1, keepdims=True)
        s1 = jnp.sum(jnp.where(in_h0, 0.0, t), axis=-1, keepdims=True)
        tiles.append(jnp.where(in_h0, s0, s1))
    return jnp.concatenate(tiles, axis=1)


def _head_norm(y, eps):
    d = y - _head_sums(y) * (1.0 / HEAD_DIM)
    var = _head_sums(d * d) * (1.0 / HEAD_DIM)
    return d * lax.rsqrt(var + eps)


def _pair_diag(x):
    in_h0 = lax.broadcasted_iota(jnp.int32, x.shape, 1) < HEAD_DIM
    zero = jnp.zeros_like(x)
    return jnp.concatenate([jnp.where(in_h0, x, zero), jnp.where(in_h0, zero, x)], axis=0)


def _retention_body(p, pos, freq_ref, dmask_ref, xi_ref, zeta_ref, gc_ref, same_head_ref, gnw_ref, gnb_ref,
                    state_ref):
    c = RET_CHUNK
    w = RET_WIDTH
    lanes = 2 * HEAD_DIM
    n_groups = w // lanes

    tb = p.shape[0]
    q, k, v, gate = p[:, :w], p[:, w:2 * w], p[:, 2 * w:3 * w], p[:, 3 * w:]

    ang = pos.astype(F32) * freq_ref[...]
    cos = jnp.concatenate([jnp.cos(ang)] * n_groups, axis=1)
    sin = jnp.concatenate([jnp.sin(ang)] * n_groups, axis=1)
    lane = lax.broadcasted_iota(jnp.int32, (tb, w), 1)
    first_half = (lane % HEAD_DIM) < (HEAD_DIM // 2)

    def rope(t):
        rot = jnp.where(first_half, -pltpu.roll(t, w - HEAD_DIM // 2, 1), pltpu.roll(t, HEAD_DIM // 2, 1))
        return t * cos + rot * sin

    q = rope(q)
    k = rope(k) * (HEAD_DIM ** -0.5)
    q_in = (q * xi_ref[...]).astype(BF16)
    k_out = (k * zeta_ref[...]).astype(BF16)
    q = q.astype(BF16)
    k = k.astype(BF16)
    vb = v.astype(BF16)

    units = [(ci, g) for ci in range(tb // c) for g in range(n_groups)]
    tile = lambda x, ci, g: x[ci * c:(ci + 1) * c, g * lanes:(g + 1) * lanes]
    s_u = {(ci, g): (_dot_nt(tile(q, ci, g), _pair_diag(tile(k, ci, g))) * dmask_ref[g]).astype(BF16)
           for ci, g in units}
    y_u = {(ci, g): _dot(s_u[ci, g], _pair_diag(tile(vb, ci, g))) for ci, g in units}
    kv_u = {(ci, g): _dot_tn(tile(k_out, ci, g), tile(vb, ci, g)) for ci, g in units}
    same_head = same_head_ref[...] > 0.5
    n_seq = state_ref.shape[0]
    chunks_per_seq = tb // c // n_seq
    chains = [(q, g) for q in range(n_seq) for g in range(n_groups)]
    r_g = {(q, g): state_ref[q, g] for q, g in chains}
    for step in range(chunks_per_seq):
        for q, g in chains:
            ci = q * chunks_per_seq + step
            y_u[ci, g] = y_u[ci, g] + _dot(tile(q_in, ci, g), r_g[q, g].astype(BF16))
        r_g = {(q, g): r_g[q, g] * gc_ref[g] + jnp.where(same_head, kv_u[q * chunks_per_seq + step, g], 0.0)
               for q, g in chains}
    for q, g in chains:
        state_ref[q, g] = r_g[q, g]
    y_all = jnp.concatenate([jnp.concatenate([y_u[ci, g] for g in range(n_groups)], axis=1)
                             for ci in range(tb // c)], axis=0)
    yn = _head_norm(y_all, GN_EPS_RET) * gnw_ref[...] + gnb_ref[...]
    return gate * jax.nn.sigmoid(gate) * yn


def _retention_consts(tb):
    c = RET_CHUNK
    half = HEAD_DIM // 2
    inv_freq = ROPE_BASE ** (-jnp.arange(half, dtype=F32) / half)
    freq = jnp.tile(inv_freq, 128 // half)[None, :]
    log_g = jnp.log(1.0 - 2.0 ** (-5.0 - jnp.arange(N_HEADS, dtype=F32)))
    idx = jnp.arange(c, dtype=F32)
    diff = idx[:, None] - idx[None, :]
    causal = diff >= 0
    dmask = jnp.where(causal[None], jnp.exp(log_g[:, None, None] * jnp.where(causal, diff, 0.0)[None]), 0.0)
    xi = jnp.exp(log_g[:, None] * (idx + 1.0)[None])
    zeta = jnp.exp(log_g[:, None] * (c - 1.0 - idx)[None])
    g_chunk = jnp.exp(log_g * c)
    xi_w = jnp.tile(jnp.repeat(xi.T, HEAD_DIM, axis=1), (tb // c, 1))
    zeta_w = jnp.tile(jnp.repeat(zeta.T, HEAD_DIM, axis=1), (tb // c, 1))
    dmask_pairs = jnp.concatenate([dmask[0::2], dmask[1::2]], axis=2)
    gc_rows = jnp.repeat(g_chunk, HEAD_DIM).reshape(N_HEADS // 2, 2 * HEAD_DIM, 1)
    gc_w = jnp.broadcast_to(gc_rows, (N_HEADS // 2, 2 * HEAD_DIM, 2 * HEAD_DIM))
    return [freq, dmask_pairs, xi_w, zeta_w, gc_w]


def _shift(p, carry_ref, mu):
    n_seq = carry_ref.shape[0]
    tb = p.shape[0] // n_seq
    rows = lax.broadcasted_iota(jnp.int32, p.shape, 0)
    prev = pltpu.roll(p, 1, 0)
    for q in range(n_seq):
        prev = jnp.where(rows == q * tb, carry_ref[q:q + 1, :], prev)
        carry_ref[q:q + 1, :] = p[(q + 1) * tb - 1:(q + 1) * tb, :]
    return p + (prev - p) * mu


def _wkv_body(prkv_raw, plora_raw, mu_rkv_ref, mu_lora_ref, w0_ref, wup_ref, a0_ref, aup_ref, gup_ref,
              kk_ref, ka_ref, rk_ref, gnw_ref, gnb_ref, tri_ref, mask_ref, eye_ref, state_mask_ref,
              carry_rkv_ref, carry_lora_ref, state_ref):
    tb = prkv_raw.shape[0]
    c = WKV_CHUNK
    w = RWKV_WIDTH
    d = HEAD_DIM

    prkv = _shift(prkv_raw, carry_rkv_ref, mu_rkv_ref[...])
    plora = _shift(plora_raw, carry_lora_ref, mu_lora_ref[...])
    r, k, v = prkv[:, :w], prkv[:, w:2 * w], prkv[:, 2 * w:]
    w_lr = plora[:, :DECAY_LORA]
    a_lr = plora[:, DECAY_LORA:DECAY_LORA + AAA_LORA]
    g_lr = plora[:, DECAY_LORA + AAA_LORA:]

    z = -(w0_ref[...] + _dot(jnp.tanh(w_lr).astype(BF16), wup_ref[...]))
    w_log = -(jnp.maximum(z, 0.0) + jnp.log1p(jnp.exp(-jnp.abs(z)))) - 0.5
    lw = -jnp.exp(w_log)
    a_sig = jax.nn.sigmoid(a0_ref[...] + _dot(a_lr.astype(BF16), aup_ref[...]))
    gate = _dot(jax.nn.sigmoid(g_lr).astype(BF16), gup_ref[...])
    kk = k * kk_ref[...]
    k2 = k * (1.0 + (a_sig - 1.0) * ka_ref[...])
    rk = r * k2 * rk_ref[...]

    kk = kk * (1.0 / jnp.maximum(jnp.sqrt(_head_sums(kk * kk)), 1e-12))
    bonus = _head_sums(rk) * v
    a_vec = -kk
    b_vec = kk * a_sig

    n_chunks = tb // c
    at, rt, bt, kt, bc, kc, pc = [], [], [], [], [], [], []
    for ci in range(n_chunks):
        rows = slice(ci * c, (ci + 1) * c)
        lw_c = lw[rows]
        cum = jnp.dot(tri_ref[...], lw_c, precision=lax.Precision.HIGHEST, preferred_element_type=F32)
        cum_end = cum[c - 1:, :]
        p_in = jnp.exp(cum)
        p_inv = jnp.exp(-cum)
        p_out = jnp.exp(cum_end - cum)
        at.append((a_vec[rows] * jnp.exp(cum - lw_c)).astype(BF16))
        rt.append((r[rows] * p_in).astype(BF16))
        bt.append((b_vec[rows] * p_inv).astype(BF16))
        kt.append((k2[rows] * p_inv).astype(BF16))
        bc.append((b_vec[rows] * p_out).astype(BF16))
        kc.append((k2[rows] * p_out).astype(BF16))
        pc.append(jnp.exp(cum_end))
    vb = v.astype(BF16)
    eye = eye_ref[...]
    mask = mask_ref[...]

    lanes = 2 * d
    n_groups = w // lanes
    pair_diag = _pair_diag
    units = [(ci, g) for ci in range(n_chunks) for g in range(n_groups)]
    gsl = lambda g: slice(g * lanes, (g + 1) * lanes)
    csl = lambda ci: slice(ci * c, (ci + 1) * c)
    tril = mask > 0.5
    vd_u = {(ci, g): pair_diag(vb[csl(ci), gsl(g)]) for ci, g in units}
    g_u = {}
    for ci, g in units:
        lhs = jnp.concatenate([at[ci][:, gsl(g)], rt[ci][:, gsl(g)]], axis=0)
        rhs = jnp.concatenate([pair_diag(bt[ci][:, gsl(g)]), pair_diag(kt[ci][:, gsl(g)])], axis=0)
        g_u[ci, g] = jnp.where(tril, _dot_nt(lhs, rhs), 0.0)
    m_u = {u: g_u[u][:c, :lanes] for u in units}
    t_u = {u: eye + m_u[u] for u in units}
    mb_u = {u: m_u[u].astype(BF16) for u in units}
    steps = 1
    while 2 * steps < c:
        mb_u = {u: _dot(mb_u[u], pair_diag(mb_u[u])).astype(BF16) for u in units}
        t_u = {u: t_u[u] + _dot(t_u[u].astype(BF16), pair_diag(mb_u[u])) for u in units}
        steps *= 2
    t_u = {u: t_u[u].astype(BF16) for u in units}
    w_u = {(ci, g): _dot(t_u[ci, g], pair_diag(at[ci][:, gsl(g)])).astype(BF16) for ci, g in units}
    akv_u = {u: _dot(g_u[u][:c, lanes:].astype(BF16), vd_u[u]).astype(BF16) for u in units}
    u0_u = {u: _dot(t_u[u], pair_diag(akv_u[u])) for u in units}
    y0_u = {u: _dot(g_u[u][c:, lanes:].astype(BF16), vd_u[u]) for u in units}

    same_head = state_mask_ref[...] > 0.5
    n_seq = carry_rkv_ref.shape[0]
    chunks_per_seq = n_chunks // n_seq
    chains = [(q, g) for q in range(n_seq) for g in range(n_groups)]
    s_g = {(q, g): state_ref[q, g] for q, g in chains}
    y_u = {}
    for step in range(chunks_per_seq):
        ci_of = lambda q: q * chunks_per_seq + step
        s_b = {k: s_g[k].astype(BF16) for k in chains}
        u_b = {(q, g): (_dot_nt(w_u[ci_of(q), g], s_b[q, g]) + u0_u[ci_of(q), g]).astype(BF16)
               for q, g in chains}
        for q, g in chains:
            ci = ci_of(q)
            y_u[ci, g] = (_dot_nt(rt[ci][:, gsl(g)], s_b[q, g])
                          + _dot(g_u[ci, g][c:, :lanes].astype(BF16), pair_diag(u_b[q, g])) + y0_u[ci, g])
        s_g = {(q, g): s_g[q, g] * pc[ci_of(q)][:, gsl(g)]
               + jnp.where(same_head,
                           _dot_tn(jnp.concatenate([u_b[q, g], vb[csl(ci_of(q)), gsl(g)]], axis=0),
                                   jnp.concatenate([bc[ci_of(q)][:, gsl(g)], kc[ci_of(q)][:, gsl(g)]], axis=0)),
                           0.0)
               for q, g in chains}
    for q, g in chains:
        state_ref[q, g] = s_g[q, g]
    y_all = jnp.concatenate([jnp.concatenate([y_u[ci, g] for g in range(n_groups)], axis=1)
                             for ci in range(n_chunks)], axis=0)
    yn = _head_norm(y_all, GN_EPS_RWKV) * gnw_ref[...] + gnb_ref[...]
    return (yn + bonus) * gate


N_RET_CONST = 8
N_WKV_CONST = 16


def _in_proj_kernel(x_ref, g_ref, w_ret_ref, w_rkv_ref, w_lora_ref, p_ret_ref, p_rkv_ref, p_lora_ref):
    h = _rms(x_ref[...], g_ref[...]).astype(BF16)
    p_ret_ref[...] = _dot(h, w_ret_ref[...])
    p_rkv_ref[...] = _dot(h, w_rkv_ref[...])
    p_lora_ref[...] = _dot(h, w_lora_ref[...])


def _in_proj(x2d, g, w_ret, w_rkv, w_lora, tm=1024):
    t = x2d.shape[0]
    assert t % tm == 0
    row = lambda i: (i, 0)
    widths = [w_ret.shape[1], w_rkv.shape[1], w_lora.shape[1]]
    return pl.pallas_call(
        _in_proj_kernel,
        grid=(t // tm,),
        in_specs=[pl.BlockSpec((tm, D_MODEL), row), _full(g.shape), _full(w_ret.shape),
                  _full(w_rkv.shape), _full(w_lora.shape)],
        out_specs=[pl.BlockSpec((tm, n), row) for n in widths],
        out_shape=[jax.ShapeDtypeStruct((t, n), F32) for n in widths],
        compiler_params=_params(1),
        name="in_proj",
    )(x2d, g, w_ret, w_rkv, w_lora)


def _mixers_kernel(*refs):
    (p_ret_ref, p_rkv_ref, p_lora_ref, pos_ref), refs = refs[:4], refs[4:]
    ret_consts, refs = refs[:N_RET_CONST], refs[N_RET_CONST:]
    wkv_consts, refs = refs[:N_WKV_CONST], refs[N_WKV_CONST:]
    o_ref, ret_state_ref, carry_rkv_ref, carry_lora_ref, wkv_state_ref = refs

    @pl.when(pl.program_id(1) == 0)
    def _():
        for ref in (ret_state_ref, carry_rkv_ref, carry_lora_ref, wkv_state_ref):
            ref[...] = jnp.zeros_like(ref)

    stacked = lambda ref: ref[...].reshape(ref.shape[0] * ref.shape[1], ref.shape[2])
    y_ret = _retention_body(stacked(p_ret_ref), stacked(pos_ref), *ret_consts, ret_state_ref)
    y_rwkv = _wkv_body(stacked(p_rkv_ref), stacked(p_lora_ref), *wkv_consts, carry_rkv_ref, carry_lora_ref,
                       wkv_state_ref)
    o_ref[...] = jnp.concatenate([y_ret, y_rwkv], axis=1).astype(o_ref.dtype).reshape(o_ref.shape)


def _mixers(p_ret, p_rkv, p_lora, positions, ret_gn_w, ret_gn_b, mu, w0, w_up, a0, a_up, g_up,
            k_k, k_a, r_k, gn_w, gn_b):
    b, s = positions.shape
    n_seq, tb, c = MIX_SEQS, MIX_BLOCK, WKV_CHUNK
    assert b % n_seq == 0 and s % tb == 0 and tb % RET_CHUNK == 0 and tb % c == 0
    mu_rkv, mu_lora = mu[:, :3 * RWKV_WIDTH], mu[:, 3 * RWKV_WIDTH:]
    idx = jnp.arange(c)
    incl = (idx[:, None] >= idx[None, :]).astype(F32)
    strict = (idx[:, None] > idx[None, :]).astype(F32)
    mask = jnp.concatenate([jnp.tile(strict, (1, 4)), jnp.tile(incl, (1, 4))], axis=0)
    eye = jnp.tile(jnp.eye(c, dtype=F32), (1, 2))
    head_of_lane = jnp.arange(2 * HEAD_DIM) // HEAD_DIM
    state_mask = (head_of_lane[:, None] == head_of_lane[None, :]).astype(F32)
    ret_consts = _retention_consts(n_seq * tb) + [state_mask, ret_gn_w, ret_gn_b]
    wkv_consts = [mu_rkv, mu_lora, w0, w_up.astype(BF16), a0, a_up.astype(BF16), g_up.astype(BF16),
                  k_k, k_a, r_k, gn_w, gn_b, incl, mask, eye, state_mask]
    assert len(ret_consts) == N_RET_CONST and len(wkv_consts) == N_WKV_CONST
    blk = lambda bi, ti: (bi, ti, 0)
    state = pltpu.VMEM((n_seq, N_HEADS // 2, 2 * HEAD_DIM, 2 * HEAD_DIM), F32)
    streams = [p.reshape(b, s, -1) for p in (p_ret, p_rkv, p_lora, positions)]
    return pl.pallas_call(
        _mixers_kernel,
        grid=(b // n_seq, s // tb),
        in_specs=[pl.BlockSpec((n_seq, tb, p.shape[2]), blk) for p in streams]
                 + [_full(a.shape) for a in ret_consts] + [_full(a.shape) for a in wkv_consts],
        out_specs=pl.BlockSpec((n_seq, tb, RET_WIDTH + RWKV_WIDTH), blk),
        out_shape=jax.ShapeDtypeStruct((b, s, RET_WIDTH + RWKV_WIDTH), BF16),
        scratch_shapes=[state, pltpu.VMEM((n_seq, 3 * RWKV_WIDTH), F32), pltpu.VMEM((n_seq, LORA_WIDTH), F32),
                        state],
        compiler_params=_params(2),
        name="mixers",
    )(*streams, *ret_consts, *wkv_consts)


def _mem_kv_kernel(m_ref, g_ref, w_ref, o_ref):
    h = _rms(m_ref[...], g_ref[...]).astype(BF16)
    o_ref[...] = _dot(h, w_ref[...]).astype(o_ref.dtype)


def _mem_kv(mem2d, g, w_kv, tm=1024):
    t = mem2d.shape[0]
    n = w_kv.shape[1]
    tm = min(tm, t)
    assert t % tm == 0
    return pl.pallas_call(
        _mem_kv_kernel,
        grid=(t // tm,),
        in_specs=[pl.BlockSpec((tm, D_MODEL), lambda i: (i, 0)), _full(g.shape), _full(w_kv.shape)],
        out_specs=pl.BlockSpec((tm, n), lambda i: (i, 0)),
        out_shape=jax.ShapeDtypeStruct((t, n), BF16),
        compiler_params=_params(1),
        name="mem_kv",
    )(mem2d, g, w_kv)


def _post_mix_kernel(x_ref, y_ref, wout_ref, gx_ref, wq_ref, kv_ref, wo_ref, o_ref):
    x1 = x_ref[0] + _dot(y_ref[0], wout_ref[...])
    q = _dot(_rms(x1, gx_ref[...]).astype(BF16), wq_ref[...]).astype(BF16)
    kv = kv_ref[0]
    heads = []
    for h in range(XATTN_HEADS):
        sl = slice(h * XATTN_HEAD_DIM, (h + 1) * XATTN_HEAD_DIM)
        s = _dot_nt(q[:, sl], kv[:, sl]) * (XATTN_HEAD_DIM ** -0.5)
        e = jnp.exp(s - jnp.max(s, axis=-1, keepdims=True))
        prob = e / jnp.sum(e, axis=-1, keepdims=True)
        heads.append(_dot(prob.astype(BF16), kv[:, D_MODEL + h * XATTN_HEAD_DIM:D_MODEL + (h + 1) * XATTN_HEAD_DIM]))
    o = jnp.concatenate(heads, axis=1).astype(BF16)
    o_ref[0] = x1 + _dot(o, wo_ref[...])


def _post_mix(x, y, w_out, g_x, w_q, kv, w_o, tm=1024):
    b, s, _ = x.shape
    m = kv.shape[1]
    assert s % tm == 0
    blk = lambda bi, ti: (bi, ti, 0)
    return pl.pallas_call(
        _post_mix_kernel,
        grid=(b, s // tm),
        in_specs=[pl.BlockSpec((1, tm, D_MODEL), blk), pl.BlockSpec((1, tm, y.shape[2]), blk),
                  _full(w_out.shape), _full(g_x.shape), _full(w_q.shape), pl.BlockSpec((1, m, 2 * D_MODEL), lambda bi, ti: (bi, 0, 0)),
                  _full(w_o.shape)],
        out_specs=pl.BlockSpec((1, tm, D_MODEL), blk),
        out_shape=jax.ShapeDtypeStruct((b, s, D_MODEL), F32),
        compiler_params=_params(2),
        name="post_mix",
    )(x, y, w_out, g_x, w_q, kv, w_o)


def _mlp_kernel(x_ref, g_ref, wup_ref, wdown_ref, gf_ref, o_ref, *, tf, final_norm):
    x = x_ref[...]
    h = _rms(x, g_ref[...]).astype(BF16)
    acc = x
    for j in range(D_FF // tf):
        u = jnp.maximum(_dot(h, wup_ref[:, j * tf:(j + 1) * tf]), 0.0)
        acc = acc + _dot((u * u).astype(BF16), wdown_ref[j * tf:(j + 1) * tf, :])
    o_ref[...] = _rms(acc, gf_ref[...]) if final_norm else acc


def _mlp(x2d, g, w_up, w_down, g_final, final_norm, tm=1024, tf=512):
    t = x2d.shape[0]
    assert t % tm == 0 and D_FF % tf == 0
    row = lambda i: (i, 0)
    return pl.pallas_call(
        functools.partial(_mlp_kernel, tf=tf, final_norm=final_norm),
        grid=(t // tm,),
        in_specs=[pl.BlockSpec((tm, D_MODEL), row), _full(g.shape), _full(w_up.shape), _full(w_down.shape),
                  _full(g_final.shape)],
        out_specs=pl.BlockSpec((tm, D_MODEL), row),
        out_shape=jax.ShapeDtypeStruct((t, D_MODEL), F32),
        compiler_params=_params(1),
        name="mlp",
    )(x2d, g, w_up, w_down, g_final)


def kernel(x, mem, positions, norm_mix, w_in, ret_gn_w, ret_gn_b, rwkv_mu, rwkv_w0, rwkv_w_up, rwkv_a0,
           rwkv_a_up, rwkv_g_up, rwkv_k_k, rwkv_k_a, rwkv_r_k, rwkv_gn_w, rwkv_gn_b, w_out, norm_xattn,
           norm_mem, xattn_w_q, xattn_w_kv, xattn_w_o, norm_mlp, mlp_w_up, mlp_w_down, norm_final):
    b, s, dm = x.shape
    n_layers = w_in.shape[0]
    rkv_end = RET_PROJ + 3 * RWKV_WIDTH
    for l in range(n_layers):
        p_ret, p_rkv, p_lora = _in_proj(x.reshape(b * s, dm), norm_mix[l][None, :],
                                        w_in[l, :, :RET_PROJ].astype(BF16),
                                        w_in[l, :, RET_PROJ:rkv_end].astype(BF16),
                                        w_in[l, :, rkv_end:].astype(BF16))
        y = _mixers(p_ret, p_rkv, p_lora, positions,
                    ret_gn_w[l][None, :], ret_gn_b[l][None, :], rwkv_mu[l][None, :],
                    rwkv_w0[l][None, :], rwkv_w_up[l], rwkv_a0[l][None, :], rwkv_a_up[l], rwkv_g_up[l],
                    rwkv_k_k[l][None, :], rwkv_k_a[l][None, :], rwkv_r_k[l].reshape(1, -1),
                    rwkv_gn_w[l][None, :], rwkv_gn_b[l][None, :])
        kv = _mem_kv(mem.reshape(-1, dm), norm_mem[l][None, :], xattn_w_kv[l].astype(BF16))
        x = _post_mix(x, y, w_out[l].astype(BF16), norm_xattn[l][None, :],
                      xattn_w_q[l].astype(BF16), kv.reshape(b, -1, 2 * dm), xattn_w_o[l].astype(BF16))
        x = _mlp(x.reshape(b * s, dm), norm_mlp[l][None, :], mlp_w_up[l].astype(BF16),
                 mlp_w_down[l].astype(BF16), norm_final[None, :], l == n_layers - 1).reshape(b, s, dm)
    return x
```

```python
import functools
import math

import jax
import jax.numpy as jnp
from jax import lax
from jax.experimental import pallas as pl
from jax.experimental.pallas import tpu as pltpu

D_MODEL = 1024
HEAD_DIM = 64
RET_WIDTH = 512
RWKV_WIDTH = 512
N_HEADS = 8
RET_CHUNK = 128
ROPE_BASE = 10000.0
DECAY_LORA = 64
AAA_LORA = 64
GATE_LORA = 160
LORA_WIDTH = DECAY_LORA + AAA_LORA + GATE_LORA
RET_PROJ = 4 * RET_WIDTH
XATTN_HEADS = 4
XATTN_HEAD_DIM = D_MODEL // XATTN_HEADS
D_FF = 4 * D_MODEL
RMS_EPS = 1e-6
GN_EPS_RET = 1e-5
GN_EPS_RWKV = 64e-5

WKV_CHUNK = 64
MIX_SEQS = 2
MIX_BLOCK = 256
VMEM_LIMIT_BYTES = 56 * 1024 * 1024

BF16 = jnp.bfloat16
F32 = jnp.float32


def _dot(a, b):
    return jnp.dot(a, b, preferred_element_type=F32)


def _dot_nt(a, b):
    return lax.dot_general(a, b, (((1,), (1,)), ((), ())), preferred_element_type=F32)


def _dot_tn(a, b):
    return lax.dot_general(a, b, (((0,), (0,)), ((), ())), preferred_element_type=F32)


def _rms(x, g):
    return x * lax.rsqrt(jnp.mean(x * x, axis=-1, keepdims=True) + RMS_EPS) * g


def _params(n_axes):
    return pltpu.CompilerParams(dimension_semantics=("arbitrary",) * n_axes,
                                vmem_limit_bytes=VMEM_LIMIT_BYTES)


def _full(shape):
    zeros = (0,) * len(shape)
    return pl.BlockSpec(shape, lambda *_: zeros, pipeline_mode=pl.Buffered(1))


def _head_sums(x):
    lanes = 2 * HEAD_DIM
    in_h0 = lax.broadcasted_iota(jnp.int32, (x.shape[0], lanes), 1) < HEAD_DIM
    tiles = []
    for lo in range(0, x.shape[1], lanes):
        t = x[:, lo:lo + lanes]
        s0 = jnp.sum(jnp.where(in_h0, t, 0.0), axis=-1, keepdims=True)
        s1 = jnp.sum(jnp.where(in_h0, 0.0, t), axis=-1, keepdims=True)
        tiles.append(jnp.where(in_h0, s0, s1))
    return jnp.concatenate(tiles, axis=1)


def _head_norm(y, eps):
    d = y - _head_sums(y) * (1.0 / HEAD_DIM)
    var = _head_sums(d * d) * (1.0 / HEAD_DIM)
    return d * lax.rsqrt(var + eps)


def _pair_diag(x):
    in_h0 = lax.broadcasted_iota(jnp.int32, x.shape, 1) < HEAD_DIM
    zero = jnp.zeros_like(x)
    return jnp.concatenate([jnp.where(in_h0, x, zero), jnp.where(in_h0, zero, x)], axis=0)


def _retention_body(p, pos, freq_ref, dmask_ref, xi_ref, zeta_ref, gc_ref, same_head_ref, gnw_ref, gnb_ref,
                    state_ref):
    c = RET_CHUNK
    w = RET_WIDTH
    lanes = 2 * HEAD_DIM
    n_groups = w // lanes

    tb = p.shape[0]
    q, k, v, gate = p[:, :w], p[:, w:2 * w], p[:, 2 * w:3 * w], p[:, 3 * w:]

    half = HEAD_DIM // 2
    ang = pos.astype(F32) * freq_ref[...]
    first_half = (lax.broadcasted_iota(jnp.int32, (tb, lanes), 1) % HEAD_DIM) < half
    cos = jnp.cos(ang)
    sin = jnp.sin(ang)
    sin = jnp.where(first_half, -sin, sin)
    k_scale = HEAD_DIM ** -0.5

    def rope(t, cos_t, sin_t):
        tiles = []
        for lo in range(0, w, lanes):
            x = t[:, lo:lo + lanes]
            partner = jnp.where(first_half, pltpu.roll(x, lanes - half, 1), pltpu.roll(x, half, 1))
            tiles.append(x * cos_t + partner * sin_t)
        return jnp.concatenate(tiles, axis=1)

    q = rope(q, cos, sin)
    k = rope(k, cos * k_scale, sin * k_scale)
    q_in = (q * xi_ref[...]).astype(BF16)
    k_out = (k * zeta_ref[...]).astype(BF16)
    q = q.astype(BF16)
    k = k.astype(BF16)
    vb = v.astype(BF16)

    units = [(ci, g) for ci in range(tb // c) for g in range(n_groups)]
    tile = lambda x, ci, g: x[ci * c:(ci + 1) * c, g * lanes:(g + 1) * lanes]
    s_u = {(ci, g): (_dot_nt(tile(q, ci, g), _pair_diag(tile(k, ci, g))) * dmask_ref[g]).astype(BF16)
           for ci, g in units}
    y_u = {(ci, g): _dot(s_u[ci, g], _pair_diag(tile(vb, ci, g))) for ci, g in units}
    kv_u = {(ci, g): _dot_tn(tile(k_out, ci, g), tile(vb, ci, g)) for ci, g in units}
    same_head = same_head_ref[...] > 0.5
    n_seq = state_ref.shape[0]
    chunks_per_seq = tb // c // n_seq
    chains = [(q, g) for q in range(n_seq) for g in range(n_groups)]
    r_g = {(q, g): state_ref[q, g] for q, g in chains}
    for step in range(chunks_per_seq):
        for q, g in chains:
            ci = q * chunks_per_seq + step
            y_u[ci, g] = y_u[ci, g] + _dot(tile(q_in, ci, g), r_g[q, g].astype(BF16))
        r_g = {(q, g): r_g[q, g] * gc_ref[g] + jnp.where(same_head, kv_u[q * chunks_per_seq + step, g], 0.0)
               for q, g in chains}
    for q, g in chains:
        state_ref[q, g] = r_g[q, g]
    y_all = jnp.concatenate([jnp.concatenate([y_u[ci, g] for g in range(n_groups)], axis=1)
                             for ci in range(tb // c)], axis=0)
    yn = _head_norm(y_all, GN_EPS_RET) * gnw_ref[...] + gnb_ref[...]
    return gate * jax.nn.sigmoid(gate) * yn


def _retention_consts(tb):
    c = RET_CHUNK
    half = HEAD_DIM // 2
    inv_freq = ROPE_BASE ** (-jnp.arange(half, dtype=F32) / half)
    freq = jnp.tile(inv_freq, 128 // half)[None, :]
    log_g = jnp.log(1.0 - 2.0 ** (-5.0 - jnp.arange(N_HEADS, dtype=F32)))
    idx = jnp.arange(c, dtype=F32)
    diff = idx[:, None] - idx[None, :]
    causal = diff >= 0
    dmask = jnp.where(causal[None], jnp.exp(log_g[:, None, None] * jnp.where(causal, diff, 0.0)[None]), 0.0)
    xi = jnp.exp(log_g[:, None] * (idx + 1.0)[None])
    zeta = jnp.exp(log_g[:, None] * (c - 1.0 - idx)[None])
    g_chunk = jnp.exp(log_g * c)
    xi_w = jnp.tile(jnp.repeat(xi.T, HEAD_DIM, axis=1), (tb // c, 1))
    zeta_w = jnp.tile(jnp.repeat(zeta.T, HEAD_DIM, axis=1), (tb // c, 1))
    dmask_pairs = jnp.concatenate([dmask[0::2], dmask[1::2]], axis=2)
    gc_rows = jnp.repeat(g_chunk, HEAD_DIM).reshape(N_HEADS // 2, 2 * HEAD_DIM, 1)
    gc_w = jnp.broadcast_to(gc_rows, (N_HEADS // 2, 2 * HEAD_DIM, 2 * HEAD_DIM))
    return [freq, dmask_pairs, xi_w, zeta_w, gc_w]


def _shift(p, carry_ref, mu):
    n_seq = carry_ref.shape[0]
    tb = p.shape[0] // n_seq
    rows = lax.broadcasted_iota(jnp.int32, p.shape, 0)
    prev = pltpu.roll(p, 1, 0)
    for q in range(n_seq):
        prev = jnp.where(rows == q * tb, carry_ref[q:q + 1, :], prev)
        carry_ref[q:q + 1, :] = p[(q + 1) * tb - 1:(q + 1) * tb, :]
    return p + (prev - p) * mu


def _wkv_body(prkv_raw, plora_raw, mu_rkv_ref, mu_lora_ref, w0_ref, wup_ref, a0_ref, aup_ref, gup_ref,
              kk_ref, ka_ref, rk_ref, gnw_ref, gnb_ref, tri_ref, mask_ref, eye_ref, state_mask_ref,
              carry_rkv_ref, carry_lora_ref, state_ref):
    tb = prkv_raw.shape[0]
    c = WKV_CHUNK
    w = RWKV_WIDTH
    d = HEAD_DIM

    prkv = _shift(prkv_raw, carry_rkv_ref, mu_rkv_ref[...])
    plora = _shift(plora_raw, carry_lora_ref, mu_lora_ref[...])
    r, k, v = prkv[:, :w], prkv[:, w:2 * w], prkv[:, 2 * w:]
    w_lr = plora[:, :DECAY_LORA]
    a_lr = plora[:, DECAY_LORA:DECAY_LORA + AAA_LORA]
    g_lr = plora[:, DECAY_LORA + AAA_LORA:]

    lw = -math.exp(-0.5) * jax.nn.sigmoid(w0_ref[...] + _dot(jnp.tanh(w_lr).astype(BF16), wup_ref[...]))
    a_sig = jax.nn.sigmoid(a0_ref[...] + _dot(a_lr.astype(BF16), aup_ref[...]))
    gate = _dot(jax.nn.sigmoid(g_lr).astype(BF16), gup_ref[...])
    kk = k * kk_ref[...]
    k2 = k * (1.0 + (a_sig - 1.0) * ka_ref[...])
    rk = r * k2 * rk_ref[...]

    kk = kk * (1.0 / jnp.maximum(jnp.sqrt(_head_sums(kk * kk)), 1e-12))
    bonus = _head_sums(rk) * v
    a_vec = -kk
    b_vec = kk * a_sig

    n_chunks = tb // c
    at, rt, bt, kt, bc, kc, pc = [], [], [], [], [], [], []
    for ci in range(n_chunks):
        rows = slice(ci * c, (ci + 1) * c)
        lw_c = lw[rows]
        cum = jnp.dot(tri_ref[...], lw_c, precision=lax.Precision.HIGHEST, preferred_element_type=F32)
        cum_end = cum[c - 1:, :]
        p_in = jnp.exp(cum)
        p_inv = jnp.exp(-cum)
        p_out = jnp.exp(cum_end - cum)
        at.append((a_vec[rows] * jnp.exp(cum - lw_c)).astype(BF16))
        rt.append((r[rows] * p_in).astype(BF16))
        bt.append((b_vec[rows] * p_inv).astype(BF16))
        kt.append((k2[rows] * p_inv).astype(BF16))
        bc.append((b_vec[rows] * p_out).astype(BF16))
        kc.append((k2[rows] * p_out).astype(BF16))
        pc.append(jnp.exp(cum_end))
    vb = v.astype(BF16)
    eye = eye_ref[...]
    mask = mask_ref[...]

    lanes = 2 * d
    n_groups = w // lanes
    pair_diag = _pair_diag
    units = [(ci, g) for ci in range(n_chunks) for g in range(n_groups)]
    gsl = lambda g: slice(g * lanes, (g + 1) * lanes)
    csl = lambda ci: slice(ci * c, (ci + 1) * c)
    tril = mask > 0.5
    vd_u = {(ci, g): pair_diag(vb[csl(ci), gsl(g)]) for ci, g in units}
    g_u = {}
    for ci, g in units:
        lhs = jnp.concatenate([at[ci][:, gsl(g)], rt[ci][:, gsl(g)]], axis=0)
        rhs = jnp.concatenate([pair_diag(bt[ci][:, gsl(g)]), pair_diag(kt[ci][:, gsl(g)])], axis=0)
        g_u[ci, g] = jnp.where(tril, _dot_nt(lhs, rhs), 0.0)
    m_u = {u: g_u[u][:c, :lanes] for u in units}
    t_u = {u: eye + m_u[u] for u in units}
    mb_u = {u: m_u[u].astype(BF16) for u in units}
    steps = 1
    while 2 * steps < c:
        mb_u = {u: _dot(mb_u[u], pair_diag(mb_u[u])).astype(BF16) for u in units}
        t_u = {u: t_u[u] + _dot(t_u[u].astype(BF16), pair_diag(mb_u[u])) for u in units}
        steps *= 2
    t_u = {u: t_u[u].astype(BF16) for u in units}
    w_u = {(ci, g): _dot(t_u[ci, g], pair_diag(at[ci][:, gsl(g)])).astype(BF16) for ci, g in units}
    akv_u = {u: _dot(g_u[u][:c, lanes:].astype(BF16), vd_u[u]).astype(BF16) for u in units}
    u0_u = {u: _dot(t_u[u], pair_diag(akv_u[u])) for u in units}
    y0_u = {u: _dot(g_u[u][c:, lanes:].astype(BF16), vd_u[u]) for u in units}

    same_head = state_mask_ref[...] > 0.5
    n_seq = carry_rkv_ref.shape[0]
    chunks_per_seq = n_chunks // n_seq
    chains = [(q, g) for q in range(n_seq) for g in range(n_groups)]
    s_g = {(q, g): state_ref[q, g] for q, g in chains}
    y_u = {}
    for step in range(chunks_per_seq):
        ci_of = lambda q: q * chunks_per_seq + step
        s_b = {k: s_g[k].astype(BF16) for k in chains}
        u_b = {(q, g): (_dot_nt(w_u[ci_of(q), g], s_b[q, g]) + u0_u[ci_of(q), g]).astype(BF16)
               for q, g in chains}
        for q, g in chains:
            ci = ci_of(q)
            y_u[ci, g] = (_dot_nt(rt[ci][:, gsl(g)], s_b[q, g])
                          + _dot(g_u[ci, g][c:, :lanes].astype(BF16), pair_diag(u_b[q, g])) + y0_u[ci, g])
        s_g = {(q, g): s_g[q, g] * pc[ci_of(q)][:, gsl(g)]
               + jnp.where(same_head,
                           _dot_tn(jnp.concatenate([u_b[q, g], vb[csl(ci_of(q)), gsl(g)]], axis=0),
                                   jnp.concatenate([bc[ci_of(q)][:, gsl(g)], kc[ci_of(q)][:, gsl(g)]], axis=0)),
                           0.0)
               for q, g in chains}
    for q, g in chains:
        state_ref[q, g] = s_g[q, g]
    y_all = jnp.concatenate([jnp.concatenate([y_u[ci, g] for g in range(n_groups)], axis=1)
                             for ci in range(n_chunks)], axis=0)
    yn = _head_norm(y_all, GN_EPS_RWKV) * gnw_ref[...] + gnb_ref[...]
    return (yn + bonus) * gate


N_RET_CONST = 8
N_WKV_CONST = 16


def _in_proj_kernel(x_ref, g_ref, w_ret_ref, w_rkv_ref, w_lora_ref, p_ret_ref, p_rkv_ref, p_lora_ref):
    h = _rms(x_ref[...], g_ref[...]).astype(BF16)
    p_ret_ref[...] = _dot(h, w_ret_ref[...])
    p_rkv_ref[...] = _dot(h, w_rkv_ref[...])
    p_lora_ref[...] = _dot(h, w_lora_ref[...])


def _in_proj(x2d, g, w_ret, w_rkv, w_lora, tm=1024):
    t = x2d.shape[0]
    assert t % tm == 0
    row = lambda i: (i, 0)
    widths = [w_ret.shape[1], w_rkv.shape[1], w_lora.shape[1]]
    return pl.pallas_call(
        _in_proj_kernel,
        grid=(t // tm,),
        in_specs=[pl.BlockSpec((tm, D_MODEL), row), _full(g.shape), _full(w_ret.shape),
                  _full(w_rkv.shape), _full(w_lora.shape)],
        out_specs=[pl.BlockSpec((tm, n), row) for n in widths],
        out_shape=[jax.ShapeDtypeStruct((t, n), F32) for n in widths],
        compiler_params=_params(1),
        name="in_proj",
    )(x2d, g, w_ret, w_rkv, w_lora)


def _mixers_kernel(*refs):
    (p_ret_ref, p_rkv_ref, p_lora_ref, pos_ref), refs = refs[:4], refs[4:]
    ret_consts, refs = refs[:N_RET_CONST], refs[N_RET_CONST:]
    wkv_consts, refs = refs[:N_WKV_CONST], refs[N_WKV_CONST:]
    o_ref, ret_state_ref, carry_rkv_ref, carry_lora_ref, wkv_state_ref = refs

    @pl.when(pl.program_id(1) == 0)
    def _():
        for ref in (ret_state_ref, carry_rkv_ref, carry_lora_ref, wkv_state_ref):
            ref[...] = jnp.zeros_like(ref)

    stacked = lambda ref: ref[...].reshape(ref.shape[0] * ref.shape[1], ref.shape[2])
    y_ret = _retention_body(stacked(p_ret_ref), stacked(pos_ref), *ret_consts, ret_state_ref)
    y_rwkv = _wkv_body(stacked(p_rkv_ref), stacked(p_lora_ref), *wkv_consts, carry_rkv_ref, carry_lora_ref,
                       wkv_state_ref)
    o_ref[...] = jnp.concatenate([y_ret, y_rwkv], axis=1).astype(o_ref.dtype).reshape(o_ref.shape)


def _mixers(p_ret, p_rkv, p_lora, positions, ret_gn_w, ret_gn_b, mu, w0, w_up, a0, a_up, g_up,
            k_k, k_a, r_k, gn_w, gn_b):
    b, s = positions.shape
    n_seq, tb, c = MIX_SEQS, MIX_BLOCK, WKV_CHUNK
    assert b % n_seq == 0 and s % tb == 0 and tb % RET_CHUNK == 0 and tb % c == 0
    mu_rkv, mu_lora = mu[:, :3 * RWKV_WIDTH], mu[:, 3 * RWKV_WIDTH:]
    idx = jnp.arange(c)
    incl = (idx[:, None] >= idx[None, :]).astype(F32)
    strict = (idx[:, None] > idx[None, :]).astype(F32)
    mask = jnp.concatenate([jnp.tile(strict, (1, 4)), jnp.tile(incl, (1, 4))], axis=0)
    eye = jnp.tile(jnp.eye(c, dtype=F32), (1, 2))
    head_of_lane = jnp.arange(2 * HEAD_DIM) // HEAD_DIM
    state_mask = (head_of_lane[:, None] == head_of_lane[None, :]).astype(F32)
    ret_consts = _retention_consts(n_seq * tb) + [state_mask, ret_gn_w, ret_gn_b]
    wkv_consts = [mu_rkv, mu_lora, w0, w_up.astype(BF16), a0, a_up.astype(BF16), g_up.astype(BF16),
                  k_k, k_a, r_k, gn_w, gn_b, incl, mask, eye, state_mask]
    assert len(ret_consts) == N_RET_CONST and len(wkv_consts) == N_WKV_CONST
    blk = lambda bi, ti: (bi, ti, 0)
    state = pltpu.VMEM((n_seq, N_HEADS // 2, 2 * HEAD_DIM, 2 * HEAD_DIM), F32)
    streams = [p.reshape(b, s, -1) for p in (p_ret, p_rkv, p_lora, positions)]
    return pl.pallas_call(
        _mixers_kernel,
        grid=(b // n_seq, s // tb),
        in_specs=[pl.BlockSpec((n_seq, tb, p.shape[2]), blk) for p in streams]
                 + [_full(a.shape) for a in ret_consts] + [_full(a.shape) for a in wkv_consts],
        out_specs=pl.BlockSpec((n_seq, tb, RET_WIDTH + RWKV_WIDTH), blk),
        out_shape=jax.ShapeDtypeStruct((b, s, RET_WIDTH + RWKV_WIDTH), BF16),
        scratch_shapes=[state, pltpu.VMEM((n_seq, 3 * RWKV_WIDTH), F32), pltpu.VMEM((n_seq, LORA_WIDTH), F32),
                        state],
        compiler_params=_params(2),
        name="mixers",
    )(*streams, *ret_consts, *wkv_consts)


def _mem_kv_kernel(m_ref, g_ref, w_ref, o_ref):
    h = _rms(m_ref[...], g_ref[...]).astype(BF16)
    o_ref[...] = _dot(h, w_ref[...]).astype(o_ref.dtype)


def _mem_kv(mem2d, g, w_kv, tm=1024):
    t = mem2d.shape[0]
    n = w_kv.shape[1]
    tm = min(tm, t)
    assert t % tm == 0
    return pl.pallas_call(
        _mem_kv_kernel,
        grid=(t // tm,),
        in_specs=[pl.BlockSpec((tm, D_MODEL), lambda i: (i, 0)), _full(g.shape), _full(w_kv.shape)],
        out_specs=pl.BlockSpec((tm, n), lambda i: (i, 0)),
        out_shape=jax.ShapeDtypeStruct((t, n), BF16),
        compiler_params=_params(1),
        name="mem_kv",
    )(mem2d, g, w_kv)


def _post_mix_kernel(x_ref, y_ref, wout_ref, gx_ref, wq_ref, kv_ref, wo_ref, o_ref):
    x1 = x_ref[0] + _dot(y_ref[0], wout_ref[...])
    q = _dot(_rms(x1, gx_ref[...]).astype(BF16), wq_ref[...]).astype(BF16)
    kv = kv_ref[0]
    heads = []
    for h in range(XATTN_HEADS):
        sl = slice(h * XATTN_HEAD_DIM, (h + 1) * XATTN_HEAD_DIM)
        s = _dot_nt(q[:, sl], kv[:, sl]) * (XATTN_HEAD_DIM ** -0.5)
        e = jnp.exp(s - jnp.max(s, axis=-1, keepdims=True))
        prob = e / jnp.sum(e, axis=-1, keepdims=True)
        heads.append(_dot(prob.astype(BF16), kv[:, D_MODEL + h * XATTN_HEAD_DIM:D_MODEL + (h + 1) * XATTN_HEAD_DIM]))
    o = jnp.concatenate(heads, axis=1).astype(BF16)
    o_ref[0] = x1 + _dot(o, wo_ref[...])


def _post_mix(x, y, w_out, g_x, w_q, kv, w_o, tm=1024):
    b, s, _ = x.shape
    m = kv.shape[1]
    assert s % tm == 0
    blk = lambda bi, ti: (bi, ti, 0)
    return pl.pallas_call(
        _post_mix_kernel,
        grid=(b, s // tm),
        in_specs=[pl.BlockSpec((1, tm, D_MODEL), blk), pl.BlockSpec((1, tm, y.shape[2]), blk),
                  _full(w_out.shape), _full(g_x.shape), _full(w_q.shape), pl.BlockSpec((1, m, 2 * D_MODEL), lambda bi, ti: (bi, 0, 0)),
                  _full(w_o.shape)],
        out_specs=pl.BlockSpec((1, tm, D_MODEL), blk),
        out_shape=jax.ShapeDtypeStruct((b, s, D_MODEL), F32),
        compiler_params=_params(2),
        name="post_mix",
    )(x, y, w_out, g_x, w_q, kv, w_o)


def _mlp_kernel(x_ref, g_ref, wup_ref, wdown_ref, gf_ref, o_ref, *, tf, final_norm):
    x = x_ref[...]
    h = _rms(x, g_ref[...]).astype(BF16)
    acc = x
    for j in range(D_FF // tf):
        u = jnp.maximum(_dot(h, wup_ref[:, j * tf:(j + 1) * tf]), 0.0)
        acc = acc + _dot((u * u).astype(BF16), wdown_ref[j * tf:(j + 1) * tf, :])
    o_ref[...] = _rms(acc, gf_ref[...]) if final_norm else acc


def _mlp(x2d, g, w_up, w_down, g_final, final_norm, tm=1024, tf=512):
    t = x2d.shape[0]
    assert t % tm == 0 and D_FF % tf == 0
    row = lambda i: (i, 0)
    return pl.pallas_call(
        functools.partial(_mlp_kernel, tf=tf, final_norm=final_norm),
        grid=(t // tm,),
        in_specs=[pl.BlockSpec((tm, D_MODEL), row), _full(g.shape), _full(w_up.shape), _full(w_down.shape),
                  _full(g_final.shape)],
        out_specs=pl.BlockSpec((tm, D_MODEL), row),
        out_shape=jax.ShapeDtypeStruct((t, D_MODEL), F32),
        compiler_params=_params(1),
        name="mlp",
    )(x2d, g, w_up, w_down, g_final)


def kernel(x, mem, positions, norm_mix, w_in, ret_gn_w, ret_gn_b, rwkv_mu, rwkv_w0, rwkv_w_up, rwkv_a0,
           rwkv_a_up, rwkv_g_up, rwkv_k_k, rwkv_k_a, rwkv_r_k, rwkv_gn_w, rwkv_gn_b, w_out, norm_xattn,
           norm_mem, xattn_w_q, xattn_w_kv, xattn_w_o, norm_mlp, mlp_w_up, mlp_w_down, norm_final):
    b, s, dm = x.shape
    n_layers = w_in.shape[0]
    rkv_end = RET_PROJ + 3 * RWKV_WIDTH
    for l in range(n_layers):
        p_ret, p_rkv, p_lora = _in_proj(x.reshape(b * s, dm), norm_mix[l][None, :],
                                        w_in[l, :, :RET_PROJ].astype(BF16),
                                        w_in[l, :, RET_PROJ:rkv_end].astype(BF16),
                                        w_in[l, :, rkv_end:].astype(BF16))
        y = _mixers(p_ret, p_rkv, p_lora, positions,
                    ret_gn_w[l][None, :], ret_gn_b[l][None, :], rwkv_mu[l][None, :],
                    rwkv_w0[l][None, :], rwkv_w_up[l], rwkv_a0[l][None, :], rwkv_a_up[l], rwkv_g_up[l],
                    rwkv_k_k[l][None, :], rwkv_k_a[l][None, :], rwkv_r_k[l].reshape(1, -1),
                    rwkv_gn_w[l][None, :], rwkv_gn_b[l][None, :])
        kv = _mem_kv(mem.reshape(-1, dm), norm_mem[l][None, :], xattn_w_kv[l].astype(BF16))
        x = _post_mix(x, y, w_out[l].astype(BF16), norm_xattn[l][None, :],
                      xattn_w_q[l].astype(BF16), kv.reshape(b, -1, 2 * dm), xattn_w_o[l].astype(BF16))
        x = _mlp(x.reshape(b * s, dm), norm_mlp[l][None, :], mlp_w_up[l].astype(BF16),
                 mlp_w_down[l].astype(BF16), norm_final[None, :], l == n_layers - 1).reshape(b, s, dm)
    return x
```

```python
import functools
import math

import jax
import jax.numpy as jnp
from jax import lax
from jax.experimental import pallas as pl
from jax.experimental.pallas import tpu as pltpu

D_MODEL = 1024
HEAD_DIM = 64
RET_WIDTH = 512
RWKV_WIDTH = 512
N_HEADS = 8
RET_CHUNK = 128
ROPE_BASE = 10000.0
DECAY_LORA = 64
AAA_LORA = 64
GATE_LORA = 160
LORA_WIDTH = DECAY_LORA + AAA_LORA + GATE_LORA
RET_PROJ = 4 * RET_WIDTH
XATTN_HEADS = 4
XATTN_HEAD_DIM = D_MODEL // XATTN_HEADS
D_FF = 4 * D_MODEL
RMS_EPS = 1e-6
GN_EPS_RET = 1e-5
GN_EPS_RWKV = 64e-5

WKV_CHUNK = 64
MIX_SEQS = 2
MIX_BLOCK = 256
PREP_ROWS = 512
VMEM_LIMIT_BYTES = 56 * 1024 * 1024

BF16 = jnp.bfloat16
F32 = jnp.float32


def _dot(a, b):
    return jnp.dot(a, b, preferred_element_type=F32)


def _dot_nt(a, b):
    return lax.dot_general(a, b, (((1,), (1,)), ((), ())), preferred_element_type=F32)


def _dot_tn(a, b):
    return lax.dot_general(a, b, (((0,), (0,)), ((), ())), preferred_element_type=F32)


def _rms(x, g):
    return x * lax.rsqrt(jnp.mean(x * x, axis=-1, keepdims=True) + RMS_EPS) * g


def _params(n_axes):
    return pltpu.CompilerParams(dimension_semantics=("arbitrary",) * n_axes,
                                vmem_limit_bytes=VMEM_LIMIT_BYTES)


def _full(shape):
    zeros = (0,) * len(shape)
    return pl.BlockSpec(shape, lambda *_: zeros, pipeline_mode=pl.Buffered(1))


def _head_sums(x):
    lanes = 2 * HEAD_DIM
    in_h0 = lax.broadcasted_iota(jnp.int32, (x.shape[0], lanes), 1) < HEAD_DIM
    tiles = []
    for lo in range(0, x.shape[1], lanes):
        t = x[:, lo:lo + lanes]
        s0 = jnp.sum(jnp.where(in_h0, t, 0.0), axis=-1, keepdims=True)
        s1 = jnp.sum(jnp.where(in_h0, 0.0, t), axis=-1, keepdims=True)
        tiles.append(jnp.where(in_h0, s0, s1))
    return jnp.concatenate(tiles, axis=1)


def _head_norm(y, eps):
    d = y - _head_sums(y) * (1.0 / HEAD_DIM)
    var = _head_sums(d * d) * (1.0 / HEAD_DIM)
    return d * lax.rsqrt(var + eps)


def _pair_diag(x):
    in_h0 = lax.broadcasted_iota(jnp.int32, x.shape, 1) < HEAD_DIM
    zero = jnp.zeros_like(x)
    return jnp.concatenate([jnp.where(in_h0, x, zero), jnp.where(in_h0, zero, x)], axis=0)


def _retention_body(q, k, q_in, k_out, vb, gsilu, dmask_ref, gc_ref, same_head_ref, gnw_ref, gnb_ref, state_ref):
    c = RET_CHUNK
    lanes = 2 * HEAD_DIM
    n_groups = RET_WIDTH // lanes
    n_chunks = q.shape[0] // c
    units = [(ci, g) for ci in range(n_chunks) for g in range(n_groups)]
    tile = lambda x, ci, g: x[ci * c:(ci + 1) * c, g * lanes:(g + 1) * lanes]
    s_u = {(ci, g): (_dot_nt(tile(q, ci, g), _pair_diag(tile(k, ci, g))) * dmask_ref[g]).astype(BF16)
           for ci, g in units}
    y_u = {(ci, g): _dot(s_u[ci, g], _pair_diag(tile(vb, ci, g))) for ci, g in units}
    kv_u = {(ci, g): _dot_tn(tile(k_out, ci, g), tile(vb, ci, g)) for ci, g in units}
    same_head = same_head_ref[...] > 0.5
    n_seq = state_ref.shape[0]
    chunks_per_seq = n_chunks // n_seq
    chains = [(sq, g) for sq in range(n_seq) for g in range(n_groups)]
    r_g = {(sq, g): state_ref[sq, g] for sq, g in chains}
    for step in range(chunks_per_seq):
        for sq, g in chains:
            ci = sq * chunks_per_seq + step
            y_u[ci, g] = y_u[ci, g] + _dot(tile(q_in, ci, g), r_g[sq, g].astype(BF16))
        r_g = {(sq, g): r_g[sq, g] * gc_ref[g] + jnp.where(same_head, kv_u[sq * chunks_per_seq + step, g], 0.0)
               for sq, g in chains}
    for sq, g in chains:
        state_ref[sq, g] = r_g[sq, g]
    y_all = jnp.concatenate([jnp.concatenate([y_u[ci, g] for g in range(n_groups)], axis=1)
                             for ci in range(n_chunks)], axis=0)
    return gsilu * (_head_norm(y_all, GN_EPS_RET) * gnw_ref[...] + gnb_ref[...])


def _retention_consts():
    c = RET_CHUNK
    half = HEAD_DIM // 2
    inv_freq = ROPE_BASE ** (-jnp.arange(half, dtype=F32) / half)
    freq = jnp.tile(inv_freq, 128 // half)[None, :]
    log_g = jnp.log(1.0 - 2.0 ** (-5.0 - jnp.arange(N_HEADS, dtype=F32)))
    idx = jnp.arange(c, dtype=F32)
    diff = idx[:, None] - idx[None, :]
    causal = diff >= 0
    dmask = jnp.where(causal[None], jnp.exp(log_g[:, None, None] * jnp.where(causal, diff, 0.0)[None]), 0.0)
    xi = jnp.exp(log_g[:, None] * (idx + 1.0)[None])
    zeta = jnp.exp(log_g[:, None] * (c - 1.0 - idx)[None])
    g_chunk = jnp.exp(log_g * c)
    xi_w = jnp.repeat(xi.T, HEAD_DIM, axis=1)
    zeta_w = jnp.repeat(zeta.T, HEAD_DIM, axis=1)
    dmask_pairs = jnp.concatenate([dmask[0::2], dmask[1::2]], axis=2)
    gc_rows = jnp.repeat(g_chunk, HEAD_DIM).reshape(N_HEADS // 2, 2 * HEAD_DIM, 1)
    gc_w = jnp.broadcast_to(gc_rows, (N_HEADS // 2, 2 * HEAD_DIM, 2 * HEAD_DIM))
    return freq, xi_w, zeta_w, dmask_pairs, gc_w


def _shift(p, carry_ref, mu):
    rows = lax.broadcasted_iota(jnp.int32, p.shape, 0)
    prev = jnp.where(rows == 0, carry_ref[...], pltpu.roll(p, 1, 0))
    carry_ref[...] = p[p.shape[0] - 1:, :]
    return p + (prev - p) * mu


def _wkv_body(at, rt, bt, kt, bc, kc, vb, bonus, gate, pc_ref, pc_row0, gnw_ref, gnb_ref, mask_ref, eye_ref,
              state_mask_ref, state_ref):
    c = WKV_CHUNK
    w = RWKV_WIDTH
    lanes = 2 * HEAD_DIM
    n_groups = w // lanes
    n_chunks = at.shape[0] // c
    eye = eye_ref[...]
    tril = mask_ref[...] > 0.5

    pair_diag = _pair_diag
    units = [(ci, g) for ci in range(n_chunks) for g in range(n_groups)]
    tile = lambda x, ci, g: x[ci * c:(ci + 1) * c, g * lanes:(g + 1) * lanes]
    vd_u = {(ci, g): pair_diag(tile(vb, ci, g)) for ci, g in units}
    g_u = {}
    for ci, g in units:
        lhs = jnp.concatenate([tile(at, ci, g), tile(rt, ci, g)], axis=0)
        rhs = jnp.concatenate([pair_diag(tile(bt, ci, g)), pair_diag(tile(kt, ci, g))], axis=0)
        g_u[ci, g] = jnp.where(tril, _dot_nt(lhs, rhs), 0.0)
    m_u = {u: g_u[u][:c, :lanes] for u in units}
    t_u = {u: eye + m_u[u] for u in units}
    mb_u = {u: m_u[u].astype(BF16) for u in units}
    steps = 1
    while 2 * steps < c:
        mb_u = {u: _dot(mb_u[u], pair_diag(mb_u[u])).astype(BF16) for u in units}
        t_u = {u: t_u[u] + _dot(t_u[u].astype(BF16), pair_diag(mb_u[u])) for u in units}
        steps *= 2
    t_u = {u: t_u[u].astype(BF16) for u in units}
    w_u = {(ci, g): _dot(t_u[ci, g], pair_diag(tile(at, ci, g))).astype(BF16) for ci, g in units}
    akv_u = {u: _dot(g_u[u][:c, lanes:].astype(BF16), vd_u[u]).astype(BF16) for u in units}
    u0_u = {u: _dot(t_u[u], pair_diag(akv_u[u])) for u in units}
    y0_u = {u: _dot(g_u[u][c:, lanes:].astype(BF16), vd_u[u]) for u in units}

    same_head = state_mask_ref[...] > 0.5
    n_seq = state_ref.shape[0]
    chunks_per_seq = n_chunks // n_seq
    chains = [(sq, g) for sq in range(n_seq) for g in range(n_groups)]
    s_g = {(sq, g): state_ref[sq, g] for sq, g in chains}
    y_u = {}
    for step in range(chunks_per_seq):
        ci_of = lambda sq: sq * chunks_per_seq + step
        pc = [pc_ref[sq, pl.ds(pc_row0 + step, 1), :] for sq in range(n_seq)]
        s_b = {key: s_g[key].astype(BF16) for key in chains}
        u_b = {(sq, g): (_dot_nt(w_u[ci_of(sq), g], s_b[sq, g]) + u0_u[ci_of(sq), g]).astype(BF16)
               for sq, g in chains}
        for sq, g in chains:
            ci = ci_of(sq)
            y_u[ci, g] = (_dot_nt(tile(rt, ci, g), s_b[sq, g])
                          + _dot(g_u[ci, g][c:, :lanes].astype(BF16), pair_diag(u_b[sq, g])) + y0_u[ci, g])
        s_g = {(sq, g): s_g[sq, g] * pc[sq][:, g * lanes:(g + 1) * lanes]
               + jnp.where(same_head,
                           _dot_tn(jnp.concatenate([u_b[sq, g], tile(vb, ci_of(sq), g)], axis=0),
                                   jnp.concatenate([tile(bc, ci_of(sq), g), tile(kc, ci_of(sq), g)], axis=0)),
                           0.0)
               for sq, g in chains}
    for sq, g in chains:
        state_ref[sq, g] = s_g[sq, g]
    y_all = jnp.concatenate([jnp.concatenate([y_u[ci, g] for g in range(n_groups)], axis=1)
                             for ci in range(n_chunks)], axis=0)
    return (_head_norm(y_all, GN_EPS_RWKV) * gnw_ref[...] + gnb_ref[...] + bonus) * gate


RET_STREAMS = 6
WKV_STREAMS = 9


def _prep_kernel(x_ref, pos_ref, g_ref, w_ret_ref, w_rkv_ref, w_lora_ref, freq_ref, xi_ref, zeta_ref,
                 mu_rkv_ref, mu_lora_ref, w0_ref, wup_ref, a0_ref, aup_ref, gup_ref, kk_ref, ka_ref, rk_ref,
                 tri_ref, *refs, tiles_per_seq):
    (q_ref, k_ref, qin_ref, kout_ref, vr_ref, gsilu_ref,
     at_ref, rt_ref, bt_ref, kt_ref, bc_ref, kc_ref, vw_ref, bonus_ref, gate_ref, pc_ref,
     carry_rkv_ref, carry_lora_ref) = refs
    tm = x_ref.shape[0]
    w = RWKV_WIDTH

    @pl.when(pl.program_id(0) % tiles_per_seq == 0)
    def _():
        carry_rkv_ref[...] = jnp.zeros_like(carry_rkv_ref)
        carry_lora_ref[...] = jnp.zeros_like(carry_lora_ref)

    h = _rms(x_ref[...], g_ref[...]).astype(BF16)

    prkv = _shift(_dot(h, w_rkv_ref[...]), carry_rkv_ref, mu_rkv_ref[...])
    plora = _shift(_dot(h, w_lora_ref[...]), carry_lora_ref, mu_lora_ref[...])
    r, k, v = prkv[:, :w], prkv[:, w:2 * w], prkv[:, 2 * w:]
    w_lr = plora[:, :DECAY_LORA]
    a_lr = plora[:, DECAY_LORA:DECAY_LORA + AAA_LORA]
    g_lr = plora[:, DECAY_LORA + AAA_LORA:]
    lw = -math.exp(-0.5) * jax.nn.sigmoid(w0_ref[...] + _dot(jnp.tanh(w_lr).astype(BF16), wup_ref[...]))
    a_sig = jax.nn.sigmoid(a0_ref[...] + _dot(a_lr.astype(BF16), aup_ref[...]))
    gate_ref[...] = _dot(jax.nn.sigmoid(g_lr).astype(BF16), gup_ref[...])
    kk = k * kk_ref[...]
    k2 = k * (1.0 + (a_sig - 1.0) * ka_ref[...])
    kk = kk * (1.0 / jnp.maximum(jnp.sqrt(_head_sums(kk * kk)), 1e-12))
    bonus_ref[...] = _head_sums(r * k2 * rk_ref[...]) * v
    vw_ref[...] = v.astype(BF16)
    a_vec = -kk
    b_vec = kk * a_sig
    c = WKV_CHUNK
    for ci in range(tm // c):
        rows = slice(ci * c, (ci + 1) * c)
        lw_c = lw[rows]
        cum = jnp.dot(tri_ref[...], lw_c, precision=lax.Precision.HIGHEST, preferred_element_type=F32)
        cum_end = cum[c - 1:, :]
        p_inv = jnp.exp(-cum)
        p_out = jnp.exp(cum_end - cum)
        at_ref[rows, :] = (a_vec[rows] * jnp.exp(cum - lw_c)).astype(BF16)
        rt_ref[rows, :] = (r[rows] * jnp.exp(cum)).astype(BF16)
        bt_ref[rows, :] = (b_vec[rows] * p_inv).astype(BF16)
        kt_ref[rows, :] = (k2[rows] * p_inv).astype(BF16)
        bc_ref[rows, :] = (b_vec[rows] * p_out).astype(BF16)
        kc_ref[rows, :] = (k2[rows] * p_out).astype(BF16)
        pc_ref[ci:ci + 1, :] = jnp.exp(cum_end)

    wr = RET_WIDTH
    lanes = 2 * HEAD_DIM
    half = HEAD_DIM // 2
    p = _dot(h, w_ret_ref[...])
    gate_r = p[:, 3 * wr:]
    gsilu_ref[...] = gate_r * jax.nn.sigmoid(gate_r)
    vr_ref[...] = p[:, 2 * wr:3 * wr].astype(BF16)
    ang = pos_ref[...].astype(F32) * freq_ref[...]
    first_half = (lax.broadcasted_iota(jnp.int32, (tm, lanes), 1) % HEAD_DIM) < half
    cos = jnp.cos(ang)
    sin = jnp.sin(ang)
    sin = jnp.where(first_half, -sin, sin)
    k_scale = HEAD_DIM ** -0.5

    def rope(t, cos_t, sin_t):
        tiles = []
        for lo in range(0, wr, lanes):
            x = t[:, lo:lo + lanes]
            partner = jnp.where(first_half, pltpu.roll(x, lanes - half, 1), pltpu.roll(x, half, 1))
            tiles.append(x * cos_t + partner * sin_t)
        return jnp.concatenate(tiles, axis=1)

    q = rope(p[:, :wr], cos, sin)
    kr = rope(p[:, wr:2 * wr], cos * k_scale, sin * k_scale)
    q_ref[...] = q.astype(BF16)
    k_ref[...] = kr.astype(BF16)
    cr = RET_CHUNK
    for ci in range(tm // cr):
        rows = slice(ci * cr, (ci + 1) * cr)
        qin_ref[rows, :] = (q[rows] * xi_ref[...]).astype(BF16)
        kout_ref[rows, :] = (kr[rows] * zeta_ref[...]).astype(BF16)


def _prep(x2d, pos2d, seq_len, consts, tm=PREP_ROWS):
    t = x2d.shape[0]
    assert t % tm == 0 and seq_len % tm == 0 and tm % RET_CHUNK == 0 and tm % WKV_CHUNK == 0
    row = lambda i: (i, 0)
    width = RET_WIDTH
    out_dtypes = [BF16] * 5 + [F32] + [BF16] * 7 + [F32] * 2
    assert len(out_dtypes) == RET_STREAMS + WKV_STREAMS
    n_pc = tm // WKV_CHUNK
    return pl.pallas_call(
        functools.partial(_prep_kernel, tiles_per_seq=seq_len // tm),
        grid=(t // tm,),
        in_specs=[pl.BlockSpec((tm, D_MODEL), row), pl.BlockSpec((tm, 1), row)] + [_full(a.shape) for a in consts],
        out_specs=[pl.BlockSpec((tm, width), row) for _ in out_dtypes] + [pl.BlockSpec((n_pc, width), row)],
        out_shape=[jax.ShapeDtypeStruct((t, width), dt) for dt in out_dtypes]
                  + [jax.ShapeDtypeStruct((t // WKV_CHUNK, width), F32)],
        scratch_shapes=[pltpu.VMEM((1, 3 * RWKV_WIDTH), F32), pltpu.VMEM((1, LORA_WIDTH), F32)],
        compiler_params=_params(1),
        name="prep",
    )(x2d, pos2d, *consts)


def _mixers_kernel(*refs):
    ret_in, refs = refs[:RET_STREAMS], refs[RET_STREAMS:]
    wkv_in, refs = refs[:WKV_STREAMS], refs[WKV_STREAMS:]
    (pc_ref, dmask_ref, gc_ref, same_head_ref, ret_gnw_ref, ret_gnb_ref, gnw_ref, gnb_ref, mask_ref, eye_ref,
     o_ref, ret_state_ref, wkv_state_ref) = refs
    ti = pl.program_id(1)

    @pl.when(ti == 0)
    def _():
        ret_state_ref[...] = jnp.zeros_like(ret_state_ref)
        wkv_state_ref[...] = jnp.zeros_like(wkv_state_ref)

    stacked = lambda ref: ref[...].reshape(ref.shape[0] * ref.shape[1], ref.shape[2])
    y_ret = _retention_body(*[stacked(r) for r in ret_in], dmask_ref, gc_ref, same_head_ref, ret_gnw_ref,
                            ret_gnb_ref, ret_state_ref)
    chunks_per_block = o_ref.shape[1] // WKV_CHUNK
    y_rwkv = _wkv_body(*[stacked(r) for r in wkv_in], pc_ref, ti * chunks_per_block, gnw_ref, gnb_ref, mask_ref,
                       eye_ref, same_head_ref, wkv_state_ref)
    o_ref[...] = jnp.concatenate([y_ret, y_rwkv], axis=1).astype(o_ref.dtype).reshape(o_ref.shape)


def _mixers(streams, pc, dmask_pairs, gc_w, ret_gn_w, ret_gn_b, gn_w, gn_b):
    b, s, width = streams[0].shape
    n_seq, tb, c = MIX_SEQS, MIX_BLOCK, WKV_CHUNK
    assert b % n_seq == 0 and s % tb == 0 and tb % RET_CHUNK == 0 and tb % c == 0
    idx = jnp.arange(c)
    incl = (idx[:, None] >= idx[None, :]).astype(F32)
    strict = (idx[:, None] > idx[None, :]).astype(F32)
    mask = jnp.concatenate([jnp.tile(strict, (1, 4)), jnp.tile(incl, (1, 4))], axis=0)
    eye = jnp.tile(jnp.eye(c, dtype=F32), (1, 2))
    head_of_lane = jnp.arange(2 * HEAD_DIM) // HEAD_DIM
    state_mask = (head_of_lane[:, None] == head_of_lane[None, :]).astype(F32)
    consts = [dmask_pairs, gc_w, state_mask, ret_gn_w, ret_gn_b, gn_w, gn_b, mask, eye]
    blk = lambda bi, ti: (bi, ti, 0)
    state = pltpu.VMEM((n_seq, N_HEADS // 2, 2 * HEAD_DIM, 2 * HEAD_DIM), F32)
    return pl.pallas_call(
        _mixers_kernel,
        grid=(b // n_seq, s // tb),
        in_specs=[pl.BlockSpec((n_seq, tb, width), blk) for _ in streams]
                 + [pl.BlockSpec((n_seq, s // c, width), lambda bi, ti: (bi, 0, 0))]
                 + [_full(a.shape) for a in consts],
        out_specs=pl.BlockSpec((n_seq, tb, RET_WIDTH + RWKV_WIDTH), blk),
        out_shape=jax.ShapeDtypeStruct((b, s, RET_WIDTH + RWKV_WIDTH), BF16),
        scratch_shapes=[state, state],
        compiler_params=_params(2),
        name="mixers",
    )(*streams, pc, *consts)


def _mem_kv_kernel(m_ref, g_ref, w_ref, o_ref):
    h = _rms(m_ref[...], g_ref[...]).astype(BF16)
    o_ref[...] = _dot(h, w_ref[...]).astype(o_ref.dtype)


def _mem_kv(mem2d, g, w_kv, tm=1024):
    t = mem2d.shape[0]
    n = w_kv.shape[1]
    tm = min(tm, t)
    assert t % tm == 0
    return pl.pallas_call(
        _mem_kv_kernel,
        grid=(t // tm,),
        in_specs=[pl.BlockSpec((tm, D_MODEL), lambda i: (i, 0)), _full(g.shape), _full(w_kv.shape)],
        out_specs=pl.BlockSpec((tm, n), lambda i: (i, 0)),
        out_shape=jax.ShapeDtypeStruct((t, n), BF16),
        compiler_params=_params(1),
        name="mem_kv",
    )(mem2d, g, w_kv)


def _post_mix_kernel(x_ref, y_ref, wout_ref, gx_ref, wq_ref, kv_ref, wo_ref, o_ref):
    x1 = x_ref[0] + _dot(y_ref[0], wout_ref[...])
    q = _dot(_rms(x1, gx_ref[...]).astype(BF16), wq_ref[...]).astype(BF16)
    kv = kv_ref[0]
    heads = []
    for h in range(XATTN_HEADS):
        sl = slice(h * XATTN_HEAD_DIM, (h + 1) * XATTN_HEAD_DIM)
        s = _dot_nt(q[:, sl], kv[:, sl]) * (XATTN_HEAD_DIM ** -0.5)
        e = jnp.exp(s - jnp.max(s, axis=-1, keepdims=True))
        prob = e / jnp.sum(e, axis=-1, keepdims=True)
        heads.append(_dot(prob.astype(BF16), kv[:, D_MODEL + h * XATTN_HEAD_DIM:D_MODEL + (h + 1) * XATTN_HEAD_DIM]))
    o = jnp.concatenate(heads, axis=1).astype(BF16)
    o_ref[0] = x1 + _dot(o, wo_ref[...])


def _post_mix(x, y, w_out, g_x, w_q, kv, w_o, tm=1024):
    b, s, _ = x.shape
    m = kv.shape[1]
    assert s % tm == 0
    blk = lambda bi, ti: (bi, ti, 0)
    return pl.pallas_call(
        _post_mix_kernel,
        grid=(b, s // tm),
        in_specs=[pl.BlockSpec((1, tm, D_MODEL), blk), pl.BlockSpec((1, tm, y.shape[2]), blk),
                  _full(w_out.shape), _full(g_x.shape), _full(w_q.shape),
                  pl.BlockSpec((1, m, 2 * D_MODEL), lambda bi, ti: (bi, 0, 0)), _full(w_o.shape)],
        out_specs=pl.BlockSpec((1, tm, D_MODEL), blk),
        out_shape=jax.ShapeDtypeStruct((b, s, D_MODEL), F32),
        compiler_params=_params(2),
        name="post_mix",
    )(x, y, w_out, g_x, w_q, kv, w_o)


def _mlp_kernel(x_ref, g_ref, wup_ref, wdown_ref, gf_ref, o_ref, *, tf, final_norm):
    x = x_ref[...]
    h = _rms(x, g_ref[...]).astype(BF16)
    acc = x
    for j in range(D_FF // tf):
        u = jnp.maximum(_dot(h, wup_ref[:, j * tf:(j + 1) * tf]), 0.0)
        acc = acc + _dot((u * u).astype(BF16), wdown_ref[j * tf:(j + 1) * tf, :])
    o_ref[...] = _rms(acc, gf_ref[...]) if final_norm else acc


def _mlp(x2d, g, w_up, w_down, g_final, final_norm, tm=1024, tf=512):
    t = x2d.shape[0]
    assert t % tm == 0 and D_FF % tf == 0
    row = lambda i: (i, 0)
    return pl.pallas_call(
        functools.partial(_mlp_kernel, tf=tf, final_norm=final_norm),
        grid=(t // tm,),
        in_specs=[pl.BlockSpec((tm, D_MODEL), row), _full(g.shape), _full(w_up.shape), _full(w_down.shape),
                  _full(g_final.shape)],
        out_specs=pl.BlockSpec((tm, D_MODEL), row),
        out_shape=jax.ShapeDtypeStruct((t, D_MODEL), F32),
        compiler_params=_params(1),
        name="mlp",
    )(x2d, g, w_up, w_down, g_final)


def kernel(x, mem, positions, norm_mix, w_in, ret_gn_w, ret_gn_b, rwkv_mu, rwkv_w0, rwkv_w_up, rwkv_a0,
           rwkv_a_up, rwkv_g_up, rwkv_k_k, rwkv_k_a, rwkv_r_k, rwkv_gn_w, rwkv_gn_b, w_out, norm_xattn,
           norm_mem, xattn_w_q, xattn_w_kv, xattn_w_o, norm_mlp, mlp_w_up, mlp_w_down, norm_final):
    b, s, dm = x.shape
    n_layers = w_in.shape[0]
    rkv_end = RET_PROJ + 3 * RWKV_WIDTH
    freq, xi_w, zeta_w, dmask_pairs, gc_w = _retention_consts()
    tri = (jnp.arange(WKV_CHUNK)[:, None] >= jnp.arange(WKV_CHUNK)[None, :]).astype(F32)
    row = lambda a: a.reshape(1, -1)
    for l in range(n_layers):
        mu = rwkv_mu[l]
        prep_consts = [row(norm_mix[l]), w_in[l, :, :RET_PROJ].astype(BF16),
                       w_in[l, :, RET_PROJ:rkv_end].astype(BF16), w_in[l, :, rkv_end:].astype(BF16),
                       freq, xi_w, zeta_w, row(mu[:3 * RWKV_WIDTH]), row(mu[3 * RWKV_WIDTH:]),
                       row(rwkv_w0[l]), rwkv_w_up[l].astype(BF16), row(rwkv_a0[l]), rwkv_a_up[l].astype(BF16),
                       rwkv_g_up[l].astype(BF16), row(rwkv_k_k[l]), row(rwkv_k_a[l]), row(rwkv_r_k[l]), tri]
        *streams, pc = _prep(x.reshape(b * s, dm), positions.reshape(b * s, 1), s, prep_consts)
        y = _mixers([a.reshape(b, s, -1) for a in streams], pc.reshape(b, s // WKV_CHUNK, -1), dmask_pairs, gc_w,
                    row(ret_gn_w[l]), row(ret_gn_b[l]), row(rwkv_gn_w[l]), row(rwkv_gn_b[l]))
        kv = _mem_kv(mem.reshape(-1, dm), norm_mem[l][None, :], xattn_w_kv[l].astype(BF16))
        x = _post_mix(x, y, w_out[l].astype(BF16), norm_xattn[l][None, :],
                      xattn_w_q[l].astype(BF16), kv.reshape(b, -1, 2 * dm), xattn_w_o[l].astype(BF16))
        x = _mlp(x.reshape(b * s, dm), norm_mlp[l][None, :], mlp_w_up[l].astype(BF16),
                 mlp_w_down[l].astype(BF16), norm_final[None, :], l == n_layers - 1).reshape(b, s, dm)
    return x
```

```python
import functools
import math

import jax
import jax.numpy as jnp
from jax import lax
from jax.experimental import pallas as pl
from jax.experimental.pallas import tpu as pltpu

D_MODEL = 1024
HEAD_DIM = 64
RET_WIDTH = 512
RWKV_WIDTH = 512
N_HEADS = 8
RET_CHUNK = 128
ROPE_BASE = 10000.0
DECAY_LORA = 64
AAA_LORA = 64
GATE_LORA = 160
LORA_WIDTH = DECAY_LORA + AAA_LORA + GATE_LORA
RET_PROJ = 4 * RET_WIDTH
XATTN_HEADS = 4
XATTN_HEAD_DIM = D_MODEL // XATTN_HEADS
D_FF = 4 * D_MODEL
RMS_EPS = 1e-6
GN_EPS_RET = 1e-5
GN_EPS_RWKV = 64e-5

WKV_CHUNK = 64
MIX_SEQS = 2
MIX_BLOCK = 256
WKV_WAVE = 16
PREP_ROWS = 512
VMEM_LIMIT_BYTES = 56 * 1024 * 1024

BF16 = jnp.bfloat16
F32 = jnp.float32


def _dot(a, b):
    return jnp.dot(a, b, preferred_element_type=F32)


def _dot_nt(a, b):
    return lax.dot_general(a, b, (((1,), (1,)), ((), ())), preferred_element_type=F32)


def _dot_tn(a, b):
    return lax.dot_general(a, b, (((0,), (0,)), ((), ())), preferred_element_type=F32)


def _rms(x, g):
    return x * lax.rsqrt(jnp.mean(x * x, axis=-1, keepdims=True) + RMS_EPS) * g


def _params(n_axes):
    return pltpu.CompilerParams(dimension_semantics=("arbitrary",) * n_axes,
                                vmem_limit_bytes=VMEM_LIMIT_BYTES)


def _full(shape):
    zeros = (0,) * len(shape)
    return pl.BlockSpec(shape, lambda *_: zeros, pipeline_mode=pl.Buffered(1))


def _head_sums(x):
    lanes = 2 * HEAD_DIM
    in_h0 = lax.broadcasted_iota(jnp.int32, (x.shape[0], lanes), 1) < HEAD_DIM
    tiles = []
    for lo in range(0, x.shape[1], lanes):
        t = x[:, lo:lo + lanes]
        s0 = jnp.sum(jnp.where(in_h0, t, 0.0), axis=-1, keepdims=True)
        s1 = jnp.sum(jnp.where(in_h0, 0.0, t), axis=-1, keepdims=True)
        tiles.append(jnp.where(in_h0, s0, s1))
    return jnp.concatenate(tiles, axis=1)


def _head_norm(y, eps):
    d = y - _head_sums(y) * (1.0 / HEAD_DIM)
    var = _head_sums(d * d) * (1.0 / HEAD_DIM)
    return d * lax.rsqrt(var + eps)


def _pair_diag(x):
    in_h0 = lax.broadcasted_iota(jnp.int32, x.shape, 1) < HEAD_DIM
    zero = jnp.zeros_like(x)
    return jnp.concatenate([jnp.where(in_h0, x, zero), jnp.where(in_h0, zero, x)], axis=0)


def _retention_body(q, k, q_in, k_out, vb, gsilu, dmask_ref, gc_ref, same_head_ref, gnw_ref, gnb_ref, state_ref):
    c = RET_CHUNK
    lanes = 2 * HEAD_DIM
    n_groups = RET_WIDTH // lanes
    n_chunks = q.shape[0] // c
    units = [(ci, g) for ci in range(n_chunks) for g in range(n_groups)]
    tile = lambda x, ci, g: x[ci * c:(ci + 1) * c, g * lanes:(g + 1) * lanes]
    s_u = {(ci, g): (_dot_nt(tile(q, ci, g), _pair_diag(tile(k, ci, g))) * dmask_ref[g]).astype(BF16)
           for ci, g in units}
    y_u = {(ci, g): _dot(s_u[ci, g], _pair_diag(tile(vb, ci, g))) for ci, g in units}
    kv_u = {(ci, g): _dot_tn(tile(k_out, ci, g), tile(vb, ci, g)) for ci, g in units}
    same_head = same_head_ref[...] > 0.5
    n_seq = state_ref.shape[0]
    chunks_per_seq = n_chunks // n_seq
    chains = [(sq, g) for sq in range(n_seq) for g in range(n_groups)]
    r_g = {(sq, g): state_ref[sq, g] for sq, g in chains}
    for step in range(chunks_per_seq):
        for sq, g in chains:
            ci = sq * chunks_per_seq + step
            y_u[ci, g] = y_u[ci, g] + _dot(tile(q_in, ci, g), r_g[sq, g].astype(BF16))
        r_g = {(sq, g): r_g[sq, g] * gc_ref[g] + jnp.where(same_head, kv_u[sq * chunks_per_seq + step, g], 0.0)
               for sq, g in chains}
    for sq, g in chains:
        state_ref[sq, g] = r_g[sq, g]
    y_all = jnp.concatenate([jnp.concatenate([y_u[ci, g] for g in range(n_groups)], axis=1)
                             for ci in range(n_chunks)], axis=0)
    return gsilu * (_head_norm(y_all, GN_EPS_RET) * gnw_ref[...] + gnb_ref[...])


def _retention_consts():
    c = RET_CHUNK
    half = HEAD_DIM // 2
    inv_freq = ROPE_BASE ** (-jnp.arange(half, dtype=F32) / half)
    freq = jnp.tile(inv_freq, 128 // half)[None, :]
    log_g = jnp.log(1.0 - 2.0 ** (-5.0 - jnp.arange(N_HEADS, dtype=F32)))
    idx = jnp.arange(c, dtype=F32)
    diff = idx[:, None] - idx[None, :]
    causal = diff >= 0
    dmask = jnp.where(causal[None], jnp.exp(log_g[:, None, None] * jnp.where(causal, diff, 0.0)[None]), 0.0)
    xi = jnp.exp(log_g[:, None] * (idx + 1.0)[None])
    zeta = jnp.exp(log_g[:, None] * (c - 1.0 - idx)[None])
    g_chunk = jnp.exp(log_g * c)
    xi_w = jnp.repeat(xi.T, HEAD_DIM, axis=1)
    zeta_w = jnp.repeat(zeta.T, HEAD_DIM, axis=1)
    dmask_pairs = jnp.concatenate([dmask[0::2], dmask[1::2]], axis=2)
    gc_rows = jnp.repeat(g_chunk, HEAD_DIM).reshape(N_HEADS // 2, 2 * HEAD_DIM, 1)
    gc_w = jnp.broadcast_to(gc_rows, (N_HEADS // 2, 2 * HEAD_DIM, 2 * HEAD_DIM))
    return freq, xi_w, zeta_w, dmask_pairs, gc_w


def _shift(p, carry_ref, mu_ref, lo):
    cols = slice(lo, lo + p.shape[1])
    rows = lax.broadcasted_iota(jnp.int32, p.shape, 0)
    prev = jnp.where(rows == 0, carry_ref[:, cols], pltpu.roll(p, 1, 0))
    carry_ref[:, cols] = p[p.shape[0] - 1:, :]
    return p + (prev - p) * mu_ref[:, cols]


def _wkv_body(at, rt, bt, kt, bc, kc, vb, bonus, gate, pc_ref, pc_row0, gnw_ref, gnb_ref, mask_ref, eye_ref,
              state_mask_ref, state_ref):
    c = WKV_CHUNK
    w = RWKV_WIDTH
    lanes = 2 * HEAD_DIM
    n_groups = w // lanes
    n_chunks = at.shape[0] // c
    eye = eye_ref[...]
    tril = mask_ref[...] > 0.5

    pair_diag = _pair_diag
    all_units = [(ci, g) for ci in range(n_chunks) for g in range(n_groups)]
    tile = lambda x, ci, g: x[ci * c:(ci + 1) * c, g * lanes:(g + 1) * lanes]
    g_u, w_u, u0_u, y0_u = {}, {}, {}, {}
    for lo in range(0, len(all_units), WKV_WAVE):
        units = all_units[lo:lo + WKV_WAVE]
        vd_u = {(ci, g): pair_diag(tile(vb, ci, g)) for ci, g in units}
        for ci, g in units:
            lhs = jnp.concatenate([tile(at, ci, g), tile(rt, ci, g)], axis=0)
            rhs = jnp.concatenate([pair_diag(tile(bt, ci, g)), pair_diag(tile(kt, ci, g))], axis=0)
            g_u[ci, g] = jnp.where(tril, _dot_nt(lhs, rhs), 0.0)
        m_u = {u: g_u[u][:c, :lanes] for u in units}
        t_u = {u: eye + m_u[u] for u in units}
        mb_u = {u: m_u[u].astype(BF16) for u in units}
        steps = 1
        while 2 * steps < c:
            mb_u = {u: _dot(mb_u[u], pair_diag(mb_u[u])).astype(BF16) for u in units}
            t_u = {u: t_u[u] + _dot(t_u[u].astype(BF16), pair_diag(mb_u[u])) for u in units}
            steps *= 2
        t_u = {u: t_u[u].astype(BF16) for u in units}
        w_u.update({(ci, g): _dot(t_u[ci, g], pair_diag(tile(at, ci, g))).astype(BF16) for ci, g in units})
        akv_u = {u: _dot(g_u[u][:c, lanes:].astype(BF16), vd_u[u]).astype(BF16) for u in units}
        u0_u.update({u: _dot(t_u[u], pair_diag(akv_u[u])) for u in units})
        y0_u.update({u: _dot(g_u[u][c:, lanes:].astype(BF16), vd_u[u]) for u in units})

    same_head = state_mask_ref[...] > 0.5
    n_seq = state_ref.shape[0]
    chunks_per_seq = n_chunks // n_seq
    chains = [(sq, g) for sq in range(n_seq) for g in range(n_groups)]
    s_g = {(sq, g): state_ref[sq, g] for sq, g in chains}
    y_u = {}
    for step in range(chunks_per_seq):
        ci_of = lambda sq: sq * chunks_per_seq + step
        pc = [pc_ref[sq, pl.ds(pc_row0 + step, 1), :] for sq in range(n_seq)]
        s_b = {key: s_g[key].astype(BF16) for key in chains}
        u_b = {(sq, g): (_dot_nt(w_u[ci_of(sq), g], s_b[sq, g]) + u0_u[ci_of(sq), g]).astype(BF16)
               for sq, g in chains}
        for sq, g in chains:
            ci = ci_of(sq)
            y_u[ci, g] = (_dot_nt(tile(rt, ci, g), s_b[sq, g])
                          + _dot(g_u[ci, g][c:, :lanes].astype(BF16), pair_diag(u_b[sq, g])) + y0_u[ci, g])
        s_g = {(sq, g): s_g[sq, g] * pc[sq][:, g * lanes:(g + 1) * lanes]
               + jnp.where(same_head,
                           _dot_tn(jnp.concatenate([u_b[sq, g], tile(vb, ci_of(sq), g)], axis=0),
                                   jnp.concatenate([tile(bc, ci_of(sq), g), tile(kc, ci_of(sq), g)], axis=0)),
                           0.0)
               for sq, g in chains}
    for sq, g in chains:
        state_ref[sq, g] = s_g[sq, g]
    y_all = jnp.concatenate([jnp.concatenate([y_u[ci, g] for g in range(n_groups)], axis=1)
                             for ci in range(n_chunks)], axis=0)
    return (_head_norm(y_all, GN_EPS_RWKV) * gnw_ref[...] + gnb_ref[...] + bonus) * gate


RET_STREAMS = 6
WKV_STREAMS = 9


def _prep_kernel(x_ref, pos_ref, g_ref, w_ret_ref, w_rkv_ref, w_lora_ref, freq_ref, xi_ref, zeta_ref,
                 mu_rkv_ref, mu_lora_ref, w0_ref, wup_ref, a0_ref, aup_ref, gup_ref, kk_ref, ka_ref, rk_ref,
                 tri_ref, *refs, tiles_per_seq):
    (q_ref, k_ref, qin_ref, kout_ref, vr_ref, gsilu_ref,
     at_ref, rt_ref, bt_ref, kt_ref, bc_ref, kc_ref, vw_ref, bonus_ref, gate_ref, pc_ref,
     carry_rkv_ref, carry_lora_ref) = refs
    tm = x_ref.shape[0]
    w = RWKV_WIDTH

    @pl.when(pl.program_id(0) % tiles_per_seq == 0)
    def _():
        carry_rkv_ref[...] = jnp.zeros_like(carry_rkv_ref)
        carry_lora_ref[...] = jnp.zeros_like(carry_lora_ref)

    h = _rms(x_ref[...], g_ref[...]).astype(BF16)

    wr = RET_WIDTH
    lanes = 2 * HEAD_DIM
    half = HEAD_DIM // 2
    p_q = _dot(h, w_ret_ref[:, :wr])
    p_k = _dot(h, w_ret_ref[:, wr:2 * wr])
    p_r = _dot(h, w_rkv_ref[:, :w])
    plora_raw = _dot(h, w_lora_ref[...])
    p_kw = _dot(h, w_rkv_ref[:, w:2 * w])
    p_vw = _dot(h, w_rkv_ref[:, 2 * w:])

    ang = pos_ref[...].astype(F32) * freq_ref[...]
    first_half = (lax.broadcasted_iota(jnp.int32, (tm, lanes), 1) % HEAD_DIM) < half
    cos = jnp.cos(ang)
    sin = jnp.sin(ang)
    sin = jnp.where(first_half, -sin, sin)
    k_scale = HEAD_DIM ** -0.5

    def rope(t, cos_t, sin_t):
        tiles = []
        for lo in range(0, wr, lanes):
            x = t[:, lo:lo + lanes]
            partner = jnp.where(first_half, pltpu.roll(x, lanes - half, 1), pltpu.roll(x, half, 1))
            tiles.append(x * cos_t + partner * sin_t)
        return jnp.concatenate(tiles, axis=1)

    def chunk_scaled(dst_ref, t, table_ref):
        for lo in range(0, tm, RET_CHUNK):
            dst_ref[lo:lo + RET_CHUNK, :] = (t[lo:lo + RET_CHUNK] * table_ref[...]).astype(BF16)

    q = rope(p_q, cos, sin)
    q_ref[...] = q.astype(BF16)
    chunk_scaled(qin_ref, q, xi_ref)

    kr = rope(p_k, cos * k_scale, sin * k_scale)
    k_ref[...] = kr.astype(BF16)
    chunk_scaled(kout_ref, kr, zeta_ref)

    p_v = _dot(h, w_ret_ref[:, 2 * wr:3 * wr])
    p_g = _dot(h, w_ret_ref[:, 3 * wr:])
    r = _shift(p_r, carry_rkv_ref, mu_rkv_ref, 0)
    plora = _shift(plora_raw, carry_lora_ref, mu_lora_ref, 0)
    w_lr = plora[:, :DECAY_LORA]
    a_lr = plora[:, DECAY_LORA:DECAY_LORA + AAA_LORA]
    g_lr = plora[:, DECAY_LORA + AAA_LORA:]
    lw = -math.exp(-0.5) * jax.nn.sigmoid(w0_ref[...] + _dot(jnp.tanh(w_lr).astype(BF16), wup_ref[...]))
    a_sig = jax.nn.sigmoid(a0_ref[...] + _dot(a_lr.astype(BF16), aup_ref[...]))
    gate_ref[...] = _dot(jax.nn.sigmoid(g_lr).astype(BF16), gup_ref[...])

    k = _shift(p_kw, carry_rkv_ref, mu_rkv_ref, w)
    kk = k * kk_ref[...]
    k2 = k * (1.0 + (a_sig - 1.0) * ka_ref[...])
    kk = kk * (1.0 / jnp.maximum(jnp.sqrt(_head_sums(kk * kk)), 1e-12))
    a_vec = -kk
    b_vec = kk * a_sig

    v = _shift(p_vw, carry_rkv_ref, mu_rkv_ref, 2 * w)
    bonus_ref[...] = _head_sums(r * k2 * rk_ref[...]) * v
    vw_ref[...] = v.astype(BF16)

    c = WKV_CHUNK
    for ci in range(tm // c):
        rows = slice(ci * c, (ci + 1) * c)
        lw_c = lw[rows]
        cum = jnp.dot(tri_ref[...], lw_c, precision=lax.Precision.HIGHEST, preferred_element_type=F32)
        cum_end = cum[c - 1:, :]
        p_inv = jnp.exp(-cum)
        p_out = jnp.exp(cum_end - cum)
        at_ref[rows, :] = (a_vec[rows] * jnp.exp(cum - lw_c)).astype(BF16)
        rt_ref[rows, :] = (r[rows] * jnp.exp(cum)).astype(BF16)
        bt_ref[rows, :] = (b_vec[rows] * p_inv).astype(BF16)
        kt_ref[rows, :] = (k2[rows] * p_inv).astype(BF16)
        bc_ref[rows, :] = (b_vec[rows] * p_out).astype(BF16)
        kc_ref[rows, :] = (k2[rows] * p_out).astype(BF16)
        pc_ref[ci:ci + 1, :] = jnp.exp(cum_end)

    vr_ref[...] = p_v.astype(BF16)
    gsilu_ref[...] = p_g * jax.nn.sigmoid(p_g)


def _prep(x2d, pos2d, seq_len, consts, tm=PREP_ROWS):
    t = x2d.shape[0]
    assert t % tm == 0 and seq_len % tm == 0 and tm % RET_CHUNK == 0 and tm % WKV_CHUNK == 0
    row = lambda i: (i, 0)
    width = RET_WIDTH
    out_dtypes = [BF16] * 5 + [F32] + [BF16] * 7 + [F32] * 2
    assert len(out_dtypes) == RET_STREAMS + WKV_STREAMS
    n_pc = tm // WKV_CHUNK
    return pl.pallas_call(
        functools.partial(_prep_kernel, tiles_per_seq=seq_len // tm),
        grid=(t // tm,),
        in_specs=[pl.BlockSpec((tm, D_MODEL), row), pl.BlockSpec((tm, 1), row)] + [_full(a.shape) for a in consts],
        out_specs=[pl.BlockSpec((tm, width), row) for _ in out_dtypes] + [pl.BlockSpec((n_pc, width), row)],
        out_shape=[jax.ShapeDtypeStruct((t, width), dt) for dt in out_dtypes]
                  + [jax.ShapeDtypeStruct((t // WKV_CHUNK, width), F32)],
        scratch_shapes=[pltpu.VMEM((1, 3 * RWKV_WIDTH), F32), pltpu.VMEM((1, LORA_WIDTH), F32)],
        compiler_params=_params(1),
        name="prep",
    )(x2d, pos2d, *consts)


def _mixers_kernel(*refs):
    ret_in, refs = refs[:RET_STREAMS], refs[RET_STREAMS:]
    wkv_in, refs = refs[:WKV_STREAMS], refs[WKV_STREAMS:]
    (pc_ref, dmask_ref, gc_ref, same_head_ref, ret_gnw_ref, ret_gnb_ref, gnw_ref, gnb_ref, mask_ref, eye_ref,
     o_ref, ret_state_ref, wkv_state_ref) = refs
    ti = pl.program_id(1)

    @pl.when(ti == 0)
    def _():
        ret_state_ref[...] = jnp.zeros_like(ret_state_ref)
        wkv_state_ref[...] = jnp.zeros_like(wkv_state_ref)

    stacked = lambda ref: ref[...].reshape(ref.shape[0] * ref.shape[1], ref.shape[2])
    y_ret = _retention_body(*[stacked(r) for r in ret_in], dmask_ref, gc_ref, same_head_ref, ret_gnw_ref,
                            ret_gnb_ref, ret_state_ref)
    chunks_per_block = o_ref.shape[1] // WKV_CHUNK
    y_rwkv = _wkv_body(*[stacked(r) for r in wkv_in], pc_ref, ti * chunks_per_block, gnw_ref, gnb_ref, mask_ref,
                       eye_ref, same_head_ref, wkv_state_ref)
    o_ref[...] = jnp.concatenate([y_ret, y_rwkv], axis=1).astype(o_ref.dtype).reshape(o_ref.shape)


def _mixers(streams, pc, dmask_pairs, gc_w, ret_gn_w, ret_gn_b, gn_w, gn_b):
    b, s, width = streams[0].shape
    n_seq, tb, c = MIX_SEQS, MIX_BLOCK, WKV_CHUNK
    assert b % n_seq == 0 and s % tb == 0 and tb % RET_CHUNK == 0 and tb % c == 0
    idx = jnp.arange(c)
    incl = (idx[:, None] >= idx[None, :]).astype(F32)
    strict = (idx[:, None] > idx[None, :]).astype(F32)
    mask = jnp.concatenate([jnp.tile(strict, (1, 4)), jnp.tile(incl, (1, 4))], axis=0)
    eye = jnp.tile(jnp.eye(c, dtype=F32), (1, 2))
    head_of_lane = jnp.arange(2 * HEAD_DIM) // HEAD_DIM
    state_mask = (head_of_lane[:, None] == head_of_lane[None, :]).astype(F32)
    consts = [dmask_pairs, gc_w, state_mask, ret_gn_w, ret_gn_b, gn_w, gn_b, mask, eye]
    blk = lambda bi, ti: (bi, ti, 0)
    state = pltpu.VMEM((n_seq, N_HEADS // 2, 2 * HEAD_DIM, 2 * HEAD_DIM), F32)
    return pl.pallas_call(
        _mixers_kernel,
        grid=(b // n_seq, s // tb),
        in_specs=[pl.BlockSpec((n_seq, tb, width), blk) for _ in streams]
                 + [pl.BlockSpec((n_seq, s // c, width), lambda bi, ti: (bi, 0, 0))]
                 + [_full(a.shape) for a in consts],
        out_specs=pl.BlockSpec((n_seq, tb, RET_WIDTH + RWKV_WIDTH), blk),
        out_shape=jax.ShapeDtypeStruct((b, s, RET_WIDTH + RWKV_WIDTH), BF16),
        scratch_shapes=[state, state],
        compiler_params=_params(2),
        name="mixers",
    )(*streams, pc, *consts)


def _mem_kv_kernel(m_ref, g_ref, w_ref, o_ref):
    h = _rms(m_ref[...], g_ref[...]).astype(BF16)
    o_ref[...] = _dot(h, w_ref[...]).astype(o_ref.dtype)


def _mem_kv(mem2d, g, w_kv, tm=1024):
    t = mem2d.shape[0]
    n = w_kv.shape[1]
    tm = min(tm, t)
    assert t % tm == 0
    return pl.pallas_call(
        _mem_kv_kernel,
        grid=(t // tm,),
        in_specs=[pl.BlockSpec((tm, D_MODEL), lambda i: (i, 0)), _full(g.shape), _full(w_kv.shape)],
        out_specs=pl.BlockSpec((tm, n), lambda i: (i, 0)),
        out_shape=jax.ShapeDtypeStruct((t, n), BF16),
        compiler_params=_params(1),
        name="mem_kv",
    )(mem2d, g, w_kv)


def _post_mix_kernel(x_ref, y_ref, wout_ref, gx_ref, wq_ref, kv_ref, wo_ref, o_ref):
    x1 = x_ref[0] + _dot(y_ref[0], wout_ref[...])
    q = _dot(_rms(x1, gx_ref[...]).astype(BF16), wq_ref[...]).astype(BF16)
    kv = kv_ref[0]
    heads = []
    for h in range(XATTN_HEADS):
        sl = slice(h * XATTN_HEAD_DIM, (h + 1) * XATTN_HEAD_DIM)
        s = _dot_nt(q[:, sl], kv[:, sl]) * (XATTN_HEAD_DIM ** -0.5)
        e = jnp.exp(s - jnp.max(s, axis=-1, keepdims=True))
        prob = e / jnp.sum(e, axis=-1, keepdims=True)
        heads.append(_dot(prob.astype(BF16), kv[:, D_MODEL + h * XATTN_HEAD_DIM:D_MODEL + (h + 1) * XATTN_HEAD_DIM]))
    o = jnp.concatenate(heads, axis=1).astype(BF16)
    o_ref[0] = x1 + _dot(o, wo_ref[...])


def _post_mix(x, y, w_out, g_x, w_q, kv, w_o, tm=1024):
    b, s, _ = x.shape
    m = kv.shape[1]
    assert s % tm == 0
    blk = lambda bi, ti: (bi, ti, 0)
    return pl.pallas_call(
        _post_mix_kernel,
        grid=(b, s // tm),
        in_specs=[pl.BlockSpec((1, tm, D_MODEL), blk), pl.BlockSpec((1, tm, y.shape[2]), blk),
                  _full(w_out.shape), _full(g_x.shape), _full(w_q.shape),
                  pl.BlockSpec((1, m, 2 * D_MODEL), lambda bi, ti: (bi, 0, 0)), _full(w_o.shape)],
        out_specs=pl.BlockSpec((1, tm, D_MODEL), blk),
        out_shape=jax.ShapeDtypeStruct((b, s, D_MODEL), F32),
        compiler_params=_params(2),
        name="post_mix",
    )(x, y, w_out, g_x, w_q, kv, w_o)


def _mlp_kernel(x_ref, g_ref, wup_ref, wdown_ref, gf_ref, o_ref, *, tf, final_norm):
    x = x_ref[...]
    h = _rms(x, g_ref[...]).astype(BF16)
    acc = x
    for j in range(D_FF // tf):
        u = jnp.maximum(_dot(h, wup_ref[:, j * tf:(j + 1) * tf]), 0.0)
        acc = acc + _dot((u * u).astype(BF16), wdown_ref[j * tf:(j + 1) * tf, :])
    o_ref[...] = _rms(acc, gf_ref[...]) if final_norm else acc


def _mlp(x2d, g, w_up, w_down, g_final, final_norm, tm=1024, tf=512):
    t = x2d.shape[0]
    assert t % tm == 0 and D_FF % tf == 0
    row = lambda i: (i, 0)
    return pl.pallas_call(
        functools.partial(_mlp_kernel, tf=tf, final_norm=final_norm),
        grid=(t // tm,),
        in_specs=[pl.BlockSpec((tm, D_MODEL), row), _full(g.shape), _full(w_up.shape), _full(w_down.shape),
                  _full(g_final.shape)],
        out_specs=pl.BlockSpec((tm, D_MODEL), row),
        out_shape=jax.ShapeDtypeStruct((t, D_MODEL), F32),
        compiler_params=_params(1),
        name="mlp",
    )(x2d, g, w_up, w_down, g_final)


def kernel(x, mem, positions, norm_mix, w_in, ret_gn_w, ret_gn_b, rwkv_mu, rwkv_w0, rwkv_w_up, rwkv_a0,
           rwkv_a_up, rwkv_g_up, rwkv_k_k, rwkv_k_a, rwkv_r_k, rwkv_gn_w, rwkv_gn_b, w_out, norm_xattn,
           norm_mem, xattn_w_q, xattn_w_kv, xattn_w_o, norm_mlp, mlp_w_up, mlp_w_down, norm_final):
    b, s, dm = x.shape
    n_layers = w_in.shape[0]
    rkv_end = RET_PROJ + 3 * RWKV_WIDTH
    freq, xi_w, zeta_w, dmask_pairs, gc_w = _retention_consts()
    tri = (jnp.arange(WKV_CHUNK)[:, None] >= jnp.arange(WKV_CHUNK)[None, :]).astype(F32)
    row = lambda a: a.reshape(1, -1)
    for l in range(n_layers):
        mu = rwkv_mu[l]
        prep_consts = [row(norm_mix[l]), w_in[l, :, :RET_PROJ].astype(BF16),
                       w_in[l, :, RET_PROJ:rkv_end].astype(BF16), w_in[l, :, rkv_end:].astype(BF16),
                       freq, xi_w, zeta_w, row(mu[:3 * RWKV_WIDTH]), row(mu[3 * RWKV_WIDTH:]),
                       row(rwkv_w0[l]), rwkv_w_up[l].astype(BF16), row(rwkv_a0[l]), rwkv_a_up[l].astype(BF16),
                       rwkv_g_up[l].astype(BF16), row(rwkv_k_k[l]), row(rwkv_k_a[l]), row(rwkv_r_k[l]), tri]
        *streams, pc = _prep(x.reshape(b * s, dm), positions.reshape(b * s, 1), s, prep_consts)
        y = _mixers([a.reshape(b, s, -1) for a in streams], pc.reshape(b, s // WKV_CHUNK, -1), dmask_pairs, gc_w,
                    row(ret_gn_w[l]), row(ret_gn_b[l]), row(rwkv_gn_w[l]), row(rwkv_gn_b[l]))
        kv = _mem_kv(mem.reshape(-1, dm), norm_mem[l][None, :], xattn_w_kv[l].astype(BF16))
        x = _post_mix(x, y, w_out[l].astype(BF16), norm_xattn[l][None, :],
                      xattn_w_q[l].astype(BF16), kv.reshape(b, -1, 2 * dm), xattn_w_o[l].astype(BF16))
        x = _mlp(x.reshape(b * s, dm), norm_mlp[l][None, :], mlp_w_up[l].astype(BF16),
                 mlp_w_down[l].astype(BF16), norm_final[None, :], l == n_layers - 1).reshape(b, s, dm)
    return x
```

```python
import functools
import math

import jax
import jax.numpy as jnp
from jax import lax
from jax.experimental import pallas as pl
from jax.experimental.pallas import tpu as pltpu

D_MODEL = 1024
HEAD_DIM = 64
RET_WIDTH = 512
RWKV_WIDTH = 512
N_HEADS = 8
RET_CHUNK = 128
ROPE_BASE = 10000.0
DECAY_LORA = 64
AAA_LORA = 64
GATE_LORA = 160
LORA_WIDTH = DECAY_LORA + AAA_LORA + GATE_LORA
RET_PROJ = 4 * RET_WIDTH
XATTN_HEADS = 4
XATTN_HEAD_DIM = D_MODEL // XATTN_HEADS
D_FF = 4 * D_MODEL
RMS_EPS = 1e-6
GN_EPS_RET = 1e-5
GN_EPS_RWKV = 64e-5

WKV_CHUNK = 64
MIX_SEQS = 2
MIX_BLOCK = 256
WKV_WAVE = 16
PREP_ROWS = 512
CAST_SLAB = 512
VMEM_LIMIT_BYTES = 56 * 1024 * 1024

BF16 = jnp.bfloat16
F32 = jnp.float32


def _dot(a, b):
    return jnp.dot(a, b, preferred_element_type=F32)


def _dot_nt(a, b):
    return lax.dot_general(a, b, (((1,), (1,)), ((), ())), preferred_element_type=F32)


def _dot_tn(a, b):
    return lax.dot_general(a, b, (((0,), (0,)), ((), ())), preferred_element_type=F32)


def _rms(x, g):
    return x * lax.rsqrt(jnp.mean(x * x, axis=-1, keepdims=True) + RMS_EPS) * g


def _params(n_axes):
    return pltpu.CompilerParams(dimension_semantics=("arbitrary",) * n_axes,
                                vmem_limit_bytes=VMEM_LIMIT_BYTES)


def _full(shape):
    zeros = (0,) * len(shape)
    return pl.BlockSpec(shape, lambda *_: zeros, pipeline_mode=pl.Buffered(1))


def _cast_weights_once(first_step, pairs):
    @pl.when(first_step)
    def _():
        for src_ref, dst_ref in pairs:
            n = src_ref.shape[1]
            for lo in range(0, n, CAST_SLAB):
                hi = min(lo + CAST_SLAB, n)
                dst_ref[:, lo:hi] = src_ref[:, lo:hi].astype(dst_ref.dtype)


def _head_sums(x):
    lanes = 2 * HEAD_DIM
    in_h0 = lax.broadcasted_iota(jnp.int32, (x.shape[0], lanes), 1) < HEAD_DIM
    tiles = []
    for lo in range(0, x.shape[1], lanes):
        t = x[:, lo:lo + lanes]
        s0 = jnp.sum(jnp.where(in_h0, t, 0.0), axis=-1, keepdims=True)
        s1 = jnp.sum(jnp.where(in_h0, 0.0, t), axis=-1, keepdims=True)
        tiles.append(jnp.where(in_h0, s0, s1))
    return jnp.concatenate(tiles, axis=1)


def _head_norm(y, eps):
    d = y - _head_sums(y) * (1.0 / HEAD_DIM)
    var = _head_sums(d * d) * (1.0 / HEAD_DIM)
    return d * lax.rsqrt(var + eps)


def _pair_diag(x):
    in_h0 = lax.broadcasted_iota(jnp.int32, x.shape, 1) < HEAD_DIM
    zero = jnp.zeros_like(x)
    return jnp.concatenate([jnp.where(in_h0, x, zero), jnp.where(in_h0, zero, x)], axis=0)


def _retention_body(q, k, q_in, k_out, vb, gsilu, dmask_ref, gc_ref, same_head_ref, gnw_ref, gnb_ref, state_ref):
    c = RET_CHUNK
    lanes = 2 * HEAD_DIM
    n_groups = RET_WIDTH // lanes
    n_chunks = q.shape[0] // c
    units = [(ci, g) for ci in range(n_chunks) for g in range(n_groups)]
    tile = lambda x, ci, g: x[ci * c:(ci + 1) * c, g * lanes:(g + 1) * lanes]
    s_u = {(ci, g): (_dot_nt(tile(q, ci, g), _pair_diag(tile(k, ci, g))) * dmask_ref[g]).astype(BF16)
           for ci, g in units}
    y_u = {(ci, g): _dot(s_u[ci, g], _pair_diag(tile(vb, ci, g))) for ci, g in units}
    kv_u = {(ci, g): _dot_tn(tile(k_out, ci, g), tile(vb, ci, g)) for ci, g in units}
    same_head = same_head_ref[...] > 0.5
    n_seq = state_ref.shape[0]
    chunks_per_seq = n_chunks // n_seq
    chains = [(sq, g) for sq in range(n_seq) for g in range(n_groups)]
    r_g = {(sq, g): state_ref[sq, g] for sq, g in chains}
    for step in range(chunks_per_seq):
        for sq, g in chains:
            ci = sq * chunks_per_seq + step
            y_u[ci, g] = y_u[ci, g] + _dot(tile(q_in, ci, g), r_g[sq, g].astype(BF16))
        r_g = {(sq, g): r_g[sq, g] * gc_ref[g] + jnp.where(same_head, kv_u[sq * chunks_per_seq + step, g], 0.0)
               for sq, g in chains}
    for sq, g in chains:
        state_ref[sq, g] = r_g[sq, g]
    y_all = jnp.concatenate([jnp.concatenate([y_u[ci, g] for g in range(n_groups)], axis=1)
                             for ci in range(n_chunks)], axis=0)
    return gsilu * (_head_norm(y_all, GN_EPS_RET) * gnw_ref[...] + gnb_ref[...])


def _retention_consts():
    c = RET_CHUNK
    half = HEAD_DIM // 2
    inv_freq = ROPE_BASE ** (-jnp.arange(half, dtype=F32) / half)
    freq = jnp.tile(inv_freq, 128 // half)[None, :]
    log_g = jnp.log(1.0 - 2.0 ** (-5.0 - jnp.arange(N_HEADS, dtype=F32)))
    idx = jnp.arange(c, dtype=F32)
    diff = idx[:, None] - idx[None, :]
    causal = diff >= 0
    dmask = jnp.where(causal[None], jnp.exp(log_g[:, None, None] * jnp.where(causal, diff, 0.0)[None]), 0.0)
    xi = jnp.exp(log_g[:, None] * (idx + 1.0)[None])
    zeta = jnp.exp(log_g[:, None] * (c - 1.0 - idx)[None])
    g_chunk = jnp.exp(log_g * c)
    xi_w = jnp.repeat(xi.T, HEAD_DIM, axis=1)
    zeta_w = jnp.repeat(zeta.T, HEAD_DIM, axis=1)
    dmask_pairs = jnp.concatenate([dmask[0::2], dmask[1::2]], axis=2)
    gc_rows = jnp.repeat(g_chunk, HEAD_DIM).reshape(N_HEADS // 2, 2 * HEAD_DIM, 1)
    gc_w = jnp.broadcast_to(gc_rows, (N_HEADS // 2, 2 * HEAD_DIM, 2 * HEAD_DIM))
    return freq, xi_w, zeta_w, dmask_pairs, gc_w


def _shift(p, carry_ref, mu_ref, lo):
    cols = slice(lo, lo + p.shape[1])
    rows = lax.broadcasted_iota(jnp.int32, p.shape, 0)
    prev = jnp.where(rows == 0, carry_ref[:, cols], pltpu.roll(p, 1, 0))
    carry_ref[:, cols] = p[p.shape[0] - 1:, :]
    return p + (prev - p) * mu_ref[:, cols]


def _wkv_body(at, rt, bt, kt, bc, kc, vb, bonus, gate, pc_ref, pc_row0, gnw_ref, gnb_ref, mask_ref, eye_ref,
              state_mask_ref, state_ref):
    c = WKV_CHUNK
    w = RWKV_WIDTH
    lanes = 2 * HEAD_DIM
    n_groups = w // lanes
    n_chunks = at.shape[0] // c
    eye = eye_ref[...]
    tril = mask_ref[...] > 0.5

    pair_diag = _pair_diag
    all_units = [(ci, g) for ci in range(n_chunks) for g in range(n_groups)]
    tile = lambda x, ci, g: x[ci * c:(ci + 1) * c, g * lanes:(g + 1) * lanes]
    g_u, w_u, u0_u, y0_u = {}, {}, {}, {}
    for lo in range(0, len(all_units), WKV_WAVE):
        units = all_units[lo:lo + WKV_WAVE]
        vd_u = {(ci, g): pair_diag(tile(vb, ci, g)) for ci, g in units}
        for ci, g in units:
            lhs = jnp.concatenate([tile(at, ci, g), tile(rt, ci, g)], axis=0)
            rhs = jnp.concatenate([pair_diag(tile(bt, ci, g)), pair_diag(tile(kt, ci, g))], axis=0)
            g_u[ci, g] = jnp.where(tril, _dot_nt(lhs, rhs), 0.0)
        m_u = {u: g_u[u][:c, :lanes] for u in units}
        t_u = {u: eye + m_u[u] for u in units}
        mb_u = {u: m_u[u].astype(BF16) for u in units}
        steps = 1
        while 2 * steps < c:
            mb_u = {u: _dot(mb_u[u], pair_diag(mb_u[u])).astype(BF16) for u in units}
            t_u = {u: t_u[u] + _dot(t_u[u].astype(BF16), pair_diag(mb_u[u])) for u in units}
            steps *= 2
        t_u = {u: t_u[u].astype(BF16) for u in units}
        w_u.update({(ci, g): _dot(t_u[ci, g], pair_diag(tile(at, ci, g))).astype(BF16) for ci, g in units})
        akv_u = {u: _dot(g_u[u][:c, lanes:].astype(BF16), vd_u[u]).astype(BF16) for u in units}
        u0_u.update({u: _dot(t_u[u], pair_diag(akv_u[u])) for u in units})
        y0_u.update({u: _dot(g_u[u][c:, lanes:].astype(BF16), vd_u[u]) for u in units})

    same_head = state_mask_ref[...] > 0.5
    n_seq = state_ref.shape[0]
    chunks_per_seq = n_chunks // n_seq
    chains = [(sq, g) for sq in range(n_seq) for g in range(n_groups)]
    s_g = {(sq, g): state_ref[sq, g] for sq, g in chains}
    y_u = {}
    for step in range(chunks_per_seq):
        ci_of = lambda sq: sq * chunks_per_seq + step
        pc = [pc_ref[sq, pl.ds(pc_row0 + step, 1), :] for sq in range(n_seq)]
        s_b = {key: s_g[key].astype(BF16) for key in chains}
        u_b = {(sq, g): (_dot_nt(w_u[ci_of(sq), g], s_b[sq, g]) + u0_u[ci_of(sq), g]).astype(BF16)
               for sq, g in chains}
        for sq, g in chains:
            ci = ci_of(sq)
            y_u[ci, g] = (_dot_nt(tile(rt, ci, g), s_b[sq, g])
                          + _dot(g_u[ci, g][c:, :lanes].astype(BF16), pair_diag(u_b[sq, g])) + y0_u[ci, g])
        s_g = {(sq, g): s_g[sq, g] * pc[sq][:, g * lanes:(g + 1) * lanes]
               + jnp.where(same_head,
                           _dot_tn(jnp.concatenate([u_b[sq, g], tile(vb, ci_of(sq), g)], axis=0),
                                   jnp.concatenate([tile(bc, ci_of(sq), g), tile(kc, ci_of(sq), g)], axis=0)),
                           0.0)
               for sq, g in chains}
    for sq, g in chains:
        state_ref[sq, g] = s_g[sq, g]
    y_all = jnp.concatenate([jnp.concatenate([y_u[ci, g] for g in range(n_groups)], axis=1)
                             for ci in range(n_chunks)], axis=0)
    return (_head_norm(y_all, GN_EPS_RWKV) * gnw_ref[...] + gnb_ref[...] + bonus) * gate


RET_STREAMS = 6
WKV_STREAMS = 9


def _prep_kernel(x_ref, pos_ref, g_ref, w_in_ref, freq_ref, xi_ref, zeta_ref,
                 mu_rkv_ref, mu_lora_ref, w0_ref, wup_ref, a0_ref, aup_ref, gup_ref, kk_ref, ka_ref, rk_ref,
                 tri_ref, *refs, tiles_per_seq):
    (q_ref, k_ref, qin_ref, kout_ref, vr_ref, gsilu_ref,
     at_ref, rt_ref, bt_ref, kt_ref, bc_ref, kc_ref, vw_ref, bonus_ref, gate_ref, pc_ref,
     w_ref, carry_rkv_ref, carry_lora_ref) = refs
    tm = x_ref.shape[0]
    w = RWKV_WIDTH
    rkv0 = RET_PROJ
    _cast_weights_once(pl.program_id(0) == 0, [(w_in_ref, w_ref)])

    @pl.when(pl.program_id(0) % tiles_per_seq == 0)
    def _():
        carry_rkv_ref[...] = jnp.zeros_like(carry_rkv_ref)
        carry_lora_ref[...] = jnp.zeros_like(carry_lora_ref)

    h = _rms(x_ref[...], g_ref[...]).astype(BF16)

    wr = RET_WIDTH
    lanes = 2 * HEAD_DIM
    half = HEAD_DIM // 2
    p_q = _dot(h, w_ref[:, :wr])
    p_k = _dot(h, w_ref[:, wr:2 * wr])
    p_r = _dot(h, w_ref[:, rkv0:rkv0 + w])
    plora_raw = _dot(h, w_ref[:, rkv0 + 3 * w:])
    p_kw = _dot(h, w_ref[:, rkv0 + w:rkv0 + 2 * w])
    p_vw = _dot(h, w_ref[:, rkv0 + 2 * w:rkv0 + 3 * w])

    ang = pos_ref[...].astype(F32) * freq_ref[...]
    first_half = (lax.broadcasted_iota(jnp.int32, (tm, lanes), 1) % HEAD_DIM) < half
    cos = jnp.cos(ang)
    sin = jnp.sin(ang)
    sin = jnp.where(first_half, -sin, sin)
    k_scale = HEAD_DIM ** -0.5

    def rope(t, cos_t, sin_t):
        tiles = []
        for lo in range(0, wr, lanes):
            x = t[:, lo:lo + lanes]
            partner = jnp.where(first_half, pltpu.roll(x, lanes - half, 1), pltpu.roll(x, half, 1))
            tiles.append(x * cos_t + partner * sin_t)
        return jnp.concatenate(tiles, axis=1)

    def chunk_scaled(dst_ref, t, table_ref):
        for lo in range(0, tm, RET_CHUNK):
            dst_ref[lo:lo + RET_CHUNK, :] = (t[lo:lo + RET_CHUNK] * table_ref[...]).astype(BF16)

    q = rope(p_q, cos, sin)
    q_ref[...] = q.astype(BF16)
    chunk_scaled(qin_ref, q, xi_ref)

    kr = rope(p_k, cos * k_scale, sin * k_scale)
    k_ref[...] = kr.astype(BF16)
    chunk_scaled(kout_ref, kr, zeta_ref)

    p_v = _dot(h, w_ref[:, 2 * wr:3 * wr])
    p_g = _dot(h, w_ref[:, 3 * wr:RET_PROJ])
    r = _shift(p_r, carry_rkv_ref, mu_rkv_ref, 0)
    plora = _shift(plora_raw, carry_lora_ref, mu_lora_ref, 0)
    w_lr = plora[:, :DECAY_LORA]
    a_lr = plora[:, DECAY_LORA:DECAY_LORA + AAA_LORA]
    g_lr = plora[:, DECAY_LORA + AAA_LORA:]
    lw = -math.exp(-0.5) * jax.nn.sigmoid(w0_ref[...] + _dot(jnp.tanh(w_lr).astype(BF16), wup_ref[...]))
    a_sig = jax.nn.sigmoid(a0_ref[...] + _dot(a_lr.astype(BF16), aup_ref[...]))
    gate_ref[...] = _dot(jax.nn.sigmoid(g_lr).astype(BF16), gup_ref[...])

    k = _shift(p_kw, carry_rkv_ref, mu_rkv_ref, w)
    kk = k * kk_ref[...]
    k2 = k * (1.0 + (a_sig - 1.0) * ka_ref[...])
    kk = kk * (1.0 / jnp.maximum(jnp.sqrt(_head_sums(kk * kk)), 1e-12))
    a_vec = -kk
    b_vec = kk * a_sig

    v = _shift(p_vw, carry_rkv_ref, mu_rkv_ref, 2 * w)
    bonus_ref[...] = _head_sums(r * k2 * rk_ref[...]) * v
    vw_ref[...] = v.astype(BF16)

    c = WKV_CHUNK
    for ci in range(tm // c):
        rows = slice(ci * c, (ci + 1) * c)
        lw_c = lw[rows]
        cum = jnp.dot(tri_ref[...], lw_c, precision=lax.Precision.HIGHEST, preferred_element_type=F32)
        cum_end = cum[c - 1:, :]
        p_inv = jnp.exp(-cum)
        p_out = jnp.exp(cum_end - cum)
        at_ref[rows, :] = (a_vec[rows] * jnp.exp(cum - lw_c)).astype(BF16)
        rt_ref[rows, :] = (r[rows] * jnp.exp(cum)).astype(BF16)
        bt_ref[rows, :] = (b_vec[rows] * p_inv).astype(BF16)
        kt_ref[rows, :] = (k2[rows] * p_inv).astype(BF16)
        bc_ref[rows, :] = (b_vec[rows] * p_out).astype(BF16)
        kc_ref[rows, :] = (k2[rows] * p_out).astype(BF16)
        pc_ref[ci:ci + 1, :] = jnp.exp(cum_end)

    vr_ref[...] = p_v.astype(BF16)
    gsilu_ref[...] = p_g * jax.nn.sigmoid(p_g)


def _prep(x2d, pos2d, seq_len, consts, tm=PREP_ROWS):
    t = x2d.shape[0]
    assert t % tm == 0 and seq_len % tm == 0 and tm % RET_CHUNK == 0 and tm % WKV_CHUNK == 0
    row = lambda i: (i, 0)
    width = RET_WIDTH
    out_dtypes = [BF16] * 5 + [F32] + [BF16] * 7 + [F32] * 2
    assert len(out_dtypes) == RET_STREAMS + WKV_STREAMS
    n_pc = tm // WKV_CHUNK
    return pl.pallas_call(
        functools.partial(_prep_kernel, tiles_per_seq=seq_len // tm),
        grid=(t // tm,),
        in_specs=[pl.BlockSpec((tm, D_MODEL), row), pl.BlockSpec((tm, 1), row)] + [_full(a.shape) for a in consts],
        out_specs=[pl.BlockSpec((tm, width), row) for _ in out_dtypes] + [pl.BlockSpec((n_pc, width), row)],
        out_shape=[jax.ShapeDtypeStruct((t, width), dt) for dt in out_dtypes]
                  + [jax.ShapeDtypeStruct((t // WKV_CHUNK, width), F32)],
        scratch_shapes=[pltpu.VMEM(consts[1].shape, BF16), pltpu.VMEM((1, 3 * RWKV_WIDTH), F32),
                        pltpu.VMEM((1, LORA_WIDTH), F32)],
        compiler_params=_params(1),
        name="prep",
    )(x2d, pos2d, *consts)


def _mixers_kernel(*refs):
    ret_in, refs = refs[:RET_STREAMS], refs[RET_STREAMS:]
    wkv_in, refs = refs[:WKV_STREAMS], refs[WKV_STREAMS:]
    (pc_ref, dmask_ref, gc_ref, same_head_ref, ret_gnw_ref, ret_gnb_ref, gnw_ref, gnb_ref, mask_ref, eye_ref,
     o_ref, ret_state_ref, wkv_state_ref) = refs
    ti = pl.program_id(1)

    @pl.when(ti == 0)
    def _():
        ret_state_ref[...] = jnp.zeros_like(ret_state_ref)
        wkv_state_ref[...] = jnp.zeros_like(wkv_state_ref)

    stacked = lambda ref: ref[...].reshape(ref.shape[0] * ref.shape[1], ref.shape[2])
    y_ret = _retention_body(*[stacked(r) for r in ret_in], dmask_ref, gc_ref, same_head_ref, ret_gnw_ref,
                            ret_gnb_ref, ret_state_ref)
    chunks_per_block = o_ref.shape[1] // WKV_CHUNK
    y_rwkv = _wkv_body(*[stacked(r) for r in wkv_in], pc_ref, ti * chunks_per_block, gnw_ref, gnb_ref, mask_ref,
                       eye_ref, same_head_ref, wkv_state_ref)
    o_ref[...] = jnp.concatenate([y_ret, y_rwkv], axis=1).astype(o_ref.dtype).reshape(o_ref.shape)


def _mixers(streams, pc, dmask_pairs, gc_w, ret_gn_w, ret_gn_b, gn_w, gn_b):
    b, s, width = streams[0].shape
    n_seq, tb, c = MIX_SEQS, MIX_BLOCK, WKV_CHUNK
    assert b % n_seq == 0 and s % tb == 0 and tb % RET_CHUNK == 0 and tb % c == 0
    idx = jnp.arange(c)
    incl = (idx[:, None] >= idx[None, :]).astype(F32)
    strict = (idx[:, None] > idx[None, :]).astype(F32)
    mask = jnp.concatenate([jnp.tile(strict, (1, 4)), jnp.tile(incl, (1, 4))], axis=0)
    eye = jnp.tile(jnp.eye(c, dtype=F32), (1, 2))
    head_of_lane = jnp.arange(2 * HEAD_DIM) // HEAD_DIM
    state_mask = (head_of_lane[:, None] == head_of_lane[None, :]).astype(F32)
    consts = [dmask_pairs, gc_w, state_mask, ret_gn_w, ret_gn_b, gn_w, gn_b, mask, eye]
    blk = lambda bi, ti: (bi, ti, 0)
    state = pltpu.VMEM((n_seq, N_HEADS // 2, 2 * HEAD_DIM, 2 * HEAD_DIM), F32)
    return pl.pallas_call(
        _mixers_kernel,
        grid=(b // n_seq, s // tb),
        in_specs=[pl.BlockSpec((n_seq, tb, width), blk) for _ in streams]
                 + [pl.BlockSpec((n_seq, s // c, width), lambda bi, ti: (bi, 0, 0))]
                 + [_full(a.shape) for a in consts],
        out_specs=pl.BlockSpec((n_seq, tb, RET_WIDTH + RWKV_WIDTH), blk),
        out_shape=jax.ShapeDtypeStruct((b, s, RET_WIDTH + RWKV_WIDTH), BF16),
        scratch_shapes=[state, state],
        compiler_params=_params(2),
        name="mixers",
    )(*streams, pc, *consts)


def _mem_kv_kernel(m_ref, g_ref, w_f32_ref, o_ref, w_ref):
    _cast_weights_once(pl.program_id(0) == 0, [(w_f32_ref, w_ref)])
    h = _rms(m_ref[...], g_ref[...]).astype(BF16)
    o_ref[...] = _dot(h, w_ref[...]).astype(o_ref.dtype)


def _mem_kv(mem2d, g, w_kv, tm=1024):
    t = mem2d.shape[0]
    n = w_kv.shape[1]
    tm = min(tm, t)
    assert t % tm == 0
    return pl.pallas_call(
        _mem_kv_kernel,
        grid=(t // tm,),
        in_specs=[pl.BlockSpec((tm, D_MODEL), lambda i: (i, 0)), _full(g.shape), _full(w_kv.shape)],
        out_specs=pl.BlockSpec((tm, n), lambda i: (i, 0)),
        out_shape=jax.ShapeDtypeStruct((t, n), BF16),
        scratch_shapes=[pltpu.VMEM(w_kv.shape, BF16)],
        compiler_params=_params(1),
        name="mem_kv",
    )(mem2d, g, w_kv)


def _post_mix_kernel(x_ref, y_ref, wout_f32_ref, gx_ref, wq_f32_ref, kv_ref, wo_f32_ref, o_ref,
                     wout_ref, wq_ref, wo_ref):
    first = jnp.logical_and(pl.program_id(0) == 0, pl.program_id(1) == 0)
    _cast_weights_once(first, [(wout_f32_ref, wout_ref), (wq_f32_ref, wq_ref), (wo_f32_ref, wo_ref)])
    x1 = x_ref[0] + _dot(y_ref[0], wout_ref[...])
    q = _dot(_rms(x1, gx_ref[...]).astype(BF16), wq_ref[...]).astype(BF16)
    kv = kv_ref[0]
    heads = []
    for h in range(XATTN_HEADS):
        sl = slice(h * XATTN_HEAD_DIM, (h + 1) * XATTN_HEAD_DIM)
        s = _dot_nt(q[:, sl], kv[:, sl]) * (XATTN_HEAD_DIM ** -0.5)
        e = jnp.exp(s - jnp.max(s, axis=-1, keepdims=True))
        prob = e / jnp.sum(e, axis=-1, keepdims=True)
        heads.append(_dot(prob.astype(BF16), kv[:, D_MODEL + h * XATTN_HEAD_DIM:D_MODEL + (h + 1) * XATTN_HEAD_DIM]))
    o = jnp.concatenate(heads, axis=1).astype(BF16)
    o_ref[0] = x1 + _dot(o, wo_ref[...])


def _post_mix(x, y, w_out, g_x, w_q, kv, w_o, tm=1024):
    b, s, _ = x.shape
    m = kv.shape[1]
    assert s % tm == 0
    blk = lambda bi, ti: (bi, ti, 0)
    return pl.pallas_call(
        _post_mix_kernel,
        grid=(b, s // tm),
        in_specs=[pl.BlockSpec((1, tm, D_MODEL), blk), pl.BlockSpec((1, tm, y.shape[2]), blk),
                  _full(w_out.shape), _full(g_x.shape), _full(w_q.shape),
                  pl.BlockSpec((1, m, 2 * D_MODEL), lambda bi, ti: (bi, 0, 0)), _full(w_o.shape)],
        out_specs=pl.BlockSpec((1, tm, D_MODEL), blk),
        out_shape=jax.ShapeDtypeStruct((b, s, D_MODEL), F32),
        scratch_shapes=[pltpu.VMEM(a.shape, BF16) for a in (w_out, w_q, w_o)],
        compiler_params=_params(2),
        name="post_mix",
    )(x, y, w_out, g_x, w_q, kv, w_o)


def _mlp_kernel(x_ref, g_ref, wup_ref, wdown_ref, gf_ref, o_ref, *, tf, final_norm):
    x = x_ref[...]
    h = _rms(x, g_ref[...]).astype(BF16)
    acc = x
    for j in range(D_FF // tf):
        u = jnp.maximum(_dot(h, wup_ref[:, j * tf:(j + 1) * tf]), 0.0)
        acc = acc + _dot((u * u).astype(BF16), wdown_ref[j * tf:(j + 1) * tf, :])
    o_ref[...] = _rms(acc, gf_ref[...]) if final_norm else acc


def _mlp(x2d, g, w_up, w_down, g_final, final_norm, tm=1024, tf=512):
    t = x2d.shape[0]
    assert t % tm == 0 and D_FF % tf == 0
    row = lambda i: (i, 0)
    return pl.pallas_call(
        functools.partial(_mlp_kernel, tf=tf, final_norm=final_norm),
        grid=(t // tm,),
        in_specs=[pl.BlockSpec((tm, D_MODEL), row), _full(g.shape), _full(w_up.shape), _full(w_down.shape),
                  _full(g_final.shape)],
        out_specs=pl.BlockSpec((tm, D_MODEL), row),
        out_shape=jax.ShapeDtypeStruct((t, D_MODEL), F32),
        compiler_params=_params(1),
        name="mlp",
    )(x2d, g, w_up, w_down, g_final)


def kernel(x, mem, positions, norm_mix, w_in, ret_gn_w, ret_gn_b, rwkv_mu, rwkv_w0, rwkv_w_up, rwkv_a0,
           rwkv_a_up, rwkv_g_up, rwkv_k_k, rwkv_k_a, rwkv_r_k, rwkv_gn_w, rwkv_gn_b, w_out, norm_xattn,
           norm_mem, xattn_w_q, xattn_w_kv, xattn_w_o, norm_mlp, mlp_w_up, mlp_w_down, norm_final):
    b, s, dm = x.shape
    n_layers = w_in.shape[0]
    freq, xi_w, zeta_w, dmask_pairs, gc_w = _retention_consts()
    tri = (jnp.arange(WKV_CHUNK)[:, None] >= jnp.arange(WKV_CHUNK)[None, :]).astype(F32)
    row = lambda a: a.reshape(1, -1)
    for l in range(n_layers):
        mu = rwkv_mu[l]
        prep_consts = [row(norm_mix[l]), w_in[l], freq, xi_w, zeta_w, row(mu[:3 * RWKV_WIDTH]), row(mu[3 * RWKV_WIDTH:]),
                       row(rwkv_w0[l]), rwkv_w_up[l].astype(BF16), row(rwkv_a0[l]), rwkv_a_up[l].astype(BF16),
                       rwkv_g_up[l].astype(BF16), row(rwkv_k_k[l]), row(rwkv_k_a[l]), row(rwkv_r_k[l]), tri]
        *streams, pc = _prep(x.reshape(b * s, dm), positions.reshape(b * s, 1), s, prep_consts)
        y = _mixers([a.reshape(b, s, -1) for a in streams], pc.reshape(b, s // WKV_CHUNK, -1), dmask_pairs, gc_w,
                    row(ret_gn_w[l]), row(ret_gn_b[l]), row(rwkv_gn_w[l]), row(rwkv_gn_b[l]))
        kv = _mem_kv(mem.reshape(-1, dm), norm_mem[l][None, :], xattn_w_kv[l])
        x = _post_mix(x, y, w_out[l], norm_xattn[l][None, :], xattn_w_q[l], kv.reshape(b, -1, 2 * dm),
                      xattn_w_o[l])
        x = _mlp(x.reshape(b * s, dm), norm_mlp[l][None, :], mlp_w_up[l].astype(BF16),
                 mlp_w_down[l].astype(BF16), norm_final[None, :], l == n_layers - 1).reshape(b, s, dm)
    return x
```

```python
import functools
import math

import jax
import jax.numpy as jnp
from jax import lax
from jax.experimental import pallas as pl
from jax.experimental.pallas import tpu as pltpu

D_MODEL = 1024
HEAD_DIM = 64
RET_WIDTH = 512
RWKV_WIDTH = 512
N_HEADS = 8
RET_CHUNK = 128
ROPE_BASE = 10000.0
DECAY_LORA = 64
AAA_LORA = 64
GATE_LORA = 160
LORA_WIDTH = DECAY_LORA + AAA_LORA + GATE_LORA
RET_PROJ = 4 * RET_WIDTH
XATTN_HEADS = 4
XATTN_HEAD_DIM = D_MODEL // XATTN_HEADS
D_FF = 4 * D_MODEL
RMS_EPS = 1e-6
GN_EPS_RET = 1e-5
GN_EPS_RWKV = 64e-5

WKV_CHUNK = 64
MIX_SEQS = 2
MIX_BLOCK = 256
WKV_WAVE = 16
PREP_ROWS = 512
CAST_SLAB = 512
VMEM_LIMIT_BYTES = 56 * 1024 * 1024

BF16 = jnp.bfloat16
F32 = jnp.float32


def _dot(a, b):
    return jnp.dot(a, b, preferred_element_type=F32)


def _dot_nt(a, b):
    return lax.dot_general(a, b, (((1,), (1,)), ((), ())), preferred_element_type=F32)


def _dot_tn(a, b):
    return lax.dot_general(a, b, (((0,), (0,)), ((), ())), preferred_element_type=F32)


def _rms(x, g):
    return x * lax.rsqrt(jnp.mean(x * x, axis=-1, keepdims=True) + RMS_EPS) * g


def _params(n_axes):
    return pltpu.CompilerParams(dimension_semantics=("arbitrary",) * n_axes,
                                vmem_limit_bytes=VMEM_LIMIT_BYTES)


def _full(shape):
    zeros = (0,) * len(shape)
    return pl.BlockSpec(shape, lambda *_: zeros, pipeline_mode=pl.Buffered(1))


def _cast_weights_once(first_step, pairs):
    @pl.when(first_step)
    def _():
        for src_ref, dst_ref in pairs:
            n = src_ref.shape[1]
            for lo in range(0, n, CAST_SLAB):
                hi = min(lo + CAST_SLAB, n)
                dst_ref[:, lo:hi] = src_ref[:, lo:hi].astype(dst_ref.dtype)


def _head_sums(x):
    lanes = 2 * HEAD_DIM
    in_h0 = lax.broadcasted_iota(jnp.int32, (x.shape[0], lanes), 1) < HEAD_DIM
    tiles = []
    for lo in range(0, x.shape[1], lanes):
        t = x[:, lo:lo + lanes]
        s0 = jnp.sum(jnp.where(in_h0, t, 0.0), axis=-1, keepdims=True)
        s1 = jnp.sum(jnp.where(in_h0, 0.0, t), axis=-1, keepdims=True)
        tiles.append(jnp.where(in_h0, s0, s1))
    return jnp.concatenate(tiles, axis=1)


def _head_norm(y, eps):
    d = y - _head_sums(y) * (1.0 / HEAD_DIM)
    var = _head_sums(d * d) * (1.0 / HEAD_DIM)
    return d * lax.rsqrt(var + eps)


def _pair_diag(x):
    in_h0 = lax.broadcasted_iota(jnp.int32, x.shape, 1) < HEAD_DIM
    zero = jnp.zeros_like(x)
    return jnp.concatenate([jnp.where(in_h0, x, zero), jnp.where(in_h0, zero, x)], axis=0)


def _retention_body(q, k, q_in, k_out, vb, gsilu, dmask_ref, gc_ref, same_head_ref, gnw_ref, gnb_ref, state_ref):
    c = RET_CHUNK
    lanes = 2 * HEAD_DIM
    n_groups = RET_WIDTH // lanes
    n_chunks = q.shape[0] // c
    units = [(ci, g) for ci in range(n_chunks) for g in range(n_groups)]
    tile = lambda x, ci, g: x[ci * c:(ci + 1) * c, g * lanes:(g + 1) * lanes]
    s_u = {(ci, g): (_dot_nt(tile(q, ci, g), _pair_diag(tile(k, ci, g))) * dmask_ref[g]).astype(BF16)
           for ci, g in units}
    y_u = {(ci, g): _dot(s_u[ci, g], _pair_diag(tile(vb, ci, g))) for ci, g in units}
    kv_u = {(ci, g): _dot_tn(tile(k_out, ci, g), tile(vb, ci, g)) for ci, g in units}
    same_head = same_head_ref[...] > 0.5
    n_seq = state_ref.shape[0]
    chunks_per_seq = n_chunks // n_seq
    chains = [(sq, g) for sq in range(n_seq) for g in range(n_groups)]
    r_g = {(sq, g): state_ref[sq, g] for sq, g in chains}
    for step in range(chunks_per_seq):
        for sq, g in chains:
            ci = sq * chunks_per_seq + step
            y_u[ci, g] = y_u[ci, g] + _dot(tile(q_in, ci, g), r_g[sq, g].astype(BF16))
        r_g = {(sq, g): r_g[sq, g] * gc_ref[g] + jnp.where(same_head, kv_u[sq * chunks_per_seq + step, g], 0.0)
               for sq, g in chains}
    for sq, g in chains:
        state_ref[sq, g] = r_g[sq, g]
    y_all = jnp.concatenate([jnp.concatenate([y_u[ci, g] for g in range(n_groups)], axis=1)
                             for ci in range(n_chunks)], axis=0)
    return gsilu * (_head_norm(y_all, GN_EPS_RET) * gnw_ref[...] + gnb_ref[...])


def _retention_consts():
    c = RET_CHUNK
    half = HEAD_DIM // 2
    inv_freq = ROPE_BASE ** (-jnp.arange(half, dtype=F32) / half)
    freq = jnp.tile(inv_freq, 128 // half)[None, :]
    log_g = jnp.log(1.0 - 2.0 ** (-5.0 - jnp.arange(N_HEADS, dtype=F32)))
    idx = jnp.arange(c, dtype=F32)
    diff = idx[:, None] - idx[None, :]
    causal = diff >= 0
    dmask = jnp.where(causal[None], jnp.exp(log_g[:, None, None] * jnp.where(causal, diff, 0.0)[None]), 0.0)
    xi = jnp.exp(log_g[:, None] * (idx + 1.0)[None])
    zeta = jnp.exp(log_g[:, None] * (c - 1.0 - idx)[None])
    g_chunk = jnp.exp(log_g * c)
    xi_w = jnp.repeat(xi.T, HEAD_DIM, axis=1)
    zeta_w = jnp.repeat(zeta.T, HEAD_DIM, axis=1)
    dmask_pairs = jnp.concatenate([dmask[0::2], dmask[1::2]], axis=2)
    gc_rows = jnp.repeat(g_chunk, HEAD_DIM).reshape(N_HEADS // 2, 2 * HEAD_DIM, 1)
    gc_w = jnp.broadcast_to(gc_rows, (N_HEADS // 2, 2 * HEAD_DIM, 2 * HEAD_DIM))
    return freq, xi_w, zeta_w, dmask_pairs, gc_w


def _shift(p, carry_ref, mu_ref, lo):
    cols = slice(lo, lo + p.shape[1])
    rows = lax.broadcasted_iota(jnp.int32, p.shape, 0)
    prev = jnp.where(rows == 0, carry_ref[:, cols], pltpu.roll(p, 1, 0))
    carry_ref[:, cols] = p[p.shape[0] - 1:, :]
    return p + (prev - p) * mu_ref[:, cols]


def _wkv_body(at, rt, bt, kt, bc, kc, vb, bonus, gate, pc_ref, pc_row0, gnw_ref, gnb_ref, mask_ref, eye_ref,
              state_mask_ref, state_ref):
    c = WKV_CHUNK
    w = RWKV_WIDTH
    lanes = 2 * HEAD_DIM
    n_groups = w // lanes
    n_chunks = at.shape[0] // c
    eye = eye_ref[...]
    tril = mask_ref[...] > 0.5

    pair_diag = _pair_diag
    all_units = [(ci, g) for ci in range(n_chunks) for g in range(n_groups)]
    tile = lambda x, ci, g: x[ci * c:(ci + 1) * c, g * lanes:(g + 1) * lanes]
    g_u, w_u, u0_u, y0_u = {}, {}, {}, {}
    for lo in range(0, len(all_units), WKV_WAVE):
        units = all_units[lo:lo + WKV_WAVE]
        vd_u = {(ci, g): pair_diag(tile(vb, ci, g)) for ci, g in units}
        for ci, g in units:
            lhs = jnp.concatenate([tile(at, ci, g), tile(rt, ci, g)], axis=0)
            rhs = jnp.concatenate([pair_diag(tile(bt, ci, g)), pair_diag(tile(kt, ci, g))], axis=0)
            g_u[ci, g] = jnp.where(tril, _dot_nt(lhs, rhs), 0.0)
        m_u = {u: g_u[u][:c, :lanes] for u in units}
        t_u = {u: eye + m_u[u] for u in units}
        mb_u = {u: m_u[u].astype(BF16) for u in units}
        steps = 1
        while 2 * steps < c:
            mb_u = {u: _dot(mb_u[u], pair_diag(mb_u[u])).astype(BF16) for u in units}
            t_u = {u: t_u[u] + _dot(t_u[u].astype(BF16), pair_diag(mb_u[u])) for u in units}
            steps *= 2
        t_u = {u: t_u[u].astype(BF16) for u in units}
        w_u.update({(ci, g): _dot(t_u[ci, g], pair_diag(tile(at, ci, g))).astype(BF16) for ci, g in units})
        akv_u = {u: _dot(g_u[u][:c, lanes:].astype(BF16), vd_u[u]).astype(BF16) for u in units}
        u0_u.update({u: _dot(t_u[u], pair_diag(akv_u[u])) for u in units})
        y0_u.update({u: _dot(g_u[u][c:, lanes:].astype(BF16), vd_u[u]) for u in units})

    same_head = state_mask_ref[...] > 0.5
    n_seq = state_ref.shape[0]
    chunks_per_seq = n_chunks // n_seq
    chains = [(sq, g) for sq in range(n_seq) for g in range(n_groups)]
    s_g = {(sq, g): state_ref[sq, g] for sq, g in chains}
    y_u = {}
    for step in range(chunks_per_seq):
        ci_of = lambda sq: sq * chunks_per_seq + step
        pc = [pc_ref[sq, pl.ds(pc_row0 + step, 1), :] for sq in range(n_seq)]
        s_b = {key: s_g[key].astype(BF16) for key in chains}
        u_b = {(sq, g): (_dot_nt(w_u[ci_of(sq), g], s_b[sq, g]) + u0_u[ci_of(sq), g]).astype(BF16)
               for sq, g in chains}
        for sq, g in chains:
            ci = ci_of(sq)
            y_u[ci, g] = (_dot_nt(tile(rt, ci, g), s_b[sq, g])
                          + _dot(g_u[ci, g][c:, :lanes].astype(BF16), pair_diag(u_b[sq, g])) + y0_u[ci, g])
        s_g = {(sq, g): s_g[sq, g] * pc[sq][:, g * lanes:(g + 1) * lanes]
               + jnp.where(same_head,
                           _dot_tn(jnp.concatenate([u_b[sq, g], tile(vb, ci_of(sq), g)], axis=0),
                                   jnp.concatenate([tile(bc, ci_of(sq), g), tile(kc, ci_of(sq), g)], axis=0)),
                           0.0)
               for sq, g in chains}
    for sq, g in chains:
        state_ref[sq, g] = s_g[sq, g]
    y_all = jnp.concatenate([jnp.concatenate([y_u[ci, g] for g in range(n_groups)], axis=1)
                             for ci in range(n_chunks)], axis=0)
    return (_head_norm(y_all, GN_EPS_RWKV) * gnw_ref[...] + gnb_ref[...] + bonus) * gate


RET_STREAMS = 6
WKV_STREAMS = 9


def _prep_kernel(x_ref, pos_ref, g_ref, w_in_ref, freq_ref, xi_ref, zeta_ref,
                 mu_rkv_ref, mu_lora_ref, w0_ref, wup_ref, a0_ref, aup_ref, gup_ref, kk_ref, ka_ref, rk_ref,
                 tri_ref, *refs, tiles_per_seq):
    (q_ref, k_ref, qin_ref, kout_ref, vr_ref, gsilu_ref,
     at_ref, rt_ref, bt_ref, kt_ref, bc_ref, kc_ref, vw_ref, bonus_ref, gate_ref, pc_ref,
     w_ref, carry_rkv_ref, carry_lora_ref) = refs
    tm = x_ref.shape[0]
    w = RWKV_WIDTH
    rkv0 = RET_PROJ
    _cast_weights_once(pl.program_id(0) == 0, [(w_in_ref, w_ref)])

    @pl.when(pl.program_id(0) % tiles_per_seq == 0)
    def _():
        carry_rkv_ref[...] = jnp.zeros_like(carry_rkv_ref)
        carry_lora_ref[...] = jnp.zeros_like(carry_lora_ref)

    h = _rms(x_ref[...], g_ref[...]).astype(BF16)

    wr = RET_WIDTH
    lanes = 2 * HEAD_DIM
    half = HEAD_DIM // 2
    p_q = _dot_nt(h, w_ref[:wr, :])
    p_k = _dot_nt(h, w_ref[wr:2 * wr, :])
    p_r = _dot_nt(h, w_ref[rkv0:rkv0 + w, :])
    plora_raw = _dot_nt(h, w_ref[rkv0 + 3 * w:, :])
    p_kw = _dot_nt(h, w_ref[rkv0 + w:rkv0 + 2 * w, :])
    p_vw = _dot_nt(h, w_ref[rkv0 + 2 * w:rkv0 + 3 * w, :])

    ang = pos_ref[...].astype(F32) * freq_ref[...]
    first_half = (lax.broadcasted_iota(jnp.int32, (tm, lanes), 1) % HEAD_DIM) < half
    cos = jnp.cos(ang)
    sin = jnp.sin(ang)
    sin = jnp.where(first_half, -sin, sin)
    k_scale = HEAD_DIM ** -0.5

    def rope(t, cos_t, sin_t):
        tiles = []
        for lo in range(0, wr, lanes):
            x = t[:, lo:lo + lanes]
            partner = jnp.where(first_half, pltpu.roll(x, lanes - half, 1), pltpu.roll(x, half, 1))
            tiles.append(x * cos_t + partner * sin_t)
        return jnp.concatenate(tiles, axis=1)

    def chunk_scaled(dst_ref, t, table_ref):
        for lo in range(0, tm, RET_CHUNK):
            dst_ref[lo:lo + RET_CHUNK, :] = (t[lo:lo + RET_CHUNK] * table_ref[...]).astype(BF16)

    q = rope(p_q, cos, sin)
    q_ref[...] = q.astype(BF16)
    chunk_scaled(qin_ref, q, xi_ref)

    kr = rope(p_k, cos * k_scale, sin * k_scale)
    k_ref[...] = kr.astype(BF16)
    chunk_scaled(kout_ref, kr, zeta_ref)

    p_v = _dot_nt(h, w_ref[2 * wr:3 * wr, :])
    p_g = _dot_nt(h, w_ref[3 * wr:RET_PROJ, :])
    r = _shift(p_r, carry_rkv_ref, mu_rkv_ref, 0)
    plora = _shift(plora_raw, carry_lora_ref, mu_lora_ref, 0)
    w_lr = plora[:, :DECAY_LORA]
    a_lr = plora[:, DECAY_LORA:DECAY_LORA + AAA_LORA]
    g_lr = plora[:, DECAY_LORA + AAA_LORA:]
    lw = -math.exp(-0.5) * jax.nn.sigmoid(w0_ref[...] + _dot(jnp.tanh(w_lr).astype(BF16), wup_ref[...]))
    a_sig = jax.nn.sigmoid(a0_ref[...] + _dot(a_lr.astype(BF16), aup_ref[...]))
    gate_ref[...] = _dot(jax.nn.sigmoid(g_lr).astype(BF16), gup_ref[...])

    k = _shift(p_kw, carry_rkv_ref, mu_rkv_ref, w)
    kk = k * kk_ref[...]
    k2 = k * (1.0 + (a_sig - 1.0) * ka_ref[...])
    kk = kk * (1.0 / jnp.maximum(jnp.sqrt(_head_sums(kk * kk)), 1e-12))
    a_vec = -kk
    b_vec = kk * a_sig

    v = _shift(p_vw, carry_rkv_ref, mu_rkv_ref, 2 * w)
    bonus_ref[...] = _head_sums(r * k2 * rk_ref[...]) * v
    vw_ref[...] = v.astype(BF16)

    c = WKV_CHUNK
    for ci in range(tm // c):
        rows = slice(ci * c, (ci + 1) * c)
        lw_c = lw[rows]
        cum = jnp.dot(tri_ref[...], lw_c, precision=lax.Precision.HIGHEST, preferred_element_type=F32)
        cum_end = cum[c - 1:, :]
        p_inv = jnp.exp(-cum)
        p_out = jnp.exp(cum_end - cum)
        at_ref[rows, :] = (a_vec[rows] * jnp.exp(cum - lw_c)).astype(BF16)
        rt_ref[rows, :] = (r[rows] * jnp.exp(cum)).astype(BF16)
        bt_ref[rows, :] = (b_vec[rows] * p_inv).astype(BF16)
        kt_ref[rows, :] = (k2[rows] * p_inv).astype(BF16)
        bc_ref[rows, :] = (b_vec[rows] * p_out).astype(BF16)
        kc_ref[rows, :] = (k2[rows] * p_out).astype(BF16)
        pc_ref[ci:ci + 1, :] = jnp.exp(cum_end)

    vr_ref[...] = p_v.astype(BF16)
    gsilu_ref[...] = p_g * jax.nn.sigmoid(p_g)


def _prep(x2d, pos2d, seq_len, consts, tm=PREP_ROWS):
    t = x2d.shape[0]
    assert t % tm == 0 and seq_len % tm == 0 and tm % RET_CHUNK == 0 and tm % WKV_CHUNK == 0
    row = lambda i: (i, 0)
    width = RET_WIDTH
    out_dtypes = [BF16] * 5 + [F32] + [BF16] * 7 + [F32] * 2
    assert len(out_dtypes) == RET_STREAMS + WKV_STREAMS
    n_pc = tm // WKV_CHUNK
    return pl.pallas_call(
        functools.partial(_prep_kernel, tiles_per_seq=seq_len // tm),
        grid=(t // tm,),
        in_specs=[pl.BlockSpec((tm, D_MODEL), row), pl.BlockSpec((tm, 1), row)] + [_full(a.shape) for a in consts],
        out_specs=[pl.BlockSpec((tm, width), row) for _ in out_dtypes] + [pl.BlockSpec((n_pc, width), row)],
        out_shape=[jax.ShapeDtypeStruct((t, width), dt) for dt in out_dtypes]
                  + [jax.ShapeDtypeStruct((t // WKV_CHUNK, width), F32)],
        scratch_shapes=[pltpu.VMEM(consts[1].shape, BF16), pltpu.VMEM((1, 3 * RWKV_WIDTH), F32),
                        pltpu.VMEM((1, LORA_WIDTH), F32)],
        compiler_params=_params(1),
        name="prep",
    )(x2d, pos2d, *consts)


def _mixers_kernel(*refs):
    ret_in, refs = refs[:RET_STREAMS], refs[RET_STREAMS:]
    wkv_in, refs = refs[:WKV_STREAMS], refs[WKV_STREAMS:]
    (pc_ref, dmask_ref, gc_ref, same_head_ref, ret_gnw_ref, ret_gnb_ref, gnw_ref, gnb_ref, mask_ref, eye_ref,
     o_ref, ret_state_ref, wkv_state_ref) = refs
    ti = pl.program_id(1)

    @pl.when(ti == 0)
    def _():
        ret_state_ref[...] = jnp.zeros_like(ret_state_ref)
        wkv_state_ref[...] = jnp.zeros_like(wkv_state_ref)

    stacked = lambda ref: ref[...].reshape(ref.shape[0] * ref.shape[1], ref.shape[2])
    y_ret = _retention_body(*[stacked(r) for r in ret_in], dmask_ref, gc_ref, same_head_ref, ret_gnw_ref,
                            ret_gnb_ref, ret_state_ref)
    chunks_per_block = o_ref.shape[1] // WKV_CHUNK
    y_rwkv = _wkv_body(*[stacked(r) for r in wkv_in], pc_ref, ti * chunks_per_block, gnw_ref, gnb_ref, mask_ref,
                       eye_ref, same_head_ref, wkv_state_ref)
    o_ref[...] = jnp.concatenate([y_ret, y_rwkv], axis=1).astype(o_ref.dtype).reshape(o_ref.shape)


def _mixers(streams, pc, dmask_pairs, gc_w, ret_gn_w, ret_gn_b, gn_w, gn_b):
    b, s, width = streams[0].shape
    n_seq, tb, c = MIX_SEQS, MIX_BLOCK, WKV_CHUNK
    assert b % n_seq == 0 and s % tb == 0 and tb % RET_CHUNK == 0 and tb % c == 0
    idx = jnp.arange(c)
    incl = (idx[:, None] >= idx[None, :]).astype(F32)
    strict = (idx[:, None] > idx[None, :]).astype(F32)
    mask = jnp.concatenate([jnp.tile(strict, (1, 4)), jnp.tile(incl, (1, 4))], axis=0)
    eye = jnp.tile(jnp.eye(c, dtype=F32), (1, 2))
    head_of_lane = jnp.arange(2 * HEAD_DIM) // HEAD_DIM
    state_mask = (head_of_lane[:, None] == head_of_lane[None, :]).astype(F32)
    consts = [dmask_pairs, gc_w, state_mask, ret_gn_w, ret_gn_b, gn_w, gn_b, mask, eye]
    blk = lambda bi, ti: (bi, ti, 0)
    state = pltpu.VMEM((n_seq, N_HEADS // 2, 2 * HEAD_DIM, 2 * HEAD_DIM), F32)
    return pl.pallas_call(
        _mixers_kernel,
        grid=(b // n_seq, s // tb),
        in_specs=[pl.BlockSpec((n_seq, tb, width), blk) for _ in streams]
                 + [pl.BlockSpec((n_seq, s // c, width), lambda bi, ti: (bi, 0, 0))]
                 + [_full(a.shape) for a in consts],
        out_specs=pl.BlockSpec((n_seq, tb, RET_WIDTH + RWKV_WIDTH), blk),
        out_shape=jax.ShapeDtypeStruct((b, s, RET_WIDTH + RWKV_WIDTH), BF16),
        scratch_shapes=[state, state],
        compiler_params=_params(2),
        name="mixers",
    )(*streams, pc, *consts)


def _mem_kv_kernel(m_ref, g_ref, w_f32_ref, o_ref, w_ref):
    _cast_weights_once(pl.program_id(0) == 0, [(w_f32_ref, w_ref)])
    h = _rms(m_ref[...], g_ref[...]).astype(BF16)
    o_ref[...] = _dot(h, w_ref[...]).astype(o_ref.dtype)


def _mem_kv(mem2d, g, w_kv, tm=1024):
    t = mem2d.shape[0]
    n = w_kv.shape[1]
    tm = min(tm, t)
    assert t % tm == 0
    return pl.pallas_call(
        _mem_kv_kernel,
        grid=(t // tm,),
        in_specs=[pl.BlockSpec((tm, D_MODEL), lambda i: (i, 0)), _full(g.shape), _full(w_kv.shape)],
        out_specs=pl.BlockSpec((tm, n), lambda i: (i, 0)),
        out_shape=jax.ShapeDtypeStruct((t, n), BF16),
        scratch_shapes=[pltpu.VMEM(w_kv.shape, BF16)],
        compiler_params=_params(1),
        name="mem_kv",
    )(mem2d, g, w_kv)


def _post_mix_kernel(x_ref, y_ref, wout_f32_ref, gx_ref, wq_f32_ref, kv_ref, wo_f32_ref, o_ref,
                     wout_ref, wq_ref, wo_ref):
    first = jnp.logical_and(pl.program_id(0) == 0, pl.program_id(1) == 0)
    _cast_weights_once(first, [(wout_f32_ref, wout_ref), (wq_f32_ref, wq_ref), (wo_f32_ref, wo_ref)])
    x1 = x_ref[0] + _dot(y_ref[0], wout_ref[...])
    q = _dot(_rms(x1, gx_ref[...]).astype(BF16), wq_ref[...]).astype(BF16)
    kv = kv_ref[0]
    heads = []
    for h in range(XATTN_HEADS):
        sl = slice(h * XATTN_HEAD_DIM, (h + 1) * XATTN_HEAD_DIM)
        s = _dot_nt(q[:, sl], kv[:, sl]) * (XATTN_HEAD_DIM ** -0.5)
        e = jnp.exp(s - jnp.max(s, axis=-1, keepdims=True))
        prob = e / jnp.sum(e, axis=-1, keepdims=True)
        heads.append(_dot(prob.astype(BF16), kv[:, D_MODEL + h * XATTN_HEAD_DIM:D_MODEL + (h + 1) * XATTN_HEAD_DIM]))
    o = jnp.concatenate(heads, axis=1).astype(BF16)
    o_ref[0] = x1 + _dot(o, wo_ref[...])


def _post_mix(x, y, w_out, g_x, w_q, kv, w_o, tm=1024):
    b, s, _ = x.shape
    m = kv.shape[1]
    assert s % tm == 0
    blk = lambda bi, ti: (bi, ti, 0)
    return pl.pallas_call(
        _post_mix_kernel,
        grid=(b, s // tm),
        in_specs=[pl.BlockSpec((1, tm, D_MODEL), blk), pl.BlockSpec((1, tm, y.shape[2]), blk),
                  _full(w_out.shape), _full(g_x.shape), _full(w_q.shape),
                  pl.BlockSpec((1, m, 2 * D_MODEL), lambda bi, ti: (bi, 0, 0)), _full(w_o.shape)],
        out_specs=pl.BlockSpec((1, tm, D_MODEL), blk),
        out_shape=jax.ShapeDtypeStruct((b, s, D_MODEL), F32),
        scratch_shapes=[pltpu.VMEM(a.shape, BF16) for a in (w_out, w_q, w_o)],
        compiler_params=_params(2),
        name="post_mix",
    )(x, y, w_out, g_x, w_q, kv, w_o)


def _mlp_kernel(x_ref, g_ref, wup_ref, wdown_ref, gf_ref, o_ref, *, tf, final_norm):
    x = x_ref[...]
    h = _rms(x, g_ref[...]).astype(BF16)
    acc = x
    for j in range(D_FF // tf):
        u = jnp.maximum(_dot(h, wup_ref[:, j * tf:(j + 1) * tf]), 0.0)
        acc = acc + _dot((u * u).astype(BF16), wdown_ref[j * tf:(j + 1) * tf, :])
    o_ref[...] = _rms(acc, gf_ref[...]) if final_norm else acc


def _mlp(x2d, g, w_up, w_down, g_final, final_norm, tm=1024, tf=512):
    t = x2d.shape[0]
    assert t % tm == 0 and D_FF % tf == 0
    row = lambda i: (i, 0)
    return pl.pallas_call(
        functools.partial(_mlp_kernel, tf=tf, final_norm=final_norm),
        grid=(t // tm,),
        in_specs=[pl.BlockSpec((tm, D_MODEL), row), _full(g.shape), _full(w_up.shape), _full(w_down.shape),
                  _full(g_final.shape)],
        out_specs=pl.BlockSpec((tm, D_MODEL), row),
        out_shape=jax.ShapeDtypeStruct((t, D_MODEL), F32),
        compiler_params=_params(1),
        name="mlp",
    )(x2d, g, w_up, w_down, g_final)


def kernel(x, mem, positions, norm_mix, w_in, ret_gn_w, ret_gn_b, rwkv_mu, rwkv_w0, rwkv_w_up, rwkv_a0,
           rwkv_a_up, rwkv_g_up, rwkv_k_k, rwkv_k_a, rwkv_r_k, rwkv_gn_w, rwkv_gn_b, w_out, norm_xattn,
           norm_mem, xattn_w_q, xattn_w_kv, xattn_w_o, norm_mlp, mlp_w_up, mlp_w_down, norm_final):
    b, s, dm = x.shape
    n_layers = w_in.shape[0]
    freq, xi_w, zeta_w, dmask_pairs, gc_w = _retention_consts()
    tri = (jnp.arange(WKV_CHUNK)[:, None] >= jnp.arange(WKV_CHUNK)[None, :]).astype(F32)
    row = lambda a: a.reshape(1, -1)
    for l in range(n_layers):
        mu = rwkv_mu[l]
        prep_consts = [row(norm_mix[l]), jnp.swapaxes(w_in[l], 0, 1), freq, xi_w, zeta_w, row(mu[:3 * RWKV_WIDTH]), row(mu[3 * RWKV_WIDTH:]),
                       row(rwkv_w0[l]), rwkv_w_up[l].astype(BF16), row(rwkv_a0[l]), rwkv_a_up[l].astype(BF16),
                       rwkv_g_up[l].astype(BF16), row(rwkv_k_k[l]), row(rwkv_k_a[l]), row(rwkv_r_k[l]), tri]
        *streams, pc = _prep(x.reshape(b * s, dm), positions.reshape(b * s, 1), s, prep_consts)
        y = _mixers([a.reshape(b, s, -1) for a in streams], pc.reshape(b, s // WKV_CHUNK, -1), dmask_pairs, gc_w,
                    row(ret_gn_w[l]), row(ret_gn_b[l]), row(rwkv_gn_w[l]), row(rwkv_gn_b[l]))
        kv = _mem_kv(mem.reshape(-1, dm), norm_mem[l][None, :], xattn_w_kv[l])
        x = _post_mix(x, y, w_out[l], norm_xattn[l][None, :], xattn_w_q[l], kv.reshape(b, -1, 2 * dm),
                      xattn_w_o[l])
        x = _mlp(x.reshape(b * s, dm), norm_mlp[l][None, :], mlp_w_up[l].astype(BF16),
                 mlp_w_down[l].astype(BF16), norm_final[None, :], l == n_layers - 1).reshape(b, s, dm)
    return x
```

```python
import functools
import math

import jax
import jax.numpy as jnp
from jax import lax
from jax.experimental import pallas as pl
from jax.experimental.pallas import tpu as pltpu

D_MODEL = 1024
HEAD_DIM = 64
RET_WIDTH = 512
RWKV_WIDTH = 512
N_HEADS = 8
RET_CHUNK = 128
ROPE_BASE = 10000.0
DECAY_LORA = 64
AAA_LORA = 64
GATE_LORA = 160
LORA_WIDTH = DECAY_LORA + AAA_LORA + GATE_LORA
RET_PROJ = 4 * RET_WIDTH
XATTN_HEADS = 4
XATTN_HEAD_DIM = D_MODEL // XATTN_HEADS
D_FF = 4 * D_MODEL
RMS_EPS = 1e-6
GN_EPS_RET = 1e-5
GN_EPS_RWKV = 64e-5

WKV_CHUNK = 64
MIX_SEQS = 4
MIX_BLOCK = 128
WKV_WAVE = 16
PREP_ROWS = 512
CAST_SLAB = 512
VMEM_LIMIT_BYTES = 56 * 1024 * 1024

BF16 = jnp.bfloat16
F32 = jnp.float32


def _dot(a, b):
    return jnp.dot(a, b, preferred_element_type=F32)


def _dot_nt(a, b):
    return lax.dot_general(a, b, (((1,), (1,)), ((), ())), preferred_element_type=F32)


def _dot_tn(a, b):
    return lax.dot_general(a, b, (((0,), (0,)), ((), ())), preferred_element_type=F32)


def _rms(x, g):
    return x * lax.rsqrt(jnp.mean(x * x, axis=-1, keepdims=True) + RMS_EPS) * g


def _params(n_axes):
    return pltpu.CompilerParams(dimension_semantics=("arbitrary",) * n_axes,
                                vmem_limit_bytes=VMEM_LIMIT_BYTES)


def _full(shape):
    zeros = (0,) * len(shape)
    return pl.BlockSpec(shape, lambda *_: zeros, pipeline_mode=pl.Buffered(1))


def _cast_weights_once(first_step, pairs):
    @pl.when(first_step)
    def _():
        for src_ref, dst_ref in pairs:
            n = src_ref.shape[1]
            for lo in range(0, n, CAST_SLAB):
                hi = min(lo + CAST_SLAB, n)
                dst_ref[:, lo:hi] = src_ref[:, lo:hi].astype(dst_ref.dtype)


def _head_sums(x):
    lanes = 2 * HEAD_DIM
    in_h0 = lax.broadcasted_iota(jnp.int32, (x.shape[0], lanes), 1) < HEAD_DIM
    tiles = []
    for lo in range(0, x.shape[1], lanes):
        t = x[:, lo:lo + lanes]
        s0 = jnp.sum(jnp.where(in_h0, t, 0.0), axis=-1, keepdims=True)
        s1 = jnp.sum(jnp.where(in_h0, 0.0, t), axis=-1, keepdims=True)
        tiles.append(jnp.where(in_h0, s0, s1))
    return jnp.concatenate(tiles, axis=1)


def _head_norm(y, eps):
    d = y - _head_sums(y) * (1.0 / HEAD_DIM)
    var = _head_sums(d * d) * (1.0 / HEAD_DIM)
    return d * lax.rsqrt(var + eps)


def _pair_diag(x):
    in_h0 = lax.broadcasted_iota(jnp.int32, x.shape, 1) < HEAD_DIM
    zero = jnp.zeros_like(x)
    return jnp.concatenate([jnp.where(in_h0, x, zero), jnp.where(in_h0, zero, x)], axis=0)


def _retention_body(q, k, q_in, k_out, vb, gsilu, dmask_ref, gc_ref, same_head_ref, gnw_ref, gnb_ref, state_ref):
    c = RET_CHUNK
    lanes = 2 * HEAD_DIM
    n_groups = RET_WIDTH // lanes
    n_chunks = q.shape[0] // c
    units = [(ci, g) for ci in range(n_chunks) for g in range(n_groups)]
    tile = lambda x, ci, g: x[ci * c:(ci + 1) * c, g * lanes:(g + 1) * lanes]
    s_u = {(ci, g): (_dot_nt(tile(q, ci, g), _pair_diag(tile(k, ci, g))) * dmask_ref[g]).astype(BF16)
           for ci, g in units}
    y_u = {(ci, g): _dot(s_u[ci, g], _pair_diag(tile(vb, ci, g))) for ci, g in units}
    kv_u = {(ci, g): _dot_tn(tile(k_out, ci, g), tile(vb, ci, g)) for ci, g in units}
    same_head = same_head_ref[...] > 0.5
    n_seq = state_ref.shape[0]
    chunks_per_seq = n_chunks // n_seq
    chains = [(sq, g) for sq in range(n_seq) for g in range(n_groups)]
    r_g = {(sq, g): state_ref[sq, g] for sq, g in chains}
    for step in range(chunks_per_seq):
        for sq, g in chains:
            ci = sq * chunks_per_seq + step
            y_u[ci, g] = y_u[ci, g] + _dot(tile(q_in, ci, g), r_g[sq, g].astype(BF16))
        r_g = {(sq, g): r_g[sq, g] * gc_ref[g] + jnp.where(same_head, kv_u[sq * chunks_per_seq + step, g], 0.0)
               for sq, g in chains}
    for sq, g in chains:
        state_ref[sq, g] = r_g[sq, g]
    y_all = jnp.concatenate([jnp.concatenate([y_u[ci, g] for g in range(n_groups)], axis=1)
                             for ci in range(n_chunks)], axis=0)
    return gsilu * (_head_norm(y_all, GN_EPS_RET) * gnw_ref[...] + gnb_ref[...])


def _retention_consts():
    c = RET_CHUNK
    half = HEAD_DIM // 2
    inv_freq = ROPE_BASE ** (-jnp.arange(half, dtype=F32) / half)
    freq = jnp.tile(inv_freq, 128 // half)[None, :]
    log_g = jnp.log(1.0 - 2.0 ** (-5.0 - jnp.arange(N_HEADS, dtype=F32)))
    idx = jnp.arange(c, dtype=F32)
    diff = idx[:, None] - idx[None, :]
    causal = diff >= 0
    dmask = jnp.where(causal[None], jnp.exp(log_g[:, None, None] * jnp.where(causal, diff, 0.0)[None]), 0.0)
    xi = jnp.exp(log_g[:, None] * (idx + 1.0)[None])
    zeta = jnp.exp(log_g[:, None] * (c - 1.0 - idx)[None])
    g_chunk = jnp.exp(log_g * c)
    xi_w = jnp.repeat(xi.T, HEAD_DIM, axis=1)
    zeta_w = jnp.repeat(zeta.T, HEAD_DIM, axis=1)
    dmask_pairs = jnp.concatenate([dmask[0::2], dmask[1::2]], axis=2)
    gc_rows = jnp.repeat(g_chunk, HEAD_DIM).reshape(N_HEADS // 2, 2 * HEAD_DIM, 1)
    gc_w = jnp.broadcast_to(gc_rows, (N_HEADS // 2, 2 * HEAD_DIM, 2 * HEAD_DIM))
    return freq, xi_w, zeta_w, dmask_pairs, gc_w


def _shift(p, carry_ref, mu_ref, lo):
    cols = slice(lo, lo + p.shape[1])
    rows = lax.broadcasted_iota(jnp.int32, p.shape, 0)
    prev = jnp.where(rows == 0, carry_ref[:, cols], pltpu.roll(p, 1, 0))
    carry_ref[:, cols] = p[p.shape[0] - 1:, :]
    return p + (prev - p) * mu_ref[:, cols]


def _wkv_body(at, rt, bt, kt, bc, kc, vb, bonus, gate, pc_ref, pc_row0, gnw_ref, gnb_ref, mask_ref, eye_ref,
              state_mask_ref, state_ref):
    c = WKV_CHUNK
    w = RWKV_WIDTH
    lanes = 2 * HEAD_DIM
    n_groups = w // lanes
    n_chunks = at.shape[0] // c
    eye = eye_ref[...]
    tril = mask_ref[...] > 0.5

    pair_diag = _pair_diag
    all_units = [(ci, g) for ci in range(n_chunks) for g in range(n_groups)]
    tile = lambda x, ci, g: x[ci * c:(ci + 1) * c, g * lanes:(g + 1) * lanes]
    g_u, w_u, u0_u, y0_u = {}, {}, {}, {}
    for lo in range(0, len(all_units), WKV_WAVE):
        units = all_units[lo:lo + WKV_WAVE]
        vd_u = {(ci, g): pair_diag(tile(vb, ci, g)) for ci, g in units}
        for ci, g in units:
            lhs = jnp.concatenate([tile(at, ci, g), tile(rt, ci, g)], axis=0)
            rhs = jnp.concatenate([pair_diag(tile(bt, ci, g)), pair_diag(tile(kt, ci, g))], axis=0)
            g_u[ci, g] = jnp.where(tril, _dot_nt(lhs, rhs), 0.0)
        m_u = {u: g_u[u][:c, :lanes] for u in units}
        t_u = {u: eye + m_u[u] for u in units}
        mb_u = {u: m_u[u].astype(BF16) for u in units}
        steps = 1
        while 2 * steps < c:
            mb_u = {u: _dot(mb_u[u], pair_diag(mb_u[u])).astype(BF16) for u in units}
            t_u = {u: t_u[u] + _dot(t_u[u].astype(BF16), pair_diag(mb_u[u])) for u in units}
            steps *= 2
        t_u = {u: t_u[u].astype(BF16) for u in units}
        w_u.update({(ci, g): _dot(t_u[ci, g], pair_diag(tile(at, ci, g))).astype(BF16) for ci, g in units})
        akv_u = {u: _dot(g_u[u][:c, lanes:].astype(BF16), vd_u[u]).astype(BF16) for u in units}
        u0_u.update({u: _dot(t_u[u], pair_diag(akv_u[u])) for u in units})
        y0_u.update({u: _dot(g_u[u][c:, lanes:].astype(BF16), vd_u[u]) for u in units})

    same_head = state_mask_ref[...] > 0.5
    n_seq = state_ref.shape[0]
    chunks_per_seq = n_chunks // n_seq
    chains = [(sq, g) for sq in range(n_seq) for g in range(n_groups)]
    s_g = {(sq, g): state_ref[sq, g] for sq, g in chains}
    y_u = {}
    for step in range(chunks_per_seq):
        ci_of = lambda sq: sq * chunks_per_seq + step
        pc = [pc_ref[sq, pl.ds(pc_row0 + step, 1), :] for sq in range(n_seq)]
        s_b = {key: s_g[key].astype(BF16) for key in chains}
        u_b = {(sq, g): (_dot_nt(w_u[ci_of(sq), g], s_b[sq, g]) + u0_u[ci_of(sq), g]).astype(BF16)
               for sq, g in chains}
        for sq, g in chains:
            ci = ci_of(sq)
            y_u[ci, g] = (_dot_nt(tile(rt, ci, g), s_b[sq, g])
                          + _dot(g_u[ci, g][c:, :lanes].astype(BF16), pair_diag(u_b[sq, g])) + y0_u[ci, g])
        s_g = {(sq, g): s_g[sq, g] * pc[sq][:, g * lanes:(g + 1) * lanes]
               + jnp.where(same_head,
                           _dot_tn(jnp.concatenate([u_b[sq, g], tile(vb, ci_of(sq), g)], axis=0),
                                   jnp.concatenate([tile(bc, ci_of(sq), g), tile(kc, ci_of(sq), g)], axis=0)),
                           0.0)
               for sq, g in chains}
    for sq, g in chains:
        state_ref[sq, g] = s_g[sq, g]
    y_all = jnp.concatenate([jnp.concatenate([y_u[ci, g] for g in range(n_groups)], axis=1)
                             for ci in range(n_chunks)], axis=0)
    return (_head_norm(y_all, GN_EPS_RWKV) * gnw_ref[...] + gnb_ref[...] + bonus) * gate


RET_STREAMS = 6
WKV_STREAMS = 9


def _prep_kernel(x_ref, pos_ref, g_ref, w_in_ref, freq_ref, xi_ref, zeta_ref,
                 mu_rkv_ref, mu_lora_ref, w0_ref, wup_ref, a0_ref, aup_ref, gup_ref, kk_ref, ka_ref, rk_ref,
                 tri_ref, *refs, tiles_per_seq):
    (q_ref, k_ref, qin_ref, kout_ref, vr_ref, gsilu_ref,
     at_ref, rt_ref, bt_ref, kt_ref, bc_ref, kc_ref, vw_ref, bonus_ref, gate_ref, pc_ref,
     w_ref, carry_rkv_ref, carry_lora_ref) = refs
    tm = x_ref.shape[0]
    w = RWKV_WIDTH
    rkv0 = RET_PROJ
    _cast_weights_once(pl.program_id(0) == 0, [(w_in_ref, w_ref)])

    @pl.when(pl.program_id(0) % tiles_per_seq == 0)
    def _():
        carry_rkv_ref[...] = jnp.zeros_like(carry_rkv_ref)
        carry_lora_ref[...] = jnp.zeros_like(carry_lora_ref)

    h = _rms(x_ref[...], g_ref[...]).astype(BF16)

    wr = RET_WIDTH
    lanes = 2 * HEAD_DIM
    half = HEAD_DIM // 2
    p_q = _dot_nt(h, w_ref[:wr, :])
    p_k = _dot_nt(h, w_ref[wr:2 * wr, :])
    p_r = _dot_nt(h, w_ref[rkv0:rkv0 + w, :])
    plora_raw = _dot_nt(h, w_ref[rkv0 + 3 * w:, :])
    p_kw = _dot_nt(h, w_ref[rkv0 + w:rkv0 + 2 * w, :])
    p_vw = _dot_nt(h, w_ref[rkv0 + 2 * w:rkv0 + 3 * w, :])

    n_pack = lanes // half
    ang = pos_ref[...].astype(F32) * freq_ref[...]
    lane_p = lax.broadcasted_iota(jnp.int32, ang.shape, 1)

    def unpack(t):
        blocks = []
        for j in range(n_pack):
            g = pltpu.roll(t, (lanes - j * half) % lanes, 1) if j else t
            spread = g
            for m in range(1, n_pack):
                spread = jnp.where(lane_p < m * half, spread, pltpu.roll(g, m * half, 1))
            blocks.append(spread)
        return jnp.concatenate(blocks, axis=0)

    first_half = (lax.broadcasted_iota(jnp.int32, (tm, lanes), 1) % HEAD_DIM) < half
    cos = unpack(jnp.cos(ang))
    sin = unpack(jnp.sin(ang))
    sin = jnp.where(first_half, -sin, sin)
    k_scale = HEAD_DIM ** -0.5

    def rope(t, cos_t, sin_t):
        tiles = []
        for lo in range(0, wr, lanes):
            x = t[:, lo:lo + lanes]
            partner = jnp.where(first_half, pltpu.roll(x, lanes - half, 1), pltpu.roll(x, half, 1))
            tiles.append(x * cos_t + partner * sin_t)
        return jnp.concatenate(tiles, axis=1)

    def chunk_scaled(dst_ref, t, table_ref):
        for lo in range(0, tm, RET_CHUNK):
            dst_ref[lo:lo + RET_CHUNK, :] = (t[lo:lo + RET_CHUNK] * table_ref[...]).astype(BF16)

    q = rope(p_q, cos, sin)
    q_ref[...] = q.astype(BF16)
    chunk_scaled(qin_ref, q, xi_ref)

    kr = rope(p_k, cos * k_scale, sin * k_scale)
    k_ref[...] = kr.astype(BF16)
    chunk_scaled(kout_ref, kr, zeta_ref)

    p_v = _dot_nt(h, w_ref[2 * wr:3 * wr, :])
    p_g = _dot_nt(h, w_ref[3 * wr:RET_PROJ, :])
    r = _shift(p_r, carry_rkv_ref, mu_rkv_ref, 0)
    plora = _shift(plora_raw, carry_lora_ref, mu_lora_ref, 0)
    w_lr = plora[:, :DECAY_LORA]
    a_lr = plora[:, DECAY_LORA:DECAY_LORA + AAA_LORA]
    g_lr = plora[:, DECAY_LORA + AAA_LORA:]
    lw = -math.exp(-0.5) * jax.nn.sigmoid(w0_ref[...] + _dot(jnp.tanh(w_lr).astype(BF16), wup_ref[...]))
    a_sig = jax.nn.sigmoid(a0_ref[...] + _dot(a_lr.astype(BF16), aup_ref[...]))
    gate_ref[...] = _dot(jax.nn.sigmoid(g_lr).astype(BF16), gup_ref[...])

    k = _shift(p_kw, carry_rkv_ref, mu_rkv_ref, w)
    kk = k * kk_ref[...]
    k2 = k * (1.0 + (a_sig - 1.0) * ka_ref[...])
    kk = kk * (1.0 / jnp.maximum(jnp.sqrt(_head_sums(kk * kk)), 1e-12))
    a_vec = -kk
    b_vec = kk * a_sig

    v = _shift(p_vw, carry_rkv_ref, mu_rkv_ref, 2 * w)
    bonus_ref[...] = _head_sums(r * k2 * rk_ref[...]) * v
    vw_ref[...] = v.astype(BF16)

    c = WKV_CHUNK
    for ci in range(tm // c):
        rows = slice(ci * c, (ci + 1) * c)
        lw_c = lw[rows]
        cum = jnp.dot(tri_ref[...], lw_c, precision=lax.Precision.HIGHEST, preferred_element_type=F32)
        cum_end = cum[c - 1:, :]
        p_inv = jnp.exp(-cum)
        p_out = jnp.exp(cum_end - cum)
        at_ref[rows, :] = (a_vec[rows] * jnp.exp(cum - lw_c)).astype(BF16)
        rt_ref[rows, :] = (r[rows] * jnp.exp(cum)).astype(BF16)
        bt_ref[rows, :] = (b_vec[rows] * p_inv).astype(BF16)
        kt_ref[rows, :] = (k2[rows] * p_inv).astype(BF16)
        bc_ref[rows, :] = (b_vec[rows] * p_out).astype(BF16)
        kc_ref[rows, :] = (k2[rows] * p_out).astype(BF16)
        pc_ref[ci:ci + 1, :] = jnp.exp(cum_end)

    vr_ref[...] = p_v.astype(BF16)
    gsilu_ref[...] = p_g * jax.nn.sigmoid(p_g)


ROPE_PACK = 2 * HEAD_DIM // (HEAD_DIM // 2)


def _packed_positions(positions, tm):
    lanes_per_group = 128 // ROPE_PACK
    tiles = positions.reshape(-1, ROPE_PACK, tm // ROPE_PACK)
    return jnp.repeat(jnp.swapaxes(tiles, 1, 2), lanes_per_group, axis=2).reshape(-1, 128)


def _prep(x2d, pos2d, seq_len, consts, tm=PREP_ROWS):
    t = x2d.shape[0]
    assert t % tm == 0 and seq_len % tm == 0 and tm % RET_CHUNK == 0 and tm % WKV_CHUNK == 0
    row = lambda i: (i, 0)
    width = RET_WIDTH
    out_dtypes = [BF16] * 5 + [F32] + [BF16] * 7 + [F32] * 2
    assert len(out_dtypes) == RET_STREAMS + WKV_STREAMS
    n_pc = tm // WKV_CHUNK
    return pl.pallas_call(
        functools.partial(_prep_kernel, tiles_per_seq=seq_len // tm),
        grid=(t // tm,),
        in_specs=[pl.BlockSpec((tm, D_MODEL), row), pl.BlockSpec((tm // ROPE_PACK, 128), row)]
                 + [_full(a.shape) for a in consts],
        out_specs=[pl.BlockSpec((tm, width), row) for _ in out_dtypes] + [pl.BlockSpec((n_pc, width), row)],
        out_shape=[jax.ShapeDtypeStruct((t, width), dt) for dt in out_dtypes]
                  + [jax.ShapeDtypeStruct((t // WKV_CHUNK, width), F32)],
        scratch_shapes=[pltpu.VMEM(consts[1].shape, BF16), pltpu.VMEM((1, 3 * RWKV_WIDTH), F32),
                        pltpu.VMEM((1, LORA_WIDTH), F32)],
        compiler_params=_params(1),
        name="prep",
    )(x2d, pos2d, *consts)


def _mixers_kernel(*refs):
    ret_in, refs = refs[:RET_STREAMS], refs[RET_STREAMS:]
    wkv_in, refs = refs[:WKV_STREAMS], refs[WKV_STREAMS:]
    (pc_ref, dmask_ref, gc_ref, same_head_ref, ret_gnw_ref, ret_gnb_ref, gnw_ref, gnb_ref, mask_ref, eye_ref,
     o_ref, ret_state_ref, wkv_state_ref) = refs
    ti = pl.program_id(1)

    @pl.when(ti == 0)
    def _():
        ret_state_ref[...] = jnp.zeros_like(ret_state_ref)
        wkv_state_ref[...] = jnp.zeros_like(wkv_state_ref)

    stacked = lambda ref: ref[...].reshape(ref.shape[0] * ref.shape[1], ref.shape[2])
    y_ret = _retention_body(*[stacked(r) for r in ret_in], dmask_ref, gc_ref, same_head_ref, ret_gnw_ref,
                            ret_gnb_ref, ret_state_ref)
    chunks_per_block = o_ref.shape[1] // WKV_CHUNK
    y_rwkv = _wkv_body(*[stacked(r) for r in wkv_in], pc_ref, ti * chunks_per_block, gnw_ref, gnb_ref, mask_ref,
                       eye_ref, same_head_ref, wkv_state_ref)
    o_ref[...] = jnp.concatenate([y_ret, y_rwkv], axis=1).astype(o_ref.dtype).reshape(o_ref.shape)


def _mixers(streams, pc, dmask_pairs, gc_w, ret_gn_w, ret_gn_b, gn_w, gn_b):
    b, s, width = streams[0].shape
    n_seq, tb, c = MIX_SEQS, MIX_BLOCK, WKV_CHUNK
    assert b % n_seq == 0 and s % tb == 0 and tb % RET_CHUNK == 0 and tb % c == 0
    idx = jnp.arange(c)
    incl = (idx[:, None] >= idx[None, :]).astype(F32)
    strict = (idx[:, None] > idx[None, :]).astype(F32)
    mask = jnp.concatenate([jnp.tile(strict, (1, 4)), jnp.tile(incl, (1, 4))], axis=0)
    eye = jnp.tile(jnp.eye(c, dtype=F32), (1, 2))
    head_of_lane = jnp.arange(2 * HEAD_DIM) // HEAD_DIM
    state_mask = (head_of_lane[:, None] == head_of_lane[None, :]).astype(F32)
    consts = [dmask_pairs, gc_w, state_mask, ret_gn_w, ret_gn_b, gn_w, gn_b, mask, eye]
    blk = lambda bi, ti: (bi, ti, 0)
    state = pltpu.VMEM((n_seq, N_HEADS // 2, 2 * HEAD_DIM, 2 * HEAD_DIM), F32)
    return pl.pallas_call(
        _mixers_kernel,
        grid=(b // n_seq, s // tb),
        in_specs=[pl.BlockSpec((n_seq, tb, width), blk) for _ in streams]
                 + [pl.BlockSpec((n_seq, s // c, width), lambda bi, ti: (bi, 0, 0))]
                 + [_full(a.shape) for a in consts],
        out_specs=pl.BlockSpec((n_seq, tb, RET_WIDTH + RWKV_WIDTH), blk),
        out_shape=jax.ShapeDtypeStruct((b, s, RET_WIDTH + RWKV_WIDTH), BF16),
        scratch_shapes=[state, state],
        compiler_params=_params(2),
        name="mixers",
    )(*streams, pc, *consts)


def _mem_kv_kernel(m_ref, g_ref, w_f32_ref, o_ref, w_ref):
    _cast_weights_once(pl.program_id(0) == 0, [(w_f32_ref, w_ref)])
    h = _rms(m_ref[...], g_ref[...]).astype(BF16)
    o_ref[...] = _dot(h, w_ref[...]).astype(o_ref.dtype)


def _mem_kv(mem2d, g, w_kv, tm=1024):
    t = mem2d.shape[0]
    n = w_kv.shape[1]
    tm = min(tm, t)
    assert t % tm == 0
    return pl.pallas_call(
        _mem_kv_kernel,
        grid=(t // tm,),
        in_specs=[pl.BlockSpec((tm, D_MODEL), lambda i: (i, 0)), _full(g.shape), _full(w_kv.shape)],
        out_specs=pl.BlockSpec((tm, n), lambda i: (i, 0)),
        out_shape=jax.ShapeDtypeStruct((t, n), BF16),
        scratch_shapes=[pltpu.VMEM(w_kv.shape, BF16)],
        compiler_params=_params(1),
        name="mem_kv",
    )(mem2d, g, w_kv)


def _post_mix_kernel(x_ref, y_ref, wout_f32_ref, gx_ref, wq_f32_ref, kv_ref, wo_f32_ref, o_ref,
                     wout_ref, wq_ref, wo_ref):
    first = jnp.logical_and(pl.program_id(0) == 0, pl.program_id(1) == 0)
    _cast_weights_once(first, [(wout_f32_ref, wout_ref), (wq_f32_ref, wq_ref), (wo_f32_ref, wo_ref)])
    x1 = x_ref[0] + _dot(y_ref[0], wout_ref[...])
    q = _dot(_rms(x1, gx_ref[...]).astype(BF16), wq_ref[...]).astype(BF16)
    kv = kv_ref[0]
    heads = []
    for h in range(XATTN_HEADS):
        sl = slice(h * XATTN_HEAD_DIM, (h + 1) * XATTN_HEAD_DIM)
        s = _dot_nt(q[:, sl], kv[:, sl]) * (XATTN_HEAD_DIM ** -0.5)
        e = jnp.exp(s - jnp.max(s, axis=-1, keepdims=True))
        prob = e / jnp.sum(e, axis=-1, keepdims=True)
        heads.append(_dot(prob.astype(BF16), kv[:, D_MODEL + h * XATTN_HEAD_DIM:D_MODEL + (h + 1) * XATTN_HEAD_DIM]))
    o = jnp.concatenate(heads, axis=1).astype(BF16)
    o_ref[0] = x1 + _dot(o, wo_ref[...])


def _post_mix(x, y, w_out, g_x, w_q, kv, w_o, tm=1024):
    b, s, _ = x.shape
    m = kv.shape[1]
    assert s % tm == 0
    blk = lambda bi, ti: (bi, ti, 0)
    return pl.pallas_call(
        _post_mix_kernel,
        grid=(b, s // tm),
        in_specs=[pl.BlockSpec((1, tm, D_MODEL), blk), pl.BlockSpec((1, tm, y.shape[2]), blk),
                  _full(w_out.shape), _full(g_x.shape), _full(w_q.shape),
                  pl.BlockSpec((1, m, 2 * D_MODEL), lambda bi, ti: (bi, 0, 0)), _full(w_o.shape)],
        out_specs=pl.BlockSpec((1, tm, D_MODEL), blk),
        out_shape=jax.ShapeDtypeStruct((b, s, D_MODEL), F32),
        scratch_shapes=[pltpu.VMEM(a.shape, BF16) for a in (w_out, w_q, w_o)],
        compiler_params=_params(2),
        name="post_mix",
    )(x, y, w_out, g_x, w_q, kv, w_o)


def _mlp_kernel(x_ref, g_ref, wup_ref, wdown_ref, gf_ref, o_ref, *, tf, final_norm):
    x = x_ref[...]
    h = _rms(x, g_ref[...]).astype(BF16)
    acc = x
    for j in range(D_FF // tf):
        u = jnp.maximum(_dot(h, wup_ref[:, j * tf:(j + 1) * tf]), 0.0)
        acc = acc + _dot((u * u).astype(BF16), wdown_ref[j * tf:(j + 1) * tf, :])
    o_ref[...] = _rms(acc, gf_ref[...]) if final_norm else acc


def _mlp(x2d, g, w_up, w_down, g_final, final_norm, tm=1024, tf=512):
    t = x2d.shape[0]
    assert t % tm == 0 and D_FF % tf == 0
    row = lambda i: (i, 0)
    return pl.pallas_call(
        functools.partial(_mlp_kernel, tf=tf, final_norm=final_norm),
        grid=(t // tm,),
        in_specs=[pl.BlockSpec((tm, D_MODEL), row), _full(g.shape), _full(w_up.shape), _full(w_down.shape),
                  _full(g_final.shape)],
        out_specs=pl.BlockSpec((tm, D_MODEL), row),
        out_shape=jax.ShapeDtypeStruct((t, D_MODEL), F32),
        compiler_params=_params(1),
        name="mlp",
    )(x2d, g, w_up, w_down, g_final)


def kernel(x, mem, positions, norm_mix, w_in, ret_gn_w, ret_gn_b, rwkv_mu, rwkv_w0, rwkv_w_up, rwkv_a0,
           rwkv_a_up, rwkv_g_up, rwkv_k_k, rwkv_k_a, rwkv_r_k, rwkv_gn_w, rwkv_gn_b, w_out, norm_xattn,
           norm_mem, xattn_w_q, xattn_w_kv, xattn_w_o, norm_mlp, mlp_w_up, mlp_w_down, norm_final):
    b, s, dm = x.shape
    n_layers = w_in.shape[0]
    freq, xi_w, zeta_w, dmask_pairs, gc_w = _retention_consts()
    tri = (jnp.arange(WKV_CHUNK)[:, None] >= jnp.arange(WKV_CHUNK)[None, :]).astype(F32)
    row = lambda a: a.reshape(1, -1)
    for l in range(n_layers):
        mu = rwkv_mu[l]
        prep_consts = [row(norm_mix[l]), jnp.swapaxes(w_in[l], 0, 1), freq, xi_w, zeta_w, row(mu[:3 * RWKV_WIDTH]), row(mu[3 * RWKV_WIDTH:]),
                       row(rwkv_w0[l]), rwkv_w_up[l].astype(BF16), row(rwkv_a0[l]), rwkv_a_up[l].astype(BF16),
                       rwkv_g_up[l].astype(BF16), row(rwkv_k_k[l]), row(rwkv_k_a[l]), row(rwkv_r_k[l]), tri]
        *streams, pc = _prep(x.reshape(b * s, dm), _packed_positions(positions, PREP_ROWS), s, prep_consts)
        y = _mixers([a.reshape(b, s, -1) for a in streams], pc.reshape(b, s // WKV_CHUNK, -1), dmask_pairs, gc_w,
                    row(ret_gn_w[l]), row(ret_gn_b[l]), row(rwkv_gn_w[l]), row(rwkv_gn_b[l]))
        kv = _mem_kv(mem.reshape(-1, dm), norm_mem[l][None, :], xattn_w_kv[l])
        x = _post_mix(x, y, w_out[l], norm_xattn[l][None, :], xattn_w_q[l], kv.reshape(b, -1, 2 * dm),
                      xattn_w_o[l])
        x = _mlp(x.reshape(b * s, dm), norm_mlp[l][None, :], mlp_w_up[l].astype(BF16),
                 mlp_w_down[l].astype(BF16), norm_final[None, :], l == n_layers - 1).reshape(b, s, dm)
    return x
```

```python
import functools
import math

import jax
import jax.numpy as jnp
from jax import lax
from jax.experimental import pallas as pl
from jax.experimental.pallas import tpu as pltpu

D_MODEL = 1024
HEAD_DIM = 64
RET_WIDTH = 512
RWKV_WIDTH = 512
N_HEADS = 8
RET_CHUNK = 128
ROPE_BASE = 10000.0
DECAY_LORA = 64
AAA_LORA = 64
GATE_LORA = 160
LORA_WIDTH = DECAY_LORA + AAA_LORA + GATE_LORA
RET_PROJ = 4 * RET_WIDTH
XATTN_HEADS = 4
XATTN_HEAD_DIM = D_MODEL // XATTN_HEADS
D_FF = 4 * D_MODEL
RMS_EPS = 1e-6
GN_EPS_RET = 1e-5
GN_EPS_RWKV = 64e-5

WKV_CHUNK = 64
MIX_SEQS = 4
MIX_BLOCK = 128
WKV_WAVE = 16
PREP_ROWS = 512
CAST_SLAB = 512
VMEM_LIMIT_BYTES = 56 * 1024 * 1024

BF16 = jnp.bfloat16
F32 = jnp.float32


def _dot(a, b):
    return jnp.dot(a, b, preferred_element_type=F32)


def _dot_nt(a, b):
    return lax.dot_general(a, b, (((1,), (1,)), ((), ())), preferred_element_type=F32)


def _dot_tn(a, b):
    return lax.dot_general(a, b, (((0,), (0,)), ((), ())), preferred_element_type=F32)


def _rms(x, g):
    return x * lax.rsqrt(jnp.mean(x * x, axis=-1, keepdims=True) + RMS_EPS) * g


def _params(n_axes):
    return pltpu.CompilerParams(dimension_semantics=("arbitrary",) * n_axes,
                                vmem_limit_bytes=VMEM_LIMIT_BYTES)


def _full(shape):
    zeros = (0,) * len(shape)
    return pl.BlockSpec(shape, lambda *_: zeros, pipeline_mode=pl.Buffered(1))


def _cast_weights_once(first_step, pairs):
    @pl.when(first_step)
    def _():
        for src_ref, dst_ref in pairs:
            n = src_ref.shape[1]
            for lo in range(0, n, CAST_SLAB):
                hi = min(lo + CAST_SLAB, n)
                dst_ref[:, lo:hi] = src_ref[:, lo:hi].astype(dst_ref.dtype)


def _head_sums(x):
    lanes = 2 * HEAD_DIM
    in_h0 = lax.broadcasted_iota(jnp.int32, (x.shape[0], lanes), 1) < HEAD_DIM
    tiles = []
    for lo in range(0, x.shape[1], lanes):
        t = x[:, lo:lo + lanes]
        s0 = jnp.sum(jnp.where(in_h0, t, 0.0), axis=-1, keepdims=True)
        s1 = jnp.sum(jnp.where(in_h0, 0.0, t), axis=-1, keepdims=True)
        tiles.append(jnp.where(in_h0, s0, s1))
    return jnp.concatenate(tiles, axis=1)


def _head_norm(y, eps):
    d = y - _head_sums(y) * (1.0 / HEAD_DIM)
    var = _head_sums(d * d) * (1.0 / HEAD_DIM)
    return d * lax.rsqrt(var + eps)


def _pair_diag(x):
    in_h0 = lax.broadcasted_iota(jnp.int32, x.shape, 1) < HEAD_DIM
    zero = jnp.zeros_like(x)
    return jnp.concatenate([jnp.where(in_h0, x, zero), jnp.where(in_h0, zero, x)], axis=0)


def _retention_body(q, k, q_in, k_out, vb, gsilu, dmask_ref, gc_ref, same_head_ref, gnw_ref, gnb_ref, state_ref):
    c = RET_CHUNK
    lanes = 2 * HEAD_DIM
    n_groups = RET_WIDTH // lanes
    n_chunks = q.shape[0] // c
    units = [(ci, g) for ci in range(n_chunks) for g in range(n_groups)]
    tile = lambda x, ci, g: x[ci * c:(ci + 1) * c, g * lanes:(g + 1) * lanes]
    s_u = {(ci, g): (_dot_nt(tile(q, ci, g), _pair_diag(tile(k, ci, g))) * dmask_ref[g]).astype(BF16)
           for ci, g in units}
    y_u = {(ci, g): _dot(s_u[ci, g], _pair_diag(tile(vb, ci, g))) for ci, g in units}
    kv_u = {(ci, g): _dot_tn(tile(k_out, ci, g), tile(vb, ci, g)) for ci, g in units}
    same_head = same_head_ref[...] > 0.5
    n_seq = state_ref.shape[0]
    chunks_per_seq = n_chunks // n_seq
    chains = [(sq, g) for sq in range(n_seq) for g in range(n_groups)]
    r_g = {(sq, g): state_ref[sq, g] for sq, g in chains}
    for step in range(chunks_per_seq):
        for sq, g in chains:
            ci = sq * chunks_per_seq + step
            y_u[ci, g] = y_u[ci, g] + _dot(tile(q_in, ci, g), r_g[sq, g].astype(BF16))
        r_g = {(sq, g): r_g[sq, g] * gc_ref[g] + jnp.where(same_head, kv_u[sq * chunks_per_seq + step, g], 0.0)
               for sq, g in chains}
    for sq, g in chains:
        state_ref[sq, g] = r_g[sq, g]
    y_all = jnp.concatenate([jnp.concatenate([y_u[ci, g] for g in range(n_groups)], axis=1)
                             for ci in range(n_chunks)], axis=0)
    return gsilu * (_head_norm(y_all, GN_EPS_RET) * gnw_ref[...] + gnb_ref[...])


def _retention_consts():
    c = RET_CHUNK
    half = HEAD_DIM // 2
    inv_freq = ROPE_BASE ** (-jnp.arange(half, dtype=F32) / half)
    freq = jnp.tile(inv_freq, 128 // half)[None, :]
    log_g = jnp.log(1.0 - 2.0 ** (-5.0 - jnp.arange(N_HEADS, dtype=F32)))
    idx = jnp.arange(c, dtype=F32)
    diff = idx[:, None] - idx[None, :]
    causal = diff >= 0
    dmask = jnp.where(causal[None], jnp.exp(log_g[:, None, None] * jnp.where(causal, diff, 0.0)[None]), 0.0)
    xi = jnp.exp(log_g[:, None] * (idx + 1.0)[None])
    zeta = jnp.exp(log_g[:, None] * (c - 1.0 - idx)[None])
    g_chunk = jnp.exp(log_g * c)
    xi_w = jnp.repeat(xi.T, HEAD_DIM, axis=1)
    zeta_w = jnp.repeat(zeta.T, HEAD_DIM, axis=1)
    dmask_pairs = jnp.concatenate([dmask[0::2], dmask[1::2]], axis=2)
    gc_rows = jnp.repeat(g_chunk, HEAD_DIM).reshape(N_HEADS // 2, 2 * HEAD_DIM, 1)
    gc_w = jnp.broadcast_to(gc_rows, (N_HEADS // 2, 2 * HEAD_DIM, 2 * HEAD_DIM))
    return freq, xi_w, zeta_w, dmask_pairs, gc_w


def _shift(p, carry_ref, mu_ref, lo):
    cols = slice(lo, lo + p.shape[1])
    rows = lax.broadcasted_iota(jnp.int32, p.shape, 0)
    prev = jnp.where(rows == 0, carry_ref[:, cols], pltpu.roll(p, 1, 0))
    carry_ref[:, cols] = p[p.shape[0] - 1:, :]
    return p + (prev - p) * mu_ref[:, cols]


def _wkv_body(at, rt, bt, kt, bc, kc, vb, bonus, gate, pc_ref, pc_row0, gnw_ref, gnb_ref, mask_ref, eye_ref,
              state_mask_ref, state_ref):
    c = WKV_CHUNK
    w = RWKV_WIDTH
    lanes = 2 * HEAD_DIM
    n_groups = w // lanes
    n_chunks = at.shape[0] // c
    eye = eye_ref[...]
    tril = mask_ref[...] > 0.5

    pair_diag = _pair_diag
    all_units = [(ci, g) for ci in range(n_chunks) for g in range(n_groups)]
    tile = lambda x, ci, g: x[ci * c:(ci + 1) * c, g * lanes:(g + 1) * lanes]
    g_u, wr_u, u0_u, y0_u = {}, {}, {}, {}
    for lo in range(0, len(all_units), WKV_WAVE):
        units = all_units[lo:lo + WKV_WAVE]
        vd_u = {(ci, g): pair_diag(tile(vb, ci, g)) for ci, g in units}
        for ci, g in units:
            lhs = jnp.concatenate([tile(at, ci, g), tile(rt, ci, g)], axis=0)
            rhs = jnp.concatenate([pair_diag(tile(bt, ci, g)), pair_diag(tile(kt, ci, g))], axis=0)
            g_u[ci, g] = jnp.where(tril, _dot_nt(lhs, rhs), 0.0)
        m_u = {u: g_u[u][:c, :lanes] for u in units}
        t_u = {u: eye + m_u[u] for u in units}
        mb_u = {u: m_u[u].astype(BF16) for u in units}
        mb_u = {u: _dot(mb_u[u], pair_diag(mb_u[u])).astype(BF16) for u in units}
        power = 2
        while 2 * power < c:
            tm_u = {u: _dot(jnp.concatenate([t_u[u].astype(BF16), mb_u[u]], axis=0), pair_diag(mb_u[u]))
                    for u in units}
            t_u = {u: t_u[u] + tm_u[u][:c] for u in units}
            mb_u = {u: tm_u[u][c:].astype(BF16) for u in units}
            power *= 2
        t_u = {u: (t_u[u] + _dot(t_u[u].astype(BF16), pair_diag(mb_u[u]))).astype(BF16) for u in units}
        kv_u = {u: _dot(g_u[u][:, lanes:].astype(BF16), vd_u[u]) for u in units}
        wu_u = {(ci, g): _dot(t_u[ci, g], jnp.concatenate(
            [pair_diag(tile(at, ci, g)), pair_diag(kv_u[ci, g][:c].astype(BF16))], axis=1)) for ci, g in units}
        wr_u.update({(ci, g): jnp.concatenate([wu_u[ci, g][:, :lanes].astype(BF16), tile(rt, ci, g)], axis=0)
                     for ci, g in units})
        u0_u.update({u: wu_u[u][:, lanes:] for u in units})
        y0_u.update({u: kv_u[u][c:] for u in units})

    same_head = state_mask_ref[...] > 0.5
    n_seq = state_ref.shape[0]
    chunks_per_seq = n_chunks // n_seq
    chains = [(sq, g) for sq in range(n_seq) for g in range(n_groups)]
    s_g = {(sq, g): state_ref[sq, g] for sq, g in chains}
    y_u = {}
    for step in range(chunks_per_seq):
        ci_of = lambda sq: sq * chunks_per_seq + step
        pc = [pc_ref[sq, pl.ds(pc_row0 + step, 1), :] for sq in range(n_seq)]
        ws = {(sq, g): _dot_nt(wr_u[ci_of(sq), g], s_g[sq, g].astype(BF16)) for sq, g in chains}
        u_b = {(sq, g): (ws[sq, g][:c] + u0_u[ci_of(sq), g]).astype(BF16) for sq, g in chains}
        for sq, g in chains:
            ci = ci_of(sq)
            y_u[ci, g] = (ws[sq, g][c:]
                          + _dot(g_u[ci, g][c:, :lanes].astype(BF16), pair_diag(u_b[sq, g])) + y0_u[ci, g])
        s_g = {(sq, g): s_g[sq, g] * pc[sq][:, g * lanes:(g + 1) * lanes]
               + jnp.where(same_head,
                           _dot_tn(jnp.concatenate([u_b[sq, g], tile(vb, ci_of(sq), g)], axis=0),
                                   jnp.concatenate([tile(bc, ci_of(sq), g), tile(kc, ci_of(sq), g)], axis=0)),
                           0.0)
               for sq, g in chains}
    for sq, g in chains:
        state_ref[sq, g] = s_g[sq, g]
    y_all = jnp.concatenate([jnp.concatenate([y_u[ci, g] for g in range(n_groups)], axis=1)
                             for ci in range(n_chunks)], axis=0)
    return (_head_norm(y_all, GN_EPS_RWKV) * gnw_ref[...] + gnb_ref[...] + bonus) * gate


RET_STREAMS = 6
WKV_STREAMS = 9


def _prep_kernel(x_ref, pos_ref, g_ref, w_in_ref, freq_ref, xi_ref, zeta_ref,
                 mu_rkv_ref, mu_lora_ref, w0_ref, wup_ref, a0_ref, aup_ref, gup_ref, kk_ref, ka_ref, rk_ref,
                 tri_ref, *refs, tiles_per_seq):
    (q_ref, k_ref, qin_ref, kout_ref, vr_ref, gsilu_ref,
     at_ref, rt_ref, bt_ref, kt_ref, bc_ref, kc_ref, vw_ref, bonus_ref, gate_ref, pc_ref,
     w_ref, carry_rkv_ref, carry_lora_ref) = refs
    tm = x_ref.shape[0]
    w = RWKV_WIDTH
    rkv0 = RET_PROJ
    _cast_weights_once(pl.program_id(0) == 0, [(w_in_ref, w_ref)])

    @pl.when(pl.program_id(0) % tiles_per_seq == 0)
    def _():
        carry_rkv_ref[...] = jnp.zeros_like(carry_rkv_ref)
        carry_lora_ref[...] = jnp.zeros_like(carry_lora_ref)

    h = _rms(x_ref[...], g_ref[...]).astype(BF16)

    wr = RET_WIDTH
    lanes = 2 * HEAD_DIM
    half = HEAD_DIM // 2
    p_q = _dot_nt(h, w_ref[:wr, :])
    p_k = _dot_nt(h, w_ref[wr:2 * wr, :])
    p_r = _dot_nt(h, w_ref[rkv0:rkv0 + w, :])
    plora_raw = _dot_nt(h, w_ref[rkv0 + 3 * w:, :])
    p_kw = _dot_nt(h, w_ref[rkv0 + w:rkv0 + 2 * w, :])
    p_vw = _dot_nt(h, w_ref[rkv0 + 2 * w:rkv0 + 3 * w, :])

    n_pack = lanes // half
    ang = pos_ref[...].astype(F32) * freq_ref[...]
    lane_p = lax.broadcasted_iota(jnp.int32, ang.shape, 1)

    def unpack(t):
        blocks = []
        for j in range(n_pack):
            g = pltpu.roll(t, (lanes - j * half) % lanes, 1) if j else t
            spread = g
            for m in range(1, n_pack):
                spread = jnp.where(lane_p < m * half, spread, pltpu.roll(g, m * half, 1))
            blocks.append(spread)
        return jnp.concatenate(blocks, axis=0)

    first_half = (lax.broadcasted_iota(jnp.int32, (tm, lanes), 1) % HEAD_DIM) < half
    cos = unpack(jnp.cos(ang))
    sin = unpack(jnp.sin(ang))
    sin = jnp.where(first_half, -sin, sin)
    k_scale = HEAD_DIM ** -0.5

    def rope(t, cos_t, sin_t):
        tiles = []
        for lo in range(0, wr, lanes):
            x = t[:, lo:lo + lanes]
            partner = jnp.where(first_half, pltpu.roll(x, lanes - half, 1), pltpu.roll(x, half, 1))
            tiles.append(x * cos_t + partner * sin_t)
        return jnp.concatenate(tiles, axis=1)

    def chunk_scaled(dst_ref, t, table_ref):
        for lo in range(0, tm, RET_CHUNK):
            dst_ref[lo:lo + RET_CHUNK, :] = (t[lo:lo + RET_CHUNK] * table_ref[...]).astype(BF16)

    q = rope(p_q, cos, sin)
    q_ref[...] = q.astype(BF16)
    chunk_scaled(qin_ref, q, xi_ref)

    kr = rope(p_k, cos * k_scale, sin * k_scale)
    k_ref[...] = kr.astype(BF16)
    chunk_scaled(kout_ref, kr, zeta_ref)

    p_v = _dot_nt(h, w_ref[2 * wr:3 * wr, :])
    p_g = _dot_nt(h, w_ref[3 * wr:RET_PROJ, :])
    r = _shift(p_r, carry_rkv_ref, mu_rkv_ref, 0)
    plora = _shift(plora_raw, carry_lora_ref, mu_lora_ref, 0)
    w_lr = plora[:, :DECAY_LORA]
    a_lr = plora[:, DECAY_LORA:DECAY_LORA + AAA_LORA]
    g_lr = plora[:, DECAY_LORA + AAA_LORA:]
    lw = -math.exp(-0.5) * jax.nn.sigmoid(w0_ref[...] + _dot(jnp.tanh(w_lr).astype(BF16), wup_ref[...]))
    a_sig = jax.nn.sigmoid(a0_ref[...] + _dot(a_lr.astype(BF16), aup_ref[...]))
    gate_ref[...] = _dot(jax.nn.sigmoid(g_lr).astype(BF16), gup_ref[...])

    k = _shift(p_kw, carry_rkv_ref, mu_rkv_ref, w)
    kk = k * kk_ref[...]
    k2 = k * (1.0 + (a_sig - 1.0) * ka_ref[...])
    kk = kk * (1.0 / jnp.maximum(jnp.sqrt(_head_sums(kk * kk)), 1e-12))
    a_vec = -kk
    b_vec = kk * a_sig

    v = _shift(p_vw, carry_rkv_ref, mu_rkv_ref, 2 * w)
    bonus_ref[...] = _head_sums(r * k2 * rk_ref[...]) * v
    vw_ref[...] = v.astype(BF16)

    c = WKV_CHUNK
    for ci in range(tm // c):
        rows = slice(ci * c, (ci + 1) * c)
        lw_c = lw[rows]
        cum = jnp.dot(tri_ref[...], lw_c, precision=lax.Precision.HIGHEST, preferred_element_type=F32)
        cum_end = cum[c - 1:, :]
        p_inv = jnp.exp(-cum)
        p_out = jnp.exp(cum_end - cum)
        at_ref[rows, :] = (a_vec[rows] * jnp.exp(cum - lw_c)).astype(BF16)
        rt_ref[rows, :] = (r[rows] * jnp.exp(cum)).astype(BF16)
        bt_ref[rows, :] = (b_vec[rows] * p_inv).astype(BF16)
        kt_ref[rows, :] = (k2[rows] * p_inv).astype(BF16)
        bc_ref[rows, :] = (b_vec[rows] * p_out).astype(BF16)
        kc_ref[rows, :] = (k2[rows] * p_out).astype(BF16)
        pc_ref[ci:ci + 1, :] = jnp.exp(cum_end)

    vr_ref[...] = p_v.astype(BF16)
    gsilu_ref[...] = p_g * jax.nn.sigmoid(p_g)


ROPE_PACK = 2 * HEAD_DIM // (HEAD_DIM // 2)


def _packed_positions(positions, tm):
    lanes_per_group = 128 // ROPE_PACK
    tiles = positions.reshape(-1, ROPE_PACK, tm // ROPE_PACK)
    return jnp.repeat(jnp.swapaxes(tiles, 1, 2), lanes_per_group, axis=2).reshape(-1, 128)


def _prep(x2d, pos2d, seq_len, consts, tm=PREP_ROWS):
    t = x2d.shape[0]
    assert t % tm == 0 and seq_len % tm == 0 and tm % RET_CHUNK == 0 and tm % WKV_CHUNK == 0
    row = lambda i: (i, 0)
    width = RET_WIDTH
    out_dtypes = [BF16] * 5 + [F32] + [BF16] * 7 + [F32] * 2
    assert len(out_dtypes) == RET_STREAMS + WKV_STREAMS
    n_pc = tm // WKV_CHUNK
    return pl.pallas_call(
        functools.partial(_prep_kernel, tiles_per_seq=seq_len // tm),
        grid=(t // tm,),
        in_specs=[pl.BlockSpec((tm, D_MODEL), row), pl.BlockSpec((tm // ROPE_PACK, 128), row)]
                 + [_full(a.shape) for a in consts],
        out_specs=[pl.BlockSpec((tm, width), row) for _ in out_dtypes] + [pl.BlockSpec((n_pc, width), row)],
        out_shape=[jax.ShapeDtypeStruct((t, width), dt) for dt in out_dtypes]
                  + [jax.ShapeDtypeStruct((t // WKV_CHUNK, width), F32)],
        scratch_shapes=[pltpu.VMEM(consts[1].shape, BF16), pltpu.VMEM((1, 3 * RWKV_WIDTH), F32),
                        pltpu.VMEM((1, LORA_WIDTH), F32)],
        compiler_params=_params(1),
        name="prep",
    )(x2d, pos2d, *consts)


def _mixers_kernel(*refs):
    ret_in, refs = refs[:RET_STREAMS], refs[RET_STREAMS:]
    wkv_in, refs = refs[:WKV_STREAMS], refs[WKV_STREAMS:]
    (pc_ref, dmask_ref, gc_ref, same_head_ref, ret_gnw_ref, ret_gnb_ref, gnw_ref, gnb_ref, mask_ref, eye_ref,
     o_ref, ret_state_ref, wkv_state_ref) = refs
    ti = pl.program_id(1)

    @pl.when(ti == 0)
    def _():
        ret_state_ref[...] = jnp.zeros_like(ret_state_ref)
        wkv_state_ref[...] = jnp.zeros_like(wkv_state_ref)

    stacked = lambda ref: ref[...].reshape(ref.shape[0] * ref.shape[1], ref.shape[2])
    y_ret = _retention_body(*[stacked(r) for r in ret_in], dmask_ref, gc_ref, same_head_ref, ret_gnw_ref,
                            ret_gnb_ref, ret_state_ref)
    chunks_per_block = o_ref.shape[1] // WKV_CHUNK
    y_rwkv = _wkv_body(*[stacked(r) for r in wkv_in], pc_ref, ti * chunks_per_block, gnw_ref, gnb_ref, mask_ref,
                       eye_ref, same_head_ref, wkv_state_ref)
    o_ref[...] = jnp.concatenate([y_ret, y_rwkv], axis=1).astype(o_ref.dtype).reshape(o_ref.shape)


def _mixers(streams, pc, dmask_pairs, gc_w, ret_gn_w, ret_gn_b, gn_w, gn_b):
    b, s, width = streams[0].shape
    n_seq, tb, c = MIX_SEQS, MIX_BLOCK, WKV_CHUNK
    assert b % n_seq == 0 and s % tb == 0 and tb % RET_CHUNK == 0 and tb % c == 0
    idx = jnp.arange(c)
    incl = (idx[:, None] >= idx[None, :]).astype(F32)
    strict = (idx[:, None] > idx[None, :]).astype(F32)
    mask = jnp.concatenate([jnp.tile(strict, (1, 4)), jnp.tile(incl, (1, 4))], axis=0)
    eye = jnp.tile(jnp.eye(c, dtype=F32), (1, 2))
    head_of_lane = jnp.arange(2 * HEAD_DIM) // HEAD_DIM
    state_mask = (head_of_lane[:, None] == head_of_lane[None, :]).astype(F32)
    consts = [dmask_pairs, gc_w, state_mask, ret_gn_w, ret_gn_b, gn_w, gn_b, mask, eye]
    blk = lambda bi, ti: (bi, ti, 0)
    state = pltpu.VMEM((n_seq, N_HEADS // 2, 2 * HEAD_DIM, 2 * HEAD_DIM), F32)
    return pl.pallas_call(
        _mixers_kernel,
        grid=(b // n_seq, s // tb),
        in_specs=[pl.BlockSpec((n_seq, tb, width), blk) for _ in streams]
                 + [pl.BlockSpec((n_seq, s // c, width), lambda bi, ti: (bi, 0, 0))]
                 + [_full(a.shape) for a in consts],
        out_specs=pl.BlockSpec((n_seq, tb, RET_WIDTH + RWKV_WIDTH), blk),
        out_shape=jax.ShapeDtypeStruct((b, s, RET_WIDTH + RWKV_WIDTH), BF16),
        scratch_shapes=[state, state],
        compiler_params=_params(2),
        name="mixers",
    )(*streams, pc, *consts)


def _mem_kv_kernel(m_ref, g_ref, w_f32_ref, o_ref, w_ref):
    _cast_weights_once(pl.program_id(0) == 0, [(w_f32_ref, w_ref)])
    h = _rms(m_ref[...], g_ref[...]).astype(BF16)
    o_ref[...] = _dot(h, w_ref[...]).astype(o_ref.dtype)


def _mem_kv(mem2d, g, w_kv, tm=1024):
    t = mem2d.shape[0]
    n = w_kv.shape[1]
    tm = min(tm, t)
    assert t % tm == 0
    return pl.pallas_call(
        _mem_kv_kernel,
        grid=(t // tm,),
        in_specs=[pl.BlockSpec((tm, D_MODEL), lambda i: (i, 0)), _full(g.shape), _full(w_kv.shape)],
        out_specs=pl.BlockSpec((tm, n), lambda i: (i, 0)),
        out_shape=jax.ShapeDtypeStruct((t, n), BF16),
        scratch_shapes=[pltpu.VMEM(w_kv.shape, BF16)],
        compiler_params=_params(1),
        name="mem_kv",
    )(mem2d, g, w_kv)


def _post_mix_kernel(x_ref, y_ref, wout_f32_ref, gx_ref, wq_f32_ref, kv_ref, wo_f32_ref, o_ref,
                     wout_ref, wq_ref, wo_ref):
    first = jnp.logical_and(pl.program_id(0) == 0, pl.program_id(1) == 0)
    _cast_weights_once(first, [(wout_f32_ref, wout_ref), (wq_f32_ref, wq_ref), (wo_f32_ref, wo_ref)])
    x1 = x_ref[0] + _dot(y_ref[0], wout_ref[...])
    q = _dot(_rms(x1, gx_ref[...]).astype(BF16), wq_ref[...]).astype(BF16)
    kv = kv_ref[0]
    heads = []
    for h in range(XATTN_HEADS):
        sl = slice(h * XATTN_HEAD_DIM, (h + 1) * XATTN_HEAD_DIM)
        s = _dot_nt(q[:, sl], kv[:, sl]) * (XATTN_HEAD_DIM ** -0.5)
        e = jnp.exp(s - jnp.max(s, axis=-1, keepdims=True))
        prob = e / jnp.sum(e, axis=-1, keepdims=True)
        heads.append(_dot(prob.astype(BF16), kv[:, D_MODEL + h * XATTN_HEAD_DIM:D_MODEL + (h + 1) * XATTN_HEAD_DIM]))
    o = jnp.concatenate(heads, axis=1).astype(BF16)
    o_ref[0] = x1 + _dot(o, wo_ref[...])


def _post_mix(x, y, w_out, g_x, w_q, kv, w_o, tm=1024):
    b, s, _ = x.shape
    m = kv.shape[1]
    assert s % tm == 0
    blk = lambda bi, ti: (bi, ti, 0)
    return pl.pallas_call(
        _post_mix_kernel,
        grid=(b, s // tm),
        in_specs=[pl.BlockSpec((1, tm, D_MODEL), blk), pl.BlockSpec((1, tm, y.shape[2]), blk),
                  _full(w_out.shape), _full(g_x.shape), _full(w_q.shape),
                  pl.BlockSpec((1, m, 2 * D_MODEL), lambda bi, ti: (bi, 0, 0)), _full(w_o.shape)],
        out_specs=pl.BlockSpec((1, tm, D_MODEL), blk),
        out_shape=jax.ShapeDtypeStruct((b, s, D_MODEL), F32),
        scratch_shapes=[pltpu.VMEM(a.shape, BF16) for a in (w_out, w_q, w_o)],
        compiler_params=_params(2),
        name="post_mix",
    )(x, y, w_out, g_x, w_q, kv, w_o)


def _mlp_kernel(x_ref, g_ref, wup_ref, wdown_ref, gf_ref, o_ref, *, tf, final_norm):
    x = x_ref[...]
    h = _rms(x, g_ref[...]).astype(BF16)
    acc = x
    for j in range(D_FF // tf):
        u = jnp.maximum(_dot(h, wup_ref[:, j * tf:(j + 1) * tf]), 0.0)
        acc = acc + _dot((u * u).astype(BF16), wdown_ref[j * tf:(j + 1) * tf, :])
    o_ref[...] = _rms(acc, gf_ref[...]) if final_norm else acc


def _mlp(x2d, g, w_up, w_down, g_final, final_norm, tm=1024, tf=512):
    t = x2d.shape[0]
    assert t % tm == 0 and D_FF % tf == 0
    row = lambda i: (i, 0)
    return pl.pallas_call(
        functools.partial(_mlp_kernel, tf=tf, final_norm=final_norm),
        grid=(t // tm,),
        in_specs=[pl.BlockSpec((tm, D_MODEL), row), _full(g.shape), _full(w_up.shape), _full(w_down.shape),
                  _full(g_final.shape)],
        out_specs=pl.BlockSpec((tm, D_MODEL), row),
        out_shape=jax.ShapeDtypeStruct((t, D_MODEL), F32),
        compiler_params=_params(1),
        name="mlp",
    )(x2d, g, w_up, w_down, g_final)


def kernel(x, mem, positions, norm_mix, w_in, ret_gn_w, ret_gn_b, rwkv_mu, rwkv_w0, rwkv_w_up, rwkv_a0,
           rwkv_a_up, rwkv_g_up, rwkv_k_k, rwkv_k_a, rwkv_r_k, rwkv_gn_w, rwkv_gn_b, w_out, norm_xattn,
           norm_mem, xattn_w_q, xattn_w_kv, xattn_w_o, norm_mlp, mlp_w_up, mlp_w_down, norm_final):
    b, s, dm = x.shape
    n_layers = w_in.shape[0]
    freq, xi_w, zeta_w, dmask_pairs, gc_w = _retention_consts()
    tri = (jnp.arange(WKV_CHUNK)[:, None] >= jnp.arange(WKV_CHUNK)[None, :]).astype(F32)
    row = lambda a: a.reshape(1, -1)
    for l in range(n_layers):
        mu = rwkv_mu[l]
        prep_consts = [row(norm_mix[l]), jnp.swapaxes(w_in[l], 0, 1), freq, xi_w, zeta_w, row(mu[:3 * RWKV_WIDTH]), row(mu[3 * RWKV_WIDTH:]),
                       row(rwkv_w0[l]), rwkv_w_up[l].astype(BF16), row(rwkv_a0[l]), rwkv_a_up[l].astype(BF16),
                       rwkv_g_up[l].astype(BF16), row(rwkv_k_k[l]), row(rwkv_k_a[l]), row(rwkv_r_k[l]), tri]
        *streams, pc = _prep(x.reshape(b * s, dm), _packed_positions(positions, PREP_ROWS), s, prep_consts)
        y = _mixers([a.reshape(b, s, -1) for a in streams], pc.reshape(b, s // WKV_CHUNK, -1), dmask_pairs, gc_w,
                    row(ret_gn_w[l]), row(ret_gn_b[l]), row(rwkv_gn_w[l]), row(rwkv_gn_b[l]))
        kv = _mem_kv(mem.reshape(-1, dm), norm_mem[l][None, :], xattn_w_kv[l])
        x = _post_mix(x, y, w_out[l], norm_xattn[l][None, :], xattn_w_q[l], kv.reshape(b, -1, 2 * dm),
                      xattn_w_o[l])
        x = _mlp(x.reshape(b * s, dm), norm_mlp[l][None, :], mlp_w_up[l].astype(BF16),
                 mlp_w_down[l].astype(BF16), norm_final[None, :], l == n_layers - 1).reshape(b, s, dm)
    return x
```

```python
import functools
import math

import jax
import jax.numpy as jnp
from jax import lax
from jax.experimental import pallas as pl
from jax.experimental.pallas import tpu as pltpu

D_MODEL = 1024
HEAD_DIM = 64
RET_WIDTH = 512
RWKV_WIDTH = 512
N_HEADS = 8
RET_CHUNK = 128
ROPE_BASE = 10000.0
DECAY_LORA = 64
AAA_LORA = 64
GATE_LORA = 160
LORA_WIDTH = DECAY_LORA + AAA_LORA + GATE_LORA
RET_PROJ = 4 * RET_WIDTH
XATTN_HEADS = 4
XATTN_HEAD_DIM = D_MODEL // XATTN_HEADS
D_FF = 4 * D_MODEL
RMS_EPS = 1e-6
GN_EPS_RET = 1e-5
GN_EPS_RWKV = 64e-5

WKV_CHUNK = 64
MIX_SEQS = 4
MIX_BLOCK = 128
WKV_WAVE = 16
PREP_ROWS = 512
CAST_SLAB = 512
VMEM_LIMIT_BYTES = 56 * 1024 * 1024

BF16 = jnp.bfloat16
F32 = jnp.float32


def _dot(a, b):
    return jnp.dot(a, b, preferred_element_type=F32)


def _dot_nt(a, b):
    return lax.dot_general(a, b, (((1,), (1,)), ((), ())), preferred_element_type=F32)


def _dot_tn(a, b):
    return lax.dot_general(a, b, (((0,), (0,)), ((), ())), preferred_element_type=F32)


def _rms(x, g):
    return x * lax.rsqrt(jnp.mean(x * x, axis=-1, keepdims=True) + RMS_EPS) * g


def _params(n_axes):
    return pltpu.CompilerParams(dimension_semantics=("arbitrary",) * n_axes,
                                vmem_limit_bytes=VMEM_LIMIT_BYTES)


def _full(shape):
    zeros = (0,) * len(shape)
    return pl.BlockSpec(shape, lambda *_: zeros, pipeline_mode=pl.Buffered(1))


def _cast_weights_once(first_step, pairs):
    @pl.when(first_step)
    def _():
        for src_ref, dst_ref in pairs:
            n = src_ref.shape[1]
            for lo in range(0, n, CAST_SLAB):
                hi = min(lo + CAST_SLAB, n)
                dst_ref[:, lo:hi] = src_ref[:, lo:hi].astype(dst_ref.dtype)


def _head_sums(x):
    lanes = 2 * HEAD_DIM
    in_h0 = lax.broadcasted_iota(jnp.int32, (x.shape[0], lanes), 1) < HEAD_DIM
    tiles = []
    for lo in range(0, x.shape[1], lanes):
        t = x[:, lo:lo + lanes]
        s0 = jnp.sum(jnp.where(in_h0, t, 0.0), axis=-1, keepdims=True)
        s1 = jnp.sum(jnp.where(in_h0, 0.0, t), axis=-1, keepdims=True)
        tiles.append(jnp.where(in_h0, s0, s1))
    return jnp.concatenate(tiles, axis=1)


def _head_norm(y, eps):
    d = y - _head_sums(y) * (1.0 / HEAD_DIM)
    var = _head_sums(d * d) * (1.0 / HEAD_DIM)
    return d * lax.rsqrt(var + eps)


def _pair_diag(x):
    in_h0 = lax.broadcasted_iota(jnp.int32, x.shape, 1) < HEAD_DIM
    zero = jnp.zeros_like(x)
    return jnp.concatenate([jnp.where(in_h0, x, zero), jnp.where(in_h0, zero, x)], axis=0)


def _retention_body(q, k, q_in, k_out, vb, gsilu, dmask_ref, gc_ref, same_head_ref, gnw_ref, gnb_ref, state_ref):
    c = RET_CHUNK
    lanes = 2 * HEAD_DIM
    n_groups = RET_WIDTH // lanes
    n_chunks = q.shape[0] // c
    units = [(ci, g) for ci in range(n_chunks) for g in range(n_groups)]
    tile = lambda x, ci, g: x[ci * c:(ci + 1) * c, g * lanes:(g + 1) * lanes]
    s_u = {(ci, g): (_dot_nt(tile(q, ci, g), _pair_diag(tile(k, ci, g))) * dmask_ref[g]).astype(BF16)
           for ci, g in units}
    y_u = {(ci, g): _dot(s_u[ci, g], _pair_diag(tile(vb, ci, g))) for ci, g in units}
    kv_u = {(ci, g): _dot_tn(tile(k_out, ci, g), tile(vb, ci, g)) for ci, g in units}
    same_head = same_head_ref[...] > 0.5
    n_seq = state_ref.shape[0]
    chunks_per_seq = n_chunks // n_seq
    chains = [(sq, g) for sq in range(n_seq) for g in range(n_groups)]
    r_g = {(sq, g): state_ref[sq, g] for sq, g in chains}
    for step in range(chunks_per_seq):
        for sq, g in chains:
            ci = sq * chunks_per_seq + step
            y_u[ci, g] = y_u[ci, g] + _dot(tile(q_in, ci, g), r_g[sq, g].astype(BF16))
        r_g = {(sq, g): r_g[sq, g] * gc_ref[g] + jnp.where(same_head, kv_u[sq * chunks_per_seq + step, g], 0.0)
               for sq, g in chains}
    for sq, g in chains:
        state_ref[sq, g] = r_g[sq, g]
    y_all = jnp.concatenate([jnp.concatenate([y_u[ci, g] for g in range(n_groups)], axis=1)
                             for ci in range(n_chunks)], axis=0)
    return gsilu * (_head_norm(y_all, GN_EPS_RET) * gnw_ref[...] + gnb_ref[...])


def _retention_consts():
    c = RET_CHUNK
    half = HEAD_DIM // 2
    inv_freq = ROPE_BASE ** (-jnp.arange(half, dtype=F32) / half)
    freq = jnp.tile(inv_freq, 128 // half)[None, :]
    log_g = jnp.log(1.0 - 2.0 ** (-5.0 - jnp.arange(N_HEADS, dtype=F32)))
    idx = jnp.arange(c, dtype=F32)
    diff = idx[:, None] - idx[None, :]
    causal = diff >= 0
    dmask = jnp.where(causal[None], jnp.exp(log_g[:, None, None] * jnp.where(causal, diff, 0.0)[None]), 0.0)
    xi = jnp.exp(log_g[:, None] * (idx + 1.0)[None])
    zeta = jnp.exp(log_g[:, None] * (c - 1.0 - idx)[None])
    g_chunk = jnp.exp(log_g * c)
    xi_w = jnp.repeat(xi.T, HEAD_DIM, axis=1)
    zeta_w = jnp.repeat(zeta.T, HEAD_DIM, axis=1)
    dmask_pairs = jnp.concatenate([dmask[0::2], dmask[1::2]], axis=2)
    gc_rows = jnp.repeat(g_chunk, HEAD_DIM).reshape(N_HEADS // 2, 2 * HEAD_DIM, 1)
    gc_w = jnp.broadcast_to(gc_rows, (N_HEADS // 2, 2 * HEAD_DIM, 2 * HEAD_DIM))
    return freq, xi_w, zeta_w, dmask_pairs, gc_w


def _shift(p, carry_ref, mu_ref, lo):
    cols = slice(lo, lo + p.shape[1])
    rows = lax.broadcasted_iota(jnp.int32, p.shape, 0)
    prev = jnp.where(rows == 0, carry_ref[:, cols], pltpu.roll(p, 1, 0))
    carry_ref[:, cols] = p[p.shape[0] - 1:, :]
    return p + (prev - p) * mu_ref[:, cols]


def _wkv_body(at, rt, bt, kt, bc, kc, vb, bonus, gate, pc_ref, pc_row0, gnw_ref, gnb_ref, mask_ref, eye_ref,
              state_mask_ref, state_ref):
    c = WKV_CHUNK
    w = RWKV_WIDTH
    lanes = 2 * HEAD_DIM
    n_groups = w // lanes
    n_chunks = at.shape[0] // c
    eye = eye_ref[...]
    tril = mask_ref[...] > 0.5

    pair_diag = _pair_diag
    all_units = [(ci, g) for ci in range(n_chunks) for g in range(n_groups)]
    tile = lambda x, ci, g: x[ci * c:(ci + 1) * c, g * lanes:(g + 1) * lanes]
    g_u, wr_u, u0_u, y0_u = {}, {}, {}, {}
    for lo in range(0, len(all_units), WKV_WAVE):
        units = all_units[lo:lo + WKV_WAVE]
        vd_u = {(ci, g): pair_diag(tile(vb, ci, g)) for ci, g in units}
        for ci, g in units:
            lhs = jnp.concatenate([tile(at, ci, g), tile(rt, ci, g)], axis=0)
            rhs = jnp.concatenate([pair_diag(tile(bt, ci, g)), pair_diag(tile(kt, ci, g))], axis=0)
            g_u[ci, g] = jnp.where(tril, _dot_nt(lhs, rhs), 0.0)
        m_u = {u: g_u[u][:c, :lanes] for u in units}
        t_u = {u: eye + m_u[u] for u in units}
        mb_u = {u: m_u[u].astype(BF16) for u in units}
        mb_u = {u: _dot(mb_u[u], pair_diag(mb_u[u])).astype(BF16) for u in units}
        power = 2
        while 2 * power < c:
            tm_u = {u: _dot(jnp.concatenate([t_u[u].astype(BF16), mb_u[u]], axis=0), pair_diag(mb_u[u]))
                    for u in units}
            t_u = {u: t_u[u] + tm_u[u][:c] for u in units}
            mb_u = {u: tm_u[u][c:].astype(BF16) for u in units}
            power *= 2
        t_u = {u: (t_u[u] + _dot(t_u[u].astype(BF16), pair_diag(mb_u[u]))).astype(BF16) for u in units}
        kv_u = {u: _dot(g_u[u][:, lanes:].astype(BF16), vd_u[u]) for u in units}
        wu_u = {(ci, g): _dot(t_u[ci, g], jnp.concatenate(
            [pair_diag(tile(at, ci, g)), pair_diag(kv_u[ci, g][:c].astype(BF16))], axis=1)) for ci, g in units}
        wr_u.update({(ci, g): jnp.concatenate([wu_u[ci, g][:, :lanes].astype(BF16), tile(rt, ci, g)], axis=0)
                     for ci, g in units})
        u0_u.update({u: wu_u[u][:, lanes:] for u in units})
        y0_u.update({u: kv_u[u][c:] for u in units})

    same_head = state_mask_ref[...] > 0.5
    n_seq = state_ref.shape[0]
    chunks_per_seq = n_chunks // n_seq
    chains = [(sq, g) for sq in range(n_seq) for g in range(n_groups)]
    s_g = {(sq, g): state_ref[sq, g] for sq, g in chains}
    y_u = {}
    for step in range(chunks_per_seq):
        ci_of = lambda sq: sq * chunks_per_seq + step
        pc = [pc_ref[sq, pl.ds(pc_row0 + step, 1), :] for sq in range(n_seq)]
        ws = {(sq, g): _dot_nt(wr_u[ci_of(sq), g], s_g[sq, g].astype(BF16)) for sq, g in chains}
        u_b = {(sq, g): (ws[sq, g][:c] + u0_u[ci_of(sq), g]).astype(BF16) for sq, g in chains}
        for sq, g in chains:
            ci = ci_of(sq)
            y_u[ci, g] = (ws[sq, g][c:]
                          + _dot(g_u[ci, g][c:, :lanes].astype(BF16), pair_diag(u_b[sq, g])) + y0_u[ci, g])
        s_g = {(sq, g): s_g[sq, g] * pc[sq][:, g * lanes:(g + 1) * lanes]
               + jnp.where(same_head,
                           _dot_tn(jnp.concatenate([u_b[sq, g], tile(vb, ci_of(sq), g)], axis=0),
                                   jnp.concatenate([tile(bc, ci_of(sq), g), tile(kc, ci_of(sq), g)], axis=0)),
                           0.0)
               for sq, g in chains}
    for sq, g in chains:
        state_ref[sq, g] = s_g[sq, g]
    y_all = jnp.concatenate([jnp.concatenate([y_u[ci, g] for g in range(n_groups)], axis=1)
                             for ci in range(n_chunks)], axis=0)
    return (_head_norm(y_all, GN_EPS_RWKV) * gnw_ref[...] + gnb_ref[...] + bonus) * gate


RET_STREAMS = 6
WKV_STREAMS = 9


def _prep_kernel(x_ref, pos_ref, g_ref, w_in_ref, freq_ref, xi_ref, zeta_ref,
                 mu_rkv_ref, mu_lora_ref, w0_ref, wup_ref, a0_ref, aup_ref, gup_ref, kk_ref, ka_ref, rk_ref,
                 tri_ref, *refs, tiles_per_seq):
    (q_ref, k_ref, qin_ref, kout_ref, vr_ref, gsilu_ref,
     at_ref, rt_ref, bt_ref, kt_ref, bc_ref, kc_ref, vw_ref, bonus_ref, gate_ref, pc_ref,
     w_ref, carry_rkv_ref, carry_lora_ref) = refs
    tm = x_ref.shape[0]
    w = RWKV_WIDTH
    rkv0 = RET_PROJ
    @pl.when(pl.program_id(0) == 0)
    def _():
        n_out = w_in_ref.shape[0]
        for lo in range(0, n_out, 128):
            hi = min(lo + 128, n_out)
            w_ref[:, lo:hi] = w_in_ref[lo:hi, :].T.astype(BF16)

    @pl.when(pl.program_id(0) % tiles_per_seq == 0)
    def _():
        carry_rkv_ref[...] = jnp.zeros_like(carry_rkv_ref)
        carry_lora_ref[...] = jnp.zeros_like(carry_lora_ref)

    h = _rms(x_ref[...], g_ref[...]).astype(BF16)

    wr = RET_WIDTH
    lanes = 2 * HEAD_DIM
    half = HEAD_DIM // 2
    p_q = _dot(h, w_ref[:, :wr])
    p_k = _dot(h, w_ref[:, wr:2 * wr])
    p_r = _dot(h, w_ref[:, rkv0:rkv0 + w])
    plora_raw = _dot(h, w_ref[:, rkv0 + 3 * w:])
    p_kw = _dot(h, w_ref[:, rkv0 + w:rkv0 + 2 * w])
    p_vw = _dot(h, w_ref[:, rkv0 + 2 * w:rkv0 + 3 * w])

    n_pack = lanes // half
    ang = pos_ref[...].astype(F32) * freq_ref[...]
    lane_p = lax.broadcasted_iota(jnp.int32, ang.shape, 1)

    def unpack(t):
        blocks = []
        for j in range(n_pack):
            g = pltpu.roll(t, (lanes - j * half) % lanes, 1) if j else t
            spread = g
            for m in range(1, n_pack):
                spread = jnp.where(lane_p < m * half, spread, pltpu.roll(g, m * half, 1))
            blocks.append(spread)
        return jnp.concatenate(blocks, axis=0)

    first_half = (lax.broadcasted_iota(jnp.int32, (tm, lanes), 1) % HEAD_DIM) < half
    cos = unpack(jnp.cos(ang))
    sin = unpack(jnp.sin(ang))
    sin = jnp.where(first_half, -sin, sin)
    k_scale = HEAD_DIM ** -0.5

    def rope(t, cos_t, sin_t):
        tiles = []
        for lo in range(0, wr, lanes):
            x = t[:, lo:lo + lanes]
            partner = jnp.where(first_half, pltpu.roll(x, lanes - half, 1), pltpu.roll(x, half, 1))
            tiles.append(x * cos_t + partner * sin_t)
        return jnp.concatenate(tiles, axis=1)

    def chunk_scaled(dst_ref, t, table_ref):
        for lo in range(0, tm, RET_CHUNK):
            dst_ref[lo:lo + RET_CHUNK, :] = (t[lo:lo + RET_CHUNK] * table_ref[...]).astype(BF16)

    q = rope(p_q, cos, sin)
    q_ref[...] = q.astype(BF16)
    chunk_scaled(qin_ref, q, xi_ref)

    kr = rope(p_k, cos * k_scale, sin * k_scale)
    k_ref[...] = kr.astype(BF16)
    chunk_scaled(kout_ref, kr, zeta_ref)

    p_v = _dot(h, w_ref[:, 2 * wr:3 * wr])
    p_g = _dot(h, w_ref[:, 3 * wr:RET_PROJ])
    r = _shift(p_r, carry_rkv_ref, mu_rkv_ref, 0)
    plora = _shift(plora_raw, carry_lora_ref, mu_lora_ref, 0)
    w_lr = plora[:, :DECAY_LORA]
    a_lr = plora[:, DECAY_LORA:DECAY_LORA + AAA_LORA]
    g_lr = plora[:, DECAY_LORA + AAA_LORA:]
    lw = -math.exp(-0.5) * jax.nn.sigmoid(w0_ref[...] + _dot(jnp.tanh(w_lr).astype(BF16), wup_ref[...]))
    a_sig = jax.nn.sigmoid(a0_ref[...] + _dot(a_lr.astype(BF16), aup_ref[...]))
    gate_ref[...] = _dot(jax.nn.sigmoid(g_lr).astype(BF16), gup_ref[...])

    k = _shift(p_kw, carry_rkv_ref, mu_rkv_ref, w)
    kk = k * kk_ref[...]
    k2 = k * (1.0 + (a_sig - 1.0) * ka_ref[...])
    kk = kk * (1.0 / jnp.maximum(jnp.sqrt(_head_sums(kk * kk)), 1e-12))
    a_vec = -kk
    b_vec = kk * a_sig

    v = _shift(p_vw, carry_rkv_ref, mu_rkv_ref, 2 * w)
    bonus_ref[...] = _head_sums(r * k2 * rk_ref[...]) * v
    vw_ref[...] = v.astype(BF16)

    c = WKV_CHUNK
    for ci in range(tm // c):
        rows = slice(ci * c, (ci + 1) * c)
        lw_c = lw[rows]
        cum = jnp.dot(tri_ref[...], lw_c, precision=lax.Precision.HIGHEST, preferred_element_type=F32)
        cum_end = cum[c - 1:, :]
        p_inv = jnp.exp(-cum)
        p_out = jnp.exp(cum_end - cum)
        at_ref[rows, :] = (a_vec[rows] * jnp.exp(cum - lw_c)).astype(BF16)
        rt_ref[rows, :] = (r[rows] * jnp.exp(cum)).astype(BF16)
        bt_ref[rows, :] = (b_vec[rows] * p_inv).astype(BF16)
        kt_ref[rows, :] = (k2[rows] * p_inv).astype(BF16)
        bc_ref[rows, :] = (b_vec[rows] * p_out).astype(BF16)
        kc_ref[rows, :] = (k2[rows] * p_out).astype(BF16)
        pc_ref[ci:ci + 1, :] = jnp.exp(cum_end)

    vr_ref[...] = p_v.astype(BF16)
    gsilu_ref[...] = p_g * jax.nn.sigmoid(p_g)


ROPE_PACK = 2 * HEAD_DIM // (HEAD_DIM // 2)


def _packed_positions(positions, tm):
    lanes_per_group = 128 // ROPE_PACK
    tiles = positions.reshape(-1, ROPE_PACK, tm // ROPE_PACK)
    return jnp.repeat(jnp.swapaxes(tiles, 1, 2), lanes_per_group, axis=2).reshape(-1, 128)


def _prep(x2d, pos2d, seq_len, consts, tm=PREP_ROWS):
    t = x2d.shape[0]
    assert t % tm == 0 and seq_len % tm == 0 and tm % RET_CHUNK == 0 and tm % WKV_CHUNK == 0
    row = lambda i: (i, 0)
    width = RET_WIDTH
    out_dtypes = [BF16] * 5 + [F32] + [BF16] * 7 + [F32] * 2
    assert len(out_dtypes) == RET_STREAMS + WKV_STREAMS
    n_pc = tm // WKV_CHUNK
    return pl.pallas_call(
        functools.partial(_prep_kernel, tiles_per_seq=seq_len // tm),
        grid=(t // tm,),
        in_specs=[pl.BlockSpec((tm, D_MODEL), row), pl.BlockSpec((tm // ROPE_PACK, 128), row)]
                 + [_full(a.shape) for a in consts],
        out_specs=[pl.BlockSpec((tm, width), row) for _ in out_dtypes] + [pl.BlockSpec((n_pc, width), row)],
        out_shape=[jax.ShapeDtypeStruct((t, width), dt) for dt in out_dtypes]
                  + [jax.ShapeDtypeStruct((t // WKV_CHUNK, width), F32)],
        scratch_shapes=[pltpu.VMEM(consts[1].shape[::-1], BF16), pltpu.VMEM((1, 3 * RWKV_WIDTH), F32),
                        pltpu.VMEM((1, LORA_WIDTH), F32)],
        compiler_params=_params(1),
        name="prep",
    )(x2d, pos2d, *consts)


def _mixers_kernel(*refs):
    ret_in, refs = refs[:RET_STREAMS], refs[RET_STREAMS:]
    wkv_in, refs = refs[:WKV_STREAMS], refs[WKV_STREAMS:]
    (pc_ref, dmask_ref, gc_ref, same_head_ref, ret_gnw_ref, ret_gnb_ref, gnw_ref, gnb_ref, mask_ref, eye_ref,
     o_ref, ret_state_ref, wkv_state_ref) = refs
    ti = pl.program_id(1)

    @pl.when(ti == 0)
    def _():
        ret_state_ref[...] = jnp.zeros_like(ret_state_ref)
        wkv_state_ref[...] = jnp.zeros_like(wkv_state_ref)

    stacked = lambda ref: ref[...].reshape(ref.shape[0] * ref.shape[1], ref.shape[2])
    y_ret = _retention_body(*[stacked(r) for r in ret_in], dmask_ref, gc_ref, same_head_ref, ret_gnw_ref,
                            ret_gnb_ref, ret_state_ref)
    chunks_per_block = o_ref.shape[1] // WKV_CHUNK
    y_rwkv = _wkv_body(*[stacked(r) for r in wkv_in], pc_ref, ti * chunks_per_block, gnw_ref, gnb_ref, mask_ref,
                       eye_ref, same_head_ref, wkv_state_ref)
    o_ref[...] = jnp.concatenate([y_ret, y_rwkv], axis=1).astype(o_ref.dtype).reshape(o_ref.shape)


def _mixers(streams, pc, dmask_pairs, gc_w, ret_gn_w, ret_gn_b, gn_w, gn_b):
    b, s, width = streams[0].shape
    n_seq, tb, c = MIX_SEQS, MIX_BLOCK, WKV_CHUNK
    assert b % n_seq == 0 and s % tb == 0 and tb % RET_CHUNK == 0 and tb % c == 0
    idx = jnp.arange(c)
    incl = (idx[:, None] >= idx[None, :]).astype(F32)
    strict = (idx[:, None] > idx[None, :]).astype(F32)
    mask = jnp.concatenate([jnp.tile(strict, (1, 4)), jnp.tile(incl, (1, 4))], axis=0)
    eye = jnp.tile(jnp.eye(c, dtype=F32), (1, 2))
    head_of_lane = jnp.arange(2 * HEAD_DIM) // HEAD_DIM
    state_mask = (head_of_lane[:, None] == head_of_lane[None, :]).astype(F32)
    consts = [dmask_pairs, gc_w, state_mask, ret_gn_w, ret_gn_b, gn_w, gn_b, mask, eye]
    blk = lambda bi, ti: (bi, ti, 0)
    state = pltpu.VMEM((n_seq, N_HEADS // 2, 2 * HEAD_DIM, 2 * HEAD_DIM), F32)
    return pl.pallas_call(
        _mixers_kernel,
        grid=(b // n_seq, s // tb),
        in_specs=[pl.BlockSpec((n_seq, tb, width), blk) for _ in streams]
                 + [pl.BlockSpec((n_seq, s // c, width), lambda bi, ti: (bi, 0, 0))]
                 + [_full(a.shape) for a in consts],
        out_specs=pl.BlockSpec((n_seq, tb, RET_WIDTH + RWKV_WIDTH), blk),
        out_shape=jax.ShapeDtypeStruct((b, s, RET_WIDTH + RWKV_WIDTH), BF16),
        scratch_shapes=[state, state],
        compiler_params=_params(2),
        name="mixers",
    )(*streams, pc, *consts)


def _mem_kv_kernel(m_ref, g_ref, w_f32_ref, o_ref, w_ref):
    _cast_weights_once(pl.program_id(0) == 0, [(w_f32_ref, w_ref)])
    h = _rms(m_ref[...], g_ref[...]).astype(BF16)
    o_ref[...] = _dot(h, w_ref[...]).astype(o_ref.dtype)


def _mem_kv(mem2d, g, w_kv, tm=1024):
    t = mem2d.shape[0]
    n = w_kv.shape[1]
    tm = min(tm, t)
    assert t % tm == 0
    return pl.pallas_call(
        _mem_kv_kernel,
        grid=(t // tm,),
        in_specs=[pl.BlockSpec((tm, D_MODEL), lambda i: (i, 0)), _full(g.shape), _full(w_kv.shape)],
        out_specs=pl.BlockSpec((tm, n), lambda i: (i, 0)),
        out_shape=jax.ShapeDtypeStruct((t, n), BF16),
        scratch_shapes=[pltpu.VMEM(w_kv.shape, BF16)],
        compiler_params=_params(1),
        name="mem_kv",
    )(mem2d, g, w_kv)


def _post_mix_kernel(x_ref, y_ref, wout_f32_ref, gx_ref, wq_f32_ref, kv_ref, wo_f32_ref, o_ref,
                     wout_ref, wq_ref, wo_ref):
    first = jnp.logical_and(pl.program_id(0) == 0, pl.program_id(1) == 0)
    _cast_weights_once(first, [(wout_f32_ref, wout_ref), (wq_f32_ref, wq_ref), (wo_f32_ref, wo_ref)])
    x1 = x_ref[0] + _dot(y_ref[0], wout_ref[...])
    q = _dot(_rms(x1, gx_ref[...]).astype(BF16), wq_ref[...]).astype(BF16)
    kv = kv_ref[0]
    heads = []
    for h in range(XATTN_HEADS):
        sl = slice(h * XATTN_HEAD_DIM, (h + 1) * XATTN_HEAD_DIM)
        s = _dot_nt(q[:, sl], kv[:, sl]) * (XATTN_HEAD_DIM ** -0.5)
        e = jnp.exp(s - jnp.max(s, axis=-1, keepdims=True))
        prob = e / jnp.sum(e, axis=-1, keepdims=True)
        heads.append(_dot(prob.astype(BF16), kv[:, D_MODEL + h * XATTN_HEAD_DIM:D_MODEL + (h + 1) * XATTN_HEAD_DIM]))
    o = jnp.concatenate(heads, axis=1).astype(BF16)
    o_ref[0] = x1 + _dot(o, wo_ref[...])


def _post_mix(x, y, w_out, g_x, w_q, kv, w_o, tm=1024):
    b, s, _ = x.shape
    m = kv.shape[1]
    assert s % tm == 0
    blk = lambda bi, ti: (bi, ti, 0)
    return pl.pallas_call(
        _post_mix_kernel,
        grid=(b, s // tm),
        in_specs=[pl.BlockSpec((1, tm, D_MODEL), blk), pl.BlockSpec((1, tm, y.shape[2]), blk),
                  _full(w_out.shape), _full(g_x.shape), _full(w_q.shape),
                  pl.BlockSpec((1, m, 2 * D_MODEL), lambda bi, ti: (bi, 0, 0)), _full(w_o.shape)],
        out_specs=pl.BlockSpec((1, tm, D_MODEL), blk),
        out_shape=jax.ShapeDtypeStruct((b, s, D_MODEL), F32),
        scratch_shapes=[pltpu.VMEM(a.shape, BF16) for a in (w_out, w_q, w_o)],
        compiler_params=_params(2),
        name="post_mix",
    )(x, y, w_out, g_x, w_q, kv, w_o)


def _mlp_kernel(x_ref, g_ref, wup_ref, wdown_ref, gf_ref, o_ref, *, tf, final_norm):
    x = x_ref[...]
    h = _rms(x, g_ref[...]).astype(BF16)
    acc = x
    for j in range(D_FF // tf):
        u = jnp.maximum(_dot(h, wup_ref[:, j * tf:(j + 1) * tf]), 0.0)
        acc = acc + _dot((u * u).astype(BF16), wdown_ref[j * tf:(j + 1) * tf, :])
    o_ref[...] = _rms(acc, gf_ref[...]) if final_norm else acc


def _mlp(x2d, g, w_up, w_down, g_final, final_norm, tm=1024, tf=512):
    t = x2d.shape[0]
    assert t % tm == 0 and D_FF % tf == 0
    row = lambda i: (i, 0)
    return pl.pallas_call(
        functools.partial(_mlp_kernel, tf=tf, final_norm=final_norm),
        grid=(t // tm,),
        in_specs=[pl.BlockSpec((tm, D_MODEL), row), _full(g.shape), _full(w_up.shape), _full(w_down.shape),
                  _full(g_final.shape)],
        out_specs=pl.BlockSpec((tm, D_MODEL), row),
        out_shape=jax.ShapeDtypeStruct((t, D_MODEL), F32),
        compiler_params=_params(1),
        name="mlp",
    )(x2d, g, w_up, w_down, g_final)


def kernel(x, mem, positions, norm_mix, w_in, ret_gn_w, ret_gn_b, rwkv_mu, rwkv_w0, rwkv_w_up, rwkv_a0,
           rwkv_a_up, rwkv_g_up, rwkv_k_k, rwkv_k_a, rwkv_r_k, rwkv_gn_w, rwkv_gn_b, w_out, norm_xattn,
           norm_mem, xattn_w_q, xattn_w_kv, xattn_w_o, norm_mlp, mlp_w_up, mlp_w_down, norm_final):
    b, s, dm = x.shape
    n_layers = w_in.shape[0]
    freq, xi_w, zeta_w, dmask_pairs, gc_w = _retention_consts()
    tri = (jnp.arange(WKV_CHUNK)[:, None] >= jnp.arange(WKV_CHUNK)[None, :]).astype(F32)
    row = lambda a: a.reshape(1, -1)
    for l in range(n_layers):
        mu = rwkv_mu[l]
        prep_consts = [row(norm_mix[l]), jnp.swapaxes(w_in[l], 0, 1), freq, xi_w, zeta_w, row(mu[:3 * RWKV_WIDTH]), row(mu[3 * RWKV_WIDTH:]),
                       row(rwkv_w0[l]), rwkv_w_up[l].astype(BF16), row(rwkv_a0[l]), rwkv_a_up[l].astype(BF16),
                       rwkv_g_up[l].astype(BF16), row(rwkv_k_k[l]), row(rwkv_k_a[l]), row(rwkv_r_k[l]), tri]
        *streams, pc = _prep(x.reshape(b * s, dm), _packed_positions(positions, PREP_ROWS), s, prep_consts)
        y = _mixers([a.reshape(b, s, -1) for a in streams], pc.reshape(b, s // WKV_CHUNK, -1), dmask_pairs, gc_w,
                    row(ret_gn_w[l]), row(ret_gn_b[l]), row(rwkv_gn_w[l]), row(rwkv_gn_b[l]))
        kv = _mem_kv(mem.reshape(-1, dm), norm_mem[l][None, :], xattn_w_kv[l])
        x = _post_mix(x, y, w_out[l], norm_xattn[l][None, :], xattn_w_q[l], kv.reshape(b, -1, 2 * dm),
                      xattn_w_o[l])
        x = _mlp(x.reshape(b * s, dm), norm_mlp[l][None, :], mlp_w_up[l].astype(BF16),
                 mlp_w_down[l].astype(BF16), norm_final[None, :], l == n_layers - 1).reshape(b, s, dm)
    return x
```

```python
import functools
import math

import jax
import jax.numpy as jnp
from jax import lax
from jax.experimental import pallas as pl
from jax.experimental.pallas import tpu as pltpu

D_MODEL = 1024
HEAD_DIM = 64
RET_WIDTH = 512
RWKV_WIDTH = 512
N_HEADS = 8
RET_CHUNK = 128
ROPE_BASE = 10000.0
DECAY_LORA = 64
AAA_LORA = 64
GATE_LORA = 160
LORA_WIDTH = DECAY_LORA + AAA_LORA + GATE_LORA
RET_PROJ = 4 * RET_WIDTH
XATTN_HEADS = 4
XATTN_HEAD_DIM = D_MODEL // XATTN_HEADS
D_FF = 4 * D_MODEL
RMS_EPS = 1e-6
GN_EPS_RET = 1e-5
GN_EPS_RWKV = 64e-5

WKV_CHUNK = 64
MIX_SEQS = 4
MIX_BLOCK = 128
WKV_WAVE = 16
PREP_ROWS = 512
MLP_STAGE_BYTES = 2 * 1024 * 1024
CAST_SLAB = 512
VMEM_LIMIT_BYTES = 56 * 1024 * 1024

BF16 = jnp.bfloat16
F32 = jnp.float32


def _dot(a, b):
    return jnp.dot(a, b, preferred_element_type=F32)


def _dot_nt(a, b):
    return lax.dot_general(a, b, (((1,), (1,)), ((), ())), preferred_element_type=F32)


def _dot_tn(a, b):
    return lax.dot_general(a, b, (((0,), (0,)), ((), ())), preferred_element_type=F32)


def _rms(x, g):
    return x * lax.rsqrt(jnp.mean(x * x, axis=-1, keepdims=True) + RMS_EPS) * g


def _params(n_axes):
    return pltpu.CompilerParams(dimension_semantics=("arbitrary",) * n_axes,
                                vmem_limit_bytes=VMEM_LIMIT_BYTES)


def _full(shape):
    zeros = (0,) * len(shape)
    return pl.BlockSpec(shape, lambda *_: zeros, pipeline_mode=pl.Buffered(1))


def _cast_weights_once(first_step, pairs):
    @pl.when(first_step)
    def _():
        for src_ref, dst_ref in pairs:
            n = src_ref.shape[1]
            for lo in range(0, n, CAST_SLAB):
                hi = min(lo + CAST_SLAB, n)
                dst_ref[:, lo:hi] = src_ref[:, lo:hi].astype(dst_ref.dtype)


def _head_sums(x):
    lanes = 2 * HEAD_DIM
    in_h0 = lax.broadcasted_iota(jnp.int32, (x.shape[0], lanes), 1) < HEAD_DIM
    tiles = []
    for lo in range(0, x.shape[1], lanes):
        t = x[:, lo:lo + lanes]
        s0 = jnp.sum(jnp.where(in_h0, t, 0.0), axis=-1, keepdims=True)
        s1 = jnp.sum(jnp.where(in_h0, 0.0, t), axis=-1, keepdims=True)
        tiles.append(jnp.where(in_h0, s0, s1))
    return jnp.concatenate(tiles, axis=1)


def _head_norm(y, eps):
    d = y - _head_sums(y) * (1.0 / HEAD_DIM)
    var = _head_sums(d * d) * (1.0 / HEAD_DIM)
    return d * lax.rsqrt(var + eps)


def _pair_diag(x):
    in_h0 = lax.broadcasted_iota(jnp.int32, x.shape, 1) < HEAD_DIM
    zero = jnp.zeros_like(x)
    return jnp.concatenate([jnp.where(in_h0, x, zero), jnp.where(in_h0, zero, x)], axis=0)


def _retention_body(q, k, q_in, k_out, vb, gsilu, dmask_ref, gc_ref, same_head_ref, gnw_ref, gnb_ref, state_ref):
    c = RET_CHUNK
    lanes = 2 * HEAD_DIM
    n_groups = RET_WIDTH // lanes
    n_chunks = q.shape[0] // c
    units = [(ci, g) for ci in range(n_chunks) for g in range(n_groups)]
    tile = lambda x, ci, g: x[ci * c:(ci + 1) * c, g * lanes:(g + 1) * lanes]
    s_u = {(ci, g): (_dot_nt(tile(q, ci, g), _pair_diag(tile(k, ci, g))) * dmask_ref[g]).astype(BF16)
           for ci, g in units}
    y_u = {(ci, g): _dot(s_u[ci, g], _pair_diag(tile(vb, ci, g))) for ci, g in units}
    kv_u = {(ci, g): _dot_tn(tile(k_out, ci, g), tile(vb, ci, g)) for ci, g in units}
    same_head = same_head_ref[...] > 0.5
    n_seq = state_ref.shape[0]
    chunks_per_seq = n_chunks // n_seq
    chains = [(sq, g) for sq in range(n_seq) for g in range(n_groups)]
    r_g = {(sq, g): state_ref[sq, g] for sq, g in chains}
    for step in range(chunks_per_seq):
        for sq, g in chains:
            ci = sq * chunks_per_seq + step
            y_u[ci, g] = y_u[ci, g] + _dot(tile(q_in, ci, g), r_g[sq, g].astype(BF16))
        r_g = {(sq, g): r_g[sq, g] * gc_ref[g] + jnp.where(same_head, kv_u[sq * chunks_per_seq + step, g], 0.0)
               for sq, g in chains}
    for sq, g in chains:
        state_ref[sq, g] = r_g[sq, g]
    y_all = jnp.concatenate([jnp.concatenate([y_u[ci, g] for g in range(n_groups)], axis=1)
                             for ci in range(n_chunks)], axis=0)
    return gsilu * (_head_norm(y_all, GN_EPS_RET) * gnw_ref[...] + gnb_ref[...])


def _retention_consts():
    c = RET_CHUNK
    half = HEAD_DIM // 2
    inv_freq = ROPE_BASE ** (-jnp.arange(half, dtype=F32) / half)
    freq = jnp.tile(inv_freq, 128 // half)[None, :]
    log_g = jnp.log(1.0 - 2.0 ** (-5.0 - jnp.arange(N_HEADS, dtype=F32)))
    idx = jnp.arange(c, dtype=F32)
    diff = idx[:, None] - idx[None, :]
    causal = diff >= 0
    dmask = jnp.where(causal[None], jnp.exp(log_g[:, None, None] * jnp.where(causal, diff, 0.0)[None]), 0.0)
    xi = jnp.exp(log_g[:, None] * (idx + 1.0)[None])
    zeta = jnp.exp(log_g[:, None] * (c - 1.0 - idx)[None])
    g_chunk = jnp.exp(log_g * c)
    xi_w = jnp.repeat(xi.T, HEAD_DIM, axis=1)
    zeta_w = jnp.repeat(zeta.T, HEAD_DIM, axis=1)
    dmask_pairs = jnp.concatenate([dmask[0::2], dmask[1::2]], axis=2)
    gc_rows = jnp.repeat(g_chunk, HEAD_DIM).reshape(N_HEADS // 2, 2 * HEAD_DIM, 1)
    gc_w = jnp.broadcast_to(gc_rows, (N_HEADS // 2, 2 * HEAD_DIM, 2 * HEAD_DIM))
    return freq, xi_w, zeta_w, dmask_pairs, gc_w


def _shift(p, carry_ref, mu_ref, lo):
    cols = slice(lo, lo + p.shape[1])
    rows = lax.broadcasted_iota(jnp.int32, p.shape, 0)
    prev = jnp.where(rows == 0, carry_ref[:, cols], pltpu.roll(p, 1, 0))
    carry_ref[:, cols] = p[p.shape[0] - 1:, :]
    return p + (prev - p) * mu_ref[:, cols]


def _wkv_body(at, rt, bt, kt, bc, kc, vb, bonus, gate, pc_ref, pc_row0, gnw_ref, gnb_ref, mask_ref, eye_ref,
              state_mask_ref, state_ref):
    c = WKV_CHUNK
    w = RWKV_WIDTH
    lanes = 2 * HEAD_DIM
    n_groups = w // lanes
    n_chunks = at.shape[0] // c
    eye = eye_ref[...]
    tril = mask_ref[...] > 0.5

    pair_diag = _pair_diag
    all_units = [(ci, g) for ci in range(n_chunks) for g in range(n_groups)]
    tile = lambda x, ci, g: x[ci * c:(ci + 1) * c, g * lanes:(g + 1) * lanes]
    g_u, wr_u, u0_u, y0_u = {}, {}, {}, {}
    for lo in range(0, len(all_units), WKV_WAVE):
        units = all_units[lo:lo + WKV_WAVE]
        vd_u = {(ci, g): pair_diag(tile(vb, ci, g)) for ci, g in units}
        for ci, g in units:
            lhs = jnp.concatenate([tile(at, ci, g), tile(rt, ci, g)], axis=0)
            rhs = jnp.concatenate([pair_diag(tile(bt, ci, g)), pair_diag(tile(kt, ci, g))], axis=0)
            g_u[ci, g] = jnp.where(tril, _dot_nt(lhs, rhs), 0.0)
        m_u = {u: g_u[u][:c, :lanes] for u in units}
        t_u = {u: eye + m_u[u] for u in units}
        mb_u = {u: m_u[u].astype(BF16) for u in units}
        mb_u = {u: _dot(mb_u[u], pair_diag(mb_u[u])).astype(BF16) for u in units}
        power = 2
        while 2 * power < c:
            tm_u = {u: _dot(jnp.concatenate([t_u[u].astype(BF16), mb_u[u]], axis=0), pair_diag(mb_u[u]))
                    for u in units}
            t_u = {u: t_u[u] + tm_u[u][:c] for u in units}
            mb_u = {u: tm_u[u][c:].astype(BF16) for u in units}
            power *= 2
        t_u = {u: (t_u[u] + _dot(t_u[u].astype(BF16), pair_diag(mb_u[u]))).astype(BF16) for u in units}
        kv_u = {u: _dot(g_u[u][:, lanes:].astype(BF16), vd_u[u]) for u in units}
        wu_u = {(ci, g): _dot(t_u[ci, g], jnp.concatenate(
            [pair_diag(tile(at, ci, g)), pair_diag(kv_u[ci, g][:c].astype(BF16))], axis=1)) for ci, g in units}
        wr_u.update({(ci, g): jnp.concatenate([wu_u[ci, g][:, :lanes].astype(BF16), tile(rt, ci, g)], axis=0)
                     for ci, g in units})
        u0_u.update({u: wu_u[u][:, lanes:] for u in units})
        y0_u.update({u: kv_u[u][c:] for u in units})

    same_head = state_mask_ref[...] > 0.5
    n_seq = state_ref.shape[0]
    chunks_per_seq = n_chunks // n_seq
    chains = [(sq, g) for sq in range(n_seq) for g in range(n_groups)]
    s_g = {(sq, g): state_ref[sq, g] for sq, g in chains}
    y_u = {}
    for step in range(chunks_per_seq):
        ci_of = lambda sq: sq * chunks_per_seq + step
        pc = [pc_ref[sq, pl.ds(pc_row0 + step, 1), :] for sq in range(n_seq)]
        ws = {(sq, g): _dot_nt(wr_u[ci_of(sq), g], s_g[sq, g].astype(BF16)) for sq, g in chains}
        u_b = {(sq, g): (ws[sq, g][:c] + u0_u[ci_of(sq), g]).astype(BF16) for sq, g in chains}
        for sq, g in chains:
            ci = ci_of(sq)
            y_u[ci, g] = (ws[sq, g][c:]
                          + _dot(g_u[ci, g][c:, :lanes].astype(BF16), pair_diag(u_b[sq, g])) + y0_u[ci, g])
        s_g = {(sq, g): s_g[sq, g] * pc[sq][:, g * lanes:(g + 1) * lanes]
               + jnp.where(same_head,
                           _dot_tn(jnp.concatenate([u_b[sq, g], tile(vb, ci_of(sq), g)], axis=0),
                                   jnp.concatenate([tile(bc, ci_of(sq), g), tile(kc, ci_of(sq), g)], axis=0)),
                           0.0)
               for sq, g in chains}
    for sq, g in chains:
        state_ref[sq, g] = s_g[sq, g]
    y_all = jnp.concatenate([jnp.concatenate([y_u[ci, g] for g in range(n_groups)], axis=1)
                             for ci in range(n_chunks)], axis=0)
    return (_head_norm(y_all, GN_EPS_RWKV) * gnw_ref[...] + gnb_ref[...] + bonus) * gate


RET_STREAMS = 6
WKV_STREAMS = 9


def _prep_kernel(x_ref, pos_ref, g_ref, w_in_ref, freq_ref, xi_ref, zeta_ref,
                 mu_rkv_ref, mu_lora_ref, w0_ref, wup_ref, a0_ref, aup_ref, gup_ref, kk_ref, ka_ref, rk_ref,
                 tri_ref, *refs, tiles_per_seq):
    (q_ref, k_ref, qin_ref, kout_ref, vr_ref, gsilu_ref,
     at_ref, rt_ref, bt_ref, kt_ref, bc_ref, kc_ref, vw_ref, bonus_ref, gate_ref, pc_ref,
     w_ref, carry_rkv_ref, carry_lora_ref) = refs
    tm = x_ref.shape[0]
    w = RWKV_WIDTH
    rkv0 = RET_PROJ
    @pl.when(pl.program_id(0) == 0)
    def _():
        n_out = w_in_ref.shape[0]
        for lo in range(0, n_out, 128):
            hi = min(lo + 128, n_out)
            w_ref[:, lo:hi] = w_in_ref[lo:hi, :].T.astype(BF16)

    @pl.when(pl.program_id(0) % tiles_per_seq == 0)
    def _():
        carry_rkv_ref[...] = jnp.zeros_like(carry_rkv_ref)
        carry_lora_ref[...] = jnp.zeros_like(carry_lora_ref)

    h = _rms(x_ref[...], g_ref[...]).astype(BF16)

    wr = RET_WIDTH
    lanes = 2 * HEAD_DIM
    half = HEAD_DIM // 2
    p_q = _dot(h, w_ref[:, :wr])
    p_k = _dot(h, w_ref[:, wr:2 * wr])
    p_r = _dot(h, w_ref[:, rkv0:rkv0 + w])
    plora_raw = _dot(h, w_ref[:, rkv0 + 3 * w:])
    p_kw = _dot(h, w_ref[:, rkv0 + w:rkv0 + 2 * w])
    p_vw = _dot(h, w_ref[:, rkv0 + 2 * w:rkv0 + 3 * w])

    n_pack = lanes // half
    ang = pos_ref[...].astype(F32) * freq_ref[...]
    lane_p = lax.broadcasted_iota(jnp.int32, ang.shape, 1)

    def unpack(t):
        blocks = []
        for j in range(n_pack):
            g = pltpu.roll(t, (lanes - j * half) % lanes, 1) if j else t
            spread = g
            for m in range(1, n_pack):
                spread = jnp.where(lane_p < m * half, spread, pltpu.roll(g, m * half, 1))
            blocks.append(spread)
        return jnp.concatenate(blocks, axis=0)

    first_half = (lax.broadcasted_iota(jnp.int32, (tm, lanes), 1) % HEAD_DIM) < half
    cos = unpack(jnp.cos(ang))
    sin = unpack(jnp.sin(ang))
    sin = jnp.where(first_half, -sin, sin)
    k_scale = HEAD_DIM ** -0.5

    def rope(t, cos_t, sin_t):
        tiles = []
        for lo in range(0, wr, lanes):
            x = t[:, lo:lo + lanes]
            partner = jnp.where(first_half, pltpu.roll(x, lanes - half, 1), pltpu.roll(x, half, 1))
            tiles.append(x * cos_t + partner * sin_t)
        return jnp.concatenate(tiles, axis=1)

    def chunk_scaled(dst_ref, t, table_ref):
        for lo in range(0, tm, RET_CHUNK):
            dst_ref[lo:lo + RET_CHUNK, :] = (t[lo:lo + RET_CHUNK] * table_ref[...]).astype(BF16)

    q = rope(p_q, cos, sin)
    q_ref[...] = q.astype(BF16)
    chunk_scaled(qin_ref, q, xi_ref)

    kr = rope(p_k, cos * k_scale, sin * k_scale)
    k_ref[...] = kr.astype(BF16)
    chunk_scaled(kout_ref, kr, zeta_ref)

    p_v = _dot(h, w_ref[:, 2 * wr:3 * wr])
    p_g = _dot(h, w_ref[:, 3 * wr:RET_PROJ])
    r = _shift(p_r, carry_rkv_ref, mu_rkv_ref, 0)
    plora = _shift(plora_raw, carry_lora_ref, mu_lora_ref, 0)
    w_lr = plora[:, :DECAY_LORA]
    a_lr = plora[:, DECAY_LORA:DECAY_LORA + AAA_LORA]
    g_lr = plora[:, DECAY_LORA + AAA_LORA:]
    lw = -math.exp(-0.5) * jax.nn.sigmoid(
        w0_ref[...] + _dot(jnp.tanh(w_lr).astype(BF16), wup_ref[...].astype(BF16)))
    a_sig = jax.nn.sigmoid(a0_ref[...] + _dot(a_lr.astype(BF16), aup_ref[...].astype(BF16)))
    gate_ref[...] = _dot(jax.nn.sigmoid(g_lr).astype(BF16), gup_ref[...].astype(BF16))

    k = _shift(p_kw, carry_rkv_ref, mu_rkv_ref, w)
    kk = k * kk_ref[...]
    k2 = k * (1.0 + (a_sig - 1.0) * ka_ref[...])
    kk = kk * (1.0 / jnp.maximum(jnp.sqrt(_head_sums(kk * kk)), 1e-12))
    a_vec = -kk
    b_vec = kk * a_sig

    v = _shift(p_vw, carry_rkv_ref, mu_rkv_ref, 2 * w)
    bonus_ref[...] = _head_sums(r * k2 * rk_ref[...]) * v
    vw_ref[...] = v.astype(BF16)

    c = WKV_CHUNK
    for ci in range(tm // c):
        rows = slice(ci * c, (ci + 1) * c)
        lw_c = lw[rows]
        cum = jnp.dot(tri_ref[...], lw_c, precision=lax.Precision.HIGHEST, preferred_element_type=F32)
        cum_end = cum[c - 1:, :]
        p_inv = jnp.exp(-cum)
        p_end = jnp.exp(cum_end)
        p_out = p_inv * p_end
        at_ref[rows, :] = (a_vec[rows] * jnp.exp(cum - lw_c)).astype(BF16)
        rt_ref[rows, :] = (r[rows] * jnp.exp(cum)).astype(BF16)
        bt_ref[rows, :] = (b_vec[rows] * p_inv).astype(BF16)
        kt_ref[rows, :] = (k2[rows] * p_inv).astype(BF16)
        bc_ref[rows, :] = (b_vec[rows] * p_out).astype(BF16)
        kc_ref[rows, :] = (k2[rows] * p_out).astype(BF16)
        pc_ref[ci:ci + 1, :] = p_end

    vr_ref[...] = p_v.astype(BF16)
    gsilu_ref[...] = p_g * jax.nn.sigmoid(p_g)


ROPE_PACK = 2 * HEAD_DIM // (HEAD_DIM // 2)


def _packed_positions(positions, tm):
    lanes_per_group = 128 // ROPE_PACK
    tiles = positions.reshape(-1, ROPE_PACK, tm // ROPE_PACK)
    return jnp.repeat(jnp.swapaxes(tiles, 1, 2), lanes_per_group, axis=2).reshape(-1, 128)


def _prep(x2d, pos2d, seq_len, consts, tm=PREP_ROWS):
    t = x2d.shape[0]
    assert t % tm == 0 and seq_len % tm == 0 and tm % RET_CHUNK == 0 and tm % WKV_CHUNK == 0
    row = lambda i: (i, 0)
    width = RET_WIDTH
    out_dtypes = [BF16] * 5 + [F32] + [BF16] * 7 + [F32] * 2
    assert len(out_dtypes) == RET_STREAMS + WKV_STREAMS
    n_pc = tm // WKV_CHUNK
    return pl.pallas_call(
        functools.partial(_prep_kernel, tiles_per_seq=seq_len // tm),
        grid=(t // tm,),
        in_specs=[pl.BlockSpec((tm, D_MODEL), row), pl.BlockSpec((tm // ROPE_PACK, 128), row)]
                 + [_full(a.shape) for a in consts],
        out_specs=[pl.BlockSpec((tm, width), row) for _ in out_dtypes] + [pl.BlockSpec((n_pc, width), row)],
        out_shape=[jax.ShapeDtypeStruct((t, width), dt) for dt in out_dtypes]
                  + [jax.ShapeDtypeStruct((t // WKV_CHUNK, width), F32)],
        scratch_shapes=[pltpu.VMEM(consts[1].shape[::-1], BF16), pltpu.VMEM((1, 3 * RWKV_WIDTH), F32),
                        pltpu.VMEM((1, LORA_WIDTH), F32)],
        compiler_params=_params(1),
        name="prep",
    )(x2d, pos2d, *consts)


def _mixers_kernel(*refs):
    ret_in, refs = refs[:RET_STREAMS], refs[RET_STREAMS:]
    wkv_in, refs = refs[:WKV_STREAMS], refs[WKV_STREAMS:]
    (pc_ref, dmask_ref, gc_ref, same_head_ref, ret_gnw_ref, ret_gnb_ref, gnw_ref, gnb_ref, mask_ref, eye_ref,
     o_ref, ret_state_ref, wkv_state_ref) = refs
    ti = pl.program_id(1)

    @pl.when(ti == 0)
    def _():
        ret_state_ref[...] = jnp.zeros_like(ret_state_ref)
        wkv_state_ref[...] = jnp.zeros_like(wkv_state_ref)

    stacked = lambda ref: ref[...].reshape(ref.shape[0] * ref.shape[1], ref.shape[2])
    y_ret = _retention_body(*[stacked(r) for r in ret_in], dmask_ref, gc_ref, same_head_ref, ret_gnw_ref,
                            ret_gnb_ref, ret_state_ref)
    chunks_per_block = o_ref.shape[1] // WKV_CHUNK
    y_rwkv = _wkv_body(*[stacked(r) for r in wkv_in], pc_ref, ti * chunks_per_block, gnw_ref, gnb_ref, mask_ref,
                       eye_ref, same_head_ref, wkv_state_ref)
    o_ref[...] = jnp.concatenate([y_ret, y_rwkv], axis=1).astype(o_ref.dtype).reshape(o_ref.shape)


def _mixers(streams, pc, dmask_pairs, gc_w, ret_gn_w, ret_gn_b, gn_w, gn_b):
    b, s, width = streams[0].shape
    n_seq, tb, c = MIX_SEQS, MIX_BLOCK, WKV_CHUNK
    assert b % n_seq == 0 and s % tb == 0 and tb % RET_CHUNK == 0 and tb % c == 0
    idx = jnp.arange(c)
    incl = (idx[:, None] >= idx[None, :]).astype(F32)
    strict = (idx[:, None] > idx[None, :]).astype(F32)
    mask = jnp.concatenate([jnp.tile(strict, (1, 4)), jnp.tile(incl, (1, 4))], axis=0)
    eye = jnp.tile(jnp.eye(c, dtype=F32), (1, 2))
    head_of_lane = jnp.arange(2 * HEAD_DIM) // HEAD_DIM
    state_mask = (head_of_lane[:, None] == head_of_lane[None, :]).astype(F32)
    consts = [dmask_pairs, gc_w, state_mask, ret_gn_w, ret_gn_b, gn_w, gn_b, mask, eye]
    blk = lambda bi, ti: (bi, ti, 0)
    state = pltpu.VMEM((n_seq, N_HEADS // 2, 2 * HEAD_DIM, 2 * HEAD_DIM), F32)
    return pl.pallas_call(
        _mixers_kernel,
        grid=(b // n_seq, s // tb),
        in_specs=[pl.BlockSpec((n_seq, tb, width), blk) for _ in streams]
                 + [pl.BlockSpec((n_seq, s // c, width), lambda bi, ti: (bi, 0, 0))]
                 + [_full(a.shape) for a in consts],
        out_specs=pl.BlockSpec((n_seq, tb, RET_WIDTH + RWKV_WIDTH), blk),
        out_shape=jax.ShapeDtypeStruct((b, s, RET_WIDTH + RWKV_WIDTH), BF16),
        scratch_shapes=[state, state],
        compiler_params=_params(2),
        name="mixers",
    )(*streams, pc, *consts)


def _mem_kv_kernel(m_ref, g_ref, w_f32_ref, o_ref, w_ref):
    _cast_weights_once(pl.program_id(0) == 0, [(w_f32_ref, w_ref)])
    h = _rms(m_ref[...], g_ref[...]).astype(BF16)
    o_ref[...] = _dot(h, w_ref[...]).astype(o_ref.dtype)


def _mem_kv(mem2d, g, w_kv, tm=1024):
    t = mem2d.shape[0]
    n = w_kv.shape[1]
    tm = min(tm, t)
    assert t % tm == 0
    return pl.pallas_call(
        _mem_kv_kernel,
        grid=(t // tm,),
        in_specs=[pl.BlockSpec((tm, D_MODEL), lambda i: (i, 0)), _full(g.shape), _full(w_kv.shape)],
        out_specs=pl.BlockSpec((tm, n), lambda i: (i, 0)),
        out_shape=jax.ShapeDtypeStruct((t, n), BF16),
        scratch_shapes=[pltpu.VMEM(w_kv.shape, BF16)],
        compiler_params=_params(1),
        name="mem_kv",
    )(mem2d, g, w_kv)


def _post_mix_kernel(x_ref, y_ref, wout_f32_ref, gx_ref, wq_f32_ref, kv_ref, wo_f32_ref, o_ref,
                     wout_ref, wq_ref, wo_ref):
    first = jnp.logical_and(pl.program_id(0) == 0, pl.program_id(1) == 0)
    _cast_weights_once(first, [(wout_f32_ref, wout_ref), (wq_f32_ref, wq_ref), (wo_f32_ref, wo_ref)])
    x1 = x_ref[0] + _dot(y_ref[0], wout_ref[...])
    q = _dot(_rms(x1, gx_ref[...]).astype(BF16), wq_ref[...]).astype(BF16)
    kv = kv_ref[0]
    heads = []
    for h in range(XATTN_HEADS):
        sl = slice(h * XATTN_HEAD_DIM, (h + 1) * XATTN_HEAD_DIM)
        s = _dot_nt(q[:, sl], kv[:, sl]) * (XATTN_HEAD_DIM ** -0.5)
        e = jnp.exp(s - jnp.max(s, axis=-1, keepdims=True))
        prob = e / jnp.sum(e, axis=-1, keepdims=True)
        heads.append(_dot(prob.astype(BF16), kv[:, D_MODEL + h * XATTN_HEAD_DIM:D_MODEL + (h + 1) * XATTN_HEAD_DIM]))
    o = jnp.concatenate(heads, axis=1).astype(BF16)
    o_ref[0] = x1 + _dot(o, wo_ref[...])


def _post_mix(x, y, w_out, g_x, w_q, kv, w_o, tm=1024):
    b, s, _ = x.shape
    m = kv.shape[1]
    assert s % tm == 0
    blk = lambda bi, ti: (bi, ti, 0)
    return pl.pallas_call(
        _post_mix_kernel,
        grid=(b, s // tm),
        in_specs=[pl.BlockSpec((1, tm, D_MODEL), blk), pl.BlockSpec((1, tm, y.shape[2]), blk),
                  _full(w_out.shape), _full(g_x.shape), _full(w_q.shape),
                  pl.BlockSpec((1, m, 2 * D_MODEL), lambda bi, ti: (bi, 0, 0)), _full(w_o.shape)],
        out_specs=pl.BlockSpec((1, tm, D_MODEL), blk),
        out_shape=jax.ShapeDtypeStruct((b, s, D_MODEL), F32),
        scratch_shapes=[pltpu.VMEM(a.shape, BF16) for a in (w_out, w_q, w_o)],
        compiler_params=_params(2),
        name="post_mix",
    )(x, y, w_out, g_x, w_q, kv, w_o)


def _stage_weights(w_hbm_ref, w_bf_ref, stage_ref, sem_ref):
    rows = stage_ref.shape[1]
    n_slabs = w_hbm_ref.shape[0] // rows

    def copy(k):
        return pltpu.make_async_copy(w_hbm_ref.at[pl.ds(k * rows, rows), :], stage_ref.at[k % 2],
                                     sem_ref.at[k % 2])

    copy(0).start()
    for k in range(n_slabs):
        if k + 1 < n_slabs:
            copy(k + 1).start()
        copy(k).wait()
        w_bf_ref[k * rows:(k + 1) * rows, :] = stage_ref[k % 2].astype(w_bf_ref.dtype)


def _mlp_kernel(x_ref, g_ref, wup_hbm_ref, wdown_hbm_ref, gf_ref, o_ref,
                wup_ref, wdown_ref, stage_up_ref, stage_down_ref, sem_up_ref, sem_down_ref, *, tf, final_norm):
    @pl.when(pl.program_id(0) == 0)
    def _():
        _stage_weights(wup_hbm_ref, wup_ref, stage_up_ref, sem_up_ref)
        _stage_weights(wdown_hbm_ref, wdown_ref, stage_down_ref, sem_down_ref)

    x = x_ref[...]
    h = _rms(x, g_ref[...]).astype(BF16)
    acc = x
    for j in range(D_FF // tf):
        u = jnp.maximum(_dot(h, wup_ref[:, j * tf:(j + 1) * tf]), 0.0)
        acc = acc + _dot((u * u).astype(BF16), wdown_ref[j * tf:(j + 1) * tf, :])
    o_ref[...] = _rms(acc, gf_ref[...]) if final_norm else acc


def _mlp(x2d, g, w_up, w_down, g_final, final_norm, tm=1024, tf=512):
    t = x2d.shape[0]
    assert t % tm == 0 and D_FF % tf == 0
    slab_elems = MLP_STAGE_BYTES // 4
    rows_up, rows_down = slab_elems // w_up.shape[1], slab_elems // w_down.shape[1]
    assert w_up.shape[0] % rows_up == 0 and w_down.shape[0] % rows_down == 0
    row = lambda i: (i, 0)
    hbm = pl.BlockSpec(memory_space=pl.ANY)
    return pl.pallas_call(
        functools.partial(_mlp_kernel, tf=tf, final_norm=final_norm),
        grid=(t // tm,),
        in_specs=[pl.BlockSpec((tm, D_MODEL), row), _full(g.shape), hbm, hbm, _full(g_final.shape)],
        out_specs=pl.BlockSpec((tm, D_MODEL), row),
        out_shape=jax.ShapeDtypeStruct((t, D_MODEL), F32),
        scratch_shapes=[pltpu.VMEM(w_up.shape, BF16), pltpu.VMEM(w_down.shape, BF16),
                        pltpu.VMEM((2, rows_up, w_up.shape[1]), F32),
                        pltpu.VMEM((2, rows_down, w_down.shape[1]), F32),
                        pltpu.SemaphoreType.DMA((2,)), pltpu.SemaphoreType.DMA((2,))],
        compiler_params=_params(1),
        name="mlp",
    )(x2d, g, w_up, w_down, g_final)


def kernel(x, mem, positions, norm_mix, w_in, ret_gn_w, ret_gn_b, rwkv_mu, rwkv_w0, rwkv_w_up, rwkv_a0,
           rwkv_a_up, rwkv_g_up, rwkv_k_k, rwkv_k_a, rwkv_r_k, rwkv_gn_w, rwkv_gn_b, w_out, norm_xattn,
           norm_mem, xattn_w_q, xattn_w_kv, xattn_w_o, norm_mlp, mlp_w_up, mlp_w_down, norm_final):
    b, s, dm = x.shape
    n_layers = w_in.shape[0]
    freq, xi_w, zeta_w, dmask_pairs, gc_w = _retention_consts()
    tri = (jnp.arange(WKV_CHUNK)[:, None] >= jnp.arange(WKV_CHUNK)[None, :]).astype(F32)
    row = lambda a: a.reshape(1, -1)
    for l in range(n_layers):
        mu = rwkv_mu[l]
        prep_consts = [row(norm_mix[l]), jnp.swapaxes(w_in[l], 0, 1), freq, xi_w, zeta_w, row(mu[:3 * RWKV_WIDTH]), row(mu[3 * RWKV_WIDTH:]),
                       row(rwkv_w0[l]), rwkv_w_up[l], row(rwkv_a0[l]), rwkv_a_up[l],
                       rwkv_g_up[l], row(rwkv_k_k[l]), row(rwkv_k_a[l]), row(rwkv_r_k[l]), tri]
        *streams, pc = _prep(x.reshape(b * s, dm), _packed_positions(positions, PREP_ROWS), s, prep_consts)
        y = _mixers([a.reshape(b, s, -1) for a in streams], pc.reshape(b, s // WKV_CHUNK, -1), dmask_pairs, gc_w,
                    row(ret_gn_w[l]), row(ret_gn_b[l]), row(rwkv_gn_w[l]), row(rwkv_gn_b[l]))
        kv = _mem_kv(mem.reshape(-1, dm), norm_mem[l][None, :], xattn_w_kv[l])
        x = _post_mix(x, y, w_out[l], norm_xattn[l][None, :], xattn_w_q[l], kv.reshape(b, -1, 2 * dm),
                      xattn_w_o[l])
        x = _mlp(x.reshape(b * s, dm), norm_mlp[l][None, :], mlp_w_up[l], mlp_w_down[l], norm_final[None, :],
                 l == n_layers - 1).reshape(b, s, dm)
    return x
```

```python
import functools
import math

import jax
import jax.numpy as jnp
from jax import lax
from jax.experimental import pallas as pl
from jax.experimental.pallas import tpu as pltpu

D_MODEL = 1024
HEAD_DIM = 64
RET_WIDTH = 512
RWKV_WIDTH = 512
N_HEADS = 8
RET_CHUNK = 128
ROPE_BASE = 10000.0
DECAY_LORA = 64
AAA_LORA = 64
GATE_LORA = 160
LORA_WIDTH = DECAY_LORA + AAA_LORA + GATE_LORA
RET_PROJ = 4 * RET_WIDTH
XATTN_HEADS = 4
XATTN_HEAD_DIM = D_MODEL // XATTN_HEADS
D_FF = 4 * D_MODEL
RMS_EPS = 1e-6
GN_EPS_RET = 1e-5
GN_EPS_RWKV = 64e-5

WKV_CHUNK = 64
MIX_SEQS = 4
MIX_BLOCK = 128
WKV_WAVE = 16
PREP_ROWS = 512
CAST_SLAB = 512
VMEM_LIMIT_BYTES = 56 * 1024 * 1024

BF16 = jnp.bfloat16
F32 = jnp.float32


def _dot(a, b):
    return jnp.dot(a, b, preferred_element_type=F32)


def _dot_nt(a, b):
    return lax.dot_general(a, b, (((1,), (1,)), ((), ())), preferred_element_type=F32)


def _dot_tn(a, b):
    return lax.dot_general(a, b, (((0,), (0,)), ((), ())), preferred_element_type=F32)


def _rms(x, g):
    return x * lax.rsqrt(jnp.mean(x * x, axis=-1, keepdims=True) + RMS_EPS) * g


def _params(n_axes):
    return pltpu.CompilerParams(dimension_semantics=("arbitrary",) * n_axes,
                                vmem_limit_bytes=VMEM_LIMIT_BYTES)


def _full(shape):
    zeros = (0,) * len(shape)
    return pl.BlockSpec(shape, lambda *_: zeros, pipeline_mode=pl.Buffered(1))


def _cast_weights_once(first_step, pairs):
    @pl.when(first_step)
    def _():
        for src_ref, dst_ref in pairs:
            n = src_ref.shape[1]
            for lo in range(0, n, CAST_SLAB):
                hi = min(lo + CAST_SLAB, n)
                dst_ref[:, lo:hi] = src_ref[:, lo:hi].astype(dst_ref.dtype)


def _head_sums(x):
    lanes = 2 * HEAD_DIM
    in_h0 = lax.broadcasted_iota(jnp.int32, (x.shape[0], lanes), 1) < HEAD_DIM
    tiles = []
    for lo in range(0, x.shape[1], lanes):
        t = x[:, lo:lo + lanes]
        s0 = jnp.sum(jnp.where(in_h0, t, 0.0), axis=-1, keepdims=True)
        s1 = jnp.sum(jnp.where(in_h0, 0.0, t), axis=-1, keepdims=True)
        tiles.append(jnp.where(in_h0, s0, s1))
    return jnp.concatenate(tiles, axis=1)


def _head_norm(y, eps):
    d = y - _head_sums(y) * (1.0 / HEAD_DIM)
    var = _head_sums(d * d) * (1.0 / HEAD_DIM)
    return d * lax.rsqrt(var + eps)


def _pair_diag(x):
    in_h0 = lax.broadcasted_iota(jnp.int32, x.shape, 1) < HEAD_DIM
    zero = jnp.zeros_like(x)
    return jnp.concatenate([jnp.where(in_h0, x, zero), jnp.where(in_h0, zero, x)], axis=0)


def _retention_body(q, k, q_in, k_out, vb, gsilu, dmask_ref, gc_ref, same_head_ref, gnw_ref, gnb_ref, state_ref):
    c = RET_CHUNK
    lanes = 2 * HEAD_DIM
    n_groups = RET_WIDTH // lanes
    n_chunks = q.shape[0] // c
    units = [(ci, g) for ci in range(n_chunks) for g in range(n_groups)]
    tile = lambda x, ci, g: x[ci * c:(ci + 1) * c, g * lanes:(g + 1) * lanes]
    s_u = {(ci, g): (_dot_nt(tile(q, ci, g), _pair_diag(tile(k, ci, g))) * dmask_ref[g]).astype(BF16)
           for ci, g in units}
    y_u = {(ci, g): _dot(s_u[ci, g], _pair_diag(tile(vb, ci, g))) for ci, g in units}
    kv_u = {(ci, g): _dot_tn(tile(k_out, ci, g), tile(vb, ci, g)) for ci, g in units}
    same_head = same_head_ref[...] > 0.5
    n_seq = state_ref.shape[0]
    chunks_per_seq = n_chunks // n_seq
    chains = [(sq, g) for sq in range(n_seq) for g in range(n_groups)]
    r_g = {(sq, g): state_ref[sq, g] for sq, g in chains}
    for step in range(chunks_per_seq):
        for sq, g in chains:
            ci = sq * chunks_per_seq + step
            y_u[ci, g] = y_u[ci, g] + _dot(tile(q_in, ci, g), r_g[sq, g].astype(BF16))
        r_g = {(sq, g): r_g[sq, g] * gc_ref[g] + jnp.where(same_head, kv_u[sq * chunks_per_seq + step, g], 0.0)
               for sq, g in chains}
    for sq, g in chains:
        state_ref[sq, g] = r_g[sq, g]
    y_all = jnp.concatenate([jnp.concatenate([y_u[ci, g] for g in range(n_groups)], axis=1)
                             for ci in range(n_chunks)], axis=0)
    return gsilu * (_head_norm(y_all, GN_EPS_RET) * gnw_ref[...] + gnb_ref[...])


def _retention_consts():
    c = RET_CHUNK
    half = HEAD_DIM // 2
    inv_freq = ROPE_BASE ** (-jnp.arange(half, dtype=F32) / half)
    freq = jnp.tile(inv_freq, 128 // half)[None, :]
    log_g = jnp.log(1.0 - 2.0 ** (-5.0 - jnp.arange(N_HEADS, dtype=F32)))
    idx = jnp.arange(c, dtype=F32)
    diff = idx[:, None] - idx[None, :]
    causal = diff >= 0
    dmask = jnp.where(causal[None], jnp.exp(log_g[:, None, None] * jnp.where(causal, diff, 0.0)[None]), 0.0)
    xi = jnp.exp(log_g[:, None] * (idx + 1.0)[None])
    zeta = jnp.exp(log_g[:, None] * (c - 1.0 - idx)[None])
    g_chunk = jnp.exp(log_g * c)
    xi_w = jnp.repeat(xi.T, HEAD_DIM, axis=1)
    zeta_w = jnp.repeat(zeta.T, HEAD_DIM, axis=1)
    dmask_pairs = jnp.concatenate([dmask[0::2], dmask[1::2]], axis=2)
    gc_rows = jnp.repeat(g_chunk, HEAD_DIM).reshape(N_HEADS // 2, 2 * HEAD_DIM, 1)
    gc_w = jnp.broadcast_to(gc_rows, (N_HEADS // 2, 2 * HEAD_DIM, 2 * HEAD_DIM))
    return freq, xi_w, zeta_w, dmask_pairs, gc_w


def _shift(p, carry_ref, mu_ref, lo):
    cols = slice(lo, lo + p.shape[1])
    rows = lax.broadcasted_iota(jnp.int32, p.shape, 0)
    prev = jnp.where(rows == 0, carry_ref[:, cols], pltpu.roll(p, 1, 0))
    carry_ref[:, cols] = p[p.shape[0] - 1:, :]
    return p + (prev - p) * mu_ref[:, cols]


def _wkv_body(at, rt, bt, kt, bc, kc, vb, bonus, gate, pc_ref, pc_row0, gnw_ref, gnb_ref, mask_ref, eye_ref,
              state_mask_ref, state_ref):
    c = WKV_CHUNK
    w = RWKV_WIDTH
    lanes = 2 * HEAD_DIM
    n_groups = w // lanes
    n_chunks = at.shape[0] // c
    eye = eye_ref[...]
    tril = mask_ref[...] > 0.5

    pair_diag = _pair_diag
    all_units = [(ci, g) for ci in range(n_chunks) for g in range(n_groups)]
    tile = lambda x, ci, g: x[ci * c:(ci + 1) * c, g * lanes:(g + 1) * lanes]
    g_u, wr_u, u0_u, y0_u = {}, {}, {}, {}
    for lo in range(0, len(all_units), WKV_WAVE):
        units = all_units[lo:lo + WKV_WAVE]
        vd_u = {(ci, g): pair_diag(tile(vb, ci, g)) for ci, g in units}
        for ci, g in units:
            lhs = jnp.concatenate([tile(at, ci, g), tile(rt, ci, g)], axis=0)
            rhs = jnp.concatenate([pair_diag(tile(bt, ci, g)), pair_diag(tile(kt, ci, g))], axis=0)
            g_u[ci, g] = jnp.where(tril, _dot_nt(lhs, rhs), 0.0)
        m_u = {u: g_u[u][:c, :lanes] for u in units}
        t_u = {u: eye + m_u[u] for u in units}
        mb_u = {u: m_u[u].astype(BF16) for u in units}
        mb_u = {u: _dot(mb_u[u], pair_diag(mb_u[u])).astype(BF16) for u in units}
        power = 2
        while 2 * power < c:
            tm_u = {u: _dot(jnp.concatenate([t_u[u].astype(BF16), mb_u[u]], axis=0), pair_diag(mb_u[u]))
                    for u in units}
            t_u = {u: t_u[u] + tm_u[u][:c] for u in units}
            mb_u = {u: tm_u[u][c:].astype(BF16) for u in units}
            power *= 2
        t_u = {u: (t_u[u] + _dot(t_u[u].astype(BF16), pair_diag(mb_u[u]))).astype(BF16) for u in units}
        kv_u = {u: _dot(g_u[u][:, lanes:].astype(BF16), vd_u[u]) for u in units}
        wu_u = {(ci, g): _dot(t_u[ci, g], jnp.concatenate(
            [pair_diag(tile(at, ci, g)), pair_diag(kv_u[ci, g][:c].astype(BF16))], axis=1)) for ci, g in units}
        wr_u.update({(ci, g): jnp.concatenate([wu_u[ci, g][:, :lanes].astype(BF16), tile(rt, ci, g)], axis=0)
                     for ci, g in units})
        u0_u.update({u: wu_u[u][:, lanes:] for u in units})
        y0_u.update({u: kv_u[u][c:] for u in units})

    same_head = state_mask_ref[...] > 0.5
    n_seq = state_ref.shape[0]
    chunks_per_seq = n_chunks // n_seq
    chains = [(sq, g) for sq in range(n_seq) for g in range(n_groups)]
    s_g = {(sq, g): state_ref[sq, g] for sq, g in chains}
    y_u = {}
    for step in range(chunks_per_seq):
        ci_of = lambda sq: sq * chunks_per_seq + step
        pc = [pc_ref[sq, pl.ds(pc_row0 + step, 1), :] for sq in range(n_seq)]
        ws = {(sq, g): _dot_nt(wr_u[ci_of(sq), g], s_g[sq, g].astype(BF16)) for sq, g in chains}
        u_b = {(sq, g): (ws[sq, g][:c] + u0_u[ci_of(sq), g]).astype(BF16) for sq, g in chains}
        for sq, g in chains:
            ci = ci_of(sq)
            y_u[ci, g] = (ws[sq, g][c:]
                          + _dot(g_u[ci, g][c:, :lanes].astype(BF16), pair_diag(u_b[sq, g])) + y0_u[ci, g])
        s_g = {(sq, g): s_g[sq, g] * pc[sq][:, g * lanes:(g + 1) * lanes]
               + jnp.where(same_head,
                           _dot_tn(jnp.concatenate([u_b[sq, g], tile(vb, ci_of(sq), g)], axis=0),
                                   jnp.concatenate([tile(bc, ci_of(sq), g), tile(kc, ci_of(sq), g)], axis=0)),
                           0.0)
               for sq, g in chains}
    for sq, g in chains:
        state_ref[sq, g] = s_g[sq, g]
    y_all = jnp.concatenate([jnp.concatenate([y_u[ci, g] for g in range(n_groups)], axis=1)
                             for ci in range(n_chunks)], axis=0)
    return (_head_norm(y_all, GN_EPS_RWKV) * gnw_ref[...] + gnb_ref[...] + bonus) * gate


N_RET_BF16 = 5
N_BF16_STREAMS = 12
N_F32_STREAMS = 3


def _prep_kernel(x_ref, pos_ref, g_ref, w_in_ref, freq_ref, xi_ref, zeta_ref,
                 mu_rkv_ref, mu_lora_ref, w0_ref, wup_ref, a0_ref, aup_ref, gup_ref, kk_ref, ka_ref, rk_ref,
                 tri_ref, *refs, tiles_per_seq):
    bf16_ref, f32_ref, pc_ref, w_ref, carry_rkv_ref, carry_lora_ref = refs
    stream = lambda ref, i: ref.at[:, pl.ds(i * RET_WIDTH, RET_WIDTH)]
    (q_ref, k_ref, qin_ref, kout_ref, vr_ref,
     at_ref, rt_ref, bt_ref, kt_ref, bc_ref, kc_ref, vw_ref) = [stream(bf16_ref, i) for i in range(N_BF16_STREAMS)]
    gsilu_ref, bonus_ref, gate_ref = [stream(f32_ref, i) for i in range(N_F32_STREAMS)]
    tm = x_ref.shape[0]
    w = RWKV_WIDTH
    rkv0 = RET_PROJ
    @pl.when(pl.program_id(0) == 0)
    def _():
        n_out = w_in_ref.shape[0]
        for lo in range(0, n_out, 128):
            hi = min(lo + 128, n_out)
            w_ref[:, lo:hi] = w_in_ref[lo:hi, :].T.astype(BF16)

    @pl.when(pl.program_id(0) % tiles_per_seq == 0)
    def _():
        carry_rkv_ref[...] = jnp.zeros_like(carry_rkv_ref)
        carry_lora_ref[...] = jnp.zeros_like(carry_lora_ref)

    h = _rms(x_ref[...], g_ref[...]).astype(BF16)

    wr = RET_WIDTH
    lanes = 2 * HEAD_DIM
    half = HEAD_DIM // 2
    p_q = _dot(h, w_ref[:, :wr])
    p_k = _dot(h, w_ref[:, wr:2 * wr])
    p_r = _dot(h, w_ref[:, rkv0:rkv0 + w])
    plora_raw = _dot(h, w_ref[:, rkv0 + 3 * w:])
    p_kw = _dot(h, w_ref[:, rkv0 + w:rkv0 + 2 * w])
    p_vw = _dot(h, w_ref[:, rkv0 + 2 * w:rkv0 + 3 * w])

    n_pack = lanes // half
    ang = pos_ref[...].astype(F32) * freq_ref[...]
    lane_p = lax.broadcasted_iota(jnp.int32, ang.shape, 1)

    def unpack(t):
        blocks = []
        for j in range(n_pack):
            g = pltpu.roll(t, (lanes - j * half) % lanes, 1) if j else t
            spread = g
            for m in range(1, n_pack):
                spread = jnp.where(lane_p < m * half, spread, pltpu.roll(g, m * half, 1))
            blocks.append(spread)
        return jnp.concatenate(blocks, axis=0)

    first_half = (lax.broadcasted_iota(jnp.int32, (tm, lanes), 1) % HEAD_DIM) < half
    cos = unpack(jnp.cos(ang))
    sin = unpack(jnp.sin(ang))
    sin = jnp.where(first_half, -sin, sin)
    k_scale = HEAD_DIM ** -0.5

    def rope(t, cos_t, sin_t):
        tiles = []
        for lo in range(0, wr, lanes):
            x = t[:, lo:lo + lanes]
            partner = jnp.where(first_half, pltpu.roll(x, lanes - half, 1), pltpu.roll(x, half, 1))
            tiles.append(x * cos_t + partner * sin_t)
        return jnp.concatenate(tiles, axis=1)

    def chunk_scaled(dst_ref, t, table_ref):
        for lo in range(0, tm, RET_CHUNK):
            dst_ref[lo:lo + RET_CHUNK, :] = (t[lo:lo + RET_CHUNK] * table_ref[...]).astype(BF16)

    q = rope(p_q, cos, sin)
    q_ref[...] = q.astype(BF16)
    chunk_scaled(qin_ref, q, xi_ref)

    kr = rope(p_k, cos * k_scale, sin * k_scale)
    k_ref[...] = kr.astype(BF16)
    chunk_scaled(kout_ref, kr, zeta_ref)

    r = _shift(p_r, carry_rkv_ref, mu_rkv_ref, 0)
    plora = _shift(plora_raw, carry_lora_ref, mu_lora_ref, 0)
    w_lr = plora[:, :DECAY_LORA]
    a_lr = plora[:, DECAY_LORA:DECAY_LORA + AAA_LORA]
    g_lr = plora[:, DECAY_LORA + AAA_LORA:]
    lw = -math.exp(-0.5) * jax.nn.sigmoid(
        w0_ref[...] + _dot(jnp.tanh(w_lr).astype(BF16), wup_ref[...].astype(BF16)))
    a_sig = jax.nn.sigmoid(a0_ref[...] + _dot(a_lr.astype(BF16), aup_ref[...].astype(BF16)))
    gate_ref[...] = _dot(jax.nn.sigmoid(g_lr).astype(BF16), gup_ref[...].astype(BF16))

    k = _shift(p_kw, carry_rkv_ref, mu_rkv_ref, w)
    kk = k * kk_ref[...]
    k2 = k * (1.0 + (a_sig - 1.0) * ka_ref[...])
    kk = kk * (1.0 / jnp.maximum(jnp.sqrt(_head_sums(kk * kk)), 1e-12))
    a_vec = -kk
    b_vec = kk * a_sig

    v = _shift(p_vw, carry_rkv_ref, mu_rkv_ref, 2 * w)
    bonus_ref[...] = _head_sums(r * k2 * rk_ref[...]) * v
    vw_ref[...] = v.astype(BF16)

    p_v = _dot(h, w_ref[:, 2 * wr:3 * wr])
    p_g = _dot(h, w_ref[:, 3 * wr:RET_PROJ])
    c = WKV_CHUNK
    for ci in range(tm // c):
        rows = slice(ci * c, (ci + 1) * c)
        lw_c = lw[rows]
        cum = jnp.dot(tri_ref[...], lw_c, precision=lax.Precision.HIGHEST, preferred_element_type=F32)
        cum_end = cum[c - 1:, :]
        p_inv = jnp.exp(-cum)
        p_end = jnp.exp(cum_end)
        p_out = p_inv * p_end
        at_ref[rows, :] = (a_vec[rows] * jnp.exp(cum - lw_c)).astype(BF16)
        rt_ref[rows, :] = (r[rows] * jnp.exp(cum)).astype(BF16)
        bt_ref[rows, :] = (b_vec[rows] * p_inv).astype(BF16)
        kt_ref[rows, :] = (k2[rows] * p_inv).astype(BF16)
        bc_ref[rows, :] = (b_vec[rows] * p_out).astype(BF16)
        kc_ref[rows, :] = (k2[rows] * p_out).astype(BF16)
        pc_ref[ci:ci + 1, :] = p_end

    vr_ref[...] = p_v.astype(BF16)
    gsilu_ref[...] = p_g * jax.nn.sigmoid(p_g)


ROPE_PACK = 2 * HEAD_DIM // (HEAD_DIM // 2)


def _packed_positions(positions, tm):
    lanes_per_group = 128 // ROPE_PACK
    tiles = positions.reshape(-1, ROPE_PACK, tm // ROPE_PACK)
    return jnp.repeat(jnp.swapaxes(tiles, 1, 2), lanes_per_group, axis=2).reshape(-1, 128)


def _prep(x2d, pos2d, seq_len, consts, tm=PREP_ROWS):
    t = x2d.shape[0]
    assert t % tm == 0 and seq_len % tm == 0 and tm % RET_CHUNK == 0 and tm % WKV_CHUNK == 0
    row = lambda i: (i, 0)
    width = RET_WIDTH
    n_pc = tm // WKV_CHUNK
    widths = [N_BF16_STREAMS * width, N_F32_STREAMS * width]
    return pl.pallas_call(
        functools.partial(_prep_kernel, tiles_per_seq=seq_len // tm),
        grid=(t // tm,),
        in_specs=[pl.BlockSpec((tm, D_MODEL), row), pl.BlockSpec((tm // ROPE_PACK, 128), row)]
                 + [_full(a.shape) for a in consts],
        out_specs=[pl.BlockSpec((tm, n), row) for n in widths] + [pl.BlockSpec((n_pc, width), row)],
        out_shape=[jax.ShapeDtypeStruct((t, widths[0]), BF16), jax.ShapeDtypeStruct((t, widths[1]), F32),
                   jax.ShapeDtypeStruct((t // WKV_CHUNK, width), F32)],
        scratch_shapes=[pltpu.VMEM(consts[1].shape[::-1], BF16), pltpu.VMEM((1, 3 * RWKV_WIDTH), F32),
                        pltpu.VMEM((1, LORA_WIDTH), F32)],
        compiler_params=_params(1),
        name="prep",
    )(x2d, pos2d, *consts)


def _mixers_kernel(*refs):
    bf16_ref, f32_ref, refs = refs[0], refs[1], refs[2:]
    (pc_ref, dmask_ref, gc_ref, same_head_ref, ret_gnw_ref, ret_gnb_ref, gnw_ref, gnb_ref, mask_ref, eye_ref,
     o_ref, ret_state_ref, wkv_state_ref) = refs
    ti = pl.program_id(1)

    @pl.when(ti == 0)
    def _():
        ret_state_ref[...] = jnp.zeros_like(ret_state_ref)
        wkv_state_ref[...] = jnp.zeros_like(wkv_state_ref)

    rows = bf16_ref.shape[0] * bf16_ref.shape[1]
    stream = lambda ref, i: ref[:, :, i * RET_WIDTH:(i + 1) * RET_WIDTH].reshape(rows, RET_WIDTH)
    ret_in = [stream(bf16_ref, i) for i in range(N_RET_BF16)] + [stream(f32_ref, 0)]
    wkv_in = ([stream(bf16_ref, i) for i in range(N_RET_BF16, N_BF16_STREAMS)]
              + [stream(f32_ref, i) for i in range(1, N_F32_STREAMS)])
    y_ret = _retention_body(*ret_in, dmask_ref, gc_ref, same_head_ref, ret_gnw_ref,
                            ret_gnb_ref, ret_state_ref)
    chunks_per_block = o_ref.shape[1] // WKV_CHUNK
    y_rwkv = _wkv_body(*wkv_in, pc_ref, ti * chunks_per_block, gnw_ref, gnb_ref, mask_ref,
                       eye_ref, same_head_ref, wkv_state_ref)
    o_ref[...] = jnp.concatenate([y_ret, y_rwkv], axis=1).astype(o_ref.dtype).reshape(o_ref.shape)


def _mixers(bf16_streams, f32_streams, pc, dmask_pairs, gc_w, ret_gn_w, ret_gn_b, gn_w, gn_b):
    b, s, _ = bf16_streams.shape
    width = RET_WIDTH
    n_seq, tb, c = MIX_SEQS, MIX_BLOCK, WKV_CHUNK
    assert b % n_seq == 0 and s % tb == 0 and tb % RET_CHUNK == 0 and tb % c == 0
    idx = jnp.arange(c)
    incl = (idx[:, None] >= idx[None, :]).astype(F32)
    strict = (idx[:, None] > idx[None, :]).astype(F32)
    mask = jnp.concatenate([jnp.tile(strict, (1, 4)), jnp.tile(incl, (1, 4))], axis=0)
    eye = jnp.tile(jnp.eye(c, dtype=F32), (1, 2))
    head_of_lane = jnp.arange(2 * HEAD_DIM) // HEAD_DIM
    state_mask = (head_of_lane[:, None] == head_of_lane[None, :]).astype(F32)
    consts = [dmask_pairs, gc_w, state_mask, ret_gn_w, ret_gn_b, gn_w, gn_b, mask, eye]
    blk = lambda bi, ti: (bi, ti, 0)
    state = pltpu.VMEM((n_seq, N_HEADS // 2, 2 * HEAD_DIM, 2 * HEAD_DIM), F32)
    return pl.pallas_call(
        _mixers_kernel,
        grid=(b // n_seq, s // tb),
        in_specs=[pl.BlockSpec((n_seq, tb, a.shape[2]), blk) for a in (bf16_streams, f32_streams)]
                 + [pl.BlockSpec((n_seq, s // c, width), lambda bi, ti: (bi, 0, 0))]
                 + [_full(a.shape) for a in consts],
        out_specs=pl.BlockSpec((n_seq, tb, RET_WIDTH + RWKV_WIDTH), blk),
        out_shape=jax.ShapeDtypeStruct((b, s, RET_WIDTH + RWKV_WIDTH), BF16),
        scratch_shapes=[state, state],
        compiler_params=_params(2),
        name="mixers",
    )(bf16_streams, f32_streams, pc, *consts)


def _mem_kv_kernel(m_ref, g_ref, w_f32_ref, o_ref, w_ref):
    _cast_weights_once(pl.program_id(0) == 0, [(w_f32_ref, w_ref)])
    h = _rms(m_ref[...], g_ref[...]).astype(BF16)
    o_ref[...] = _dot(h, w_ref[...]).astype(o_ref.dtype)


def _mem_kv(mem2d, g, w_kv, tm=1024):
    t = mem2d.shape[0]
    n = w_kv.shape[1]
    tm = min(tm, t)
    assert t % tm == 0
    return pl.pallas_call(
        _mem_kv_kernel,
        grid=(t // tm,),
        in_specs=[pl.BlockSpec((tm, D_MODEL), lambda i: (i, 0)), _full(g.shape), _full(w_kv.shape)],
        out_specs=pl.BlockSpec((tm, n), lambda i: (i, 0)),
        out_shape=jax.ShapeDtypeStruct((t, n), BF16),
        scratch_shapes=[pltpu.VMEM(w_kv.shape, BF16)],
        compiler_params=_params(1),
        name="mem_kv",
    )(mem2d, g, w_kv)


def _post_mix_kernel(x_ref, y_ref, wout_f32_ref, gx_ref, wq_f32_ref, kv_ref, wo_f32_ref, o_ref,
                     wout_ref, wq_ref, wo_ref):
    first = jnp.logical_and(pl.program_id(0) == 0, pl.program_id(1) == 0)
    _cast_weights_once(first, [(wout_f32_ref, wout_ref), (wq_f32_ref, wq_ref), (wo_f32_ref, wo_ref)])
    x1 = x_ref[0] + _dot(y_ref[0], wout_ref[...])
    q = _dot(_rms(x1, gx_ref[...]).astype(BF16), wq_ref[...]).astype(BF16)
    kv = kv_ref[0]
    heads = []
    for h in range(XATTN_HEADS):
        sl = slice(h * XATTN_HEAD_DIM, (h + 1) * XATTN_HEAD_DIM)
        s = _dot_nt(q[:, sl], kv[:, sl]) * (XATTN_HEAD_DIM ** -0.5)
        e = jnp.exp(s - jnp.max(s, axis=-1, keepdims=True))
        prob = e / jnp.sum(e, axis=-1, keepdims=True)
        heads.append(_dot(prob.astype(BF16), kv[:, D_MODEL + h * XATTN_HEAD_DIM:D_MODEL + (h + 1) * XATTN_HEAD_DIM]))
    o = jnp.concatenate(heads, axis=1).astype(BF16)
    o_ref[0] = x1 + _dot(o, wo_ref[...])


def _post_mix(x, y, w_out, g_x, w_q, kv, w_o, tm=1024):
    b, s, _ = x.shape
    m = kv.shape[1]
    assert s % tm == 0
    blk = lambda bi, ti: (bi, ti, 0)
    return pl.pallas_call(
        _post_mix_kernel,
        grid=(b, s // tm),
        in_specs=[pl.BlockSpec((1, tm, D_MODEL), blk), pl.BlockSpec((1, tm, y.shape[2]), blk),
                  _full(w_out.shape), _full(g_x.shape), _full(w_q.shape),
                  pl.BlockSpec((1, m, 2 * D_MODEL), lambda bi, ti: (bi, 0, 0)), _full(w_o.shape)],
        out_specs=pl.BlockSpec((1, tm, D_MODEL), blk),
        out_shape=jax.ShapeDtypeStruct((b, s, D_MODEL), F32),
        scratch_shapes=[pltpu.VMEM(a.shape, BF16) for a in (w_out, w_q, w_o)],
        compiler_params=_params(2),
        name="post_mix",
    )(x, y, w_out, g_x, w_q, kv, w_o)


def _mlp_kernel(x_ref, g_ref, wup_ref, wdown_ref, gf_ref, o_ref, *, tf, final_norm):
    x = x_ref[...]
    h = _rms(x, g_ref[...]).astype(BF16)
    acc = x
    for j in range(D_FF // tf):
        u = jnp.maximum(_dot(h, wup_ref[:, j * tf:(j + 1) * tf]), 0.0)
        acc = acc + _dot((u * u).astype(BF16), wdown_ref[j * tf:(j + 1) * tf, :])
    o_ref[...] = _rms(acc, gf_ref[...]) if final_norm else acc


def _mlp(x2d, g, w_up, w_down, g_final, final_norm, tm=1024, tf=512):
    t = x2d.shape[0]
    assert t % tm == 0 and D_FF % tf == 0
    row = lambda i: (i, 0)
    return pl.pallas_call(
        functools.partial(_mlp_kernel, tf=tf, final_norm=final_norm),
        grid=(t // tm,),
        in_specs=[pl.BlockSpec((tm, D_MODEL), row), _full(g.shape), _full(w_up.shape), _full(w_down.shape),
                  _full(g_final.shape)],
        out_specs=pl.BlockSpec((tm, D_MODEL), row),
        out_shape=jax.ShapeDtypeStruct((t, D_MODEL), F32),
        compiler_params=_params(1),
        name="mlp",
    )(x2d, g, w_up, w_down, g_final)


def kernel(x, mem, positions, norm_mix, w_in, ret_gn_w, ret_gn_b, rwkv_mu, rwkv_w0, rwkv_w_up, rwkv_a0,
           rwkv_a_up, rwkv_g_up, rwkv_k_k, rwkv_k_a, rwkv_r_k, rwkv_gn_w, rwkv_gn_b, w_out, norm_xattn,
           norm_mem, xattn_w_q, xattn_w_kv, xattn_w_o, norm_mlp, mlp_w_up, mlp_w_down, norm_final):
    b, s, dm = x.shape
    n_layers = w_in.shape[0]
    freq, xi_w, zeta_w, dmask_pairs, gc_w = _retention_consts()
    tri = (jnp.arange(WKV_CHUNK)[:, None] >= jnp.arange(WKV_CHUNK)[None, :]).astype(F32)
    row = lambda a: a.reshape(1, -1)
    for l in range(n_layers):
        mu = rwkv_mu[l]
        prep_consts = [row(norm_mix[l]), jnp.swapaxes(w_in[l], 0, 1), freq, xi_w, zeta_w, row(mu[:3 * RWKV_WIDTH]), row(mu[3 * RWKV_WIDTH:]),
                       row(rwkv_w0[l]), rwkv_w_up[l], row(rwkv_a0[l]), rwkv_a_up[l],
                       rwkv_g_up[l], row(rwkv_k_k[l]), row(rwkv_k_a[l]), row(rwkv_r_k[l]), tri]
        sb, sf, pc = _prep(x.reshape(b * s, dm), _packed_positions(positions, PREP_ROWS), s, prep_consts)
        y = _mixers(sb.reshape(b, s, -1), sf.reshape(b, s, -1), pc.reshape(b, s // WKV_CHUNK, -1), dmask_pairs, gc_w,
                    row(ret_gn_w[l]), row(ret_gn_b[l]), row(rwkv_gn_w[l]), row(rwkv_gn_b[l]))
        kv = _mem_kv(mem.reshape(-1, dm), norm_mem[l][None, :], xattn_w_kv[l])
        x = _post_mix(x, y, w_out[l], norm_xattn[l][None, :], xattn_w_q[l], kv.reshape(b, -1, 2 * dm),
                      xattn_w_o[l])
        x = _mlp(x.reshape(b * s, dm), norm_mlp[l][None, :], mlp_w_up[l].astype(BF16),
                 mlp_w_down[l].astype(BF16), norm_final[None, :], l == n_layers - 1).reshape(b, s, dm)
    return x
```

```python
import functools
import math

import jax
import jax.numpy as jnp
from jax import lax
from jax.experimental import pallas as pl
from jax.experimental.pallas import tpu as pltpu

D_MODEL = 1024
HEAD_DIM = 64
RET_WIDTH = 512
RWKV_WIDTH = 512
N_HEADS = 8
RET_CHUNK = 128
ROPE_BASE = 10000.0
DECAY_LORA = 64
AAA_LORA = 64
GATE_LORA = 160
LORA_WIDTH = DECAY_LORA + AAA_LORA + GATE_LORA
RET_PROJ = 4 * RET_WIDTH
XATTN_HEADS = 4
XATTN_HEAD_DIM = D_MODEL // XATTN_HEADS
D_FF = 4 * D_MODEL
RMS_EPS = 1e-6
GN_EPS_RET = 1e-5
GN_EPS_RWKV = 64e-5

WKV_CHUNK = 64
MIX_SEQS = 4
MIX_BLOCK = 128
WKV_WAVE = 16
PREP_ROWS = 512
CAST_SLAB = 512
VMEM_LIMIT_BYTES = 56 * 1024 * 1024

BF16 = jnp.bfloat16
F32 = jnp.float32


def _dot(a, b):
    return jnp.dot(a, b, preferred_element_type=F32)


def _dot_nt(a, b):
    return lax.dot_general(a, b, (((1,), (1,)), ((), ())), preferred_element_type=F32)


def _dot_tn(a, b):
    return lax.dot_general(a, b, (((0,), (0,)), ((), ())), preferred_element_type=F32)


def _rms(x, g):
    return x * lax.rsqrt(jnp.mean(x * x, axis=-1, keepdims=True) + RMS_EPS) * g


def _params(n_axes):
    return pltpu.CompilerParams(dimension_semantics=("arbitrary",) * n_axes,
                                vmem_limit_bytes=VMEM_LIMIT_BYTES)


def _full(shape):
    zeros = (0,) * len(shape)
    return pl.BlockSpec(shape, lambda *_: zeros, pipeline_mode=pl.Buffered(1))


def _cast_weights_once(first_step, pairs):
    @pl.when(first_step)
    def _():
        for src_ref, dst_ref in pairs:
            n = src_ref.shape[1]
            for lo in range(0, n, CAST_SLAB):
                hi = min(lo + CAST_SLAB, n)
                dst_ref[:, lo:hi] = src_ref[:, lo:hi].astype(dst_ref.dtype)


def _head_sums(x):
    lanes = 2 * HEAD_DIM
    in_h0 = lax.broadcasted_iota(jnp.int32, (x.shape[0], lanes), 1) < HEAD_DIM
    tiles = []
    for lo in range(0, x.shape[1], lanes):
        t = x[:, lo:lo + lanes]
        s0 = jnp.sum(jnp.where(in_h0, t, 0.0), axis=-1, keepdims=True)
        s1 = jnp.sum(jnp.where(in_h0, 0.0, t), axis=-1, keepdims=True)
        tiles.append(jnp.where(in_h0, s0, s1))
    return jnp.concatenate(tiles, axis=1)


def _head_norm(y, eps):
    d = y - _head_sums(y) * (1.0 / HEAD_DIM)
    var = _head_sums(d * d) * (1.0 / HEAD_DIM)
    return d * lax.rsqrt(var + eps)


def _pair_diag(x):
    in_h0 = lax.broadcasted_iota(jnp.int32, x.shape, 1) < HEAD_DIM
    zero = jnp.zeros_like(x)
    return jnp.concatenate([jnp.where(in_h0, x, zero), jnp.where(in_h0, zero, x)], axis=0)


def _retention_body(q, k, q_in, k_out, vb, gsilu, dmask_ref, gc_ref, same_head_ref, gnw_ref, gnb_ref, state_ref):
    c = RET_CHUNK
    lanes = 2 * HEAD_DIM
    n_groups = RET_WIDTH // lanes
    n_chunks = q.shape[0] // c
    units = [(ci, g) for ci in range(n_chunks) for g in range(n_groups)]
    tile = lambda x, ci, g: x[ci * c:(ci + 1) * c, g * lanes:(g + 1) * lanes]
    s_u = {(ci, g): (_dot_nt(tile(q, ci, g), _pair_diag(tile(k, ci, g))) * dmask_ref[g]).astype(BF16)
           for ci, g in units}
    y_u = {(ci, g): _dot(s_u[ci, g], _pair_diag(tile(vb, ci, g))) for ci, g in units}
    kv_u = {(ci, g): _dot_tn(tile(k_out, ci, g), tile(vb, ci, g)) for ci, g in units}
    same_head = same_head_ref[...] > 0.5
    n_seq = state_ref.shape[0]
    chunks_per_seq = n_chunks // n_seq
    chains = [(sq, g) for sq in range(n_seq) for g in range(n_groups)]
    r_g = {(sq, g): state_ref[sq, g] for sq, g in chains}
    for step in range(chunks_per_seq):
        for sq, g in chains:
            ci = sq * chunks_per_seq + step
            y_u[ci, g] = y_u[ci, g] + _dot(tile(q_in, ci, g), r_g[sq, g].astype(BF16))
        r_g = {(sq, g): r_g[sq, g] * gc_ref[g] + jnp.where(same_head, kv_u[sq * chunks_per_seq + step, g], 0.0)
               for sq, g in chains}
    for sq, g in chains:
        state_ref[sq, g] = r_g[sq, g]
    y_all = jnp.concatenate([jnp.concatenate([y_u[ci, g] for g in range(n_groups)], axis=1)
                             for ci in range(n_chunks)], axis=0)
    return gsilu * (_head_norm(y_all, GN_EPS_RET) * gnw_ref[...] + gnb_ref[...])


def _retention_consts():
    c = RET_CHUNK
    half = HEAD_DIM // 2
    inv_freq = ROPE_BASE ** (-jnp.arange(half, dtype=F32) / half)
    freq = jnp.tile(inv_freq, 128 // half)[None, :]
    log_g = jnp.log(1.0 - 2.0 ** (-5.0 - jnp.arange(N_HEADS, dtype=F32)))
    idx = jnp.arange(c, dtype=F32)
    diff = idx[:, None] - idx[None, :]
    causal = diff >= 0
    dmask = jnp.where(causal[None], jnp.exp(log_g[:, None, None] * jnp.where(causal, diff, 0.0)[None]), 0.0)
    xi = jnp.exp(log_g[:, None] * (idx + 1.0)[None])
    zeta = jnp.exp(log_g[:, None] * (c - 1.0 - idx)[None])
    g_chunk = jnp.exp(log_g * c)
    xi_w = jnp.repeat(xi.T, HEAD_DIM, axis=1)
    zeta_w = jnp.repeat(zeta.T, HEAD_DIM, axis=1)
    dmask_pairs = jnp.concatenate([dmask[0::2], dmask[1::2]], axis=2)
    gc_rows = jnp.repeat(g_chunk, HEAD_DIM).reshape(N_HEADS // 2, 2 * HEAD_DIM, 1)
    gc_w = jnp.broadcast_to(gc_rows, (N_HEADS // 2, 2 * HEAD_DIM, 2 * HEAD_DIM))
    return freq, xi_w, zeta_w, dmask_pairs, gc_w


def _shift(p, carry_ref, mu_ref, lo):
    cols = slice(lo, lo + p.shape[1])
    rows = lax.broadcasted_iota(jnp.int32, p.shape, 0)
    prev = jnp.where(rows == 0, carry_ref[:, cols], pltpu.roll(p, 1, 0))
    carry_ref[:, cols] = p[p.shape[0] - 1:, :]
    return p + (prev - p) * mu_ref[:, cols]


def _wkv_body(at, rt, bt, kt, bc, kc, vb, bonus, gate, pc_ref, pc_row0, gnw_ref, gnb_ref, mask_ref, eye_ref,
              state_mask_ref, state_ref):
    c = WKV_CHUNK
    w = RWKV_WIDTH
    lanes = 2 * HEAD_DIM
    n_groups = w // lanes
    n_chunks = at.shape[0] // c
    eye = eye_ref[...]
    tril = mask_ref[...] > 0.5

    pair_diag = _pair_diag
    all_units = [(ci, g) for ci in range(n_chunks) for g in range(n_groups)]
    tile = lambda x, ci, g: x[ci * c:(ci + 1) * c, g * lanes:(g + 1) * lanes]
    g_u, wr_u, u0_u, y0_u = {}, {}, {}, {}
    for lo in range(0, len(all_units), WKV_WAVE):
        units = all_units[lo:lo + WKV_WAVE]
        vd_u = {(ci, g): pair_diag(tile(vb, ci, g)) for ci, g in units}
        for ci, g in units:
            lhs = jnp.concatenate([tile(at, ci, g), tile(rt, ci, g)], axis=0)
            rhs = jnp.concatenate([pair_diag(tile(bt, ci, g)), pair_diag(tile(kt, ci, g))], axis=0)
            g_u[ci, g] = jnp.where(tril, _dot_nt(lhs, rhs), 0.0)
        m_u = {u: g_u[u][:c, :lanes] for u in units}
        t_u = {u: eye + m_u[u] for u in units}
        mb_u = {u: m_u[u].astype(BF16) for u in units}
        mb_u = {u: _dot(mb_u[u], pair_diag(mb_u[u])).astype(BF16) for u in units}
        power = 2
        while 2 * power < c:
            tm_u = {u: _dot(jnp.concatenate([t_u[u].astype(BF16), mb_u[u]], axis=0), pair_diag(mb_u[u]))
                    for u in units}
            t_u = {u: t_u[u] + tm_u[u][:c] for u in units}
            mb_u = {u: tm_u[u][c:].astype(BF16) for u in units}
            power *= 2
        t_u = {u: (t_u[u] + _dot(t_u[u].astype(BF16), pair_diag(mb_u[u]))).astype(BF16) for u in units}
        kv_u = {u: _dot(g_u[u][:, lanes:].astype(BF16), vd_u[u]) for u in units}
        wu_u = {(ci, g): _dot(t_u[ci, g], jnp.concatenate(
            [pair_diag(tile(at, ci, g)), pair_diag(kv_u[ci, g][:c].astype(BF16))], axis=1)) for ci, g in units}
        wr_u.update({(ci, g): jnp.concatenate([wu_u[ci, g][:, :lanes].astype(BF16), tile(rt, ci, g)], axis=0)
                     for ci, g in units})
        u0_u.update({u: wu_u[u][:, lanes:] for u in units})
        y0_u.update({u: kv_u[u][c:] for u in units})

    same_head = state_mask_ref[...] > 0.5
    n_seq = state_ref.shape[0]
    chunks_per_seq = n_chunks // n_seq
    chains = [(sq, g) for sq in range(n_seq) for g in range(n_groups)]
    s_g = {(sq, g): state_ref[sq, g] for sq, g in chains}
    y_u = {}
    for step in range(chunks_per_seq):
        ci_of = lambda sq: sq * chunks_per_seq + step
        pc = [pc_ref[sq, pl.ds(pc_row0 + step, 1), :] for sq in range(n_seq)]
        ws = {(sq, g): _dot_nt(wr_u[ci_of(sq), g], s_g[sq, g].astype(BF16)) for sq, g in chains}
        u_b = {(sq, g): (ws[sq, g][:c] + u0_u[ci_of(sq), g]).astype(BF16) for sq, g in chains}
        for sq, g in chains:
            ci = ci_of(sq)
            y_u[ci, g] = (ws[sq, g][c:]
                          + _dot(g_u[ci, g][c:, :lanes].astype(BF16), pair_diag(u_b[sq, g])) + y0_u[ci, g])
        s_g = {(sq, g): s_g[sq, g] * pc[sq][:, g * lanes:(g + 1) * lanes]
               + jnp.where(same_head,
                           _dot_tn(jnp.concatenate([u_b[sq, g], tile(vb, ci_of(sq), g)], axis=0),
                                   jnp.concatenate([tile(bc, ci_of(sq), g), tile(kc, ci_of(sq), g)], axis=0)),
                           0.0)
               for sq, g in chains}
    for sq, g in chains:
        state_ref[sq, g] = s_g[sq, g]
    y_all = jnp.concatenate([jnp.concatenate([y_u[ci, g] for g in range(n_groups)], axis=1)
                             for ci in range(n_chunks)], axis=0)
    return (_head_norm(y_all, GN_EPS_RWKV) * gnw_ref[...] + gnb_ref[...] + bonus) * gate


N_RET_BF16 = 5
N_BF16_STREAMS = 12
N_F32_STREAMS = 3


def _prep_kernel(x_ref, pos_ref, g_ref, w_in_ref, freq_ref, xi_ref, zeta_ref,
                 mu_rkv_ref, mu_lora_ref, w0_ref, wup_ref, a0_ref, aup_ref, gup_ref, kk_ref, ka_ref, rk_ref,
                 tri_ref, *refs, tiles_per_seq):
    bf16_ref, f32_ref, pc_ref, w_ref, carry_rkv_ref, carry_lora_ref = refs
    stream = lambda ref, i: ref.at[:, pl.ds(i * RET_WIDTH, RET_WIDTH)]
    (q_ref, k_ref, qin_ref, kout_ref, vr_ref,
     at_ref, rt_ref, bt_ref, kt_ref, bc_ref, kc_ref, vw_ref) = [stream(bf16_ref, i) for i in range(N_BF16_STREAMS)]
    gsilu_ref, bonus_ref, gate_ref = [stream(f32_ref, i) for i in range(N_F32_STREAMS)]
    tm = x_ref.shape[0]
    w = RWKV_WIDTH
    rkv0 = RET_PROJ
    @pl.when(pl.program_id(0) == 0)
    def _():
        n_out = w_in_ref.shape[0]
        for lo in range(0, n_out, 128):
            hi = min(lo + 128, n_out)
            w_ref[:, lo:hi] = w_in_ref[lo:hi, :].T.astype(BF16)

    @pl.when(pl.program_id(0) % tiles_per_seq == 0)
    def _():
        carry_rkv_ref[...] = jnp.zeros_like(carry_rkv_ref)
        carry_lora_ref[...] = jnp.zeros_like(carry_lora_ref)

    h = _rms(x_ref[...], g_ref[...]).astype(BF16)

    wr = RET_WIDTH
    lanes = 2 * HEAD_DIM
    half = HEAD_DIM // 2
    p_q = _dot(h, w_ref[:, :wr])
    p_k = _dot(h, w_ref[:, wr:2 * wr])
    p_r = _dot(h, w_ref[:, rkv0:rkv0 + w])
    plora_raw = _dot(h, w_ref[:, rkv0 + 3 * w:])
    p_kw = _dot(h, w_ref[:, rkv0 + w:rkv0 + 2 * w])
    p_vw = _dot(h, w_ref[:, rkv0 + 2 * w:rkv0 + 3 * w])

    n_pack = lanes // half
    ang = pos_ref[...].astype(F32) * freq_ref[...]
    lane_p = lax.broadcasted_iota(jnp.int32, ang.shape, 1)

    def unpack(t):
        blocks = []
        for j in range(n_pack):
            g = pltpu.roll(t, (lanes - j * half) % lanes, 1) if j else t
            spread = g
            for m in range(1, n_pack):
                spread = jnp.where(lane_p < m * half, spread, pltpu.roll(g, m * half, 1))
            blocks.append(spread)
        return jnp.concatenate(blocks, axis=0)

    first_half = (lax.broadcasted_iota(jnp.int32, (tm, lanes), 1) % HEAD_DIM) < half
    cos = unpack(jnp.cos(ang))
    sin = unpack(jnp.sin(ang))
    sin = jnp.where(first_half, -sin, sin)
    k_scale = HEAD_DIM ** -0.5

    def rope(t, cos_t, sin_t):
        tiles = []
        for lo in range(0, wr, lanes):
            x = t[:, lo:lo + lanes]
            partner = jnp.where(first_half, pltpu.roll(x, lanes - half, 1), pltpu.roll(x, half, 1))
            tiles.append(x * cos_t + partner * sin_t)
        return jnp.concatenate(tiles, axis=1)

    def chunk_scaled(dst_ref, t, table_ref):
        for lo in range(0, tm, RET_CHUNK):
            dst_ref[lo:lo + RET_CHUNK, :] = (t[lo:lo + RET_CHUNK] * table_ref[...]).astype(BF16)

    q = rope(p_q, cos, sin)
    q_ref[...] = q.astype(BF16)
    chunk_scaled(qin_ref, q, xi_ref)

    kr = rope(p_k, cos * k_scale, sin * k_scale)
    k_ref[...] = kr.astype(BF16)
    chunk_scaled(kout_ref, kr, zeta_ref)

    r = _shift(p_r, carry_rkv_ref, mu_rkv_ref, 0)
    plora = _shift(plora_raw, carry_lora_ref, mu_lora_ref, 0)
    w_lr = plora[:, :DECAY_LORA]
    a_lr = plora[:, DECAY_LORA:DECAY_LORA + AAA_LORA]
    g_lr = plora[:, DECAY_LORA + AAA_LORA:]
    lw = -math.exp(-0.5) * jax.nn.sigmoid(
        w0_ref[...] + _dot(jnp.tanh(w_lr).astype(BF16), wup_ref[...].astype(BF16)))
    a_sig = jax.nn.sigmoid(a0_ref[...] + _dot(a_lr.astype(BF16), aup_ref[...].astype(BF16)))
    gate_ref[...] = _dot(jax.nn.sigmoid(g_lr).astype(BF16), gup_ref[...].astype(BF16))

    k = _shift(p_kw, carry_rkv_ref, mu_rkv_ref, w)
    kk = k * kk_ref[...]
    k2 = k * (1.0 + (a_sig - 1.0) * ka_ref[...])
    kk = kk * (1.0 / jnp.maximum(jnp.sqrt(_head_sums(kk * kk)), 1e-12))
    a_vec = -kk
    b_vec = kk * a_sig

    v = _shift(p_vw, carry_rkv_ref, mu_rkv_ref, 2 * w)
    bonus_ref[...] = _head_sums(r * k2 * rk_ref[...]) * v
    vw_ref[...] = v.astype(BF16)

    p_v = _dot(h, w_ref[:, 2 * wr:3 * wr])
    p_g = _dot(h, w_ref[:, 3 * wr:RET_PROJ])
    c = WKV_CHUNK
    for ci in range(tm // c):
        rows = slice(ci * c, (ci + 1) * c)
        lw_c = lw[rows]
        cum = jnp.dot(tri_ref[...], lw_c, precision=lax.Precision.HIGHEST, preferred_element_type=F32)
        cum_end = cum[c - 1:, :]
        p_inv = jnp.exp(-cum)
        p_end = jnp.exp(cum_end)
        p_out = p_inv * p_end
        at_ref[rows, :] = (a_vec[rows] * jnp.exp(cum - lw_c)).astype(BF16)
        rt_ref[rows, :] = (r[rows] * jnp.exp(cum)).astype(BF16)
        bt_ref[rows, :] = (b_vec[rows] * p_inv).astype(BF16)
        kt_ref[rows, :] = (k2[rows] * p_inv).astype(BF16)
        bc_ref[rows, :] = (b_vec[rows] * p_out).astype(BF16)
        kc_ref[rows, :] = (k2[rows] * p_out).astype(BF16)
        pc_ref[ci:ci + 1, :] = p_end

    vr_ref[...] = p_v.astype(BF16)
    gsilu_ref[...] = p_g * jax.nn.sigmoid(p_g)


ROPE_PACK = 2 * HEAD_DIM // (HEAD_DIM // 2)


def _packed_positions(positions, tm):
    lanes_per_group = 128 // ROPE_PACK
    tiles = positions.reshape(-1, ROPE_PACK, tm // ROPE_PACK)
    return jnp.repeat(jnp.swapaxes(tiles, 1, 2), lanes_per_group, axis=2).reshape(-1, 128)


def _prep(x2d, pos2d, seq_len, consts, tm=PREP_ROWS):
    t = x2d.shape[0]
    assert t % tm == 0 and seq_len % tm == 0 and tm % RET_CHUNK == 0 and tm % WKV_CHUNK == 0
    row = lambda i: (i, 0)
    width = RET_WIDTH
    n_pc = tm // WKV_CHUNK
    widths = [N_BF16_STREAMS * width, N_F32_STREAMS * width]
    return pl.pallas_call(
        functools.partial(_prep_kernel, tiles_per_seq=seq_len // tm),
        grid=(t // tm,),
        in_specs=[pl.BlockSpec((tm, D_MODEL), row), pl.BlockSpec((tm // ROPE_PACK, 128), row)]
                 + [_full(a.shape) for a in consts],
        out_specs=[pl.BlockSpec((tm, n), row) for n in widths] + [pl.BlockSpec((n_pc, width), row)],
        out_shape=[jax.ShapeDtypeStruct((t, widths[0]), BF16), jax.ShapeDtypeStruct((t, widths[1]), F32),
                   jax.ShapeDtypeStruct((t // WKV_CHUNK, width), F32)],
        scratch_shapes=[pltpu.VMEM(consts[1].shape[::-1], BF16), pltpu.VMEM((1, 3 * RWKV_WIDTH), F32),
                        pltpu.VMEM((1, LORA_WIDTH), F32)],
        compiler_params=_params(1),
        name="prep",
    )(x2d, pos2d, *consts)


def _mixers_kernel(*refs):
    bf16_ref, f32_ref, refs = refs[0], refs[1], refs[2:]
    (pc_ref, dmask_ref, gc_ref, same_head_ref, ret_gnw_ref, ret_gnb_ref, gnw_ref, gnb_ref, mask_ref, eye_ref,
     o_ref, ret_state_ref, wkv_state_ref) = refs
    ti = pl.program_id(1)

    @pl.when(ti == 0)
    def _():
        ret_state_ref[...] = jnp.zeros_like(ret_state_ref)
        wkv_state_ref[...] = jnp.zeros_like(wkv_state_ref)

    rows = bf16_ref.shape[0] * bf16_ref.shape[1]
    stream = lambda ref, i: ref[:, :, i * RET_WIDTH:(i + 1) * RET_WIDTH].reshape(rows, RET_WIDTH)
    ret_in = [stream(bf16_ref, i) for i in range(N_RET_BF16)] + [stream(f32_ref, 0)]
    wkv_in = ([stream(bf16_ref, i) for i in range(N_RET_BF16, N_BF16_STREAMS)]
              + [stream(f32_ref, i) for i in range(1, N_F32_STREAMS)])
    y_ret = _retention_body(*ret_in, dmask_ref, gc_ref, same_head_ref, ret_gnw_ref,
                            ret_gnb_ref, ret_state_ref)
    chunks_per_block = o_ref.shape[1] // WKV_CHUNK
    y_rwkv = _wkv_body(*wkv_in, pc_ref, ti * chunks_per_block, gnw_ref, gnb_ref, mask_ref,
                       eye_ref, same_head_ref, wkv_state_ref)
    o_ref[...] = jnp.concatenate([y_ret, y_rwkv], axis=1).astype(o_ref.dtype).reshape(o_ref.shape)


def _mixers(bf16_streams, f32_streams, pc, dmask_pairs, gc_w, ret_gn_w, ret_gn_b, gn_w, gn_b):
    b, s, _ = bf16_streams.shape
    width = RET_WIDTH
    n_seq, tb, c = MIX_SEQS, MIX_BLOCK, WKV_CHUNK
    assert b % n_seq == 0 and s % tb == 0 and tb % RET_CHUNK == 0 and tb % c == 0
    idx = jnp.arange(c)
    incl = (idx[:, None] >= idx[None, :]).astype(F32)
    strict = (idx[:, None] > idx[None, :]).astype(F32)
    mask = jnp.concatenate([jnp.tile(strict, (1, 4)), jnp.tile(incl, (1, 4))], axis=0)
    eye = jnp.tile(jnp.eye(c, dtype=F32), (1, 2))
    head_of_lane = jnp.arange(2 * HEAD_DIM) // HEAD_DIM
    state_mask = (head_of_lane[:, None] == head_of_lane[None, :]).astype(F32)
    consts = [dmask_pairs, gc_w, state_mask, ret_gn_w, ret_gn_b, gn_w, gn_b, mask, eye]
    blk = lambda bi, ti: (bi, ti, 0)
    state = pltpu.VMEM((n_seq, N_HEADS // 2, 2 * HEAD_DIM, 2 * HEAD_DIM), F32)
    return pl.pallas_call(
        _mixers_kernel,
        grid=(b // n_seq, s // tb),
        in_specs=[pl.BlockSpec((n_seq, tb, a.shape[2]), blk) for a in (bf16_streams, f32_streams)]
                 + [pl.BlockSpec((n_seq, s // c, width), lambda bi, ti: (bi, 0, 0))]
                 + [_full(a.shape) for a in consts],
        out_specs=pl.BlockSpec((n_seq, tb, RET_WIDTH + RWKV_WIDTH), blk),
        out_shape=jax.ShapeDtypeStruct((b, s, RET_WIDTH + RWKV_WIDTH), BF16),
        scratch_shapes=[state, state],
        compiler_params=_params(2),
        name="mixers",
    )(bf16_streams, f32_streams, pc, *consts)


def _mem_kv_kernel(m_ref, g_ref, w_f32_ref, o_ref, w_ref):
    _cast_weights_once(pl.program_id(0) == 0, [(w_f32_ref, w_ref)])
    h = _rms(m_ref[...], g_ref[...]).astype(BF16)
    o_ref[...] = _dot(h, w_ref[...]).astype(o_ref.dtype)


def _mem_kv(mem2d, g, w_kv, tm=1024):
    t = mem2d.shape[0]
    n = w_kv.shape[1]
    tm = min(tm, t)
    assert t % tm == 0
    return pl.pallas_call(
        _mem_kv_kernel,
        grid=(t // tm,),
        in_specs=[pl.BlockSpec((tm, D_MODEL), lambda i: (i, 0)), _full(g.shape), _full(w_kv.shape)],
        out_specs=pl.BlockSpec((tm, n), lambda i: (i, 0)),
        out_shape=jax.ShapeDtypeStruct((t, n), BF16),
        scratch_shapes=[pltpu.VMEM(w_kv.shape, BF16)],
        compiler_params=_params(1),
        name="mem_kv",
    )(mem2d, g, w_kv)


def _post_mix_kernel(x_ref, y_ref, wout_f32_ref, gx_ref, wq_f32_ref, kv_ref, wo_f32_ref, o_ref,
                     wout_ref, wq_ref, wo_ref):
    first = jnp.logical_and(pl.program_id(0) == 0, pl.program_id(1) == 0)
    _cast_weights_once(first, [(wout_f32_ref, wout_ref), (wq_f32_ref, wq_ref), (wo_f32_ref, wo_ref)])
    x1 = x_ref[0] + _dot(y_ref[0], wout_ref[...])
    q = _dot(_rms(x1, gx_ref[...]).astype(BF16), wq_ref[...]).astype(BF16)
    kv = kv_ref[0]
    head = lambda h: slice(h * XATTN_HEAD_DIM, (h + 1) * XATTN_HEAD_DIM)
    scores = [_dot_nt(q[:, head(h)], kv[:, head(h)]) * (XATTN_HEAD_DIM ** -0.5) for h in range(XATTN_HEADS)]
    heads = []
    for h in range(XATTN_HEADS):
        s = scores[h]
        e = jnp.exp(s - jnp.max(s, axis=-1, keepdims=True))
        prob = e / jnp.sum(e, axis=-1, keepdims=True)
        heads.append(_dot(prob.astype(BF16), kv[:, D_MODEL + h * XATTN_HEAD_DIM:D_MODEL + (h + 1) * XATTN_HEAD_DIM]))
    o = jnp.concatenate(heads, axis=1).astype(BF16)
    o_ref[0] = x1 + _dot(o, wo_ref[...])


def _post_mix(x, y, w_out, g_x, w_q, kv, w_o, tm=1024):
    b, s, _ = x.shape
    m = kv.shape[1]
    assert s % tm == 0
    blk = lambda bi, ti: (bi, ti, 0)
    return pl.pallas_call(
        _post_mix_kernel,
        grid=(b, s // tm),
        in_specs=[pl.BlockSpec((1, tm, D_MODEL), blk), pl.BlockSpec((1, tm, y.shape[2]), blk),
                  _full(w_out.shape), _full(g_x.shape), _full(w_q.shape),
                  pl.BlockSpec((1, m, 2 * D_MODEL), lambda bi, ti: (bi, 0, 0)), _full(w_o.shape)],
        out_specs=pl.BlockSpec((1, tm, D_MODEL), blk),
        out_shape=jax.ShapeDtypeStruct((b, s, D_MODEL), F32),
        scratch_shapes=[pltpu.VMEM(a.shape, BF16) for a in (w_out, w_q, w_o)],
        compiler_params=_params(2),
        name="post_mix",
    )(x, y, w_out, g_x, w_q, kv, w_o)


def _mlp_kernel(x_ref, g_ref, wup_ref, wdown_ref, gf_ref, o_ref, *, tf, final_norm):
    x = x_ref[...]
    h = _rms(x, g_ref[...]).astype(BF16)
    acc = x
    for j in range(D_FF // tf):
        u = jnp.maximum(_dot(h, wup_ref[:, j * tf:(j + 1) * tf]), 0.0)
        acc = acc + _dot((u * u).astype(BF16), wdown_ref[j * tf:(j + 1) * tf, :])
    o_ref[...] = _rms(acc, gf_ref[...]) if final_norm else acc


def _mlp(x2d, g, w_up, w_down, g_final, final_norm, tm=1024, tf=512):
    t = x2d.shape[0]
    assert t % tm == 0 and D_FF % tf == 0
    row = lambda i: (i, 0)
    return pl.pallas_call(
        functools.partial(_mlp_kernel, tf=tf, final_norm=final_norm),
        grid=(t // tm,),
        in_specs=[pl.BlockSpec((tm, D_MODEL), row), _full(g.shape), _full(w_up.shape), _full(w_down.shape),
                  _full(g_final.shape)],
        out_specs=pl.BlockSpec((tm, D_MODEL), row),
        out_shape=jax.ShapeDtypeStruct((t, D_MODEL), F32),
        compiler_params=_params(1),
        name="mlp",
    )(x2d, g, w_up, w_down, g_final)


def kernel(x, mem, positions, norm_mix, w_in, ret_gn_w, ret_gn_b, rwkv_mu, rwkv_w0, rwkv_w_up, rwkv_a0,
           rwkv_a_up, rwkv_g_up, rwkv_k_k, rwkv_k_a, rwkv_r_k, rwkv_gn_w, rwkv_gn_b, w_out, norm_xattn,
           norm_mem, xattn_w_q, xattn_w_kv, xattn_w_o, norm_mlp, mlp_w_up, mlp_w_down, norm_final):
    b, s, dm = x.shape
    n_layers = w_in.shape[0]
    freq, xi_w, zeta_w, dmask_pairs, gc_w = _retention_consts()
    tri = (jnp.arange(WKV_CHUNK)[:, None] >= jnp.arange(WKV_CHUNK)[None, :]).astype(F32)
    row = lambda a: a.reshape(1, -1)
    for l in range(n_layers):
        mu = rwkv_mu[l]
        prep_consts = [row(norm_mix[l]), jnp.swapaxes(w_in[l], 0, 1), freq, xi_w, zeta_w, row(mu[:3 * RWKV_WIDTH]), row(mu[3 * RWKV_WIDTH:]),
                       row(rwkv_w0[l]), rwkv_w_up[l], row(rwkv_a0[l]), rwkv_a_up[l],
                       rwkv_g_up[l], row(rwkv_k_k[l]), row(rwkv_k_a[l]), row(rwkv_r_k[l]), tri]
        sb, sf, pc = _prep(x.reshape(b * s, dm), _packed_positions(positions, PREP_ROWS), s, prep_consts)
        y = _mixers(sb.reshape(b, s, -1), sf.reshape(b, s, -1), pc.reshape(b, s // WKV_CHUNK, -1), dmask_pairs, gc_w,
                    row(ret_gn_w[l]), row(ret_gn_b[l]), row(rwkv_gn_w[l]), row(rwkv_gn_b[l]))
        kv = _mem_kv(mem.reshape(-1, dm), norm_mem[l][None, :], xattn_w_kv[l])
        x = _post_mix(x, y, w_out[l], norm_xattn[l][None, :], xattn_w_q[l], kv.reshape(b, -1, 2 * dm),
                      xattn_w_o[l])
        x = _mlp(x.reshape(b * s, dm), norm_mlp[l][None, :], mlp_w_up[l].astype(BF16),
                 mlp_w_down[l].astype(BF16), norm_final[None, :], l == n_layers - 1).reshape(b, s, dm)
    return x
```

```python
import functools
import math

import jax
import jax.numpy as jnp
from jax import lax
from jax.experimental import pallas as pl
from jax.experimental.pallas import tpu as pltpu

D_MODEL = 1024
HEAD_DIM = 64
RET_WIDTH = 512
RWKV_WIDTH = 512
N_HEADS = 8
RET_CHUNK = 128
ROPE_BASE = 10000.0
DECAY_LORA = 64
AAA_LORA = 64
GATE_LORA = 160
LORA_WIDTH = DECAY_LORA + AAA_LORA + GATE_LORA
RET_PROJ = 4 * RET_WIDTH
XATTN_HEADS = 4
XATTN_HEAD_DIM = D_MODEL // XATTN_HEADS
D_FF = 4 * D_MODEL
RMS_EPS = 1e-6
GN_EPS_RET = 1e-5
GN_EPS_RWKV = 64e-5

WKV_CHUNK = 64
MIX_SEQS = 4
MIX_BLOCK = 128
WKV_WAVE = 16
PREP_ROWS = 512
CAST_SLAB = 512
VMEM_LIMIT_BYTES = 56 * 1024 * 1024

BF16 = jnp.bfloat16
F32 = jnp.float32


def _dot(a, b):
    return jnp.dot(a, b, preferred_element_type=F32)


def _dot_nt(a, b):
    return lax.dot_general(a, b, (((1,), (1,)), ((), ())), preferred_element_type=F32)


def _dot_tn(a, b):
    return lax.dot_general(a, b, (((0,), (0,)), ((), ())), preferred_element_type=F32)


def _rms(x, g):
    return x * lax.rsqrt(jnp.mean(x * x, axis=-1, keepdims=True) + RMS_EPS) * g


def _params(n_axes):
    return pltpu.CompilerParams(dimension_semantics=("arbitrary",) * n_axes,
                                vmem_limit_bytes=VMEM_LIMIT_BYTES)


def _full(shape):
    zeros = (0,) * len(shape)
    return pl.BlockSpec(shape, lambda *_: zeros, pipeline_mode=pl.Buffered(1))


def _cast_weights_once(first_step, pairs):
    @pl.when(first_step)
    def _():
        for src_ref, dst_ref in pairs:
            n = src_ref.shape[1]
            for lo in range(0, n, CAST_SLAB):
                hi = min(lo + CAST_SLAB, n)
                dst_ref[:, lo:hi] = src_ref[:, lo:hi].astype(dst_ref.dtype)


def _head_sums(x):
    lanes = 2 * HEAD_DIM
    in_h0 = lax.broadcasted_iota(jnp.int32, (x.shape[0], lanes), 1) < HEAD_DIM
    tiles = []
    for lo in range(0, x.shape[1], lanes):
        t = x[:, lo:lo + lanes]
        s0 = jnp.sum(jnp.where(in_h0, t, 0.0), axis=-1, keepdims=True)
        s1 = jnp.sum(jnp.where(in_h0, 0.0, t), axis=-1, keepdims=True)
        tiles.append(jnp.where(in_h0, s0, s1))
    return jnp.concatenate(tiles, axis=1)


def _head_norm(y, eps):
    d = y - _head_sums(y) * (1.0 / HEAD_DIM)
    var = _head_sums(d * d) * (1.0 / HEAD_DIM)
    return d * lax.rsqrt(var + eps)


def _pair_diag(x):
    in_h0 = lax.broadcasted_iota(jnp.int32, x.shape, 1) < HEAD_DIM
    zero = jnp.zeros_like(x)
    return jnp.concatenate([jnp.where(in_h0, x, zero), jnp.where(in_h0, zero, x)], axis=0)


def _retention_body(q, k, q_in, k_out, vb, gsilu, dmask_ref, gc_ref, same_head_ref, gnw_ref, gnb_ref, state_ref):
    c = RET_CHUNK
    lanes = 2 * HEAD_DIM
    n_groups = RET_WIDTH // lanes
    n_chunks = q.shape[0] // c
    units = [(ci, g) for ci in range(n_chunks) for g in range(n_groups)]
    tile = lambda x, ci, g: x[ci * c:(ci + 1) * c, g * lanes:(g + 1) * lanes]
    s_u = {(ci, g): (_dot_nt(tile(q, ci, g), _pair_diag(tile(k, ci, g))) * dmask_ref[g]).astype(BF16)
           for ci, g in units}
    y_u = {(ci, g): _dot(s_u[ci, g], _pair_diag(tile(vb, ci, g))) for ci, g in units}
    kv_u = {(ci, g): _dot_tn(tile(k_out, ci, g), tile(vb, ci, g)) for ci, g in units}
    same_head = same_head_ref[...] > 0.5
    n_seq = state_ref.shape[0]
    chunks_per_seq = n_chunks // n_seq
    chains = [(sq, g) for sq in range(n_seq) for g in range(n_groups)]
    r_g = {(sq, g): state_ref[sq, g] for sq, g in chains}
    for step in range(chunks_per_seq):
        for sq, g in chains:
            ci = sq * chunks_per_seq + step
            y_u[ci, g] = y_u[ci, g] + _dot(tile(q_in, ci, g), r_g[sq, g].astype(BF16))
        r_g = {(sq, g): r_g[sq, g] * gc_ref[g] + jnp.where(same_head, kv_u[sq * chunks_per_seq + step, g], 0.0)
               for sq, g in chains}
    for sq, g in chains:
        state_ref[sq, g] = r_g[sq, g]
    y_all = jnp.concatenate([jnp.concatenate([y_u[ci, g] for g in range(n_groups)], axis=1)
                             for ci in range(n_chunks)], axis=0)
    return gsilu * (_head_norm(y_all, GN_EPS_RET) * gnw_ref[...] + gnb_ref[...])


def _retention_consts():
    c = RET_CHUNK
    half = HEAD_DIM // 2
    inv_freq = ROPE_BASE ** (-jnp.arange(half, dtype=F32) / half)
    freq = jnp.tile(inv_freq, 128 // half)[None, :]
    log_g = jnp.log(1.0 - 2.0 ** (-5.0 - jnp.arange(N_HEADS, dtype=F32)))
    idx = jnp.arange(c, dtype=F32)
    diff = idx[:, None] - idx[None, :]
    causal = diff >= 0
    dmask = jnp.where(causal[None], jnp.exp(log_g[:, None, None] * jnp.where(causal, diff, 0.0)[None]), 0.0)
    xi = jnp.exp(log_g[:, None] * (idx + 1.0)[None])
    zeta = jnp.exp(log_g[:, None] * (c - 1.0 - idx)[None])
    g_chunk = jnp.exp(log_g * c)
    xi_w = jnp.repeat(xi.T, HEAD_DIM, axis=1)
    zeta_w = jnp.repeat(zeta.T, HEAD_DIM, axis=1)
    dmask_pairs = jnp.concatenate([dmask[0::2], dmask[1::2]], axis=2)
    gc_rows = jnp.repeat(g_chunk, HEAD_DIM).reshape(N_HEADS // 2, 2 * HEAD_DIM, 1)
    gc_w = jnp.broadcast_to(gc_rows, (N_HEADS // 2, 2 * HEAD_DIM, 2 * HEAD_DIM))
    return freq, xi_w, zeta_w, dmask_pairs, gc_w


def _shift(p, carry_ref, mu_ref, lo):
    cols = slice(lo, lo + p.shape[1])
    rows = lax.broadcasted_iota(jnp.int32, p.shape, 0)
    prev = jnp.where(rows == 0, carry_ref[:, cols], pltpu.roll(p, 1, 0))
    carry_ref[:, cols] = p[p.shape[0] - 1:, :]
    return p + (prev - p) * mu_ref[:, cols]


def _wkv_body(at, rt, bt, kt, bc, kc, vb, bonus, gate, pc_ref, pc_row0, gnw_ref, gnb_ref, mask_ref, eye_ref,
              state_mask_ref, state_ref):
    c = WKV_CHUNK
    w = RWKV_WIDTH
    lanes = 2 * HEAD_DIM
    n_groups = w // lanes
    n_chunks = at.shape[0] // c
    eye = eye_ref[...]
    tril = mask_ref[...] > 0.5

    pair_diag = _pair_diag
    all_units = [(ci, g) for ci in range(n_chunks) for g in range(n_groups)]
    tile = lambda x, ci, g: x[ci * c:(ci + 1) * c, g * lanes:(g + 1) * lanes]
    g_u, wr_u, u0_u, y0_u = {}, {}, {}, {}
    for lo in range(0, len(all_units), WKV_WAVE):
        units = all_units[lo:lo + WKV_WAVE]
        vd_u = {(ci, g): pair_diag(tile(vb, ci, g)) for ci, g in units}
        for ci, g in units:
            lhs = jnp.concatenate([tile(at, ci, g), tile(rt, ci, g)], axis=0)
            rhs = jnp.concatenate([pair_diag(tile(bt, ci, g)), pair_diag(tile(kt, ci, g))], axis=0)
            g_u[ci, g] = jnp.where(tril, _dot_nt(lhs, rhs), 0.0)
        m_u = {u: g_u[u][:c, :lanes] for u in units}
        t_u = {u: eye + m_u[u] for u in units}
        mb_u = {u: m_u[u].astype(BF16) for u in units}
        mb_u = {u: _dot(mb_u[u], pair_diag(mb_u[u])).astype(BF16) for u in units}
        power = 2
        while 2 * power < c:
            tm_u = {u: _dot(jnp.concatenate([t_u[u].astype(BF16), mb_u[u]], axis=0), pair_diag(mb_u[u]))
                    for u in units}
            t_u = {u: t_u[u] + tm_u[u][:c] for u in units}
            mb_u = {u: tm_u[u][c:].astype(BF16) for u in units}
            power *= 2
        t_u = {u: (t_u[u] + _dot(t_u[u].astype(BF16), pair_diag(mb_u[u]))).astype(BF16) for u in units}
        kv_u = {u: _dot(g_u[u][:, lanes:].astype(BF16), vd_u[u]) for u in units}
        wu_u = {(ci, g): _dot(t_u[ci, g], jnp.concatenate(
            [pair_diag(tile(at, ci, g)), pair_diag(kv_u[ci, g][:c].astype(BF16))], axis=1)) for ci, g in units}
        wr_u.update({(ci, g): jnp.concatenate([wu_u[ci, g][:, :lanes].astype(BF16), tile(rt, ci, g)], axis=0)
                     for ci, g in units})
        u0_u.update({u: wu_u[u][:, lanes:] for u in units})
        y0_u.update({u: kv_u[u][c:] for u in units})

    same_head = state_mask_ref[...] > 0.5
    n_seq = state_ref.shape[0]
    chunks_per_seq = n_chunks // n_seq
    chains = [(sq, g) for sq in range(n_seq) for g in range(n_groups)]
    s_g = {(sq, g): state_ref[sq, g] for sq, g in chains}
    y_u = {}
    for step in range(chunks_per_seq):
        ci_of = lambda sq: sq * chunks_per_seq + step
        pc = [pc_ref[sq, pl.ds(pc_row0 + step, 1), :] for sq in range(n_seq)]
        ws = {(sq, g): _dot_nt(wr_u[ci_of(sq), g], s_g[sq, g].astype(BF16)) for sq, g in chains}
        u_b = {(sq, g): (ws[sq, g][:c] + u0_u[ci_of(sq), g]).astype(BF16) for sq, g in chains}
        for sq, g in chains:
            ci = ci_of(sq)
            y_u[ci, g] = (ws[sq, g][c:]
                          + _dot(g_u[ci, g][c:, :lanes].astype(BF16), pair_diag(u_b[sq, g])) + y0_u[ci, g])
        s_g = {(sq, g): s_g[sq, g] * pc[sq][:, g * lanes:(g + 1) * lanes]
               + jnp.where(same_head,
                           _dot_tn(jnp.concatenate([u_b[sq, g], tile(vb, ci_of(sq), g)], axis=0),
                                   jnp.concatenate([tile(bc, ci_of(sq), g), tile(kc, ci_of(sq), g)], axis=0)),
                           0.0)
               for sq, g in chains}
    for sq, g in chains:
        state_ref[sq, g] = s_g[sq, g]
    y_all = jnp.concatenate([jnp.concatenate([y_u[ci, g] for g in range(n_groups)], axis=1)
                             for ci in range(n_chunks)], axis=0)
    return (_head_norm(y_all, GN_EPS_RWKV) * gnw_ref[...] + gnb_ref[...] + bonus) * gate


N_RET_BF16 = 5
N_BF16_STREAMS = 12
N_F32_STREAMS = 3


def _prep_kernel(x_ref, pos_ref, g_ref, w_in_ref, freq_ref, xi_ref, zeta_ref,
                 mu_rkv_ref, mu_lora_ref, w0_ref, wup_ref, a0_ref, aup_ref, gup_ref, kk_ref, ka_ref, rk_ref,
                 tri_ref, *refs, tiles_per_seq):
    bf16_ref, f32_ref, pc_ref, w_ref, carry_rkv_ref, carry_lora_ref = refs
    stream = lambda ref, i: ref.at[:, pl.ds(i * RET_WIDTH, RET_WIDTH)]
    (q_ref, k_ref, qin_ref, kout_ref, vr_ref,
     at_ref, rt_ref, bt_ref, kt_ref, bc_ref, kc_ref, vw_ref) = [stream(bf16_ref, i) for i in range(N_BF16_STREAMS)]
    gsilu_ref, bonus_ref, gate_ref = [stream(f32_ref, i) for i in range(N_F32_STREAMS)]
    tm = x_ref.shape[0]
    w = RWKV_WIDTH
    rkv0 = RET_PROJ
    @pl.when(pl.program_id(0) == 0)
    def _():
        n_out = w_in_ref.shape[0]
        for lo in range(0, n_out, 128):
            hi = min(lo + 128, n_out)
            w_ref[:, lo:hi] = w_in_ref[lo:hi, :].T.astype(BF16)

    @pl.when(pl.program_id(0) % tiles_per_seq == 0)
    def _():
        carry_rkv_ref[...] = jnp.zeros_like(carry_rkv_ref)
        carry_lora_ref[...] = jnp.zeros_like(carry_lora_ref)

    h = _rms(x_ref[...], g_ref[...]).astype(BF16)

    wr = RET_WIDTH
    lanes = 2 * HEAD_DIM
    half = HEAD_DIM // 2
    p_q = _dot(h, w_ref[:, :wr])
    p_k = _dot(h, w_ref[:, wr:2 * wr])
    p_r = _dot(h, w_ref[:, rkv0:rkv0 + w])
    plora_raw = _dot(h, w_ref[:, rkv0 + 3 * w:])
    p_kw = _dot(h, w_ref[:, rkv0 + w:rkv0 + 2 * w])
    p_vw = _dot(h, w_ref[:, rkv0 + 2 * w:rkv0 + 3 * w])

    n_pack = lanes // half
    ang = pos_ref[...].astype(F32) * freq_ref[...]
    lane_p = lax.broadcasted_iota(jnp.int32, ang.shape, 1)

    def unpack(t):
        blocks = []
        for j in range(n_pack):
            g = pltpu.roll(t, (lanes - j * half) % lanes, 1) if j else t
            spread = g
            for m in range(1, n_pack):
                spread = jnp.where(lane_p < m * half, spread, pltpu.roll(g, m * half, 1))
            blocks.append(spread)
        return jnp.concatenate(blocks, axis=0)

    first_half = (lax.broadcasted_iota(jnp.int32, (tm, lanes), 1) % HEAD_DIM) < half
    cos = unpack(jnp.cos(ang))
    sin = unpack(jnp.sin(ang))
    sin = jnp.where(first_half, -sin, sin)
    k_scale = HEAD_DIM ** -0.5

    def rope(t, cos_t, sin_t):
        tiles = []
        for lo in range(0, wr, lanes):
            x = t[:, lo:lo + lanes]
            partner = jnp.where(first_half, pltpu.roll(x, lanes - half, 1), pltpu.roll(x, half, 1))
            tiles.append(x * cos_t + partner * sin_t)
        return jnp.concatenate(tiles, axis=1)

    def chunk_scaled(dst_ref, t, table_ref):
        for lo in range(0, tm, RET_CHUNK):
            dst_ref[lo:lo + RET_CHUNK, :] = (t[lo:lo + RET_CHUNK] * table_ref[...]).astype(BF16)

    q = rope(p_q, cos, sin)
    q_ref[...] = q.astype(BF16)
    chunk_scaled(qin_ref, q, xi_ref)

    kr = rope(p_k, cos * k_scale, sin * k_scale)
    k_ref[...] = kr.astype(BF16)
    chunk_scaled(kout_ref, kr, zeta_ref)

    r = _shift(p_r, carry_rkv_ref, mu_rkv_ref, 0)
    plora = _shift(plora_raw, carry_lora_ref, mu_lora_ref, 0)
    w_lr = plora[:, :DECAY_LORA]
    a_lr = plora[:, DECAY_LORA:DECAY_LORA + AAA_LORA]
    g_lr = plora[:, DECAY_LORA + AAA_LORA:]
    lw = -math.exp(-0.5) * jax.nn.sigmoid(
        w0_ref[...] + _dot(jnp.tanh(w_lr).astype(BF16), wup_ref[...].astype(BF16)))
    a_sig = jax.nn.sigmoid(a0_ref[...] + _dot(a_lr.astype(BF16), aup_ref[...].astype(BF16)))
    gate_ref[...] = _dot(jax.nn.sigmoid(g_lr).astype(BF16), gup_ref[...].astype(BF16))

    k = _shift(p_kw, carry_rkv_ref, mu_rkv_ref, w)
    kk = k * kk_ref[...]
    k2 = k * (1.0 + (a_sig - 1.0) * ka_ref[...])
    kk = kk * (1.0 / jnp.maximum(jnp.sqrt(_head_sums(kk * kk)), 1e-12))
    a_vec = -kk
    b_vec = kk * a_sig

    v = _shift(p_vw, carry_rkv_ref, mu_rkv_ref, 2 * w)
    bonus_ref[...] = _head_sums(r * k2 * rk_ref[...]) * v
    vw_ref[...] = v.astype(BF16)

    p_v = _dot(h, w_ref[:, 2 * wr:3 * wr])
    p_g = _dot(h, w_ref[:, 3 * wr:RET_PROJ])
    c = WKV_CHUNK
    for ci in range(tm // c):
        rows = slice(ci * c, (ci + 1) * c)
        lw_c = lw[rows]
        cum = jnp.dot(tri_ref[...], lw_c, precision=lax.Precision.HIGHEST, preferred_element_type=F32)
        cum_end = cum[c - 1:, :]
        p_inv = jnp.exp(-cum)
        p_end = jnp.exp(cum_end)
        p_out = p_inv * p_end
        at_ref[rows, :] = (a_vec[rows] * jnp.exp(cum - lw_c)).astype(BF16)
        rt_ref[rows, :] = (r[rows] * jnp.exp(cum)).astype(BF16)
        bt_ref[rows, :] = (b_vec[rows] * p_inv).astype(BF16)
        kt_ref[rows, :] = (k2[rows] * p_inv).astype(BF16)
        bc_ref[rows, :] = (b_vec[rows] * p_out).astype(BF16)
        kc_ref[rows, :] = (k2[rows] * p_out).astype(BF16)
        pc_ref[ci:ci + 1, :] = p_end

    vr_ref[...] = p_v.astype(BF16)
    gsilu_ref[...] = p_g * jax.nn.sigmoid(p_g)


ROPE_PACK = 2 * HEAD_DIM // (HEAD_DIM // 2)


def _packed_positions(positions, tm):
    lanes_per_group = 128 // ROPE_PACK
    tiles = positions.reshape(-1, ROPE_PACK, tm // ROPE_PACK)
    return jnp.repeat(jnp.swapaxes(tiles, 1, 2), lanes_per_group, axis=2).reshape(-1, 128)


def _prep(x2d, pos2d, seq_len, consts, tm=PREP_ROWS):
    t = x2d.shape[0]
    assert t % tm == 0 and seq_len % tm == 0 and tm % RET_CHUNK == 0 and tm % WKV_CHUNK == 0
    row = lambda i: (i, 0)
    width = RET_WIDTH
    n_pc = tm // WKV_CHUNK
    widths = [N_BF16_STREAMS * width, N_F32_STREAMS * width]
    return pl.pallas_call(
        functools.partial(_prep_kernel, tiles_per_seq=seq_len // tm),
        grid=(t // tm,),
        in_specs=[pl.BlockSpec((tm, D_MODEL), row), pl.BlockSpec((tm // ROPE_PACK, 128), row)]
                 + [_full(a.shape) for a in consts],
        out_specs=[pl.BlockSpec((tm, n), row) for n in widths] + [pl.BlockSpec((n_pc, width), row)],
        out_shape=[jax.ShapeDtypeStruct((t, widths[0]), BF16), jax.ShapeDtypeStruct((t, widths[1]), F32),
                   jax.ShapeDtypeStruct((t // WKV_CHUNK, width), F32)],
        scratch_shapes=[pltpu.VMEM(consts[1].shape[::-1], BF16), pltpu.VMEM((1, 3 * RWKV_WIDTH), F32),
                        pltpu.VMEM((1, LORA_WIDTH), F32)],
        compiler_params=_params(1),
        name="prep",
    )(x2d, pos2d, *consts)


def _mixers_kernel(*refs):
    bf16_ref, f32_ref, refs = refs[0], refs[1], refs[2:]
    (pc_ref, dmask_ref, gc_ref, same_head_ref, ret_gnw_ref, ret_gnb_ref, gnw_ref, gnb_ref, mask_ref, eye_ref,
     o_ref, ret_state_ref, wkv_state_ref) = refs
    ti = pl.program_id(1)

    @pl.when(ti == 0)
    def _():
        ret_state_ref[...] = jnp.zeros_like(ret_state_ref)
        wkv_state_ref[...] = jnp.zeros_like(wkv_state_ref)

    rows = bf16_ref.shape[0] * bf16_ref.shape[1]
    stream = lambda ref, i: ref[:, :, i * RET_WIDTH:(i + 1) * RET_WIDTH].reshape(rows, RET_WIDTH)
    ret_in = [stream(bf16_ref, i) for i in range(N_RET_BF16)] + [stream(f32_ref, 0)]
    wkv_in = ([stream(bf16_ref, i) for i in range(N_RET_BF16, N_BF16_STREAMS)]
              + [stream(f32_ref, i) for i in range(1, N_F32_STREAMS)])
    y_ret = _retention_body(*ret_in, dmask_ref, gc_ref, same_head_ref, ret_gnw_ref,
                            ret_gnb_ref, ret_state_ref)
    chunks_per_block = o_ref.shape[1] // WKV_CHUNK
    y_rwkv = _wkv_body(*wkv_in, pc_ref, ti * chunks_per_block, gnw_ref, gnb_ref, mask_ref,
                       eye_ref, same_head_ref, wkv_state_ref)
    o_ref[...] = jnp.concatenate([y_ret, y_rwkv], axis=1).astype(o_ref.dtype).reshape(o_ref.shape)


def _mixers(bf16_streams, f32_streams, pc, dmask_pairs, gc_w, ret_gn_w, ret_gn_b, gn_w, gn_b):
    b, s, _ = bf16_streams.shape
    width = RET_WIDTH
    n_seq, tb, c = MIX_SEQS, MIX_BLOCK, WKV_CHUNK
    assert b % n_seq == 0 and s % tb == 0 and tb % RET_CHUNK == 0 and tb % c == 0
    idx = jnp.arange(c)
    incl = (idx[:, None] >= idx[None, :]).astype(F32)
    strict = (idx[:, None] > idx[None, :]).astype(F32)
    mask = jnp.concatenate([jnp.tile(strict, (1, 4)), jnp.tile(incl, (1, 4))], axis=0)
    eye = jnp.tile(jnp.eye(c, dtype=F32), (1, 2))
    head_of_lane = jnp.arange(2 * HEAD_DIM) // HEAD_DIM
    state_mask = (head_of_lane[:, None] == head_of_lane[None, :]).astype(F32)
    consts = [dmask_pairs, gc_w, state_mask, ret_gn_w, ret_gn_b, gn_w, gn_b, mask, eye]
    blk = lambda bi, ti: (bi, ti, 0)
    state = pltpu.VMEM((n_seq, N_HEADS // 2, 2 * HEAD_DIM, 2 * HEAD_DIM), F32)
    return pl.pallas_call(
        _mixers_kernel,
        grid=(b // n_seq, s // tb),
        in_specs=[pl.BlockSpec((n_seq, tb, a.shape[2]), blk) for a in (bf16_streams, f32_streams)]
                 + [pl.BlockSpec((n_seq, s // c, width), lambda bi, ti: (bi, 0, 0))]
                 + [_full(a.shape) for a in consts],
        out_specs=pl.BlockSpec((n_seq, tb, RET_WIDTH + RWKV_WIDTH), blk),
        out_shape=jax.ShapeDtypeStruct((b, s, RET_WIDTH + RWKV_WIDTH), BF16),
        scratch_shapes=[state, state],
        compiler_params=_params(2),
        name="mixers",
    )(bf16_streams, f32_streams, pc, *consts)


def _mem_kv_kernel(m_ref, g_ref, w_f32_ref, o_ref, w_ref):
    _cast_weights_once(pl.program_id(0) == 0, [(w_f32_ref, w_ref)])
    h = _rms(m_ref[...], g_ref[...]).astype(BF16)
    o_ref[...] = _dot(h, w_ref[...]).astype(o_ref.dtype)


def _mem_kv(mem2d, g, w_kv, tm=1024):
    t = mem2d.shape[0]
    n = w_kv.shape[1]
    tm = min(tm, t)
    assert t % tm == 0
    return pl.pallas_call(
        _mem_kv_kernel,
        grid=(t // tm,),
        in_specs=[pl.BlockSpec((tm, D_MODEL), lambda i: (i, 0)), _full(g.shape), _full(w_kv.shape)],
        out_specs=pl.BlockSpec((tm, n), lambda i: (i, 0)),
        out_shape=jax.ShapeDtypeStruct((t, n), BF16),
        scratch_shapes=[pltpu.VMEM(w_kv.shape, BF16)],
        compiler_params=_params(1),
        name="mem_kv",
    )(mem2d, g, w_kv)


def _post_mix_kernel(x_ref, y_ref, wout_f32_ref, gx_ref, wq_f32_ref, kv_ref, wo_f32_ref, wup_f32_ref, wdown_f32_ref,
                     o_ref, wup_ref, wdown_ref, wout_ref, wq_ref, wo_ref):
    first = jnp.logical_and(pl.program_id(0) == 0, pl.program_id(1) == 0)
    _cast_weights_once(first, [(wout_f32_ref, wout_ref), (wq_f32_ref, wq_ref), (wo_f32_ref, wo_ref)])
    x1 = x_ref[0] + _dot(y_ref[0], wout_ref[...])
    wup_ref[...] = wup_f32_ref[...].astype(BF16)
    wdown_ref[...] = wdown_f32_ref[...].astype(BF16)
    q = _dot(_rms(x1, gx_ref[...]).astype(BF16), wq_ref[...]).astype(BF16)
    kv = kv_ref[0]
    head = lambda h: slice(h * XATTN_HEAD_DIM, (h + 1) * XATTN_HEAD_DIM)
    scores = [_dot_nt(q[:, head(h)], kv[:, head(h)]) * (XATTN_HEAD_DIM ** -0.5) for h in range(XATTN_HEADS)]
    heads = []
    for h in range(XATTN_HEADS):
        s = scores[h]
        e = jnp.exp(s - jnp.max(s, axis=-1, keepdims=True))
        prob = e / jnp.sum(e, axis=-1, keepdims=True)
        heads.append(_dot(prob.astype(BF16), kv[:, D_MODEL + h * XATTN_HEAD_DIM:D_MODEL + (h + 1) * XATTN_HEAD_DIM]))
    o = jnp.concatenate(heads, axis=1).astype(BF16)
    o_ref[0] = x1 + _dot(o, wo_ref[...])


def _post_mix(x, y, w_out, g_x, w_q, kv, w_o, w_up, w_down, tm=1024):
    b, s, _ = x.shape
    m = kv.shape[1]
    assert s % tm == 0
    n_steps = b * (s // tm)
    assert D_FF % (n_steps * 128) == 0
    slab = D_FF // n_steps
    blk = lambda bi, ti: (bi, ti, 0)
    up_spec = pl.BlockSpec((D_MODEL, slab), lambda bi, ti: (0, bi * (s // tm) + ti))
    down_spec = pl.BlockSpec((slab, D_MODEL), lambda bi, ti: (bi * (s // tm) + ti, 0))
    return pl.pallas_call(
        _post_mix_kernel,
        grid=(b, s // tm),
        in_specs=[pl.BlockSpec((1, tm, D_MODEL), blk), pl.BlockSpec((1, tm, y.shape[2]), blk),
                  _full(w_out.shape), _full(g_x.shape), _full(w_q.shape),
                  pl.BlockSpec((1, m, 2 * D_MODEL), lambda bi, ti: (bi, 0, 0)), _full(w_o.shape), up_spec, down_spec],
        out_specs=[pl.BlockSpec((1, tm, D_MODEL), blk), up_spec, down_spec],
        out_shape=[jax.ShapeDtypeStruct((b, s, D_MODEL), F32), jax.ShapeDtypeStruct(w_up.shape, BF16),
                   jax.ShapeDtypeStruct(w_down.shape, BF16)],
        scratch_shapes=[pltpu.VMEM(a.shape, BF16) for a in (w_out, w_q, w_o)],
        compiler_params=_params(2),
        name="post_mix",
    )(x, y, w_out, g_x, w_q, kv, w_o, w_up, w_down)


def _mlp_kernel(x_ref, g_ref, wup_ref, wdown_ref, gf_ref, o_ref, *, tf, final_norm):
    x = x_ref[...]
    h = _rms(x, g_ref[...]).astype(BF16)
    acc = x
    for j in range(D_FF // tf):
        u = jnp.maximum(_dot(h, wup_ref[:, j * tf:(j + 1) * tf]), 0.0)
        acc = acc + _dot((u * u).astype(BF16), wdown_ref[j * tf:(j + 1) * tf, :])
    o_ref[...] = _rms(acc, gf_ref[...]) if final_norm else acc


def _mlp(x2d, g, w_up, w_down, g_final, final_norm, tm=1024, tf=512):
    t = x2d.shape[0]
    assert t % tm == 0 and D_FF % tf == 0
    row = lambda i: (i, 0)
    return pl.pallas_call(
        functools.partial(_mlp_kernel, tf=tf, final_norm=final_norm),
        grid=(t // tm,),
        in_specs=[pl.BlockSpec((tm, D_MODEL), row), _full(g.shape), _full(w_up.shape), _full(w_down.shape),
                  _full(g_final.shape)],
        out_specs=pl.BlockSpec((tm, D_MODEL), row),
        out_shape=jax.ShapeDtypeStruct((t, D_MODEL), F32),
        compiler_params=_params(1),
        name="mlp",
    )(x2d, g, w_up, w_down, g_final)


def kernel(x, mem, positions, norm_mix, w_in, ret_gn_w, ret_gn_b, rwkv_mu, rwkv_w0, rwkv_w_up, rwkv_a0,
           rwkv_a_up, rwkv_g_up, rwkv_k_k, rwkv_k_a, rwkv_r_k, rwkv_gn_w, rwkv_gn_b, w_out, norm_xattn,
           norm_mem, xattn_w_q, xattn_w_kv, xattn_w_o, norm_mlp, mlp_w_up, mlp_w_down, norm_final):
    b, s, dm = x.shape
    n_layers = w_in.shape[0]
    freq, xi_w, zeta_w, dmask_pairs, gc_w = _retention_consts()
    tri = (jnp.arange(WKV_CHUNK)[:, None] >= jnp.arange(WKV_CHUNK)[None, :]).astype(F32)
    row = lambda a: a.reshape(1, -1)
    for l in range(n_layers):
        mu = rwkv_mu[l]
        prep_consts = [row(norm_mix[l]), jnp.swapaxes(w_in[l], 0, 1), freq, xi_w, zeta_w, row(mu[:3 * RWKV_WIDTH]), row(mu[3 * RWKV_WIDTH:]),
                       row(rwkv_w0[l]), rwkv_w_up[l], row(rwkv_a0[l]), rwkv_a_up[l],
                       rwkv_g_up[l], row(rwkv_k_k[l]), row(rwkv_k_a[l]), row(rwkv_r_k[l]), tri]
        sb, sf, pc = _prep(x.reshape(b * s, dm), _packed_positions(positions, PREP_ROWS), s, prep_consts)
        y = _mixers(sb.reshape(b, s, -1), sf.reshape(b, s, -1), pc.reshape(b, s // WKV_CHUNK, -1), dmask_pairs, gc_w,
                    row(ret_gn_w[l]), row(ret_gn_b[l]), row(rwkv_gn_w[l]), row(rwkv_gn_b[l]))
        kv = _mem_kv(mem.reshape(-1, dm), norm_mem[l][None, :], xattn_w_kv[l])
        x, w_up, w_down = _post_mix(x, y, w_out[l], norm_xattn[l][None, :], xattn_w_q[l], kv.reshape(b, -1, 2 * dm),
                                    xattn_w_o[l], mlp_w_up[l], mlp_w_down[l])
        x = _mlp(x.reshape(b * s, dm), norm_mlp[l][None, :], w_up, w_down, norm_final[None, :],
                 l == n_layers - 1).reshape(b, s, dm)
    return x
```

```python
import functools
import math

import jax
import jax.numpy as jnp
import numpy as np
from jax import lax
from jax.experimental import pallas as pl
from jax.experimental.pallas import tpu as pltpu

D_MODEL = 1024
HEAD_DIM = 64
RET_WIDTH = 512
RWKV_WIDTH = 512
N_HEADS = 8
RET_CHUNK = 128
ROPE_BASE = 10000.0
DECAY_LORA = 64
AAA_LORA = 64
GATE_LORA = 160
LORA_WIDTH = DECAY_LORA + AAA_LORA + GATE_LORA
RET_PROJ = 4 * RET_WIDTH
XATTN_HEADS = 4
XATTN_HEAD_DIM = D_MODEL // XATTN_HEADS
D_FF = 4 * D_MODEL
RMS_EPS = 1e-6
GN_EPS_RET = 1e-5
GN_EPS_RWKV = 64e-5

WKV_CHUNK = 64
MIX_SEQS = 4
MIX_BLOCK = 128
WKV_WAVE = 16
PREP_ROWS = 512
CAST_SLAB = 512
VMEM_LIMIT_BYTES = 56 * 1024 * 1024

BF16 = jnp.bfloat16
F32 = jnp.float32


def _dot(a, b):
    return jnp.dot(a, b, preferred_element_type=F32)


def _dot_nt(a, b):
    return lax.dot_general(a, b, (((1,), (1,)), ((), ())), preferred_element_type=F32)


def _dot_tn(a, b):
    return lax.dot_general(a, b, (((0,), (0,)), ((), ())), preferred_element_type=F32)


def _rms(x, g):
    return x * lax.rsqrt(jnp.mean(x * x, axis=-1, keepdims=True) + RMS_EPS) * g


def _params(n_axes):
    return pltpu.CompilerParams(dimension_semantics=("arbitrary",) * n_axes,
                                vmem_limit_bytes=VMEM_LIMIT_BYTES)


def _full(shape):
    zeros = (0,) * len(shape)
    return pl.BlockSpec(shape, lambda *_: zeros, pipeline_mode=pl.Buffered(1))


def _cast_weights_once(first_step, pairs):
    @pl.when(first_step)
    def _():
        for src_ref, dst_ref in pairs:
            n = src_ref.shape[1]
            for lo in range(0, n, CAST_SLAB):
                hi = min(lo + CAST_SLAB, n)
                dst_ref[:, lo:hi] = src_ref[:, lo:hi].astype(dst_ref.dtype)


def _head_sums(x):
    lanes = 2 * HEAD_DIM
    in_h0 = lax.broadcasted_iota(jnp.int32, (x.shape[0], lanes), 1) < HEAD_DIM
    tiles = []
    for lo in range(0, x.shape[1], lanes):
        t = x[:, lo:lo + lanes]
        s0 = jnp.sum(jnp.where(in_h0, t, 0.0), axis=-1, keepdims=True)
        s1 = jnp.sum(jnp.where(in_h0, 0.0, t), axis=-1, keepdims=True)
        tiles.append(jnp.where(in_h0, s0, s1))
    return jnp.concatenate(tiles, axis=1)


def _head_norm(y, eps):
    d = y - _head_sums(y) * (1.0 / HEAD_DIM)
    var = _head_sums(d * d) * (1.0 / HEAD_DIM)
    return d * lax.rsqrt(var + eps)


def _pair_diag(x):
    in_h0 = lax.broadcasted_iota(jnp.int32, x.shape, 1) < HEAD_DIM
    zero = jnp.zeros_like(x)
    return jnp.concatenate([jnp.where(in_h0, x, zero), jnp.where(in_h0, zero, x)], axis=0)


def _retention_body(q, k, q_in, k_out, vb, gsilu, dmask_ref, gc_ref, same_head_ref, gnw_ref, gnb_ref, state_ref):
    c = RET_CHUNK
    lanes = 2 * HEAD_DIM
    n_groups = RET_WIDTH // lanes
    n_chunks = q.shape[0] // c
    units = [(ci, g) for ci in range(n_chunks) for g in range(n_groups)]
    tile = lambda x, ci, g: x[ci * c:(ci + 1) * c, g * lanes:(g + 1) * lanes]
    s_u = {(ci, g): (_dot_nt(tile(q, ci, g), _pair_diag(tile(k, ci, g))) * dmask_ref[g]).astype(BF16)
           for ci, g in units}
    y_u = {(ci, g): _dot(s_u[ci, g], _pair_diag(tile(vb, ci, g))) for ci, g in units}
    kv_u = {(ci, g): _dot_tn(tile(k_out, ci, g), tile(vb, ci, g)) for ci, g in units}
    same_head = same_head_ref[...] > 0.5
    n_seq = state_ref.shape[0]
    chunks_per_seq = n_chunks // n_seq
    chains = [(sq, g) for sq in range(n_seq) for g in range(n_groups)]
    r_g = {(sq, g): state_ref[sq, g] for sq, g in chains}
    for step in range(chunks_per_seq):
        for sq, g in chains:
            ci = sq * chunks_per_seq + step
            y_u[ci, g] = y_u[ci, g] + _dot(tile(q_in, ci, g), r_g[sq, g].astype(BF16))
        r_g = {(sq, g): r_g[sq, g] * gc_ref[g] + jnp.where(same_head, kv_u[sq * chunks_per_seq + step, g], 0.0)
               for sq, g in chains}
    for sq, g in chains:
        state_ref[sq, g] = r_g[sq, g]
    y_all = jnp.concatenate([jnp.concatenate([y_u[ci, g] for g in range(n_groups)], axis=1)
                             for ci in range(n_chunks)], axis=0)
    return gsilu * (_head_norm(y_all, GN_EPS_RET) * gnw_ref[...] + gnb_ref[...])


def _retention_consts():
    c = RET_CHUNK
    half = HEAD_DIM // 2
    inv_freq = ROPE_BASE ** (-jnp.arange(half, dtype=F32) / half)
    freq = jnp.tile(inv_freq, 128 // half)[None, :]
    log_g = np.log(1.0 - 2.0 ** (-5.0 - np.arange(N_HEADS, dtype=np.float64)))
    idx = np.arange(c, dtype=np.float64)
    diff = idx[:, None] - idx[None, :]
    causal = diff >= 0
    dmask = np.where(causal[None], np.exp(log_g[:, None, None] * np.where(causal, diff, 0.0)[None]), 0.0)
    xi = np.exp(log_g[:, None] * (idx + 1.0)[None])
    zeta = np.exp(log_g[:, None] * (c - 1.0 - idx)[None])
    g_chunk = np.exp(log_g * c)
    xi_w = np.repeat(xi.T, HEAD_DIM, axis=1)
    zeta_w = np.repeat(zeta.T, HEAD_DIM, axis=1)
    dmask_pairs = np.concatenate([dmask[0::2], dmask[1::2]], axis=2)
    gc_rows = np.repeat(g_chunk, HEAD_DIM).reshape(N_HEADS // 2, 2 * HEAD_DIM, 1)
    gc_w = np.broadcast_to(gc_rows, (N_HEADS // 2, 2 * HEAD_DIM, 2 * HEAD_DIM))
    return (freq,) + tuple(jnp.asarray(a, F32) for a in (xi_w, zeta_w, dmask_pairs, gc_w))


def _shift(p, carry_ref, mu_ref, lo):
    cols = slice(lo, lo + p.shape[1])
    rows = lax.broadcasted_iota(jnp.int32, p.shape, 0)
    prev = jnp.where(rows == 0, carry_ref[:, cols], pltpu.roll(p, 1, 0))
    carry_ref[:, cols] = p[p.shape[0] - 1:, :]
    return p + (prev - p) * mu_ref[:, cols]


def _wkv_body(at, rt, bt, kt, bc, kc, vb, bonus, gate, pc_ref, pc_row0, gnw_ref, gnb_ref, mask_ref, eye_ref,
              state_mask_ref, state_ref):
    c = WKV_CHUNK
    w = RWKV_WIDTH
    lanes = 2 * HEAD_DIM
    n_groups = w // lanes
    n_chunks = at.shape[0] // c
    eye = eye_ref[...]
    tril = mask_ref[...] > 0.5

    pair_diag = _pair_diag
    all_units = [(ci, g) for ci in range(n_chunks) for g in range(n_groups)]
    tile = lambda x, ci, g: x[ci * c:(ci + 1) * c, g * lanes:(g + 1) * lanes]
    g_u, wr_u, u0_u, y0_u = {}, {}, {}, {}
    for lo in range(0, len(all_units), WKV_WAVE):
        units = all_units[lo:lo + WKV_WAVE]
        vd_u = {(ci, g): pair_diag(tile(vb, ci, g)) for ci, g in units}
        for ci, g in units:
            lhs = jnp.concatenate([tile(at, ci, g), tile(rt, ci, g)], axis=0)
            rhs = jnp.concatenate([pair_diag(tile(bt, ci, g)), pair_diag(tile(kt, ci, g))], axis=0)
            g_u[ci, g] = jnp.where(tril, _dot_nt(lhs, rhs), 0.0)
        m_u = {u: g_u[u][:c, :lanes] for u in units}
        t_u = {u: eye + m_u[u] for u in units}
        mb_u = {u: m_u[u].astype(BF16) for u in units}
        mb_u = {u: _dot(mb_u[u], pair_diag(mb_u[u])).astype(BF16) for u in units}
        power = 2
        while 2 * power < c:
            tm_u = {u: _dot(jnp.concatenate([t_u[u].astype(BF16), mb_u[u]], axis=0), pair_diag(mb_u[u]))
                    for u in units}
            t_u = {u: t_u[u] + tm_u[u][:c] for u in units}
            mb_u = {u: tm_u[u][c:].astype(BF16) for u in units}
            power *= 2
        t_u = {u: (t_u[u] + _dot(t_u[u].astype(BF16), pair_diag(mb_u[u]))).astype(BF16) for u in units}
        kv_u = {u: _dot(g_u[u][:, lanes:].astype(BF16), vd_u[u]) for u in units}
        wu_u = {(ci, g): _dot(t_u[ci, g], jnp.concatenate(
            [pair_diag(tile(at, ci, g)), pair_diag(kv_u[ci, g][:c].astype(BF16))], axis=1)) for ci, g in units}
        wr_u.update({(ci, g): jnp.concatenate([wu_u[ci, g][:, :lanes].astype(BF16), tile(rt, ci, g)], axis=0)
                     for ci, g in units})
        u0_u.update({u: wu_u[u][:, lanes:] for u in units})
        y0_u.update({u: kv_u[u][c:] for u in units})

    same_head = state_mask_ref[...] > 0.5
    n_seq = state_ref.shape[0]
    chunks_per_seq = n_chunks // n_seq
    chains = [(sq, g) for sq in range(n_seq) for g in range(n_groups)]
    s_g = {(sq, g): state_ref[sq, g] for sq, g in chains}
    y_u = {}
    for step in range(chunks_per_seq):
        ci_of = lambda sq: sq * chunks_per_seq + step
        pc = [pc_ref[sq, pl.ds(pc_row0 + step, 1), :] for sq in range(n_seq)]
        ws = {(sq, g): _dot_nt(wr_u[ci_of(sq), g], s_g[sq, g].astype(BF16)) for sq, g in chains}
        u_b = {(sq, g): (ws[sq, g][:c] + u0_u[ci_of(sq), g]).astype(BF16) for sq, g in chains}
        for sq, g in chains:
            ci = ci_of(sq)
            y_u[ci, g] = (ws[sq, g][c:]
                          + _dot(g_u[ci, g][c:, :lanes].astype(BF16), pair_diag(u_b[sq, g])) + y0_u[ci, g])
        s_g = {(sq, g): s_g[sq, g] * pc[sq][:, g * lanes:(g + 1) * lanes]
               + jnp.where(same_head,
                           _dot_tn(jnp.concatenate([u_b[sq, g], tile(vb, ci_of(sq), g)], axis=0),
                                   jnp.concatenate([tile(bc, ci_of(sq), g), tile(kc, ci_of(sq), g)], axis=0)),
                           0.0)
               for sq, g in chains}
    for sq, g in chains:
        state_ref[sq, g] = s_g[sq, g]
    y_all = jnp.concatenate([jnp.concatenate([y_u[ci, g] for g in range(n_groups)], axis=1)
                             for ci in range(n_chunks)], axis=0)
    return (_head_norm(y_all, GN_EPS_RWKV) * gnw_ref[...] + gnb_ref[...] + bonus) * gate


N_RET_BF16 = 5
N_BF16_STREAMS = 12
N_F32_STREAMS = 3


def _prep_kernel(x_ref, pos_ref, g_ref, w_in_ref, freq_ref, xi_ref, zeta_ref,
                 mu_rkv_ref, mu_lora_ref, w0_ref, wup_ref, a0_ref, aup_ref, gup_ref, kk_ref, ka_ref, rk_ref,
                 tri_ref, *refs, tiles_per_seq):
    bf16_ref, f32_ref, pc_ref, w_ref, carry_rkv_ref, carry_lora_ref = refs
    stream = lambda ref, i: ref.at[:, pl.ds(i * RET_WIDTH, RET_WIDTH)]
    (q_ref, k_ref, qin_ref, kout_ref, vr_ref,
     at_ref, rt_ref, bt_ref, kt_ref, bc_ref, kc_ref, vw_ref) = [stream(bf16_ref, i) for i in range(N_BF16_STREAMS)]
    gsilu_ref, bonus_ref, gate_ref = [stream(f32_ref, i) for i in range(N_F32_STREAMS)]
    tm = x_ref.shape[0]
    w = RWKV_WIDTH
    rkv0 = RET_PROJ
    @pl.when(pl.program_id(0) == 0)
    def _():
        n_out = w_in_ref.shape[0]
        for lo in range(0, n_out, 128):
            hi = min(lo + 128, n_out)
            w_ref[:, lo:hi] = w_in_ref[lo:hi, :].T.astype(BF16)

    @pl.when(pl.program_id(0) % tiles_per_seq == 0)
    def _():
        carry_rkv_ref[...] = jnp.zeros_like(carry_rkv_ref)
        carry_lora_ref[...] = jnp.zeros_like(carry_lora_ref)

    h = _rms(x_ref[...], g_ref[...]).astype(BF16)

    wr = RET_WIDTH
    lanes = 2 * HEAD_DIM
    half = HEAD_DIM // 2
    p_q = _dot(h, w_ref[:, :wr])
    p_k = _dot(h, w_ref[:, wr:2 * wr])
    p_r = _dot(h, w_ref[:, rkv0:rkv0 + w])
    plora_raw = _dot(h, w_ref[:, rkv0 + 3 * w:])
    p_kw = _dot(h, w_ref[:, rkv0 + w:rkv0 + 2 * w])
    p_vw = _dot(h, w_ref[:, rkv0 + 2 * w:rkv0 + 3 * w])

    n_pack = lanes // half
    ang = pos_ref[...].astype(F32) * freq_ref[...]
    lane_p = lax.broadcasted_iota(jnp.int32, ang.shape, 1)

    def unpack(t):
        blocks = []
        for j in range(n_pack):
            g = pltpu.roll(t, (lanes - j * half) % lanes, 1) if j else t
            spread = g
            for m in range(1, n_pack):
                spread = jnp.where(lane_p < m * half, spread, pltpu.roll(g, m * half, 1))
            blocks.append(spread)
        return jnp.concatenate(blocks, axis=0)

    first_half = (lax.broadcasted_iota(jnp.int32, (tm, lanes), 1) % HEAD_DIM) < half
    cos = unpack(jnp.cos(ang))
    sin = unpack(jnp.sin(ang))
    sin = jnp.where(first_half, -sin, sin)
    k_scale = HEAD_DIM ** -0.5

    def rope(t, cos_t, sin_t):
        tiles = []
        for lo in range(0, wr, lanes):
            x = t[:, lo:lo + lanes]
            partner = jnp.where(first_half, pltpu.roll(x, lanes - half, 1), pltpu.roll(x, half, 1))
            tiles.append(x * cos_t + partner * sin_t)
        return jnp.concatenate(tiles, axis=1)

    def chunk_scaled(dst_ref, t, table_ref):
        for lo in range(0, tm, RET_CHUNK):
            dst_ref[lo:lo + RET_CHUNK, :] = (t[lo:lo + RET_CHUNK] * table_ref[...]).astype(BF16)

    q = rope(p_q, cos, sin)
    q_ref[...] = q.astype(BF16)
    chunk_scaled(qin_ref, q, xi_ref)

    kr = rope(p_k, cos * k_scale, sin * k_scale)
    k_ref[...] = kr.astype(BF16)
    chunk_scaled(kout_ref, kr, zeta_ref)

    r = _shift(p_r, carry_rkv_ref, mu_rkv_ref, 0)
    plora = _shift(plora_raw, carry_lora_ref, mu_lora_ref, 0)
    w_lr = plora[:, :DECAY_LORA]
    a_lr = plora[:, DECAY_LORA:DECAY_LORA + AAA_LORA]
    g_lr = plora[:, DECAY_LORA + AAA_LORA:]
    lw = -math.exp(-0.5) * jax.nn.sigmoid(
        w0_ref[...] + _dot(jnp.tanh(w_lr).astype(BF16), wup_ref[...].astype(BF16)))
    a_sig = jax.nn.sigmoid(a0_ref[...] + _dot(a_lr.astype(BF16), aup_ref[...].astype(BF16)))
    gate_ref[...] = _dot(jax.nn.sigmoid(g_lr).astype(BF16), gup_ref[...].astype(BF16))

    k = _shift(p_kw, carry_rkv_ref, mu_rkv_ref, w)
    kk = k * kk_ref[...]
    k2 = k * (1.0 + (a_sig - 1.0) * ka_ref[...])
    kk = kk * (1.0 / jnp.maximum(jnp.sqrt(_head_sums(kk * kk)), 1e-12))
    a_vec = -kk
    b_vec = kk * a_sig

    v = _shift(p_vw, carry_rkv_ref, mu_rkv_ref, 2 * w)
    bonus_ref[...] = _head_sums(r * k2 * rk_ref[...]) * v
    vw_ref[...] = v.astype(BF16)

    p_v = _dot(h, w_ref[:, 2 * wr:3 * wr])
    p_g = _dot(h, w_ref[:, 3 * wr:RET_PROJ])
    c = WKV_CHUNK
    for ci in range(tm // c):
        rows = slice(ci * c, (ci + 1) * c)
        lw_c = lw[rows]
        cum = jnp.dot(tri_ref[...], lw_c, precision=lax.Precision.HIGHEST, preferred_element_type=F32)
        cum_end = cum[c - 1:, :]
        p_inv = jnp.exp(-cum)
        p_end = jnp.exp(cum_end)
        p_out = p_inv * p_end
        at_ref[rows, :] = (a_vec[rows] * jnp.exp(cum - lw_c)).astype(BF16)
        rt_ref[rows, :] = (r[rows] * jnp.exp(cum)).astype(BF16)
        bt_ref[rows, :] = (b_vec[rows] * p_inv).astype(BF16)
        kt_ref[rows, :] = (k2[rows] * p_inv).astype(BF16)
        bc_ref[rows, :] = (b_vec[rows] * p_out).astype(BF16)
        kc_ref[rows, :] = (k2[rows] * p_out).astype(BF16)
        pc_ref[ci:ci + 1, :] = p_end

    vr_ref[...] = p_v.astype(BF16)
    gsilu_ref[...] = p_g * jax.nn.sigmoid(p_g)


ROPE_PACK = 2 * HEAD_DIM // (HEAD_DIM // 2)


def _packed_positions(positions, tm):
    lanes_per_group = 128 // ROPE_PACK
    tiles = positions.reshape(-1, ROPE_PACK, tm // ROPE_PACK)
    return jnp.repeat(jnp.swapaxes(tiles, 1, 2), lanes_per_group, axis=2).reshape(-1, 128)


def _prep(x2d, pos2d, seq_len, consts, tm=PREP_ROWS):
    t = x2d.shape[0]
    assert t % tm == 0 and seq_len % tm == 0 and tm % RET_CHUNK == 0 and tm % WKV_CHUNK == 0
    row = lambda i: (i, 0)
    width = RET_WIDTH
    n_pc = tm // WKV_CHUNK
    widths = [N_BF16_STREAMS * width, N_F32_STREAMS * width]
    return pl.pallas_call(
        functools.partial(_prep_kernel, tiles_per_seq=seq_len // tm),
        grid=(t // tm,),
        in_specs=[pl.BlockSpec((tm, D_MODEL), row), pl.BlockSpec((tm // ROPE_PACK, 128), row)]
                 + [_full(a.shape) for a in consts],
        out_specs=[pl.BlockSpec((tm, n), row) for n in widths] + [pl.BlockSpec((n_pc, width), row)],
        out_shape=[jax.ShapeDtypeStruct((t, widths[0]), BF16), jax.ShapeDtypeStruct((t, widths[1]), F32),
                   jax.ShapeDtypeStruct((t // WKV_CHUNK, width), F32)],
        scratch_shapes=[pltpu.VMEM(consts[1].shape[::-1], BF16), pltpu.VMEM((1, 3 * RWKV_WIDTH), F32),
                        pltpu.VMEM((1, LORA_WIDTH), F32)],
        compiler_params=_params(1),
        name="prep",
    )(x2d, pos2d, *consts)


def _mixers_kernel(*refs):
    bf16_ref, f32_ref, refs = refs[0], refs[1], refs[2:]
    (pc_ref, dmask_ref, gc_ref, same_head_ref, ret_gnw_ref, ret_gnb_ref, gnw_ref, gnb_ref, mask_ref, eye_ref,
     o_ref, ret_state_ref, wkv_state_ref) = refs
    ti = pl.program_id(1)

    @pl.when(ti == 0)
    def _():
        ret_state_ref[...] = jnp.zeros_like(ret_state_ref)
        wkv_state_ref[...] = jnp.zeros_like(wkv_state_ref)

    rows = bf16_ref.shape[0] * bf16_ref.shape[1]
    stream = lambda ref, i: ref[:, :, i * RET_WIDTH:(i + 1) * RET_WIDTH].reshape(rows, RET_WIDTH)
    ret_in = [stream(bf16_ref, i) for i in range(N_RET_BF16)] + [stream(f32_ref, 0)]
    wkv_in = ([stream(bf16_ref, i) for i in range(N_RET_BF16, N_BF16_STREAMS)]
              + [stream(f32_ref, i) for i in range(1, N_F32_STREAMS)])
    y_ret = _retention_body(*ret_in, dmask_ref, gc_ref, same_head_ref, ret_gnw_ref,
                            ret_gnb_ref, ret_state_ref)
    chunks_per_block = o_ref.shape[1] // WKV_CHUNK
    y_rwkv = _wkv_body(*wkv_in, pc_ref, ti * chunks_per_block, gnw_ref, gnb_ref, mask_ref,
                       eye_ref, same_head_ref, wkv_state_ref)
    o_ref[...] = jnp.concatenate([y_ret, y_rwkv], axis=1).astype(o_ref.dtype).reshape(o_ref.shape)


def _mixers(bf16_streams, f32_streams, pc, dmask_pairs, gc_w, ret_gn_w, ret_gn_b, gn_w, gn_b):
    b, s, _ = bf16_streams.shape
    width = RET_WIDTH
    n_seq, tb, c = MIX_SEQS, MIX_BLOCK, WKV_CHUNK
    assert b % n_seq == 0 and s % tb == 0 and tb % RET_CHUNK == 0 and tb % c == 0
    idx = jnp.arange(c)
    incl = (idx[:, None] >= idx[None, :]).astype(F32)
    strict = (idx[:, None] > idx[None, :]).astype(F32)
    mask = jnp.concatenate([jnp.tile(strict, (1, 4)), jnp.tile(incl, (1, 4))], axis=0)
    eye = jnp.tile(jnp.eye(c, dtype=F32), (1, 2))
    head_of_lane = jnp.arange(2 * HEAD_DIM) // HEAD_DIM
    state_mask = (head_of_lane[:, None] == head_of_lane[None, :]).astype(F32)
    consts = [dmask_pairs, gc_w, state_mask, ret_gn_w, ret_gn_b, gn_w, gn_b, mask, eye]
    blk = lambda bi, ti: (bi, ti, 0)
    state = pltpu.VMEM((n_seq, N_HEADS // 2, 2 * HEAD_DIM, 2 * HEAD_DIM), F32)
    return pl.pallas_call(
        _mixers_kernel,
        grid=(b // n_seq, s // tb),
        in_specs=[pl.BlockSpec((n_seq, tb, a.shape[2]), blk) for a in (bf16_streams, f32_streams)]
                 + [pl.BlockSpec((n_seq, s // c, width), lambda bi, ti: (bi, 0, 0))]
                 + [_full(a.shape) for a in consts],
        out_specs=pl.BlockSpec((n_seq, tb, RET_WIDTH + RWKV_WIDTH), blk),
        out_shape=jax.ShapeDtypeStruct((b, s, RET_WIDTH + RWKV_WIDTH), BF16),
        scratch_shapes=[state, state],
        compiler_params=_params(2),
        name="mixers",
    )(bf16_streams, f32_streams, pc, *consts)


def _mem_kv_kernel(m_ref, g_ref, w_f32_ref, o_ref, w_ref):
    _cast_weights_once(pl.program_id(0) == 0, [(w_f32_ref, w_ref)])
    h = _rms(m_ref[...], g_ref[...]).astype(BF16)
    o_ref[...] = _dot(h, w_ref[...]).astype(o_ref.dtype)


def _mem_kv(mem2d, g, w_kv, tm=1024):
    t = mem2d.shape[0]
    n = w_kv.shape[1]
    tm = min(tm, t)
    assert t % tm == 0
    return pl.pallas_call(
        _mem_kv_kernel,
        grid=(t // tm,),
        in_specs=[pl.BlockSpec((tm, D_MODEL), lambda i: (i, 0)), _full(g.shape), _full(w_kv.shape)],
        out_specs=pl.BlockSpec((tm, n), lambda i: (i, 0)),
        out_shape=jax.ShapeDtypeStruct((t, n), BF16),
        scratch_shapes=[pltpu.VMEM(w_kv.shape, BF16)],
        compiler_params=_params(1),
        name="mem_kv",
    )(mem2d, g, w_kv)


def _post_mix_kernel(x_ref, y_ref, wout_f32_ref, gx_ref, wq_f32_ref, kv_ref, wo_f32_ref, wup_f32_ref, wdown_f32_ref,
                     o_ref, wup_ref, wdown_ref, wout_ref, wq_ref, wo_ref):
    first = jnp.logical_and(pl.program_id(0) == 0, pl.program_id(1) == 0)
    _cast_weights_once(first, [(wout_f32_ref, wout_ref), (wq_f32_ref, wq_ref), (wo_f32_ref, wo_ref)])
    x1 = x_ref[0] + _dot(y_ref[0], wout_ref[...])
    wup_ref[...] = wup_f32_ref[...].astype(BF16)
    wdown_ref[...] = wdown_f32_ref[...].astype(BF16)
    q = _dot(_rms(x1, gx_ref[...]).astype(BF16), wq_ref[...]).astype(BF16)
    kv = kv_ref[0]
    head = lambda h: slice(h * XATTN_HEAD_DIM, (h + 1) * XATTN_HEAD_DIM)
    scores = [_dot_nt(q[:, head(h)], kv[:, head(h)]) * (XATTN_HEAD_DIM ** -0.5) for h in range(XATTN_HEADS)]
    heads = []
    for h in range(XATTN_HEADS):
        s = scores[h]
        e = jnp.exp(s - jnp.max(s, axis=-1, keepdims=True))
        prob = e / jnp.sum(e, axis=-1, keepdims=True)
        heads.append(_dot(prob.astype(BF16), kv[:, D_MODEL + h * XATTN_HEAD_DIM:D_MODEL + (h + 1) * XATTN_HEAD_DIM]))
    o = jnp.concatenate(heads, axis=1).astype(BF16)
    o_ref[0] = x1 + _dot(o, wo_ref[...])


def _post_mix(x, y, w_out, g_x, w_q, kv, w_o, w_up, w_down, tm=1024):
    b, s, _ = x.shape
    m = kv.shape[1]
    assert s % tm == 0
    n_steps = b * (s // tm)
    assert D_FF % (n_steps * 128) == 0
    slab = D_FF // n_steps
    blk = lambda bi, ti: (bi, ti, 0)
    up_spec = pl.BlockSpec((D_MODEL, slab), lambda bi, ti: (0, bi * (s // tm) + ti))
    down_spec = pl.BlockSpec((slab, D_MODEL), lambda bi, ti: (bi * (s // tm) + ti, 0))
    return pl.pallas_call(
        _post_mix_kernel,
        grid=(b, s // tm),
        in_specs=[pl.BlockSpec((1, tm, D_MODEL), blk), pl.BlockSpec((1, tm, y.shape[2]), blk),
                  _full(w_out.shape), _full(g_x.shape), _full(w_q.shape),
                  pl.BlockSpec((1, m, 2 * D_MODEL), lambda bi, ti: (bi, 0, 0)), _full(w_o.shape), up_spec, down_spec],
        out_specs=[pl.BlockSpec((1, tm, D_MODEL), blk), up_spec, down_spec],
        out_shape=[jax.ShapeDtypeStruct((b, s, D_MODEL), F32), jax.ShapeDtypeStruct(w_up.shape, BF16),
                   jax.ShapeDtypeStruct(w_down.shape, BF16)],
        scratch_shapes=[pltpu.VMEM(a.shape, BF16) for a in (w_out, w_q, w_o)],
        compiler_params=_params(2),
        name="post_mix",
    )(x, y, w_out, g_x, w_q, kv, w_o, w_up, w_down)


def _mlp_kernel(x_ref, g_ref, wup_ref, wdown_ref, gf_ref, o_ref, *, tf, final_norm):
    x = x_ref[...]
    h = _rms(x, g_ref[...]).astype(BF16)
    acc = x
    for j in range(D_FF // tf):
        u = jnp.maximum(_dot(h, wup_ref[:, j * tf:(j + 1) * tf]), 0.0)
        acc = acc + _dot((u * u).astype(BF16), wdown_ref[j * tf:(j + 1) * tf, :])
    o_ref[...] = _rms(acc, gf_ref[...]) if final_norm else acc


def _mlp(x2d, g, w_up, w_down, g_final, final_norm, tm=1024, tf=512):
    t = x2d.shape[0]
    assert t % tm == 0 and D_FF % tf == 0
    row = lambda i: (i, 0)
    return pl.pallas_call(
        functools.partial(_mlp_kernel, tf=tf, final_norm=final_norm),
        grid=(t // tm,),
        in_specs=[pl.BlockSpec((tm, D_MODEL), row), _full(g.shape), _full(w_up.shape), _full(w_down.shape),
                  _full(g_final.shape)],
        out_specs=pl.BlockSpec((tm, D_MODEL), row),
        out_shape=jax.ShapeDtypeStruct((t, D_MODEL), F32),
        compiler_params=_params(1),
        name="mlp",
    )(x2d, g, w_up, w_down, g_final)


def kernel(x, mem, positions, norm_mix, w_in, ret_gn_w, ret_gn_b, rwkv_mu, rwkv_w0, rwkv_w_up, rwkv_a0,
           rwkv_a_up, rwkv_g_up, rwkv_k_k, rwkv_k_a, rwkv_r_k, rwkv_gn_w, rwkv_gn_b, w_out, norm_xattn,
           norm_mem, xattn_w_q, xattn_w_kv, xattn_w_o, norm_mlp, mlp_w_up, mlp_w_down, norm_final):
    b, s, dm = x.shape
    n_layers = w_in.shape[0]
    freq, xi_w, zeta_w, dmask_pairs, gc_w = _retention_consts()
    tri = jnp.asarray(np.tri(WKV_CHUNK), F32)
    row = lambda a: a.reshape(1, -1)
    for l in range(n_layers):
        mu = rwkv_mu[l]
        prep_consts = [row(norm_mix[l]), jnp.swapaxes(w_in[l], 0, 1), freq, xi_w, zeta_w, row(mu[:3 * RWKV_WIDTH]), row(mu[3 * RWKV_WIDTH:]),
                       row(rwkv_w0[l]), rwkv_w_up[l], row(rwkv_a0[l]), rwkv_a_up[l],
                       rwkv_g_up[l], row(rwkv_k_k[l]), row(rwkv_k_a[l]), row(rwkv_r_k[l]), tri]
        sb, sf, pc = _prep(x.reshape(b * s, dm), _packed_positions(positions, PREP_ROWS), s, prep_consts)
        y = _mixers(sb.reshape(b, s, -1), sf.reshape(b, s, -1), pc.reshape(b, s // WKV_CHUNK, -1), dmask_pairs, gc_w,
                    row(ret_gn_w[l]), row(ret_gn_b[l]), row(rwkv_gn_w[l]), row(rwkv_gn_b[l]))
        kv = _mem_kv(mem.reshape(-1, dm), norm_mem[l][None, :], xattn_w_kv[l])
        x, w_up, w_down = _post_mix(x, y, w_out[l], norm_xattn[l][None, :], xattn_w_q[l], kv.reshape(b, -1, 2 * dm),
                                    xattn_w_o[l], mlp_w_up[l], mlp_w_down[l])
        x = _mlp(x.reshape(b * s, dm), norm_mlp[l][None, :], w_up, w_down, norm_final[None, :],
                 l == n_layers - 1).reshape(b, s, dm)
    return x
```

```python
import functools
import math

import jax
import jax.numpy as jnp
import numpy as np
from jax import lax
from jax.experimental import pallas as pl
from jax.experimental.pallas import tpu as pltpu

D_MODEL = 1024
HEAD_DIM = 64
RET_WIDTH = 512
RWKV_WIDTH = 512
N_HEADS = 8
RET_CHUNK = 128
ROPE_BASE = 10000.0
DECAY_LORA = 64
AAA_LORA = 64
GATE_LORA = 160
LORA_WIDTH = DECAY_LORA + AAA_LORA + GATE_LORA
RET_PROJ = 4 * RET_WIDTH
XATTN_HEADS = 4
XATTN_HEAD_DIM = D_MODEL // XATTN_HEADS
D_FF = 4 * D_MODEL
RMS_EPS = 1e-6
GN_EPS_RET = 1e-5
GN_EPS_RWKV = 64e-5

WKV_CHUNK = 64
MIX_SEQS = 4
MIX_BLOCK = 128
WKV_WAVE = 16
PREP_ROWS = 512
CAST_SLAB = 512
VMEM_LIMIT_BYTES = 56 * 1024 * 1024

BF16 = jnp.bfloat16
F32 = jnp.float32


def _dot(a, b):
    return jnp.dot(a, b, preferred_element_type=F32)


def _dot_nt(a, b):
    return lax.dot_general(a, b, (((1,), (1,)), ((), ())), preferred_element_type=F32)


def _dot_tn(a, b):
    return lax.dot_general(a, b, (((0,), (0,)), ((), ())), preferred_element_type=F32)


def _rms(x, g):
    return x * lax.rsqrt(jnp.mean(x * x, axis=-1, keepdims=True) + RMS_EPS) * g


def _params(n_axes):
    return pltpu.CompilerParams(dimension_semantics=("arbitrary",) * n_axes,
                                vmem_limit_bytes=VMEM_LIMIT_BYTES)


def _full(shape):
    zeros = (0,) * len(shape)
    return pl.BlockSpec(shape, lambda *_: zeros, pipeline_mode=pl.Buffered(1))


def _cast_weights_once(first_step, pairs):
    @pl.when(first_step)
    def _():
        for src_ref, dst_ref in pairs:
            n = src_ref.shape[1]
            for lo in range(0, n, CAST_SLAB):
                hi = min(lo + CAST_SLAB, n)
                dst_ref[:, lo:hi] = src_ref[:, lo:hi].astype(dst_ref.dtype)


def _head_sums(x):
    lanes = 2 * HEAD_DIM
    in_h0 = lax.broadcasted_iota(jnp.int32, (x.shape[0], lanes), 1) < HEAD_DIM
    tiles = []
    for lo in range(0, x.shape[1], lanes):
        t = x[:, lo:lo + lanes]
        s0 = jnp.sum(jnp.where(in_h0, t, 0.0), axis=-1, keepdims=True)
        s1 = jnp.sum(jnp.where(in_h0, 0.0, t), axis=-1, keepdims=True)
        tiles.append(jnp.where(in_h0, s0, s1))
    return jnp.concatenate(tiles, axis=1)


def _head_norm(y, eps):
    d = y - _head_sums(y) * (1.0 / HEAD_DIM)
    var = _head_sums(d * d) * (1.0 / HEAD_DIM)
    return d * lax.rsqrt(var + eps)


def _pair_diag(x):
    in_h0 = lax.broadcasted_iota(jnp.int32, x.shape, 1) < HEAD_DIM
    zero = jnp.zeros_like(x)
    return jnp.concatenate([jnp.where(in_h0, x, zero), jnp.where(in_h0, zero, x)], axis=0)


def _retention_body(q, k, q_in, k_out, vb, gsilu, dmask_ref, gc_ref, same_head_ref, gnw_ref, gnb_ref, state_ref):
    c = RET_CHUNK
    lanes = 2 * HEAD_DIM
    n_groups = RET_WIDTH // lanes
    n_chunks = q.shape[0] // c
    units = [(ci, g) for ci in range(n_chunks) for g in range(n_groups)]
    tile = lambda x, ci, g: x[ci * c:(ci + 1) * c, g * lanes:(g + 1) * lanes]
    s_u = {(ci, g): (_dot_nt(tile(q, ci, g), _pair_diag(tile(k, ci, g))) * dmask_ref[g]).astype(BF16)
           for ci, g in units}
    y_u = {(ci, g): _dot(s_u[ci, g], _pair_diag(tile(vb, ci, g))) for ci, g in units}
    kv_u = {(ci, g): _dot_tn(tile(k_out, ci, g), tile(vb, ci, g)) for ci, g in units}
    same_head = same_head_ref[...] > 0.5
    n_seq = state_ref.shape[0]
    chunks_per_seq = n_chunks // n_seq
    chains = [(sq, g) for sq in range(n_seq) for g in range(n_groups)]
    r_g = {(sq, g): state_ref[sq, g] for sq, g in chains}
    for step in range(chunks_per_seq):
        for sq, g in chains:
            ci = sq * chunks_per_seq + step
            y_u[ci, g] = y_u[ci, g] + _dot(tile(q_in, ci, g), r_g[sq, g].astype(BF16))
        r_g = {(sq, g): r_g[sq, g] * gc_ref[g] + jnp.where(same_head, kv_u[sq * chunks_per_seq + step, g], 0.0)
               for sq, g in chains}
    for sq, g in chains:
        state_ref[sq, g] = r_g[sq, g]
    y_all = jnp.concatenate([jnp.concatenate([y_u[ci, g] for g in range(n_groups)], axis=1)
                             for ci in range(n_chunks)], axis=0)
    return gsilu * (_head_norm(y_all, GN_EPS_RET) * gnw_ref[...] + gnb_ref[...])


def _retention_consts():
    c = RET_CHUNK
    half = HEAD_DIM // 2
    inv_freq = ROPE_BASE ** (-jnp.arange(half, dtype=F32) / half)
    freq = jnp.tile(inv_freq, 128 // half)[None, :]
    log_g = np.log(1.0 - 2.0 ** (-5.0 - np.arange(N_HEADS, dtype=np.float64)))
    idx = np.arange(c, dtype=np.float64)
    diff = idx[:, None] - idx[None, :]
    causal = diff >= 0
    dmask = np.where(causal[None], np.exp(log_g[:, None, None] * np.where(causal, diff, 0.0)[None]), 0.0)
    xi = np.exp(log_g[:, None] * (idx + 1.0)[None])
    zeta = np.exp(log_g[:, None] * (c - 1.0 - idx)[None])
    g_chunk = np.exp(log_g * c)
    xi_w = np.repeat(xi.T, HEAD_DIM, axis=1)
    zeta_w = np.repeat(zeta.T, HEAD_DIM, axis=1)
    dmask_pairs = np.concatenate([dmask[0::2], dmask[1::2]], axis=2)
    gc_rows = np.repeat(g_chunk, HEAD_DIM).reshape(N_HEADS // 2, 2 * HEAD_DIM, 1)
    gc_w = np.broadcast_to(gc_rows, (N_HEADS // 2, 2 * HEAD_DIM, 2 * HEAD_DIM))
    return (freq,) + tuple(jnp.asarray(a, F32) for a in (xi_w, zeta_w, dmask_pairs, gc_w))


def _shift(p, carry_ref, mu_ref, lo):
    cols = slice(lo, lo + p.shape[1])
    rows = lax.broadcasted_iota(jnp.int32, p.shape, 0)
    prev = jnp.where(rows == 0, carry_ref[:, cols], pltpu.roll(p, 1, 0))
    carry_ref[:, cols] = p[p.shape[0] - 1:, :]
    return p + (prev - p) * mu_ref[:, cols]


def _wkv_body(at, rt, bt, kt, bc, kc, vb, bonus, gate, pc_ref, pc_row0, gnw_ref, gnb_ref, mask_ref, eye_ref,
              state_mask_ref, state_ref):
    c = WKV_CHUNK
    w = RWKV_WIDTH
    lanes = 2 * HEAD_DIM
    n_groups = w // lanes
    n_chunks = at.shape[0] // c
    eye = eye_ref[...]
    tril = mask_ref[...] > 0.5

    pair_diag = _pair_diag
    all_units = [(ci, g) for ci in range(n_chunks) for g in range(n_groups)]
    tile = lambda x, ci, g: x[ci * c:(ci + 1) * c, g * lanes:(g + 1) * lanes]
    g_u, wr_u, u0_u, y0_u = {}, {}, {}, {}
    for lo in range(0, len(all_units), WKV_WAVE):
        units = all_units[lo:lo + WKV_WAVE]
        vd_u = {(ci, g): pair_diag(tile(vb, ci, g)) for ci, g in units}
        for ci, g in units:
            lhs = jnp.concatenate([tile(at, ci, g), tile(rt, ci, g)], axis=0)
            rhs = jnp.concatenate([pair_diag(tile(bt, ci, g)), pair_diag(tile(kt, ci, g))], axis=0)
            g_u[ci, g] = jnp.where(tril, _dot_nt(lhs, rhs), 0.0)
        m_u = {u: g_u[u][:c, :lanes] for u in units}
        t_u = {u: eye + m_u[u] for u in units}
        mb_u = {u: m_u[u].astype(BF16) for u in units}
        mb_u = {u: _dot(mb_u[u], pair_diag(mb_u[u])).astype(BF16) for u in units}
        power = 2
        while 2 * power < c:
            tm_u = {u: _dot(jnp.concatenate([t_u[u].astype(BF16), mb_u[u]], axis=0), pair_diag(mb_u[u]))
                    for u in units}
            t_u = {u: t_u[u] + tm_u[u][:c] for u in units}
            mb_u = {u: tm_u[u][c:].astype(BF16) for u in units}
            power *= 2
        t_u = {u: (t_u[u] + _dot(t_u[u].astype(BF16), pair_diag(mb_u[u]))).astype(BF16) for u in units}
        kv_u = {u: _dot(g_u[u][:, lanes:].astype(BF16), vd_u[u]) for u in units}
        wu_u = {(ci, g): _dot(t_u[ci, g], jnp.concatenate(
            [pair_diag(tile(at, ci, g)), pair_diag(kv_u[ci, g][:c].astype(BF16))], axis=1)) for ci, g in units}
        wr_u.update({(ci, g): jnp.concatenate([wu_u[ci, g][:, :lanes].astype(BF16), tile(rt, ci, g)], axis=0)
                     for ci, g in units})
        u0_u.update({u: wu_u[u][:, lanes:] for u in units})
        y0_u.update({u: kv_u[u][c:] for u in units})

    same_head = state_mask_ref[...] > 0.5
    n_seq = state_ref.shape[0]
    chunks_per_seq = n_chunks // n_seq
    chains = [(sq, g) for sq in range(n_seq) for g in range(n_groups)]
    s_g = {(sq, g): state_ref[sq, g] for sq, g in chains}
    y_u = {}
    for step in range(chunks_per_seq):
        ci_of = lambda sq: sq * chunks_per_seq + step
        pc = [pc_ref[sq, pl.ds(pc_row0 + step, 1), :] for sq in range(n_seq)]
        ws = {(sq, g): _dot_nt(wr_u[ci_of(sq), g], s_g[sq, g].astype(BF16)) for sq, g in chains}
        u_b = {(sq, g): (ws[sq, g][:c] + u0_u[ci_of(sq), g]).astype(BF16) for sq, g in chains}
        for sq, g in chains:
            ci = ci_of(sq)
            y_u[ci, g] = (ws[sq, g][c:]
                          + _dot(g_u[ci, g][c:, :lanes].astype(BF16), pair_diag(u_b[sq, g])) + y0_u[ci, g])
        s_g = {(sq, g): s_g[sq, g] * pc[sq][:, g * lanes:(g + 1) * lanes]
               + jnp.where(same_head,
                           _dot_tn(jnp.concatenate([u_b[sq, g], tile(vb, ci_of(sq), g)], axis=0),
                                   jnp.concatenate([tile(bc, ci_of(sq), g), tile(kc, ci_of(sq), g)], axis=0)),
                           0.0)
               for sq, g in chains}
    for sq, g in chains:
        state_ref[sq, g] = s_g[sq, g]
    y_all = jnp.concatenate([jnp.concatenate([y_u[ci, g] for g in range(n_groups)], axis=1)
                             for ci in range(n_chunks)], axis=0)
    return (_head_norm(y_all, GN_EPS_RWKV) * gnw_ref[...] + gnb_ref[...] + bonus) * gate


N_RET_BF16 = 5
N_BF16_STREAMS = 12
N_F32_STREAMS = 3


def _prep_kernel(x_ref, pos_ref, g_ref, w_in_ref, freq_ref, xi_ref, zeta_ref,
                 mu_rkv_ref, mu_lora_ref, w0_ref, wup_ref, a0_ref, aup_ref, gup_ref, kk_ref, ka_ref, rk_ref,
                 tri_ref, *refs, tiles_per_seq):
    bf16_ref, f32_ref, pc_ref, w_ref, carry_rkv_ref, carry_lora_ref = refs
    stream = lambda ref, i: ref.at[:, pl.ds(i * RET_WIDTH, RET_WIDTH)]
    (q_ref, k_ref, qin_ref, kout_ref, vr_ref,
     at_ref, rt_ref, bt_ref, kt_ref, bc_ref, kc_ref, vw_ref) = [stream(bf16_ref, i) for i in range(N_BF16_STREAMS)]
    gsilu_ref, bonus_ref, gate_ref = [stream(f32_ref, i) for i in range(N_F32_STREAMS)]
    tm = x_ref.shape[0]
    w = RWKV_WIDTH
    rkv0 = RET_PROJ
    @pl.when(pl.program_id(0) == 0)
    def _():
        n_out = w_in_ref.shape[0]
        for lo in range(0, n_out, 128):
            hi = min(lo + 128, n_out)
            w_ref[:, lo:hi] = w_in_ref[lo:hi, :].T.astype(BF16)

    @pl.when(pl.program_id(0) % tiles_per_seq == 0)
    def _():
        carry_rkv_ref[...] = jnp.zeros_like(carry_rkv_ref)
        carry_lora_ref[...] = jnp.zeros_like(carry_lora_ref)

    h = _rms(x_ref[...], g_ref[...]).astype(BF16)

    wr = RET_WIDTH
    lanes = 2 * HEAD_DIM
    half = HEAD_DIM // 2
    p_q = _dot(h, w_ref[:, :wr])
    p_k = _dot(h, w_ref[:, wr:2 * wr])
    p_r = _dot(h, w_ref[:, rkv0:rkv0 + w])
    plora_raw = _dot(h, w_ref[:, rkv0 + 3 * w:])
    p_kw = _dot(h, w_ref[:, rkv0 + w:rkv0 + 2 * w])
    p_vw = _dot(h, w_ref[:, rkv0 + 2 * w:rkv0 + 3 * w])

    n_pack = lanes // half
    ang = pos_ref[...].astype(F32) * freq_ref[...]
    lane_p = lax.broadcasted_iota(jnp.int32, ang.shape, 1)

    def unpack(t):
        blocks = []
        for j in range(n_pack):
            g = pltpu.roll(t, (lanes - j * half) % lanes, 1) if j else t
            spread = g
            for m in range(1, n_pack):
                spread = jnp.where(lane_p < m * half, spread, pltpu.roll(g, m * half, 1))
            blocks.append(spread)
        return jnp.concatenate(blocks, axis=0)

    first_half = (lax.broadcasted_iota(jnp.int32, (tm, lanes), 1) % HEAD_DIM) < half
    cos = unpack(jnp.cos(ang))
    sin = unpack(jnp.sin(ang))
    sin = jnp.where(first_half, -sin, sin)
    k_scale = HEAD_DIM ** -0.5

    def rope(t, cos_t, sin_t):
        tiles = []
        for lo in range(0, wr, lanes):
            x = t[:, lo:lo + lanes]
            partner = jnp.where(first_half, pltpu.roll(x, lanes - half, 1), pltpu.roll(x, half, 1))
            tiles.append(x * cos_t + partner * sin_t)
        return jnp.concatenate(tiles, axis=1)

    def chunk_scaled(dst_ref, t, table_ref):
        for lo in range(0, tm, RET_CHUNK):
            dst_ref[lo:lo + RET_CHUNK, :] = (t[lo:lo + RET_CHUNK] * table_ref[...]).astype(BF16)

    q = rope(p_q, cos, sin)
    q_ref[...] = q.astype(BF16)
    chunk_scaled(qin_ref, q, xi_ref)

    kr = rope(p_k, cos * k_scale, sin * k_scale)
    k_ref[...] = kr.astype(BF16)
    chunk_scaled(kout_ref, kr, zeta_ref)

    r = _shift(p_r, carry_rkv_ref, mu_rkv_ref, 0)
    plora = _shift(plora_raw, carry_lora_ref, mu_lora_ref, 0)
    w_lr = plora[:, :DECAY_LORA]
    a_lr = plora[:, DECAY_LORA:DECAY_LORA + AAA_LORA]
    g_lr = plora[:, DECAY_LORA + AAA_LORA:]
    lw = -math.exp(-0.5) * jax.nn.sigmoid(
        w0_ref[...] + _dot(jnp.tanh(w_lr).astype(BF16), wup_ref[...].astype(BF16)))
    a_sig = jax.nn.sigmoid(a0_ref[...] + _dot(a_lr.astype(BF16), aup_ref[...].astype(BF16)))
    gate_ref[...] = _dot(jax.nn.sigmoid(g_lr).astype(BF16), gup_ref[...].astype(BF16))

    k = _shift(p_kw, carry_rkv_ref, mu_rkv_ref, w)
    kk = k * kk_ref[...]
    k2 = k * (1.0 + (a_sig - 1.0) * ka_ref[...])
    kk = kk * (1.0 / jnp.maximum(jnp.sqrt(_head_sums(kk * kk)), 1e-12))
    a_vec = -kk
    b_vec = kk * a_sig

    v = _shift(p_vw, carry_rkv_ref, mu_rkv_ref, 2 * w)
    bonus_ref[...] = _head_sums(r * k2 * rk_ref[...]) * v
    vw_ref[...] = v.astype(BF16)

    p_v = _dot(h, w_ref[:, 2 * wr:3 * wr])
    p_g = _dot(h, w_ref[:, 3 * wr:RET_PROJ])
    c = WKV_CHUNK
    for ci in range(tm // c):
        rows = slice(ci * c, (ci + 1) * c)
        lw_c = lw[rows]
        cum = jnp.dot(tri_ref[...], lw_c, precision=lax.Precision.HIGHEST, preferred_element_type=F32)
        cum_end = cum[c - 1:, :]
        p_inv = jnp.exp(-cum)
        p_end = jnp.exp(cum_end)
        p_out = p_inv * p_end
        at_ref[rows, :] = (a_vec[rows] * jnp.exp(cum - lw_c)).astype(BF16)
        rt_ref[rows, :] = (r[rows] * jnp.exp(cum)).astype(BF16)
        bt_ref[rows, :] = (b_vec[rows] * p_inv).astype(BF16)
        kt_ref[rows, :] = (k2[rows] * p_inv).astype(BF16)
        bc_ref[rows, :] = (b_vec[rows] * p_out).astype(BF16)
        kc_ref[rows, :] = (k2[rows] * p_out).astype(BF16)
        pc_ref[ci:ci + 1, :] = p_end

    vr_ref[...] = p_v.astype(BF16)
    gsilu_ref[...] = p_g * jax.nn.sigmoid(p_g)


ROPE_PACK = 2 * HEAD_DIM // (HEAD_DIM // 2)


def _packed_positions(positions, tm):
    lanes_per_group = 128 // ROPE_PACK
    tiles = positions.reshape(-1, ROPE_PACK, tm // ROPE_PACK)
    return jnp.repeat(jnp.swapaxes(tiles, 1, 2), lanes_per_group, axis=2).reshape(-1, 128)


def _prep(x2d, pos2d, seq_len, consts, tm=PREP_ROWS):
    t = x2d.shape[0]
    assert t % tm == 0 and seq_len % tm == 0 and tm % RET_CHUNK == 0 and tm % WKV_CHUNK == 0
    row = lambda i: (i, 0)
    width = RET_WIDTH
    n_pc = tm // WKV_CHUNK
    widths = [N_BF16_STREAMS * width, N_F32_STREAMS * width]
    return pl.pallas_call(
        functools.partial(_prep_kernel, tiles_per_seq=seq_len // tm),
        grid=(t // tm,),
        in_specs=[pl.BlockSpec((tm, D_MODEL), row), pl.BlockSpec((tm // ROPE_PACK, 128), row)]
                 + [_full(a.shape) for a in consts],
        out_specs=[pl.BlockSpec((tm, n), row) for n in widths] + [pl.BlockSpec((n_pc, width), row)],
        out_shape=[jax.ShapeDtypeStruct((t, widths[0]), BF16), jax.ShapeDtypeStruct((t, widths[1]), F32),
                   jax.ShapeDtypeStruct((t // WKV_CHUNK, width), F32)],
        scratch_shapes=[pltpu.VMEM(consts[1].shape[::-1], BF16), pltpu.VMEM((1, 3 * RWKV_WIDTH), F32),
                        pltpu.VMEM((1, LORA_WIDTH), F32)],
        compiler_params=_params(1),
        name="prep",
    )(x2d, pos2d, *consts)


def _mixers_kernel(*refs):
    bf16_ref, f32_ref, refs = refs[0], refs[1], refs[2:]
    (pc_ref, dmask_ref, gc_ref, same_head_ref, ret_gnw_ref, ret_gnb_ref, gnw_ref, gnb_ref, mask_ref, eye_ref,
     o_ref, ret_state_ref, wkv_state_ref) = refs
    ti = pl.program_id(1)

    @pl.when(ti == 0)
    def _():
        ret_state_ref[...] = jnp.zeros_like(ret_state_ref)
        wkv_state_ref[...] = jnp.zeros_like(wkv_state_ref)

    rows = bf16_ref.shape[0] * bf16_ref.shape[1]
    stream = lambda ref, i: ref[:, :, i * RET_WIDTH:(i + 1) * RET_WIDTH].reshape(rows, RET_WIDTH)
    ret_in = [stream(bf16_ref, i) for i in range(N_RET_BF16)] + [stream(f32_ref, 0)]
    wkv_in = ([stream(bf16_ref, i) for i in range(N_RET_BF16, N_BF16_STREAMS)]
              + [stream(f32_ref, i) for i in range(1, N_F32_STREAMS)])
    y_ret = _retention_body(*ret_in, dmask_ref, gc_ref, same_head_ref, ret_gnw_ref,
                            ret_gnb_ref, ret_state_ref)
    chunks_per_block = o_ref.shape[1] // WKV_CHUNK
    y_rwkv = _wkv_body(*wkv_in, pc_ref, ti * chunks_per_block, gnw_ref, gnb_ref, mask_ref,
                       eye_ref, same_head_ref, wkv_state_ref)
    o_ref[...] = jnp.concatenate([y_ret, y_rwkv], axis=1).astype(o_ref.dtype).reshape(o_ref.shape)


def _mixers(bf16_streams, f32_streams, pc, dmask_pairs, gc_w, ret_gn_w, ret_gn_b, gn_w, gn_b):
    b, s, _ = bf16_streams.shape
    width = RET_WIDTH
    n_seq, tb, c = MIX_SEQS, MIX_BLOCK, WKV_CHUNK
    assert b % n_seq == 0 and s % tb == 0 and tb % RET_CHUNK == 0 and tb % c == 0
    mask = np.concatenate([np.tile(np.tri(c, k=-1), (1, 4)), np.tile(np.tri(c), (1, 4))], axis=0)
    eye = np.tile(np.eye(c), (1, 2))
    head_of_lane = np.arange(2 * HEAD_DIM) // HEAD_DIM
    state_mask = head_of_lane[:, None] == head_of_lane[None, :]
    mask, eye, state_mask = (jnp.asarray(a, F32) for a in (mask, eye, state_mask))
    consts = [dmask_pairs, gc_w, state_mask, ret_gn_w, ret_gn_b, gn_w, gn_b, mask, eye]
    blk = lambda bi, ti: (bi, ti, 0)
    state = pltpu.VMEM((n_seq, N_HEADS // 2, 2 * HEAD_DIM, 2 * HEAD_DIM), F32)
    return pl.pallas_call(
        _mixers_kernel,
        grid=(b // n_seq, s // tb),
        in_specs=[pl.BlockSpec((n_seq, tb, a.shape[2]), blk) for a in (bf16_streams, f32_streams)]
                 + [pl.BlockSpec((n_seq, s // c, width), lambda bi, ti: (bi, 0, 0))]
                 + [_full(a.shape) for a in consts],
        out_specs=pl.BlockSpec((n_seq, tb, RET_WIDTH + RWKV_WIDTH), blk),
        out_shape=jax.ShapeDtypeStruct((b, s, RET_WIDTH + RWKV_WIDTH), BF16),
        scratch_shapes=[state, state],
        compiler_params=_params(2),
        name="mixers",
    )(bf16_streams, f32_streams, pc, *consts)


def _mem_kv_kernel(m_ref, g_ref, w_f32_ref, o_ref, w_ref):
    _cast_weights_once(pl.program_id(0) == 0, [(w_f32_ref, w_ref)])
    h = _rms(m_ref[...], g_ref[...]).astype(BF16)
    o_ref[...] = _dot(h, w_ref[...]).astype(o_ref.dtype)


def _mem_kv(mem2d, g, w_kv, tm=1024):
    t = mem2d.shape[0]
    n = w_kv.shape[1]
    tm = min(tm, t)
    assert t % tm == 0
    return pl.pallas_call(
        _mem_kv_kernel,
        grid=(t // tm,),
        in_specs=[pl.BlockSpec((tm, D_MODEL), lambda i: (i, 0)), _full(g.shape), _full(w_kv.shape)],
        out_specs=pl.BlockSpec((tm, n), lambda i: (i, 0)),
        out_shape=jax.ShapeDtypeStruct((t, n), BF16),
        scratch_shapes=[pltpu.VMEM(w_kv.shape, BF16)],
        compiler_params=_params(1),
        name="mem_kv",
    )(mem2d, g, w_kv)


def _post_mix_kernel(x_ref, y_ref, wout_f32_ref, gx_ref, wq_f32_ref, kv_ref, wo_f32_ref, wup_f32_ref, wdown_f32_ref,
                     o_ref, wup_ref, wdown_ref, wout_ref, wq_ref, wo_ref):
    first = jnp.logical_and(pl.program_id(0) == 0, pl.program_id(1) == 0)
    _cast_weights_once(first, [(wout_f32_ref, wout_ref), (wq_f32_ref, wq_ref), (wo_f32_ref, wo_ref)])
    x1 = x_ref[0] + _dot(y_ref[0], wout_ref[...])
    wup_ref[...] = wup_f32_ref[...].astype(BF16)
    wdown_ref[...] = wdown_f32_ref[...].astype(BF16)
    q = _dot(_rms(x1, gx_ref[...]).astype(BF16), wq_ref[...]).astype(BF16)
    kv = kv_ref[0]
    head = lambda h: slice(h * XATTN_HEAD_DIM, (h + 1) * XATTN_HEAD_DIM)
    scores = [_dot_nt(q[:, head(h)], kv[:, head(h)]) * (XATTN_HEAD_DIM ** -0.5) for h in range(XATTN_HEADS)]
    heads = []
    for h in range(XATTN_HEADS):
        s = scores[h]
        e = jnp.exp(s - jnp.max(s, axis=-1, keepdims=True))
        prob = e / jnp.sum(e, axis=-1, keepdims=True)
        heads.append(_dot(prob.astype(BF16), kv[:, D_MODEL + h * XATTN_HEAD_DIM:D_MODEL + (h + 1) * XATTN_HEAD_DIM]))
    o = jnp.concatenate(heads, axis=1).astype(BF16)
    o_ref[0] = x1 + _dot(o, wo_ref[...])


def _post_mix(x, y, w_out, g_x, w_q, kv, w_o, w_up, w_down, tm=1024):
    b, s, _ = x.shape
    m = kv.shape[1]
    assert s % tm == 0
    n_steps = b * (s // tm)
    assert D_FF % (n_steps * 128) == 0
    slab = D_FF // n_steps
    blk = lambda bi, ti: (bi, ti, 0)
    up_spec = pl.BlockSpec((D_MODEL, slab), lambda bi, ti: (0, bi * (s // tm) + ti))
    down_spec = pl.BlockSpec((slab, D_MODEL), lambda bi, ti: (bi * (s // tm) + ti, 0))
    return pl.pallas_call(
        _post_mix_kernel,
        grid=(b, s // tm),
        in_specs=[pl.BlockSpec((1, tm, D_MODEL), blk), pl.BlockSpec((1, tm, y.shape[2]), blk),
                  _full(w_out.shape), _full(g_x.shape), _full(w_q.shape),
                  pl.BlockSpec((1, m, 2 * D_MODEL), lambda bi, ti: (bi, 0, 0)), _full(w_o.shape), up_spec, down_spec],
        out_specs=[pl.BlockSpec((1, tm, D_MODEL), blk), up_spec, down_spec],
        out_shape=[jax.ShapeDtypeStruct((b, s, D_MODEL), F32), jax.ShapeDtypeStruct(w_up.shape, BF16),
                   jax.ShapeDtypeStruct(w_down.shape, BF16)],
        scratch_shapes=[pltpu.VMEM(a.shape, BF16) for a in (w_out, w_q, w_o)],
        compiler_params=_params(2),
        name="post_mix",
    )(x, y, w_out, g_x, w_q, kv, w_o, w_up, w_down)


def _mlp_kernel(x_ref, g_ref, wup_ref, wdown_ref, gf_ref, o_ref, *, tf, final_norm):
    x = x_ref[...]
    h = _rms(x, g_ref[...]).astype(BF16)
    acc = x
    for j in range(D_FF // tf):
        u = jnp.maximum(_dot(h, wup_ref[:, j * tf:(j + 1) * tf]), 0.0)
        acc = acc + _dot((u * u).astype(BF16), wdown_ref[j * tf:(j + 1) * tf, :])
    o_ref[...] = _rms(acc, gf_ref[...]) if final_norm else acc


def _mlp(x2d, g, w_up, w_down, g_final, final_norm, tm=1024, tf=512):
    t = x2d.shape[0]
    assert t % tm == 0 and D_FF % tf == 0
    row = lambda i: (i, 0)
    return pl.pallas_call(
        functools.partial(_mlp_kernel, tf=tf, final_norm=final_norm),
        grid=(t // tm,),
        in_specs=[pl.BlockSpec((tm, D_MODEL), row), _full(g.shape), _full(w_up.shape), _full(w_down.shape),
                  _full(g_final.shape)],
        out_specs=pl.BlockSpec((tm, D_MODEL), row),
        out_shape=jax.ShapeDtypeStruct((t, D_MODEL), F32),
        compiler_params=_params(1),
        name="mlp",
    )(x2d, g, w_up, w_down, g_final)


def kernel(x, mem, positions, norm_mix, w_in, ret_gn_w, ret_gn_b, rwkv_mu, rwkv_w0, rwkv_w_up, rwkv_a0,
           rwkv_a_up, rwkv_g_up, rwkv_k_k, rwkv_k_a, rwkv_r_k, rwkv_gn_w, rwkv_gn_b, w_out, norm_xattn,
           norm_mem, xattn_w_q, xattn_w_kv, xattn_w_o, norm_mlp, mlp_w_up, mlp_w_down, norm_final):
    b, s, dm = x.shape
    n_layers = w_in.shape[0]
    freq, xi_w, zeta_w, dmask_pairs, gc_w = _retention_consts()
    tri = jnp.asarray(np.tri(WKV_CHUNK), F32)
    row = lambda a: a.reshape(1, -1)
    for l in range(n_layers):
        mu = rwkv_mu[l]
        prep_consts = [row(norm_mix[l]), jnp.swapaxes(w_in[l], 0, 1), freq, xi_w, zeta_w, row(mu[:3 * RWKV_WIDTH]), row(mu[3 * RWKV_WIDTH:]),
                       row(rwkv_w0[l]), rwkv_w_up[l], row(rwkv_a0[l]), rwkv_a_up[l],
                       rwkv_g_up[l], row(rwkv_k_k[l]), row(rwkv_k_a[l]), row(rwkv_r_k[l]), tri]
        sb, sf, pc = _prep(x.reshape(b * s, dm), _packed_positions(positions, PREP_ROWS), s, prep_consts)
        y = _mixers(sb.reshape(b, s, -1), sf.reshape(b, s, -1), pc.reshape(b, s // WKV_CHUNK, -1), dmask_pairs, gc_w,
                    row(ret_gn_w[l]), row(ret_gn_b[l]), row(rwkv_gn_w[l]), row(rwkv_gn_b[l]))
        kv = _mem_kv(mem.reshape(-1, dm), norm_mem[l][None, :], xattn_w_kv[l])
        x, w_up, w_down = _post_mix(x, y, w_out[l], norm_xattn[l][None, :], xattn_w_q[l], kv.reshape(b, -1, 2 * dm),
                                    xattn_w_o[l], mlp_w_up[l], mlp_w_down[l])
        x = _mlp(x.reshape(b * s, dm), norm_mlp[l][None, :], w_up, w_down, norm_final[None, :],
                 l == n_layers - 1).reshape(b, s, dm)
    return x
```

```python
import functools
import math

import jax
import jax.numpy as jnp
import numpy as np
from jax import lax
from jax.experimental import pallas as pl
from jax.experimental.pallas import tpu as pltpu

D_MODEL = 1024
HEAD_DIM = 64
RET_WIDTH = 512
RWKV_WIDTH = 512
N_HEADS = 8
RET_CHUNK = 128
ROPE_BASE = 10000.0
DECAY_LORA = 64
AAA_LORA = 64
GATE_LORA = 160
LORA_WIDTH = DECAY_LORA + AAA_LORA + GATE_LORA
RET_PROJ = 4 * RET_WIDTH
XATTN_HEADS = 4
XATTN_HEAD_DIM = D_MODEL // XATTN_HEADS
D_FF = 4 * D_MODEL
RMS_EPS = 1e-6
GN_EPS_RET = 1e-5
GN_EPS_RWKV = 64e-5

WKV_CHUNK = 64
MIX_SEQS = 4
MIX_BLOCK = 128
WKV_WAVE = 16
PREP_ROWS = 512
CAST_SLAB = 512
VMEM_LIMIT_BYTES = 56 * 1024 * 1024

BF16 = jnp.bfloat16
F32 = jnp.float32


def _dot(a, b):
    return jnp.dot(a, b, preferred_element_type=F32)


def _dot_nt(a, b):
    return lax.dot_general(a, b, (((1,), (1,)), ((), ())), preferred_element_type=F32)


def _dot_tn(a, b):
    return lax.dot_general(a, b, (((0,), (0,)), ((), ())), preferred_element_type=F32)


def _rms(x, g):
    return x * lax.rsqrt(jnp.mean(x * x, axis=-1, keepdims=True) + RMS_EPS) * g


def _params(n_axes):
    return pltpu.CompilerParams(dimension_semantics=("arbitrary",) * n_axes,
                                vmem_limit_bytes=VMEM_LIMIT_BYTES)


def _full(shape):
    zeros = (0,) * len(shape)
    return pl.BlockSpec(shape, lambda *_: zeros, pipeline_mode=pl.Buffered(1))


def _cast_weights_once(first_step, pairs):
    @pl.when(first_step)
    def _():
        for src_ref, dst_ref in pairs:
            n = src_ref.shape[1]
            for lo in range(0, n, CAST_SLAB):
                hi = min(lo + CAST_SLAB, n)
                dst_ref[:, lo:hi] = src_ref[:, lo:hi].astype(dst_ref.dtype)


def _head_sums(x):
    lanes = 2 * HEAD_DIM
    in_h0 = lax.broadcasted_iota(jnp.int32, (x.shape[0], lanes), 1) < HEAD_DIM
    tiles = []
    for lo in range(0, x.shape[1], lanes):
        t = x[:, lo:lo + lanes]
        s0 = jnp.sum(jnp.where(in_h0, t, 0.0), axis=-1, keepdims=True)
        s1 = jnp.sum(jnp.where(in_h0, 0.0, t), axis=-1, keepdims=True)
        tiles.append(jnp.where(in_h0, s0, s1))
    return jnp.concatenate(tiles, axis=1)


def _head_norm(y, eps):
    d = y - _head_sums(y) * (1.0 / HEAD_DIM)
    var = _head_sums(d * d) * (1.0 / HEAD_DIM)
    return d * lax.rsqrt(var + eps)


def _pair_diag(x):
    in_h0 = lax.broadcasted_iota(jnp.int32, x.shape, 1) < HEAD_DIM
    zero = jnp.zeros_like(x)
    return jnp.concatenate([jnp.where(in_h0, x, zero), jnp.where(in_h0, zero, x)], axis=0)


def _retention_body(q, k, q_in, k_out, vb, gsilu, dmask_ref, gc_ref, same_head_ref, gnw_ref, gnb_ref, state_ref):
    c = RET_CHUNK
    lanes = 2 * HEAD_DIM
    n_groups = RET_WIDTH // lanes
    n_chunks = q.shape[0] // c
    units = [(ci, g) for ci in range(n_chunks) for g in range(n_groups)]
    tile = lambda x, ci, g: x[ci * c:(ci + 1) * c, g * lanes:(g + 1) * lanes]
    s_u = {(ci, g): (_dot_nt(tile(q, ci, g), _pair_diag(tile(k, ci, g))) * dmask_ref[g]).astype(BF16)
           for ci, g in units}
    y_u = {(ci, g): _dot(s_u[ci, g], _pair_diag(tile(vb, ci, g))) for ci, g in units}
    kv_u = {(ci, g): _dot_tn(tile(k_out, ci, g), tile(vb, ci, g)) for ci, g in units}
    same_head = same_head_ref[...] > 0.5
    n_seq = state_ref.shape[0]
    chunks_per_seq = n_chunks // n_seq
    chains = [(sq, g) for sq in range(n_seq) for g in range(n_groups)]
    r_g = {(sq, g): state_ref[sq, g] for sq, g in chains}
    for step in range(chunks_per_seq):
        for sq, g in chains:
            ci = sq * chunks_per_seq + step
            y_u[ci, g] = y_u[ci, g] + _dot(tile(q_in, ci, g), r_g[sq, g].astype(BF16))
        r_g = {(sq, g): r_g[sq, g] * gc_ref[g] + jnp.where(same_head, kv_u[sq * chunks_per_seq + step, g], 0.0)
               for sq, g in chains}
    for sq, g in chains:
        state_ref[sq, g] = r_g[sq, g]
    y_all = jnp.concatenate([jnp.concatenate([y_u[ci, g] for g in range(n_groups)], axis=1)
                             for ci in range(n_chunks)], axis=0)
    return gsilu * (_head_norm(y_all, GN_EPS_RET) * gnw_ref[...] + gnb_ref[...])


def _retention_consts():
    c = RET_CHUNK
    half = HEAD_DIM // 2
    inv_freq = ROPE_BASE ** (-jnp.arange(half, dtype=F32) / half)
    freq = jnp.tile(inv_freq, 128 // half)[None, :]
    log_g = np.log(1.0 - 2.0 ** (-5.0 - np.arange(N_HEADS, dtype=np.float64)))
    idx = np.arange(c, dtype=np.float64)
    diff = idx[:, None] - idx[None, :]
    causal = diff >= 0
    dmask = np.where(causal[None], np.exp(log_g[:, None, None] * np.where(causal, diff, 0.0)[None]), 0.0)
    xi = np.exp(log_g[:, None] * (idx + 1.0)[None])
    zeta = np.exp(log_g[:, None] * (c - 1.0 - idx)[None])
    g_chunk = np.exp(log_g * c)
    xi_w = np.repeat(xi.T, HEAD_DIM, axis=1)
    zeta_w = np.repeat(zeta.T, HEAD_DIM, axis=1)
    dmask_pairs = np.concatenate([dmask[0::2], dmask[1::2]], axis=2)
    gc_rows = np.repeat(g_chunk, HEAD_DIM).reshape(N_HEADS // 2, 2 * HEAD_DIM, 1)
    gc_w = np.broadcast_to(gc_rows, (N_HEADS // 2, 2 * HEAD_DIM, 2 * HEAD_DIM))
    return (freq,) + tuple(jnp.asarray(a, F32) for a in (xi_w, zeta_w, dmask_pairs, gc_w))


def _shift(p, carry_ref, mu_ref, lo):
    cols = slice(lo, lo + p.shape[1])
    rows = lax.broadcasted_iota(jnp.int32, p.shape, 0)
    prev = jnp.where(rows == 0, carry_ref[:, cols], pltpu.roll(p, 1, 0))
    carry_ref[:, cols] = p[p.shape[0] - 1:, :]
    return p + (prev - p) * mu_ref[:, cols]


def _wkv_body(at, rt, bt, kt, bc, kc, vb, bonus, gate, pc_ref, pc_row0, gnw_ref, gnb_ref, mask_ref, eye_ref,
              state_mask_ref, state_ref):
    c = WKV_CHUNK
    w = RWKV_WIDTH
    lanes = 2 * HEAD_DIM
    n_groups = w // lanes
    n_chunks = at.shape[0] // c
    eye = eye_ref[...]
    tril = mask_ref[...] > 0.5

    pair_diag = _pair_diag
    all_units = [(ci, g) for ci in range(n_chunks) for g in range(n_groups)]
    tile = lambda x, ci, g: x[ci * c:(ci + 1) * c, g * lanes:(g + 1) * lanes]
    g_u, wr_u, u0_u, y0_u = {}, {}, {}, {}
    for lo in range(0, len(all_units), WKV_WAVE):
        units = all_units[lo:lo + WKV_WAVE]
        vd_u = {(ci, g): pair_diag(tile(vb, ci, g)) for ci, g in units}
        for ci, g in units:
            lhs = jnp.concatenate([tile(at, ci, g), tile(rt, ci, g)], axis=0)
            rhs = jnp.concatenate([pair_diag(tile(bt, ci, g)), pair_diag(tile(kt, ci, g))], axis=0)
            g_u[ci, g] = jnp.where(tril, _dot_nt(lhs, rhs), 0.0)
        m_u = {u: g_u[u][:c, :lanes] for u in units}
        t_u = {u: eye + m_u[u] for u in units}
        mb_u = {u: m_u[u].astype(BF16) for u in units}
        mb_u = {u: _dot(mb_u[u], pair_diag(mb_u[u])).astype(BF16) for u in units}
        power = 2
        while 2 * power < c:
            tm_u = {u: _dot(jnp.concatenate([t_u[u].astype(BF16), mb_u[u]], axis=0), pair_diag(mb_u[u]))
                    for u in units}
            t_u = {u: t_u[u] + tm_u[u][:c] for u in units}
            mb_u = {u: tm_u[u][c:].astype(BF16) for u in units}
            power *= 2
        t_u = {u: (t_u[u] + _dot(t_u[u].astype(BF16), pair_diag(mb_u[u]))).astype(BF16) for u in units}
        kv_u = {u: _dot(g_u[u][:, lanes:].astype(BF16), vd_u[u]) for u in units}
        wu_u = {(ci, g): _dot(t_u[ci, g], jnp.concatenate(
            [pair_diag(tile(at, ci, g)), pair_diag(kv_u[ci, g][:c].astype(BF16))], axis=1)) for ci, g in units}
        wr_u.update({(ci, g): jnp.concatenate([wu_u[ci, g][:, :lanes].astype(BF16), tile(rt, ci, g)], axis=0)
                     for ci, g in units})
        u0_u.update({u: wu_u[u][:, lanes:] for u in units})
        y0_u.update({u: kv_u[u][c:] for u in units})

    same_head = state_mask_ref[...] > 0.5
    n_seq = state_ref.shape[0]
    chunks_per_seq = n_chunks // n_seq
    chains = [(sq, g) for sq in range(n_seq) for g in range(n_groups)]
    s_g = {(sq, g): state_ref[sq, g] for sq, g in chains}
    y_u = {}
    for step in range(chunks_per_seq):
        ci_of = lambda sq: sq * chunks_per_seq + step
        pc = [pc_ref[sq, pl.ds(pc_row0 + step, 1), :] for sq in range(n_seq)]
        ws = {(sq, g): _dot_nt(wr_u[ci_of(sq), g], s_g[sq, g].astype(BF16)) for sq, g in chains}
        u_b = {(sq, g): (ws[sq, g][:c] + u0_u[ci_of(sq), g]).astype(BF16) for sq, g in chains}
        for sq, g in chains:
            ci = ci_of(sq)
            y_u[ci, g] = (ws[sq, g][c:]
                          + _dot(g_u[ci, g][c:, :lanes].astype(BF16), pair_diag(u_b[sq, g])) + y0_u[ci, g])
        s_g = {(sq, g): s_g[sq, g] * pc[sq][:, g * lanes:(g + 1) * lanes]
               + jnp.where(same_head,
                           _dot_tn(jnp.concatenate([u_b[sq, g], tile(vb, ci_of(sq), g)], axis=0),
                                   jnp.concatenate([tile(bc, ci_of(sq), g), tile(kc, ci_of(sq), g)], axis=0)),
                           0.0)
               for sq, g in chains}
    for sq, g in chains:
        state_ref[sq, g] = s_g[sq, g]
    y_all = jnp.concatenate([jnp.concatenate([y_u[ci, g] for g in range(n_groups)], axis=1)
                             for ci in range(n_chunks)], axis=0)
    return (_head_norm(y_all, GN_EPS_RWKV) * gnw_ref[...] + gnb_ref[...] + bonus) * gate


N_RET_BF16 = 5
N_BF16_STREAMS = 12
N_F32_STREAMS = 3


def _prep_kernel(x_ref, pos_ref, g_ref, w_in_ref, freq_ref, xi_ref, zeta_ref,
                 mu_rkv_ref, mu_lora_ref, w0_ref, wup_ref, a0_ref, aup_ref, gup_ref, kk_ref, ka_ref, rk_ref,
                 tri_ref, *refs, tiles_per_seq):
    bf16_ref, f32_ref, pc_ref, w_ref, carry_rkv_ref, carry_lora_ref = refs
    stream = lambda ref, i: ref.at[:, pl.ds(i * RET_WIDTH, RET_WIDTH)]
    (q_ref, k_ref, qin_ref, kout_ref, vr_ref,
     at_ref, rt_ref, bt_ref, kt_ref, bc_ref, kc_ref, vw_ref) = [stream(bf16_ref, i) for i in range(N_BF16_STREAMS)]
    gsilu_ref, bonus_ref, gate_ref = [stream(f32_ref, i) for i in range(N_F32_STREAMS)]
    tm = x_ref.shape[0]
    w = RWKV_WIDTH
    rkv0 = RET_PROJ
    @pl.when(pl.program_id(0) == 0)
    def _():
        n_out = w_in_ref.shape[0]
        for lo in range(0, n_out, 128):
            hi = min(lo + 128, n_out)
            w_ref[:, lo:hi] = w_in_ref[lo:hi, :].T.astype(BF16)

    @pl.when(pl.program_id(0) % tiles_per_seq == 0)
    def _():
        carry_rkv_ref[...] = jnp.zeros_like(carry_rkv_ref)
        carry_lora_ref[...] = jnp.zeros_like(carry_lora_ref)

    h = _rms(x_ref[...], g_ref[...]).astype(BF16)

    wr = RET_WIDTH
    lanes = 2 * HEAD_DIM
    half = HEAD_DIM // 2
    p_q = _dot(h, w_ref[:, :wr])
    p_k = _dot(h, w_ref[:, wr:2 * wr])
    p_r = _dot(h, w_ref[:, rkv0:rkv0 + w])
    plora_raw = _dot(h, w_ref[:, rkv0 + 3 * w:])
    p_kw = _dot(h, w_ref[:, rkv0 + w:rkv0 + 2 * w])
    p_vw = _dot(h, w_ref[:, rkv0 + 2 * w:rkv0 + 3 * w])

    n_pack = lanes // half
    ang = pos_ref[...].astype(F32) * freq_ref[...]
    lane_p = lax.broadcasted_iota(jnp.int32, ang.shape, 1)

    def unpack(t):
        blocks = []
        for j in range(n_pack):
            g = pltpu.roll(t, (lanes - j * half) % lanes, 1) if j else t
            spread = g
            for m in range(1, n_pack):
                spread = jnp.where(lane_p < m * half, spread, pltpu.roll(g, m * half, 1))
            blocks.append(spread)
        return jnp.concatenate(blocks, axis=0)

    first_half = (lax.broadcasted_iota(jnp.int32, (tm, lanes), 1) % HEAD_DIM) < half
    cos = unpack(jnp.cos(ang))
    sin = unpack(jnp.sin(ang))
    sin = jnp.where(first_half, -sin, sin)
    k_scale = HEAD_DIM ** -0.5

    def rope(t, cos_t, sin_t):
        tiles = []
        for lo in range(0, wr, lanes):
            x = t[:, lo:lo + lanes]
            partner = jnp.where(first_half, pltpu.roll(x, lanes - half, 1), pltpu.roll(x, half, 1))
            tiles.append(x * cos_t + partner * sin_t)
        return jnp.concatenate(tiles, axis=1)

    def chunk_scaled(dst_ref, t, table_ref):
        for lo in range(0, tm, RET_CHUNK):
            dst_ref[lo:lo + RET_CHUNK, :] = (t[lo:lo + RET_CHUNK] * table_ref[...]).astype(BF16)

    q = rope(p_q, cos, sin)
    q_ref[...] = q.astype(BF16)
    chunk_scaled(qin_ref, q, xi_ref)

    kr = rope(p_k, cos * k_scale, sin * k_scale)
    k_ref[...] = kr.astype(BF16)
    chunk_scaled(kout_ref, kr, zeta_ref)

    r = _shift(p_r, carry_rkv_ref, mu_rkv_ref, 0)
    plora = _shift(plora_raw, carry_lora_ref, mu_lora_ref, 0)
    w_lr = plora[:, :DECAY_LORA]
    a_lr = plora[:, DECAY_LORA:DECAY_LORA + AAA_LORA]
    g_lr = plora[:, DECAY_LORA + AAA_LORA:]
    lw = -math.exp(-0.5) * jax.nn.sigmoid(
        w0_ref[...] + _dot(jnp.tanh(w_lr).astype(BF16), wup_ref[...].astype(BF16)))
    a_sig = jax.nn.sigmoid(a0_ref[...] + _dot(a_lr.astype(BF16), aup_ref[...].astype(BF16)))
    gate_ref[...] = _dot(jax.nn.sigmoid(g_lr).astype(BF16), gup_ref[...].astype(BF16))

    k = _shift(p_kw, carry_rkv_ref, mu_rkv_ref, w)
    kk = k * kk_ref[...]
    k2 = k * (1.0 + (a_sig - 1.0) * ka_ref[...])
    kk = kk * (1.0 / jnp.maximum(jnp.sqrt(_head_sums(kk * kk)), 1e-12))
    a_vec = -kk
    b_vec = kk * a_sig

    v = _shift(p_vw, carry_rkv_ref, mu_rkv_ref, 2 * w)
    bonus_ref[...] = _head_sums(r * k2 * rk_ref[...]) * v
    vw_ref[...] = v.astype(BF16)

    p_v = _dot(h, w_ref[:, 2 * wr:3 * wr])
    p_g = _dot(h, w_ref[:, 3 * wr:RET_PROJ])
    c = WKV_CHUNK
    for ci in range(tm // c):
        rows = slice(ci * c, (ci + 1) * c)
        lw_c = lw[rows]
        cum = jnp.dot(tri_ref[...], lw_c, precision=lax.Precision.HIGHEST, preferred_element_type=F32)
        cum_end = cum[c - 1:, :]
        p_inv = jnp.exp(-cum)
        p_end = jnp.exp(cum_end)
        p_out = p_inv * p_end
        at_ref[rows, :] = (a_vec[rows] * jnp.exp(cum - lw_c)).astype(BF16)
        rt_ref[rows, :] = (r[rows] * jnp.exp(cum)).astype(BF16)
        bt_ref[rows, :] = (b_vec[rows] * p_inv).astype(BF16)
        kt_ref[rows, :] = (k2[rows] * p_inv).astype(BF16)
        bc_ref[rows, :] = (b_vec[rows] * p_out).astype(BF16)
        kc_ref[rows, :] = (k2[rows] * p_out).astype(BF16)
        pc_ref[ci:ci + 1, :] = p_end

    vr_ref[...] = p_v.astype(BF16)
    gsilu_ref[...] = p_g * jax.nn.sigmoid(p_g)


ROPE_PACK = 2 * HEAD_DIM // (HEAD_DIM // 2)


def _packed_positions(positions, tm):
    lanes_per_group = 128 // ROPE_PACK
    tiles = positions.reshape(-1, ROPE_PACK, tm // ROPE_PACK)
    return jnp.repeat(jnp.swapaxes(tiles, 1, 2), lanes_per_group, axis=2).reshape(-1, 128)


def _prep(x2d, pos2d, seq_len, consts, tm=PREP_ROWS):
    t = x2d.shape[0]
    assert t % tm == 0 and seq_len % tm == 0 and tm % RET_CHUNK == 0 and tm % WKV_CHUNK == 0
    row = lambda i: (i, 0)
    width = RET_WIDTH
    n_pc = tm // WKV_CHUNK
    widths = [N_BF16_STREAMS * width, N_F32_STREAMS * width]
    return pl.pallas_call(
        functools.partial(_prep_kernel, tiles_per_seq=seq_len // tm),
        grid=(t // tm,),
        in_specs=[pl.BlockSpec((tm, D_MODEL), row), pl.BlockSpec((tm // ROPE_PACK, 128), row)]
                 + [_full(a.shape) for a in consts],
        out_specs=[pl.BlockSpec((tm, n), row) for n in widths] + [pl.BlockSpec((n_pc, width), row)],
        out_shape=[jax.ShapeDtypeStruct((t, widths[0]), BF16), jax.ShapeDtypeStruct((t, widths[1]), F32),
                   jax.ShapeDtypeStruct((t // WKV_CHUNK, width), F32)],
        scratch_shapes=[pltpu.VMEM(consts[1].shape[::-1], BF16), pltpu.VMEM((1, 3 * RWKV_WIDTH), F32),
                        pltpu.VMEM((1, LORA_WIDTH), F32)],
        compiler_params=_params(1),
        name="prep",
    )(x2d, pos2d, *consts)


def _mixers_kernel(*refs):
    bf16_ref, f32_ref, refs = refs[0], refs[1], refs[2:]
    (pc_ref, dmask_ref, gc_ref, same_head_ref, ret_gnw_ref, ret_gnb_ref, gnw_ref, gnb_ref, mask_ref, eye_ref,
     o_ref, ret_state_ref, wkv_state_ref) = refs
    ti = pl.program_id(1)

    @pl.when(ti == 0)
    def _():
        ret_state_ref[...] = jnp.zeros_like(ret_state_ref)
        wkv_state_ref[...] = jnp.zeros_like(wkv_state_ref)

    rows = bf16_ref.shape[0] * bf16_ref.shape[1]
    stream = lambda ref, i: ref[:, :, i * RET_WIDTH:(i + 1) * RET_WIDTH].reshape(rows, RET_WIDTH)
    ret_in = [stream(bf16_ref, i) for i in range(N_RET_BF16)] + [stream(f32_ref, 0)]
    wkv_in = ([stream(bf16_ref, i) for i in range(N_RET_BF16, N_BF16_STREAMS)]
              + [stream(f32_ref, i) for i in range(1, N_F32_STREAMS)])
    y_ret = _retention_body(*ret_in, dmask_ref, gc_ref, same_head_ref, ret_gnw_ref,
                            ret_gnb_ref, ret_state_ref)
    chunks_per_block = o_ref.shape[1] // WKV_CHUNK
    y_rwkv = _wkv_body(*wkv_in, pc_ref, ti * chunks_per_block, gnw_ref, gnb_ref, mask_ref,
                       eye_ref, same_head_ref, wkv_state_ref)
    o_ref[...] = jnp.concatenate([y_ret, y_rwkv], axis=1).astype(o_ref.dtype).reshape(o_ref.shape)


def _mixers(bf16_streams, f32_streams, pc, dmask_pairs, gc_w, ret_gn_w, ret_gn_b, gn_w, gn_b):
    b, s, _ = bf16_streams.shape
    width = RET_WIDTH
    n_seq, tb, c = MIX_SEQS, MIX_BLOCK, WKV_CHUNK
    assert b % n_seq == 0 and s % tb == 0 and tb % RET_CHUNK == 0 and tb % c == 0
    mask = np.concatenate([np.tile(np.tri(c, k=-1), (1, 4)), np.tile(np.tri(c), (1, 4))], axis=0)
    eye = np.tile(np.eye(c), (1, 2))
    head_of_lane = np.arange(2 * HEAD_DIM) // HEAD_DIM
    state_mask = head_of_lane[:, None] == head_of_lane[None, :]
    mask, eye, state_mask = (jnp.asarray(a, F32) for a in (mask, eye, state_mask))
    consts = [dmask_pairs, gc_w, state_mask, ret_gn_w, ret_gn_b, gn_w, gn_b, mask, eye]
    blk = lambda bi, ti: (bi, ti, 0)
    state = pltpu.VMEM((n_seq, N_HEADS // 2, 2 * HEAD_DIM, 2 * HEAD_DIM), F32)
    return pl.pallas_call(
        _mixers_kernel,
        grid=(b // n_seq, s // tb),
        in_specs=[pl.BlockSpec((n_seq, tb, a.shape[2]), blk) for a in (bf16_streams, f32_streams)]
                 + [pl.BlockSpec((n_seq, s // c, width), lambda bi, ti: (bi, 0, 0))]
                 + [_full(a.shape) for a in consts],
        out_specs=pl.BlockSpec((n_seq, tb, RET_WIDTH + RWKV_WIDTH), blk),
        out_shape=jax.ShapeDtypeStruct((b, s, RET_WIDTH + RWKV_WIDTH), BF16),
        scratch_shapes=[state, state],
        compiler_params=_params(2),
        name="mixers",
    )(bf16_streams, f32_streams, pc, *consts)


def _mem_kv_kernel(m_ref, g_ref, w_f32_ref, o_ref, w_ref):
    _cast_weights_once(pl.program_id(0) == 0, [(w_f32_ref, w_ref)])
    h = _rms(m_ref[...], g_ref[...]).astype(BF16)
    o_ref[...] = _dot(h, w_ref[...]).astype(o_ref.dtype)


def _mem_kv(mem2d, g, w_kv, tm=1024):
    t = mem2d.shape[0]
    n = w_kv.shape[1]
    tm = min(tm, t)
    assert t % tm == 0
    return pl.pallas_call(
        _mem_kv_kernel,
        grid=(t // tm,),
        in_specs=[pl.BlockSpec((tm, D_MODEL), lambda i: (i, 0)), _full(g.shape), _full(w_kv.shape)],
        out_specs=pl.BlockSpec((tm, n), lambda i: (i, 0)),
        out_shape=jax.ShapeDtypeStruct((t, n), BF16),
        scratch_shapes=[pltpu.VMEM(w_kv.shape, BF16)],
        compiler_params=_params(1),
        name="mem_kv",
    )(mem2d, g, w_kv)


def _post_mix_kernel(x_ref, y_ref, wout_f32_ref, gx_ref, wq_f32_ref, kv_ref, wo_f32_ref, wup_f32_ref, wdown_f32_ref,
                     o_ref, wup_ref, wdown_ref, wout_ref, wq_ref, wo_ref):
    first = jnp.logical_and(pl.program_id(0) == 0, pl.program_id(1) == 0)
    _cast_weights_once(first, [(wout_f32_ref, wout_ref), (wq_f32_ref, wq_ref), (wo_f32_ref, wo_ref)])
    x1 = x_ref[0] + _dot(y_ref[0], wout_ref[...])
    wup_ref[...] = wup_f32_ref[...].astype(BF16)
    wdown_ref[...] = wdown_f32_ref[...].astype(BF16)
    q = _dot(_rms(x1, gx_ref[...]).astype(BF16), wq_ref[...]).astype(BF16)
    kv = kv_ref[0]
    head = lambda h: slice(h * XATTN_HEAD_DIM, (h + 1) * XATTN_HEAD_DIM)
    scores = [_dot_nt(q[:, head(h)], kv[:, head(h)]) * (XATTN_HEAD_DIM ** -0.5) for h in range(XATTN_HEADS)]
    heads = []
    for h in range(XATTN_HEADS):
        s = scores[h]
        e = jnp.exp(s - jnp.max(s, axis=-1, keepdims=True))
        prob = e / jnp.sum(e, axis=-1, keepdims=True)
        heads.append(_dot(prob.astype(BF16), kv[:, D_MODEL + h * XATTN_HEAD_DIM:D_MODEL + (h + 1) * XATTN_HEAD_DIM]))
    o = jnp.concatenate(heads, axis=1).astype(BF16)
    o_ref[0] = x1 + _dot(o, wo_ref[...])


def _post_mix(x, y, w_out, g_x, w_q, kv, w_o, w_up, w_down, tm=1024):
    b, s, _ = x.shape
    m = kv.shape[1]
    assert s % tm == 0
    n_steps = b * (s // tm)
    assert D_FF % (n_steps * 128) == 0
    slab = D_FF // n_steps
    blk = lambda bi, ti: (bi, ti, 0)
    up_spec = pl.BlockSpec((D_MODEL, slab), lambda bi, ti: (0, bi * (s // tm) + ti))
    down_spec = pl.BlockSpec((slab, D_MODEL), lambda bi, ti: (bi * (s // tm) + ti, 0))
    return pl.pallas_call(
        _post_mix_kernel,
        grid=(b, s // tm),
        in_specs=[pl.BlockSpec((1, tm, D_MODEL), blk), pl.BlockSpec((1, tm, y.shape[2]), blk),
                  _full(w_out.shape), _full(g_x.shape), _full(w_q.shape),
                  pl.BlockSpec((1, m, 2 * D_MODEL), lambda bi, ti: (bi, 0, 0)), _full(w_o.shape), up_spec, down_spec],
        out_specs=[pl.BlockSpec((1, tm, D_MODEL), blk), up_spec, down_spec],
        out_shape=[jax.ShapeDtypeStruct((b, s, D_MODEL), F32), jax.ShapeDtypeStruct(w_up.shape, BF16),
                   jax.ShapeDtypeStruct(w_down.shape, BF16)],
        scratch_shapes=[pltpu.VMEM(a.shape, BF16) for a in (w_out, w_q, w_o)],
        compiler_params=_params(2),
        name="post_mix",
    )(x, y, w_out, g_x, w_q, kv, w_o, w_up, w_down)


def _mlp_kernel(x_ref, g_ref, wup_ref, wdown_ref, gf_ref, o_ref, *, tf, final_norm):
    x = x_ref[...]
    h = _rms(x, g_ref[...]).astype(BF16)
    acc = x
    for j in range(D_FF // tf):
        u = jnp.maximum(_dot(h, wup_ref[:, j * tf:(j + 1) * tf]), 0.0)
        acc = acc + _dot((u * u).astype(BF16), wdown_ref[j * tf:(j + 1) * tf, :])
    o_ref[...] = _rms(acc, gf_ref[...]) if final_norm else acc


def _mlp(x2d, g, w_up, w_down, g_final, final_norm, tm=1024, tf=512):
    t = x2d.shape[0]
    assert t % tm == 0 and D_FF % tf == 0
    row = lambda i: (i, 0)
    return pl.pallas_call(
        functools.partial(_mlp_kernel, tf=tf, final_norm=final_norm),
        grid=(t // tm,),
        in_specs=[pl.BlockSpec((tm, D_MODEL), row), _full(g.shape), _full(w_up.shape), _full(w_down.shape),
                  _full(g_final.shape)],
        out_specs=pl.BlockSpec((tm, D_MODEL), row),
        out_shape=jax.ShapeDtypeStruct((t, D_MODEL), F32),
        input_output_aliases={0: 0},
        compiler_params=_params(1),
        name="mlp",
    )(x2d, g, w_up, w_down, g_final)


def kernel(x, mem, positions, norm_mix, w_in, ret_gn_w, ret_gn_b, rwkv_mu, rwkv_w0, rwkv_w_up, rwkv_a0,
           rwkv_a_up, rwkv_g_up, rwkv_k_k, rwkv_k_a, rwkv_r_k, rwkv_gn_w, rwkv_gn_b, w_out, norm_xattn,
           norm_mem, xattn_w_q, xattn_w_kv, xattn_w_o, norm_mlp, mlp_w_up, mlp_w_down, norm_final):
    b, s, dm = x.shape
    n_layers = w_in.shape[0]
    freq, xi_w, zeta_w, dmask_pairs, gc_w = _retention_consts()
    tri = jnp.asarray(np.tri(WKV_CHUNK), F32)
    row = lambda a: a.reshape(1, -1)
    for l in range(n_layers):
        mu = rwkv_mu[l]
        prep_consts = [row(norm_mix[l]), jnp.swapaxes(w_in[l], 0, 1), freq, xi_w, zeta_w, row(mu[:3 * RWKV_WIDTH]), row(mu[3 * RWKV_WIDTH:]),
                       row(rwkv_w0[l]), rwkv_w_up[l], row(rwkv_a0[l]), rwkv_a_up[l],
                       rwkv_g_up[l], row(rwkv_k_k[l]), row(rwkv_k_a[l]), row(rwkv_r_k[l]), tri]
        sb, sf, pc = _prep(x.reshape(b * s, dm), _packed_positions(positions, PREP_ROWS), s, prep_consts)
        y = _mixers(sb.reshape(b, s, -1), sf.reshape(b, s, -1), pc.reshape(b, s // WKV_CHUNK, -1), dmask_pairs, gc_w,
                    row(ret_gn_w[l]), row(ret_gn_b[l]), row(rwkv_gn_w[l]), row(rwkv_gn_b[l]))
        kv = _mem_kv(mem.reshape(-1, dm), norm_mem[l][None, :], xattn_w_kv[l])
        x, w_up, w_down = _post_mix(x, y, w_out[l], norm_xattn[l][None, :], xattn_w_q[l], kv.reshape(b, -1, 2 * dm),
                                    xattn_w_o[l], mlp_w_up[l], mlp_w_down[l])
        x = _mlp(x.reshape(b * s, dm), norm_mlp[l][None, :], w_up, w_down, norm_final[None, :],
                 l == n_layers - 1).reshape(b, s, dm)
    return x
```

```python
import functools
import math

import jax
import jax.numpy as jnp
import numpy as np
from jax import lax
from jax.experimental import pallas as pl
from jax.experimental.pallas import tpu as pltpu

D_MODEL = 1024
HEAD_DIM = 64
RET_WIDTH = 512
RWKV_WIDTH = 512
N_HEADS = 8
RET_CHUNK = 128
ROPE_BASE = 10000.0
DECAY_LORA = 64
AAA_LORA = 64
GATE_LORA = 160
LORA_WIDTH = DECAY_LORA + AAA_LORA + GATE_LORA
RET_PROJ = 4 * RET_WIDTH
XATTN_HEADS = 4
XATTN_HEAD_DIM = D_MODEL // XATTN_HEADS
D_FF = 4 * D_MODEL
RMS_EPS = 1e-6
GN_EPS_RET = 1e-5
GN_EPS_RWKV = 64e-5

WKV_CHUNK = 64
MIX_SEQS = 4
MIX_BLOCK = 128
WKV_WAVE = 16
PREP_ROWS = 512
CAST_SLAB = 512
VMEM_LIMIT_BYTES = 56 * 1024 * 1024

BF16 = jnp.bfloat16
F32 = jnp.float32


def _dot(a, b):
    return jnp.dot(a, b, preferred_element_type=F32)


def _dot_nt(a, b):
    return lax.dot_general(a, b, (((1,), (1,)), ((), ())), preferred_element_type=F32)


def _dot_tn(a, b):
    return lax.dot_general(a, b, (((0,), (0,)), ((), ())), preferred_element_type=F32)


def _rms(x, g):
    return x * lax.rsqrt(jnp.mean(x * x, axis=-1, keepdims=True) + RMS_EPS) * g


def _params(n_axes):
    return pltpu.CompilerParams(dimension_semantics=("arbitrary",) * n_axes,
                                vmem_limit_bytes=VMEM_LIMIT_BYTES)


def _full(shape):
    zeros = (0,) * len(shape)
    return pl.BlockSpec(shape, lambda *_: zeros, pipeline_mode=pl.Buffered(1))


def _cast_weights_once(first_step, pairs):
    @pl.when(first_step)
    def _():
        for src_ref, dst_ref in pairs:
            n = src_ref.shape[1]
            for lo in range(0, n, CAST_SLAB):
                hi = min(lo + CAST_SLAB, n)
                dst_ref[:, lo:hi] = src_ref[:, lo:hi].astype(dst_ref.dtype)


def _head_sums(x):
    lanes = 2 * HEAD_DIM
    in_h0 = lax.broadcasted_iota(jnp.int32, (x.shape[0], lanes), 1) < HEAD_DIM
    tiles = []
    for lo in range(0, x.shape[1], lanes):
        t = x[:, lo:lo + lanes]
        s0 = jnp.sum(jnp.where(in_h0, t, 0.0), axis=-1, keepdims=True)
        s1 = jnp.sum(jnp.where(in_h0, 0.0, t), axis=-1, keepdims=True)
        tiles.append(jnp.where(in_h0, s0, s1))
    return jnp.concatenate(tiles, axis=1)


def _head_norm(y, eps):
    d = y - _head_sums(y) * (1.0 / HEAD_DIM)
    var = _head_sums(d * d) * (1.0 / HEAD_DIM)
    return d * lax.rsqrt(var + eps)


def _pair_diag(x):
    in_h0 = lax.broadcasted_iota(jnp.int32, x.shape, 1) < HEAD_DIM
    zero = jnp.zeros_like(x)
    return jnp.concatenate([jnp.where(in_h0, x, zero), jnp.where(in_h0, zero, x)], axis=0)


def _retention_body(q, k, q_in, k_out, vb, gsilu, dmask_ref, gc_ref, same_head_ref, gnw_ref, gnb_ref, state_ref):
    c = RET_CHUNK
    lanes = 2 * HEAD_DIM
    n_groups = RET_WIDTH // lanes
    n_chunks = q.shape[0] // c
    units = [(ci, g) for ci in range(n_chunks) for g in range(n_groups)]
    tile = lambda x, ci, g: x[ci * c:(ci + 1) * c, g * lanes:(g + 1) * lanes]
    s_u = {(ci, g): (_dot_nt(tile(q, ci, g), _pair_diag(tile(k, ci, g))) * dmask_ref[g]).astype(BF16)
           for ci, g in units}
    y_u = {(ci, g): _dot(s_u[ci, g], _pair_diag(tile(vb, ci, g))) for ci, g in units}
    kv_u = {(ci, g): _dot_tn(tile(k_out, ci, g), tile(vb, ci, g)) for ci, g in units}
    same_head = same_head_ref[...] > 0.5
    n_seq = state_ref.shape[0]
    chunks_per_seq = n_chunks // n_seq
    chains = [(sq, g) for sq in range(n_seq) for g in range(n_groups)]
    r_g = {(sq, g): state_ref[sq, g] for sq, g in chains}
    for step in range(chunks_per_seq):
        for sq, g in chains:
            ci = sq * chunks_per_seq + step
            y_u[ci, g] = y_u[ci, g] + _dot(tile(q_in, ci, g), r_g[sq, g].astype(BF16))
        r_g = {(sq, g): r_g[sq, g] * gc_ref[g] + jnp.where(same_head, kv_u[sq * chunks_per_seq + step, g], 0.0)
               for sq, g in chains}
    for sq, g in chains:
        state_ref[sq, g] = r_g[sq, g]
    y_all = jnp.concatenate([jnp.concatenate([y_u[ci, g] for g in range(n_groups)], axis=1)
                             for ci in range(n_chunks)], axis=0)
    return gsilu * (_head_norm(y_all, GN_EPS_RET) * gnw_ref[...] + gnb_ref[...])


def _retention_consts():
    c = RET_CHUNK
    half = HEAD_DIM // 2
    inv_freq = ROPE_BASE ** (-jnp.arange(half, dtype=F32) / half)
    freq = jnp.tile(inv_freq, 128 // half)[None, :]
    log_g = np.log(1.0 - 2.0 ** (-5.0 - np.arange(N_HEADS, dtype=np.float64)))
    idx = np.arange(c, dtype=np.float64)
    diff = idx[:, None] - idx[None, :]
    causal = diff >= 0
    dmask = np.where(causal[None], np.exp(log_g[:, None, None] * np.where(causal, diff, 0.0)[None]), 0.0)
    xi = np.exp(log_g[:, None] * (idx + 1.0)[None])
    zeta = np.exp(log_g[:, None] * (c - 1.0 - idx)[None])
    g_chunk = np.exp(log_g * c)
    xi_w = np.repeat(xi.T, HEAD_DIM, axis=1)
    zeta_w = np.repeat(zeta.T, HEAD_DIM, axis=1)
    dmask_pairs = np.concatenate([dmask[0::2], dmask[1::2]], axis=2)
    gc_rows = np.repeat(g_chunk, HEAD_DIM).reshape(N_HEADS // 2, 2 * HEAD_DIM, 1)
    gc_w = np.broadcast_to(gc_rows, (N_HEADS // 2, 2 * HEAD_DIM, 2 * HEAD_DIM))
    return (freq,) + tuple(jnp.asarray(a, F32) for a in (xi_w, zeta_w, dmask_pairs, gc_w))


def _shift(p, carry_ref, mu_ref, lo):
    cols = slice(lo, lo + p.shape[1])
    rows = lax.broadcasted_iota(jnp.int32, p.shape, 0)
    prev = jnp.where(rows == 0, carry_ref[:, cols], pltpu.roll(p, 1, 0))
    carry_ref[:, cols] = p[p.shape[0] - 1:, :]
    return p + (prev - p) * mu_ref[:, cols]


def _wkv_body(at, rt, bt, kt, bc, kc, vb, bonus, gate, pc_ref, pc_row0, gnw_ref, gnb_ref, mask_ref, eye_ref,
              state_mask_ref, state_ref):
    c = WKV_CHUNK
    w = RWKV_WIDTH
    lanes = 2 * HEAD_DIM
    n_groups = w // lanes
    n_chunks = at.shape[0] // c
    eye = eye_ref[...]
    tril = mask_ref[...] > 0.5

    pair_diag = _pair_diag
    all_units = [(ci, g) for ci in range(n_chunks) for g in range(n_groups)]
    tile = lambda x, ci, g: x[ci * c:(ci + 1) * c, g * lanes:(g + 1) * lanes]
    g_u, wr_u, u0_u, y0_u = {}, {}, {}, {}
    for lo in range(0, len(all_units), WKV_WAVE):
        units = all_units[lo:lo + WKV_WAVE]
        vd_u = {(ci, g): pair_diag(tile(vb, ci, g)) for ci, g in units}
        for ci, g in units:
            lhs = jnp.concatenate([tile(at, ci, g), tile(rt, ci, g)], axis=0)
            rhs = jnp.concatenate([pair_diag(tile(bt, ci, g)), pair_diag(tile(kt, ci, g))], axis=0)
            g_u[ci, g] = jnp.where(tril, _dot_nt(lhs, rhs), 0.0)
        m_u = {u: g_u[u][:c, :lanes] for u in units}
        t_u = {u: eye + m_u[u] for u in units}
        mb_u = {u: m_u[u].astype(BF16) for u in units}
        mb_u = {u: _dot(mb_u[u], pair_diag(mb_u[u])).astype(BF16) for u in units}
        power = 2
        while 2 * power < c:
            tm_u = {u: _dot(jnp.concatenate([t_u[u].astype(BF16), mb_u[u]], axis=0), pair_diag(mb_u[u]))
                    for u in units}
            t_u = {u: t_u[u] + tm_u[u][:c] for u in units}
            mb_u = {u: tm_u[u][c:].astype(BF16) for u in units}
            power *= 2
        t_u = {u: (t_u[u] + _dot(t_u[u].astype(BF16), pair_diag(mb_u[u]))).astype(BF16) for u in units}
        kv_u = {u: _dot(g_u[u][:, lanes:].astype(BF16), vd_u[u]) for u in units}
        wu_u = {(ci, g): _dot(t_u[ci, g], jnp.concatenate(
            [pair_diag(tile(at, ci, g)), pair_diag(kv_u[ci, g][:c].astype(BF16))], axis=1)) for ci, g in units}
        wr_u.update({(ci, g): jnp.concatenate([wu_u[ci, g][:, :lanes].astype(BF16), tile(rt, ci, g)], axis=0)
                     for ci, g in units})
        u0_u.update({u: wu_u[u][:, lanes:] for u in units})
        y0_u.update({u: kv_u[u][c:] for u in units})

    same_head = state_mask_ref[...] > 0.5
    n_seq = state_ref.shape[0]
    chunks_per_seq = n_chunks // n_seq
    chains = [(sq, g) for sq in range(n_seq) for g in range(n_groups)]
    s_g = {(sq, g): state_ref[sq, g] for sq, g in chains}
    y_u = {}
    for step in range(chunks_per_seq):
        ci_of = lambda sq: sq * chunks_per_seq + step
        pc = [pc_ref[sq, pl.ds(pc_row0 + step, 1), :] for sq in range(n_seq)]
        ws = {(sq, g): _dot_nt(wr_u[ci_of(sq), g], s_g[sq, g].astype(BF16)) for sq, g in chains}
        u_b = {(sq, g): (ws[sq, g][:c] + u0_u[ci_of(sq), g]).astype(BF16) for sq, g in chains}
        for sq, g in chains:
            ci = ci_of(sq)
            y_u[ci, g] = (ws[sq, g][c:]
                          + _dot(g_u[ci, g][c:, :lanes].astype(BF16), pair_diag(u_b[sq, g])) + y0_u[ci, g])
        s_g = {(sq, g): s_g[sq, g] * pc[sq][:, g * lanes:(g + 1) * lanes]
               + jnp.where(same_head,
                           _dot_tn(jnp.concatenate([u_b[sq, g], tile(vb, ci_of(sq), g)], axis=0),
                                   jnp.concatenate([tile(bc, ci_of(sq), g), tile(kc, ci_of(sq), g)], axis=0)),
                           0.0)
               for sq, g in chains}
    for sq, g in chains:
        state_ref[sq, g] = s_g[sq, g]
    y_all = jnp.concatenate([jnp.concatenate([y_u[ci, g] for g in range(n_groups)], axis=1)
                             for ci in range(n_chunks)], axis=0)
    return (_head_norm(y_all, GN_EPS_RWKV) * gnw_ref[...] + gnb_ref[...] + bonus) * gate


N_RET_BF16 = 5
N_BF16_STREAMS = 12
N_F32_STREAMS = 3


def _prep_kernel(x_ref, pos_ref, g_ref, w_in_ref, freq_ref, xi_ref, zeta_ref,
                 mu_rkv_ref, mu_lora_ref, w0_ref, wup_ref, a0_ref, aup_ref, gup_ref, kk_ref, ka_ref, rk_ref,
                 tri_ref, *refs, tiles_per_seq):
    bf16_ref, f32_ref, pc_ref, w_ref, carry_rkv_ref, carry_lora_ref = refs
    stream = lambda ref, i: ref.at[:, pl.ds(i * RET_WIDTH, RET_WIDTH)]
    (q_ref, k_ref, qin_ref, kout_ref, vr_ref,
     at_ref, rt_ref, bt_ref, kt_ref, bc_ref, kc_ref, vw_ref) = [stream(bf16_ref, i) for i in range(N_BF16_STREAMS)]
    gsilu_ref, bonus_ref, gate_ref = [stream(f32_ref, i) for i in range(N_F32_STREAMS)]
    tm = x_ref.shape[0]
    w = RWKV_WIDTH
    rkv0 = RET_PROJ
    @pl.when(pl.program_id(0) == 0)
    def _():
        n_out = w_in_ref.shape[0]
        for lo in range(0, n_out, 128):
            hi = min(lo + 128, n_out)
            w_ref[:, lo:hi] = w_in_ref[lo:hi, :].T.astype(BF16)

    @pl.when(pl.program_id(0) % tiles_per_seq == 0)
    def _():
        carry_rkv_ref[...] = jnp.zeros_like(carry_rkv_ref)
        carry_lora_ref[...] = jnp.zeros_like(carry_lora_ref)

    h = _rms(x_ref[...], g_ref[...]).astype(BF16)

    wr = RET_WIDTH
    lanes = 2 * HEAD_DIM
    half = HEAD_DIM // 2
    p_q = _dot(h, w_ref[:, :wr])
    p_k = _dot(h, w_ref[:, wr:2 * wr])
    p_r = _dot(h, w_ref[:, rkv0:rkv0 + w])
    plora_raw = _dot(h, w_ref[:, rkv0 + 3 * w:])
    p_kw = _dot(h, w_ref[:, rkv0 + w:rkv0 + 2 * w])
    p_vw = _dot(h, w_ref[:, rkv0 + 2 * w:rkv0 + 3 * w])

    n_pack = lanes // half
    ang = pos_ref[...].astype(F32) * freq_ref[...]
    lane_p = lax.broadcasted_iota(jnp.int32, ang.shape, 1)

    def unpack(t):
        blocks = []
        for j in range(n_pack):
            g = pltpu.roll(t, (lanes - j * half) % lanes, 1) if j else t
            spread = g
            for m in range(1, n_pack):
                spread = jnp.where(lane_p < m * half, spread, pltpu.roll(g, m * half, 1))
            blocks.append(spread)
        return jnp.concatenate(blocks, axis=0)

    first_half = (lax.broadcasted_iota(jnp.int32, (tm, lanes), 1) % HEAD_DIM) < half
    cos = unpack(jnp.cos(ang))
    sin = unpack(jnp.sin(ang))
    sin = jnp.where(first_half, -sin, sin)
    k_scale = HEAD_DIM ** -0.5

    def rope(t, cos_t, sin_t):
        tiles = []
        for lo in range(0, wr, lanes):
            x = t[:, lo:lo + lanes]
            partner = jnp.where(first_half, pltpu.roll(x, lanes - half, 1), pltpu.roll(x, half, 1))
            tiles.append(x * cos_t + partner * sin_t)
        return jnp.concatenate(tiles, axis=1)

    def chunk_scaled(dst_ref, t, table_ref):
        for lo in range(0, tm, RET_CHUNK):
            dst_ref[lo:lo + RET_CHUNK, :] = (t[lo:lo + RET_CHUNK] * table_ref[...]).astype(BF16)

    q = rope(p_q, cos, sin)
    q_ref[...] = q.astype(BF16)
    chunk_scaled(qin_ref, q, xi_ref)

    kr = rope(p_k, cos * k_scale, sin * k_scale)
    k_ref[...] = kr.astype(BF16)
    chunk_scaled(kout_ref, kr, zeta_ref)

    r = _shift(p_r, carry_rkv_ref, mu_rkv_ref, 0)
    plora = _shift(plora_raw, carry_lora_ref, mu_lora_ref, 0)
    w_lr = plora[:, :DECAY_LORA]
    a_lr = plora[:, DECAY_LORA:DECAY_LORA + AAA_LORA]
    g_lr = plora[:, DECAY_LORA + AAA_LORA:]
    lw = -math.exp(-0.5) * jax.nn.sigmoid(
        w0_ref[...] + _dot(jnp.tanh(w_lr).astype(BF16), wup_ref[...].astype(BF16)))
    a_sig = jax.nn.sigmoid(a0_ref[...] + _dot(a_lr.astype(BF16), aup_ref[...].astype(BF16)))
    gate_ref[...] = _dot(jax.nn.sigmoid(g_lr).astype(BF16), gup_ref[...].astype(BF16))

    k = _shift(p_kw, carry_rkv_ref, mu_rkv_ref, w)
    kk = k * kk_ref[...]
    k2 = k * (1.0 + (a_sig - 1.0) * ka_ref[...])
    kk = kk * (1.0 / jnp.maximum(jnp.sqrt(_head_sums(kk * kk)), 1e-12))
    a_vec = -kk
    b_vec = kk * a_sig

    v = _shift(p_vw, carry_rkv_ref, mu_rkv_ref, 2 * w)
    bonus_ref[...] = _head_sums(r * k2 * rk_ref[...]) * v
    vw_ref[...] = v.astype(BF16)

    p_v = _dot(h, w_ref[:, 2 * wr:3 * wr])
    p_g = _dot(h, w_ref[:, 3 * wr:RET_PROJ])
    c = WKV_CHUNK
    for ci in range(tm // c):
        rows = slice(ci * c, (ci + 1) * c)
        lw_c = lw[rows]
        cum = jnp.dot(tri_ref[...], lw_c, precision=lax.Precision.HIGHEST, preferred_element_type=F32)
        cum_end = cum[c - 1:, :]
        p_inv = jnp.exp(-cum)
        p_end = jnp.exp(cum_end)
        p_out = p_inv * p_end
        at_ref[rows, :] = (a_vec[rows] * jnp.exp(cum - lw_c)).astype(BF16)
        rt_ref[rows, :] = (r[rows] * jnp.exp(cum)).astype(BF16)
        bt_ref[rows, :] = (b_vec[rows] * p_inv).astype(BF16)
        kt_ref[rows, :] = (k2[rows] * p_inv).astype(BF16)
        bc_ref[rows, :] = (b_vec[rows] * p_out).astype(BF16)
        kc_ref[rows, :] = (k2[rows] * p_out).astype(BF16)
        pc_ref[ci:ci + 1, :] = p_end

    vr_ref[...] = p_v.astype(BF16)
    gsilu_ref[...] = p_g * jax.nn.sigmoid(p_g)


ROPE_PACK = 2 * HEAD_DIM // (HEAD_DIM // 2)


def _packed_positions(positions, tm):
    lanes_per_group = 128 // ROPE_PACK
    tiles = positions.reshape(-1, ROPE_PACK, tm // ROPE_PACK)
    return jnp.repeat(jnp.swapaxes(tiles, 1, 2), lanes_per_group, axis=2).reshape(-1, 128)


def _prep(x2d, pos2d, seq_len, consts, tm=PREP_ROWS):
    t = x2d.shape[0]
    assert t % tm == 0 and seq_len % tm == 0 and tm % RET_CHUNK == 0 and tm % WKV_CHUNK == 0
    row = lambda i: (i, 0)
    width = RET_WIDTH
    n_pc = tm // WKV_CHUNK
    widths = [N_BF16_STREAMS * width, N_F32_STREAMS * width]
    return pl.pallas_call(
        functools.partial(_prep_kernel, tiles_per_seq=seq_len // tm),
        grid=(t // tm,),
        in_specs=[pl.BlockSpec((tm, D_MODEL), row), pl.BlockSpec((tm // ROPE_PACK, 128), row)]
                 + [_full(a.shape) for a in consts],
        out_specs=[pl.BlockSpec((tm, n), row) for n in widths] + [pl.BlockSpec((n_pc, width), row)],
        out_shape=[jax.ShapeDtypeStruct((t, widths[0]), BF16), jax.ShapeDtypeStruct((t, widths[1]), F32),
                   jax.ShapeDtypeStruct((t // WKV_CHUNK, width), F32)],
        scratch_shapes=[pltpu.VMEM(consts[1].shape[::-1], BF16), pltpu.VMEM((1, 3 * RWKV_WIDTH), F32),
                        pltpu.VMEM((1, LORA_WIDTH), F32)],
        compiler_params=_params(1),
        name="prep",
    )(x2d, pos2d, *consts)


def _mixers_kernel(*refs):
    bf16_ref, f32_ref, refs = refs[0], refs[1], refs[2:]
    (pc_ref, dmask_ref, gc_ref, same_head_ref, ret_gnw_ref, ret_gnb_ref, gnw_ref, gnb_ref, mask_ref, eye_ref,
     o_ref, ret_state_ref, wkv_state_ref) = refs
    ti = pl.program_id(1)

    @pl.when(ti == 0)
    def _():
        ret_state_ref[...] = jnp.zeros_like(ret_state_ref)
        wkv_state_ref[...] = jnp.zeros_like(wkv_state_ref)

    rows = bf16_ref.shape[0] * bf16_ref.shape[1]
    stream = lambda ref, i: ref[:, :, i * RET_WIDTH:(i + 1) * RET_WIDTH].reshape(rows, RET_WIDTH)
    ret_in = [stream(bf16_ref, i) for i in range(N_RET_BF16)] + [stream(f32_ref, 0)]
    wkv_in = ([stream(bf16_ref, i) for i in range(N_RET_BF16, N_BF16_STREAMS)]
              + [stream(f32_ref, i) for i in range(1, N_F32_STREAMS)])
    y_ret = _retention_body(*ret_in, dmask_ref, gc_ref, same_head_ref, ret_gnw_ref,
                            ret_gnb_ref, ret_state_ref)
    chunks_per_block = o_ref.shape[1] // WKV_CHUNK
    y_rwkv = _wkv_body(*wkv_in, pc_ref, ti * chunks_per_block, gnw_ref, gnb_ref, mask_ref,
                       eye_ref, same_head_ref, wkv_state_ref)
    o_ref[...] = jnp.concatenate([y_ret, y_rwkv], axis=1).astype(o_ref.dtype).reshape(o_ref.shape)


def _mixers(bf16_streams, f32_streams, pc, dmask_pairs, gc_w, ret_gn_w, ret_gn_b, gn_w, gn_b):
    b, s, _ = bf16_streams.shape
    width = RET_WIDTH
    n_seq, tb, c = MIX_SEQS, MIX_BLOCK, WKV_CHUNK
    assert b % n_seq == 0 and s % tb == 0 and tb % RET_CHUNK == 0 and tb % c == 0
    mask = np.concatenate([np.tile(np.tri(c, k=-1), (1, 4)), np.tile(np.tri(c), (1, 4))], axis=0)
    eye = np.tile(np.eye(c), (1, 2))
    head_of_lane = np.arange(2 * HEAD_DIM) // HEAD_DIM
    state_mask = head_of_lane[:, None] == head_of_lane[None, :]
    mask, eye, state_mask = (jnp.asarray(a, F32) for a in (mask, eye, state_mask))
    consts = [dmask_pairs, gc_w, state_mask, ret_gn_w, ret_gn_b, gn_w, gn_b, mask, eye]
    blk = lambda bi, ti: (bi, ti, 0)
    state = pltpu.VMEM((n_seq, N_HEADS // 2, 2 * HEAD_DIM, 2 * HEAD_DIM), F32)
    return pl.pallas_call(
        _mixers_kernel,
        grid=(b // n_seq, s // tb),
        in_specs=[pl.BlockSpec((n_seq, tb, a.shape[2]), blk) for a in (bf16_streams, f32_streams)]
                 + [pl.BlockSpec((n_seq, s // c, width), lambda bi, ti: (bi, 0, 0))]
                 + [_full(a.shape) for a in consts],
        out_specs=pl.BlockSpec((n_seq, tb, RET_WIDTH + RWKV_WIDTH), blk),
        out_shape=jax.ShapeDtypeStruct((b, s, RET_WIDTH + RWKV_WIDTH), BF16),
        scratch_shapes=[state, state],
        compiler_params=_params(2),
        name="mixers",
    )(bf16_streams, f32_streams, pc, *consts)


def _mem_kv_kernel(m_ref, g_ref, w_f32_ref, o_ref, w_ref):
    _cast_weights_once(pl.program_id(0) == 0, [(w_f32_ref, w_ref)])
    h = _rms(m_ref[...], g_ref[...]).astype(BF16)
    o_ref[...] = _dot(h, w_ref[...]).astype(o_ref.dtype)


def _mem_kv(mem2d, g, w_kv, tm=1024):
    t = mem2d.shape[0]
    n = w_kv.shape[1]
    tm = min(tm, t)
    assert t % tm == 0
    return pl.pallas_call(
        _mem_kv_kernel,
        grid=(t // tm,),
        in_specs=[pl.BlockSpec((tm, D_MODEL), lambda i: (i, 0)), _full(g.shape), _full(w_kv.shape)],
        out_specs=pl.BlockSpec((tm, n), lambda i: (i, 0)),
        out_shape=jax.ShapeDtypeStruct((t, n), BF16),
        scratch_shapes=[pltpu.VMEM(w_kv.shape, BF16)],
        compiler_params=_params(1),
        name="mem_kv",
    )(mem2d, g, w_kv)


def _post_mix_kernel(x_ref, y_ref, wout_f32_ref, gx_ref, wq_f32_ref, kv_ref, wo_f32_ref, wup_f32_ref, wdown_f32_ref,
                     o_ref, wup_ref, wdown_ref, wout_ref, wq_ref, wo_ref):
    first = jnp.logical_and(pl.program_id(0) == 0, pl.program_id(1) == 0)
    _cast_weights_once(first, [(wout_f32_ref, wout_ref), (wq_f32_ref, wq_ref), (wo_f32_ref, wo_ref)])
    x1 = x_ref[0] + _dot(y_ref[0], wout_ref[...])
    wup_ref[...] = wup_f32_ref[...].astype(BF16)
    wdown_ref[...] = wdown_f32_ref[...].astype(BF16)
    q = _dot(_rms(x1, gx_ref[...]).astype(BF16), wq_ref[...]).astype(BF16)
    kv = kv_ref[0]
    head = lambda h: slice(h * XATTN_HEAD_DIM, (h + 1) * XATTN_HEAD_DIM)
    scores = [_dot_nt(q[:, head(h)], kv[:, head(h)]) * (XATTN_HEAD_DIM ** -0.5) for h in range(XATTN_HEADS)]
    heads = []
    for h in range(XATTN_HEADS):
        s = scores[h]
        e = jnp.exp(s - jnp.max(s, axis=-1, keepdims=True))
        prob = e / jnp.sum(e, axis=-1, keepdims=True)
        heads.append(_dot(prob.astype(BF16), kv[:, D_MODEL + h * XATTN_HEAD_DIM:D_MODEL + (h + 1) * XATTN_HEAD_DIM]))
    o = jnp.concatenate(heads, axis=1).astype(BF16)
    o_ref[0] = x1 + _dot(o, wo_ref[...])


def _post_mix(x, y, w_out, g_x, w_q, kv, w_o, w_up, w_down, tm=1024):
    b, s, _ = x.shape
    m = kv.shape[1]
    assert s % tm == 0
    n_steps = b * (s // tm)
    assert D_FF % (n_steps * 128) == 0
    slab = D_FF // n_steps
    blk = lambda bi, ti: (bi, ti, 0)
    up_spec = pl.BlockSpec((D_MODEL, slab), lambda bi, ti: (0, bi * (s // tm) + ti))
    down_spec = pl.BlockSpec((slab, D_MODEL), lambda bi, ti: (bi * (s // tm) + ti, 0))
    return pl.pallas_call(
        _post_mix_kernel,
        grid=(b, s // tm),
        in_specs=[pl.BlockSpec((1, tm, D_MODEL), blk), pl.BlockSpec((1, tm, y.shape[2]), blk),
                  _full(w_out.shape), _full(g_x.shape), _full(w_q.shape),
                  pl.BlockSpec((1, m, 2 * D_MODEL), lambda bi, ti: (bi, 0, 0)), _full(w_o.shape), up_spec, down_spec],
        out_specs=[pl.BlockSpec((1, tm, D_MODEL), blk), up_spec, down_spec],
        out_shape=[jax.ShapeDtypeStruct((b, s, D_MODEL), F32), jax.ShapeDtypeStruct(w_up.shape, BF16),
                   jax.ShapeDtypeStruct(w_down.shape, BF16)],
        scratch_shapes=[pltpu.VMEM(a.shape, BF16) for a in (w_out, w_q, w_o)],
        compiler_params=_params(2),
        name="post_mix",
    )(x, y, w_out, g_x, w_q, kv, w_o, w_up, w_down)


def _mlp_kernel(x_ref, g_ref, wup_ref, wdown_ref, gf_ref, o_ref, *, tf, final_norm):
    x = x_ref[...]
    h = _rms(x, g_ref[...]).astype(BF16)
    acc = x
    for j in range(D_FF // tf):
        u = jnp.maximum(_dot(h, wup_ref[:, j * tf:(j + 1) * tf]), 0.0)
        acc = acc + _dot((u * u).astype(BF16), wdown_ref[j * tf:(j + 1) * tf, :])
    o_ref[...] = _rms(acc, gf_ref[...]) if final_norm else acc


def _mlp(x2d, g, w_up, w_down, g_final, final_norm, tm=1024, tf=512):
    t = x2d.shape[0]
    assert t % tm == 0 and D_FF % tf == 0
    row = lambda i: (i, 0)
    return pl.pallas_call(
        functools.partial(_mlp_kernel, tf=tf, final_norm=final_norm),
        grid=(t // tm,),
        in_specs=[pl.BlockSpec((tm, D_MODEL), row), _full(g.shape), _full(w_up.shape), _full(w_down.shape),
                  _full(g_final.shape)],
        out_specs=pl.BlockSpec((tm, D_MODEL), row),
        out_shape=jax.ShapeDtypeStruct((t, D_MODEL), F32),
        compiler_params=_params(1),
        name="mlp",
    )(x2d, g, w_up, w_down, g_final)


def kernel(x, mem, positions, norm_mix, w_in, ret_gn_w, ret_gn_b, rwkv_mu, rwkv_w0, rwkv_w_up, rwkv_a0,
           rwkv_a_up, rwkv_g_up, rwkv_k_k, rwkv_k_a, rwkv_r_k, rwkv_gn_w, rwkv_gn_b, w_out, norm_xattn,
           norm_mem, xattn_w_q, xattn_w_kv, xattn_w_o, norm_mlp, mlp_w_up, mlp_w_down, norm_final):
    b, s, dm = x.shape
    n_layers = w_in.shape[0]
    freq, xi_w, zeta_w, dmask_pairs, gc_w = _retention_consts()
    tri = jnp.asarray(np.tri(WKV_CHUNK), F32)
    row = lambda a: a.reshape(1, -1)
    for l in range(n_layers):
        kv = _mem_kv(mem.reshape(-1, dm), norm_mem[l][None, :], xattn_w_kv[l])
        mu = rwkv_mu[l]
        prep_consts = [row(norm_mix[l]), jnp.swapaxes(w_in[l], 0, 1), freq, xi_w, zeta_w, row(mu[:3 * RWKV_WIDTH]), row(mu[3 * RWKV_WIDTH:]),
                       row(rwkv_w0[l]), rwkv_w_up[l], row(rwkv_a0[l]), rwkv_a_up[l],
                       rwkv_g_up[l], row(rwkv_k_k[l]), row(rwkv_k_a[l]), row(rwkv_r_k[l]), tri]
        sb, sf, pc = _prep(x.reshape(b * s, dm), _packed_positions(positions, PREP_ROWS), s, prep_consts)
        y = _mixers(sb.reshape(b, s, -1), sf.reshape(b, s, -1), pc.reshape(b, s // WKV_CHUNK, -1), dmask_pairs, gc_w,
                    row(ret_gn_w[l]), row(ret_gn_b[l]), row(rwkv_gn_w[l]), row(rwkv_gn_b[l]))
        x, w_up, w_down = _post_mix(x, y, w_out[l], norm_xattn[l][None, :], xattn_w_q[l], kv.reshape(b, -1, 2 * dm),
                                    xattn_w_o[l], mlp_w_up[l], mlp_w_down[l])
        x = _mlp(x.reshape(b * s, dm), norm_mlp[l][None, :], w_up, w_down, norm_final[None, :],
                 l == n_layers - 1).reshape(b, s, dm)
    return x
```

```python
import functools
import math

import jax
import jax.numpy as jnp
import numpy as np
from jax import lax
from jax.experimental import pallas as pl
from jax.experimental.pallas import tpu as pltpu

D_MODEL = 1024
HEAD_DIM = 64
RET_WIDTH = 512
RWKV_WIDTH = 512
N_HEADS = 8
RET_CHUNK = 128
ROPE_BASE = 10000.0
DECAY_LORA = 64
AAA_LORA = 64
GATE_LORA = 160
LORA_WIDTH = DECAY_LORA + AAA_LORA + GATE_LORA
RET_PROJ = 4 * RET_WIDTH
XATTN_HEADS = 4
XATTN_HEAD_DIM = D_MODEL // XATTN_HEADS
D_FF = 4 * D_MODEL
RMS_EPS = 1e-6
GN_EPS_RET = 1e-5
GN_EPS_RWKV = 64e-5

WKV_CHUNK = 64
MIX_SEQS = 4
MIX_BLOCK = 128
WKV_WAVE = 16
PREP_ROWS = 512
CAST_SLAB = 512
VMEM_LIMIT_BYTES = 56 * 1024 * 1024

BF16 = jnp.bfloat16
F32 = jnp.float32


def _dot(a, b):
    return jnp.dot(a, b, preferred_element_type=F32)


def _dot_nt(a, b):
    return lax.dot_general(a, b, (((1,), (1,)), ((), ())), preferred_element_type=F32)


def _dot_tn(a, b):
    return lax.dot_general(a, b, (((0,), (0,)), ((), ())), preferred_element_type=F32)


def _rms(x, g):
    return x * lax.rsqrt(jnp.mean(x * x, axis=-1, keepdims=True) + RMS_EPS) * g


def _params(n_axes):
    return pltpu.CompilerParams(dimension_semantics=("arbitrary",) * n_axes,
                                vmem_limit_bytes=VMEM_LIMIT_BYTES)


def _full(shape):
    zeros = (0,) * len(shape)
    return pl.BlockSpec(shape, lambda *_: zeros, pipeline_mode=pl.Buffered(1))


def _cast_weights_once(first_step, pairs):
    @pl.when(first_step)
    def _():
        for src_ref, dst_ref in pairs:
            n = src_ref.shape[1]
            for lo in range(0, n, CAST_SLAB):
                hi = min(lo + CAST_SLAB, n)
                dst_ref[:, lo:hi] = src_ref[:, lo:hi].astype(dst_ref.dtype)


def _head_sums(x):
    lanes = 2 * HEAD_DIM
    in_h0 = lax.broadcasted_iota(jnp.int32, (x.shape[0], lanes), 1) < HEAD_DIM
    tiles = []
    for lo in range(0, x.shape[1], lanes):
        t = x[:, lo:lo + lanes]
        s0 = jnp.sum(jnp.where(in_h0, t, 0.0), axis=-1, keepdims=True)
        s1 = jnp.sum(jnp.where(in_h0, 0.0, t), axis=-1, keepdims=True)
        tiles.append(jnp.where(in_h0, s0, s1))
    return jnp.concatenate(tiles, axis=1)


def _head_norm(y, eps):
    d = y - _head_sums(y) * (1.0 / HEAD_DIM)
    var = _head_sums(d * d) * (1.0 / HEAD_DIM)
    return d * lax.rsqrt(var + eps)


def _pair_diag(x):
    in_h0 = lax.broadcasted_iota(jnp.int32, x.shape, 1) < HEAD_DIM
    zero = jnp.zeros_like(x)
    return jnp.concatenate([jnp.where(in_h0, x, zero), jnp.where(in_h0, zero, x)], axis=0)


def _retention_body(q, k, q_in, k_out, vb, gsilu, dmask_ref, gc_ref, same_head_ref, gnw_ref, gnb_ref, state_ref):
    c = RET_CHUNK
    lanes = 2 * HEAD_DIM
    n_groups = RET_WIDTH // lanes
    n_chunks = q.shape[0] // c
    units = [(ci, g) for ci in range(n_chunks) for g in range(n_groups)]
    tile = lambda x, ci, g: x[ci * c:(ci + 1) * c, g * lanes:(g + 1) * lanes]
    s_u = {(ci, g): (_dot_nt(tile(q, ci, g), _pair_diag(tile(k, ci, g))) * dmask_ref[g]).astype(BF16)
           for ci, g in units}
    y_u = {(ci, g): _dot(s_u[ci, g], _pair_diag(tile(vb, ci, g))) for ci, g in units}
    kv_u = {(ci, g): _dot_tn(tile(k_out, ci, g), tile(vb, ci, g)) for ci, g in units}
    same_head = same_head_ref[...] > 0.5
    n_seq = state_ref.shape[0]
    chunks_per_seq = n_chunks // n_seq
    chains = [(sq, g) for sq in range(n_seq) for g in range(n_groups)]
    r_g = {(sq, g): state_ref[sq, g] for sq, g in chains}
    for step in range(chunks_per_seq):
        for sq, g in chains:
            ci = sq * chunks_per_seq + step
            y_u[ci, g] = y_u[ci, g] + _dot(tile(q_in, ci, g), r_g[sq, g].astype(BF16))
        r_g = {(sq, g): r_g[sq, g] * gc_ref[g] + jnp.where(same_head, kv_u[sq * chunks_per_seq + step, g], 0.0)
               for sq, g in chains}
    for sq, g in chains:
        state_ref[sq, g] = r_g[sq, g]
    y_all = jnp.concatenate([jnp.concatenate([y_u[ci, g] for g in range(n_groups)], axis=1)
                             for ci in range(n_chunks)], axis=0)
    return gsilu * (_head_norm(y_all, GN_EPS_RET) * gnw_ref[...] + gnb_ref[...])


def _retention_consts():
    c = RET_CHUNK
    half = HEAD_DIM // 2
    inv_freq = ROPE_BASE ** (-jnp.arange(half, dtype=F32) / half)
    freq = jnp.tile(inv_freq, 128 // half)[None, :]
    log_g = np.log(1.0 - 2.0 ** (-5.0 - np.arange(N_HEADS, dtype=np.float64)))
    idx = np.arange(c, dtype=np.float64)
    diff = idx[:, None] - idx[None, :]
    causal = diff >= 0
    dmask = np.where(causal[None], np.exp(log_g[:, None, None] * np.where(causal, diff, 0.0)[None]), 0.0)
    xi = np.exp(log_g[:, None] * (idx + 1.0)[None])
    zeta = np.exp(log_g[:, None] * (c - 1.0 - idx)[None])
    g_chunk = np.exp(log_g * c)
    xi_w = np.repeat(xi.T, HEAD_DIM, axis=1)
    zeta_w = np.repeat(zeta.T, HEAD_DIM, axis=1)
    dmask_pairs = np.concatenate([dmask[0::2], dmask[1::2]], axis=2)
    gc_rows = np.repeat(g_chunk, HEAD_DIM).reshape(N_HEADS // 2, 2 * HEAD_DIM, 1)
    gc_w = np.broadcast_to(gc_rows, (N_HEADS // 2, 2 * HEAD_DIM, 2 * HEAD_DIM))
    return (freq,) + tuple(jnp.asarray(a, F32) for a in (xi_w, zeta_w, dmask_pairs, gc_w))


def _shift(p, carry_ref, mu_ref, lo):
    cols = slice(lo, lo + p.shape[1])
    rows = lax.broadcasted_iota(jnp.int32, p.shape, 0)
    prev = jnp.where(rows == 0, carry_ref[:, cols], pltpu.roll(p, 1, 0))
    carry_ref[:, cols] = p[p.shape[0] - 1:, :]
    return p + (prev - p) * mu_ref[:, cols]


def _wkv_body(at, rt, bt, kt, bc, kc, vb, bonus, gate, pc_ref, pc_row0, gnw_ref, gnb_ref, mask_ref, eye_ref,
              state_mask_ref, state_ref):
    c = WKV_CHUNK
    w = RWKV_WIDTH
    lanes = 2 * HEAD_DIM
    n_groups = w // lanes
    n_chunks = at.shape[0] // c
    eye = eye_ref[...]
    tril = mask_ref[...] > 0.5

    pair_diag = _pair_diag
    all_units = [(ci, g) for ci in range(n_chunks) for g in range(n_groups)]
    tile = lambda x, ci, g: x[ci * c:(ci + 1) * c, g * lanes:(g + 1) * lanes]
    g_u, wr_u, u0_u, y0_u = {}, {}, {}, {}
    for lo in range(0, len(all_units), WKV_WAVE):
        units = all_units[lo:lo + WKV_WAVE]
        vd_u = {(ci, g): pair_diag(tile(vb, ci, g)) for ci, g in units}
        for ci, g in units:
            lhs = jnp.concatenate([tile(at, ci, g), tile(rt, ci, g)], axis=0)
            rhs = jnp.concatenate([pair_diag(tile(bt, ci, g)), pair_diag(tile(kt, ci, g))], axis=0)
            g_u[ci, g] = jnp.where(tril, _dot_nt(lhs, rhs), 0.0)
        m_u = {u: g_u[u][:c, :lanes] for u in units}
        t_u = {u: eye + m_u[u] for u in units}
        mb_u = {u: m_u[u].astype(BF16) for u in units}
        mb_u = {u: _dot(mb_u[u], pair_diag(mb_u[u])).astype(BF16) for u in units}
        power = 2
        while 2 * power < c:
            tm_u = {u: _dot(jnp.concatenate([t_u[u].astype(BF16), mb_u[u]], axis=0), pair_diag(mb_u[u]))
                    for u in units}
            t_u = {u: t_u[u] + tm_u[u][:c] for u in units}
            mb_u = {u: tm_u[u][c:].astype(BF16) for u in units}
            power *= 2
        t_u = {u: (t_u[u] + _dot(t_u[u].astype(BF16), pair_diag(mb_u[u]))).astype(BF16) for u in units}
        kv_u = {u: _dot(g_u[u][:, lanes:].astype(BF16), vd_u[u]) for u in units}
        wu_u = {(ci, g): _dot(t_u[ci, g], jnp.concatenate(
            [pair_diag(tile(at, ci, g)), pair_diag(kv_u[ci, g][:c].astype(BF16))], axis=1)) for ci, g in units}
        wr_u.update({(ci, g): jnp.concatenate([wu_u[ci, g][:, :lanes].astype(BF16), tile(rt, ci, g)], axis=0)
                     for ci, g in units})
        u0_u.update({u: wu_u[u][:, lanes:] for u in units})
        y0_u.update({u: kv_u[u][c:] for u in units})

    same_head = state_mask_ref[...] > 0.5
    n_seq = state_ref.shape[0]
    chunks_per_seq = n_chunks // n_seq
    chains = [(sq, g) for sq in range(n_seq) for g in range(n_groups)]
    s_g = {(sq, g): state_ref[sq, g] for sq, g in chains}
    y_u = {}
    for step in range(chunks_per_seq):
        ci_of = lambda sq: sq * chunks_per_seq + step
        pc = [pc_ref[sq, pl.ds(pc_row0 + step, 1), :] for sq in range(n_seq)]
        ws = {(sq, g): _dot_nt(wr_u[ci_of(sq), g], s_g[sq, g].astype(BF16)) for sq, g in chains}
        u_b = {(sq, g): (ws[sq, g][:c] + u0_u[ci_of(sq), g]).astype(BF16) for sq, g in chains}
        for sq, g in chains:
            ci = ci_of(sq)
            y_u[ci, g] = (ws[sq, g][c:]
                          + _dot(g_u[ci, g][c:, :lanes].astype(BF16), pair_diag(u_b[sq, g])) + y0_u[ci, g])
        s_g = {(sq, g): s_g[sq, g] * pc[sq][:, g * lanes:(g + 1) * lanes]
               + jnp.where(same_head,
                           _dot_tn(jnp.concatenate([u_b[sq, g], tile(vb, ci_of(sq), g)], axis=0),
                                   jnp.concatenate([tile(bc, ci_of(sq), g), tile(kc, ci_of(sq), g)], axis=0)),
                           0.0)
               for sq, g in chains}
    for sq, g in chains:
        state_ref[sq, g] = s_g[sq, g]
    y_all = jnp.concatenate([jnp.concatenate([y_u[ci, g] for g in range(n_groups)], axis=1)
                             for ci in range(n_chunks)], axis=0)
    return (_head_norm(y_all, GN_EPS_RWKV) * gnw_ref[...] + gnb_ref[...] + bonus) * gate


N_RET_BF16 = 5
N_BF16_STREAMS = 12
N_F32_STREAMS = 3


def _prep_kernel(x_ref, pos_ref, g_ref, w_in_ref, freq_ref, xi_ref, zeta_ref,
                 mu_rkv_ref, mu_lora_ref, w0_ref, wup_ref, a0_ref, aup_ref, gup_ref, kk_ref, ka_ref, rk_ref,
                 tri_ref, *refs, tiles_per_seq):
    bf16_ref, f32_ref, pc_ref, w_ref, carry_rkv_ref, carry_lora_ref = refs
    stream = lambda ref, i: ref.at[:, pl.ds(i * RET_WIDTH, RET_WIDTH)]
    (q_ref, k_ref, qin_ref, kout_ref, vr_ref,
     at_ref, rt_ref, bt_ref, kt_ref, bc_ref, kc_ref, vw_ref) = [stream(bf16_ref, i) for i in range(N_BF16_STREAMS)]
    gsilu_ref, bonus_ref, gate_ref = [stream(f32_ref, i) for i in range(N_F32_STREAMS)]
    tm = x_ref.shape[0]
    w = RWKV_WIDTH
    rkv0 = RET_PROJ
    @pl.when(pl.program_id(0) == 0)
    def _():
        n_out = w_in_ref.shape[0]
        for lo in range(0, n_out, 128):
            hi = min(lo + 128, n_out)
            w_ref[:, lo:hi] = w_in_ref[lo:hi, :].T.astype(BF16)

    @pl.when(pl.program_id(0) % tiles_per_seq == 0)
    def _():
        carry_rkv_ref[...] = jnp.zeros_like(carry_rkv_ref)
        carry_lora_ref[...] = jnp.zeros_like(carry_lora_ref)

    h = _rms(x_ref[...], g_ref[...]).astype(BF16)

    wr = RET_WIDTH
    lanes = 2 * HEAD_DIM
    half = HEAD_DIM // 2
    p_q = _dot(h, w_ref[:, :wr])
    p_k = _dot(h, w_ref[:, wr:2 * wr])
    p_r = _dot(h, w_ref[:, rkv0:rkv0 + w])
    plora_raw = _dot(h, w_ref[:, rkv0 + 3 * w:])
    p_kw = _dot(h, w_ref[:, rkv0 + w:rkv0 + 2 * w])
    p_vw = _dot(h, w_ref[:, rkv0 + 2 * w:rkv0 + 3 * w])

    n_pack = lanes // half
    ang = pos_ref[...].astype(F32) * freq_ref[...]
    lane_p = lax.broadcasted_iota(jnp.int32, ang.shape, 1)

    def unpack(t):
        blocks = []
        for j in range(n_pack):
            g = pltpu.roll(t, (lanes - j * half) % lanes, 1) if j else t
            spread = g
            for m in range(1, n_pack):
                spread = jnp.where(lane_p < m * half, spread, pltpu.roll(g, m * half, 1))
            blocks.append(spread)
        return jnp.concatenate(blocks, axis=0)

    first_half = (lax.broadcasted_iota(jnp.int32, (tm, lanes), 1) % HEAD_DIM) < half
    cos = unpack(jnp.cos(ang))
    sin = unpack(jnp.sin(ang))
    sin = jnp.where(first_half, -sin, sin)
    k_scale = HEAD_DIM ** -0.5

    def rope(t, cos_t, sin_t):
        tiles = []
        for lo in range(0, wr, lanes):
            x = t[:, lo:lo + lanes]
            partner = jnp.where(first_half, pltpu.roll(x, lanes - half, 1), pltpu.roll(x, half, 1))
            tiles.append(x * cos_t + partner * sin_t)
        return jnp.concatenate(tiles, axis=1)

    def chunk_scaled(dst_ref, t, table_ref):
        for lo in range(0, tm, RET_CHUNK):
            dst_ref[lo:lo + RET_CHUNK, :] = (t[lo:lo + RET_CHUNK] * table_ref[...]).astype(BF16)

    q = rope(p_q, cos, sin)
    q_ref[...] = q.astype(BF16)
    chunk_scaled(qin_ref, q, xi_ref)

    kr = rope(p_k, cos * k_scale, sin * k_scale)
    k_ref[...] = kr.astype(BF16)
    chunk_scaled(kout_ref, kr, zeta_ref)

    r = _shift(p_r, carry_rkv_ref, mu_rkv_ref, 0)
    plora = _shift(plora_raw, carry_lora_ref, mu_lora_ref, 0)
    w_lr = plora[:, :DECAY_LORA]
    a_lr = plora[:, DECAY_LORA:DECAY_LORA + AAA_LORA]
    g_lr = plora[:, DECAY_LORA + AAA_LORA:]
    lw = -math.exp(-0.5) * jax.nn.sigmoid(
        w0_ref[...] + _dot(jnp.tanh(w_lr).astype(BF16), wup_ref[...].astype(BF16)))
    a_sig = jax.nn.sigmoid(a0_ref[...] + _dot(a_lr.astype(BF16), aup_ref[...].astype(BF16)))
    gate_ref[...] = _dot(jax.nn.sigmoid(g_lr).astype(BF16), gup_ref[...].astype(BF16))

    k = _shift(p_kw, carry_rkv_ref, mu_rkv_ref, w)
    kk = k * kk_ref[...]
    k2 = k * (1.0 + (a_sig - 1.0) * ka_ref[...])
    kk = kk * (1.0 / jnp.maximum(jnp.sqrt(_head_sums(kk * kk)), 1e-12))
    a_vec = -kk
    b_vec = kk * a_sig

    v = _shift(p_vw, carry_rkv_ref, mu_rkv_ref, 2 * w)
    bonus_ref[...] = _head_sums(r * k2 * rk_ref[...]) * v
    vw_ref[...] = v.astype(BF16)

    p_v = _dot(h, w_ref[:, 2 * wr:3 * wr])
    p_g = _dot(h, w_ref[:, 3 * wr:RET_PROJ])
    c = WKV_CHUNK
    for ci in range(tm // c):
        rows = slice(ci * c, (ci + 1) * c)
        lw_c = lw[rows]
        cum = jnp.dot(tri_ref[...], lw_c, precision=lax.Precision.HIGHEST, preferred_element_type=F32)
        cum_end = cum[c - 1:, :]
        p_inv = jnp.exp(-cum)
        p_end = jnp.exp(cum_end)
        p_out = p_inv * p_end
        at_ref[rows, :] = (a_vec[rows] * jnp.exp(cum - lw_c)).astype(BF16)
        rt_ref[rows, :] = (r[rows] * jnp.exp(cum)).astype(BF16)
        bt_ref[rows, :] = (b_vec[rows] * p_inv).astype(BF16)
        kt_ref[rows, :] = (k2[rows] * p_inv).astype(BF16)
        bc_ref[rows, :] = (b_vec[rows] * p_out).astype(BF16)
        kc_ref[rows, :] = (k2[rows] * p_out).astype(BF16)
        pc_ref[ci:ci + 1, :] = p_end

    vr_ref[...] = p_v.astype(BF16)
    gsilu_ref[...] = p_g * jax.nn.sigmoid(p_g)


ROPE_PACK = 2 * HEAD_DIM // (HEAD_DIM // 2)


def _packed_positions(positions, tm):
    lanes_per_group = 128 // ROPE_PACK
    tiles = positions.reshape(-1, ROPE_PACK, tm // ROPE_PACK)
    return jnp.repeat(jnp.swapaxes(tiles, 1, 2), lanes_per_group, axis=2).reshape(-1, 128)


def _prep(x2d, pos2d, seq_len, consts, tm=PREP_ROWS):
    t = x2d.shape[0]
    assert t % tm == 0 and seq_len % tm == 0 and tm % RET_CHUNK == 0 and tm % WKV_CHUNK == 0
    row = lambda i: (i, 0)
    width = RET_WIDTH
    n_pc = tm // WKV_CHUNK
    widths = [N_BF16_STREAMS * width, N_F32_STREAMS * width]
    return pl.pallas_call(
        functools.partial(_prep_kernel, tiles_per_seq=seq_len // tm),
        grid=(t // tm,),
        in_specs=[pl.BlockSpec((tm, D_MODEL), row), pl.BlockSpec((tm // ROPE_PACK, 128), row)]
                 + [_full(a.shape) for a in consts],
        out_specs=[pl.BlockSpec((tm, n), row) for n in widths] + [pl.BlockSpec((n_pc, width), row)],
        out_shape=[jax.ShapeDtypeStruct((t, widths[0]), BF16), jax.ShapeDtypeStruct((t, widths[1]), F32),
                   jax.ShapeDtypeStruct((t // WKV_CHUNK, width), F32)],
        scratch_shapes=[pltpu.VMEM(consts[1].shape[::-1], BF16), pltpu.VMEM((1, 3 * RWKV_WIDTH), F32),
                        pltpu.VMEM((1, LORA_WIDTH), F32)],
        compiler_params=_params(1),
        name="prep",
    )(x2d, pos2d, *consts)


def _mixers_kernel(*refs):
    bf16_ref, f32_ref, refs = refs[0], refs[1], refs[2:]
    (pc_ref, dmask_ref, gc_ref, same_head_ref, ret_gnw_ref, ret_gnb_ref, gnw_ref, gnb_ref, mask_ref, eye_ref,
     o_ref, ret_state_ref, wkv_state_ref) = refs
    ti = pl.program_id(1)

    @pl.when(ti == 0)
    def _():
        ret_state_ref[...] = jnp.zeros_like(ret_state_ref)
        wkv_state_ref[...] = jnp.zeros_like(wkv_state_ref)

    rows = bf16_ref.shape[0] * bf16_ref.shape[1]
    stream = lambda ref, i: ref[:, :, i * RET_WIDTH:(i + 1) * RET_WIDTH].reshape(rows, RET_WIDTH)
    ret_in = [stream(bf16_ref, i) for i in range(N_RET_BF16)] + [stream(f32_ref, 0)]
    wkv_in = ([stream(bf16_ref, i) for i in range(N_RET_BF16, N_BF16_STREAMS)]
              + [stream(f32_ref, i) for i in range(1, N_F32_STREAMS)])
    y_ret = _retention_body(*ret_in, dmask_ref, gc_ref, same_head_ref, ret_gnw_ref,
                            ret_gnb_ref, ret_state_ref)
    chunks_per_block = o_ref.shape[1] // WKV_CHUNK
    y_rwkv = _wkv_body(*wkv_in, pc_ref, ti * chunks_per_block, gnw_ref, gnb_ref, mask_ref,
                       eye_ref, same_head_ref, wkv_state_ref)
    o_ref[...] = jnp.concatenate([y_ret, y_rwkv], axis=1).astype(o_ref.dtype).reshape(o_ref.shape)


def _mixers(bf16_streams, f32_streams, pc, dmask_pairs, gc_w, ret_gn_w, ret_gn_b, gn_w, gn_b):
    b, s, _ = bf16_streams.shape
    width = RET_WIDTH
    n_seq, tb, c = MIX_SEQS, MIX_BLOCK, WKV_CHUNK
    assert b % n_seq == 0 and s % tb == 0 and tb % RET_CHUNK == 0 and tb % c == 0
    mask = np.concatenate([np.tile(np.tri(c, k=-1), (1, 4)), np.tile(np.tri(c), (1, 4))], axis=0)
    eye = np.tile(np.eye(c), (1, 2))
    head_of_lane = np.arange(2 * HEAD_DIM) // HEAD_DIM
    state_mask = head_of_lane[:, None] == head_of_lane[None, :]
    mask, eye, state_mask = (jnp.asarray(a, F32) for a in (mask, eye, state_mask))
    consts = [dmask_pairs, gc_w, state_mask, ret_gn_w, ret_gn_b, gn_w, gn_b, mask, eye]
    blk = lambda bi, ti: (bi, ti, 0)
    state = pltpu.VMEM((n_seq, N_HEADS // 2, 2 * HEAD_DIM, 2 * HEAD_DIM), F32)
    return pl.pallas_call(
        _mixers_kernel,
        grid=(b // n_seq, s // tb),
        in_specs=[pl.BlockSpec((n_seq, tb, a.shape[2]), blk) for a in (bf16_streams, f32_streams)]
                 + [pl.BlockSpec((n_seq, s // c, width), lambda bi, ti: (bi, 0, 0))]
                 + [_full(a.shape) for a in consts],
        out_specs=pl.BlockSpec((n_seq, tb, RET_WIDTH + RWKV_WIDTH), blk),
        out_shape=jax.ShapeDtypeStruct((b, s, RET_WIDTH + RWKV_WIDTH), BF16),
        scratch_shapes=[state, state],
        compiler_params=_params(2),
        name="mixers",
    )(bf16_streams, f32_streams, pc, *consts)


def _mem_kv_kernel(m_ref, g_ref, w_f32_ref, o_ref, w_ref):
    _cast_weights_once(pl.program_id(0) == 0, [(w_f32_ref, w_ref)])
    h = _rms(m_ref[...], g_ref[...]).astype(BF16)
    o_ref[...] = _dot(h, w_ref[...]).astype(o_ref.dtype)


def _mem_kv(mem2d, g, w_kv, tm=1024):
    t = mem2d.shape[0]
    n = w_kv.shape[1]
    tm = min(tm, t)
    assert t % tm == 0
    return pl.pallas_call(
        _mem_kv_kernel,
        grid=(t // tm,),
        in_specs=[pl.BlockSpec((tm, D_MODEL), lambda i: (i, 0)), _full(g.shape), _full(w_kv.shape)],
        out_specs=pl.BlockSpec((tm, n), lambda i: (i, 0)),
        out_shape=jax.ShapeDtypeStruct((t, n), BF16),
        scratch_shapes=[pltpu.VMEM(w_kv.shape, BF16)],
        compiler_params=_params(1),
        name="mem_kv",
    )(mem2d, g, w_kv)


def _post_mix_kernel(x_ref, y_ref, wout_f32_ref, gx_ref, wq_f32_ref, kv_ref, wo_f32_ref, wup_f32_ref, wdown_f32_ref,
                     o_ref, wup_ref, wdown_ref, wout_ref, wq_ref, wo_ref):
    first = jnp.logical_and(pl.program_id(0) == 0, pl.program_id(1) == 0)
    _cast_weights_once(first, [(wout_f32_ref, wout_ref), (wq_f32_ref, wq_ref), (wo_f32_ref, wo_ref)])
    x1 = x_ref[0] + _dot(y_ref[0], wout_ref[...])
    wup_ref[...] = wup_f32_ref[...].astype(BF16)
    wdown_ref[...] = wdown_f32_ref[...].astype(BF16)
    q = _dot(_rms(x1, gx_ref[...]).astype(BF16), wq_ref[...]).astype(BF16)
    kv = kv_ref[0]
    head = lambda h: slice(h * XATTN_HEAD_DIM, (h + 1) * XATTN_HEAD_DIM)
    scores = [_dot_nt(q[:, head(h)], kv[:, head(h)]) * (XATTN_HEAD_DIM ** -0.5) for h in range(XATTN_HEADS)]
    heads = []
    for h in range(XATTN_HEADS):
        s = scores[h]
        e = jnp.exp(s - jnp.max(s, axis=-1, keepdims=True))
        prob = e / jnp.sum(e, axis=-1, keepdims=True)
        heads.append(_dot(prob.astype(BF16), kv[:, D_MODEL + h * XATTN_HEAD_DIM:D_MODEL + (h + 1) * XATTN_HEAD_DIM]))
    o = jnp.concatenate(heads, axis=1).astype(BF16)
    o_ref[0] = x1 + _dot(o, wo_ref[...])


def _post_mix(x, y, w_out, g_x, w_q, kv, w_o, w_up, w_down, tm=1024):
    b, s, _ = x.shape
    m = kv.shape[1]
    assert s % tm == 0
    n_steps = b * (s // tm)
    assert D_FF % (n_steps * 128) == 0
    slab = D_FF // n_steps
    blk = lambda bi, ti: (bi, ti, 0)
    up_spec = pl.BlockSpec((D_MODEL, slab), lambda bi, ti: (0, bi * (s // tm) + ti))
    down_spec = pl.BlockSpec((slab, D_MODEL), lambda bi, ti: (bi * (s // tm) + ti, 0))
    return pl.pallas_call(
        _post_mix_kernel,
        grid=(b, s // tm),
        in_specs=[pl.BlockSpec((1, tm, D_MODEL), blk), pl.BlockSpec((1, tm, y.shape[2]), blk),
                  _full(w_out.shape), _full(g_x.shape), _full(w_q.shape),
                  pl.BlockSpec((1, m, 2 * D_MODEL), lambda bi, ti: (bi, 0, 0)), _full(w_o.shape), up_spec, down_spec],
        out_specs=[pl.BlockSpec((1, tm, D_MODEL), blk), up_spec, down_spec],
        out_shape=[jax.ShapeDtypeStruct((b, s, D_MODEL), F32), jax.ShapeDtypeStruct(w_up.shape, BF16),
                   jax.ShapeDtypeStruct(w_down.shape, BF16)],
        scratch_shapes=[pltpu.VMEM(a.shape, BF16) for a in (w_out, w_q, w_o)],
        compiler_params=_params(2),
        name="post_mix",
    )(x, y, w_out, g_x, w_q, kv, w_o, w_up, w_down)


def _mlp_kernel(x_ref, g_ref, wup_ref, wdown_ref, gf_ref, o_ref, *, tf, final_norm):
    x = x_ref[...]
    h = _rms(x, g_ref[...]).astype(BF16)
    acc = x
    for j in range(D_FF // tf):
        u = jnp.maximum(_dot(h, wup_ref[:, j * tf:(j + 1) * tf]), 0.0)
        acc = acc + _dot((u * u).astype(BF16), wdown_ref[j * tf:(j + 1) * tf, :])
    o_ref[...] = _rms(acc, gf_ref[...]) if final_norm else acc


def _mlp(x2d, g, w_up, w_down, g_final, final_norm, tm=1024, tf=512):
    t = x2d.shape[0]
    assert t % tm == 0 and D_FF % tf == 0
    row_tile = pl.BlockSpec((tm, D_MODEL), lambda i: (i, 0))

    def whole_call(x_hbm, g_ref, wup_ref, wdown_ref, gf_ref, o_hbm):
        def step(x_ref, o_ref):
            _mlp_kernel(x_ref, g_ref, wup_ref, wdown_ref, gf_ref, o_ref, tf=tf, final_norm=final_norm)
        pltpu.emit_pipeline(step, grid=(t // tm,), in_specs=[row_tile], out_specs=[row_tile])(x_hbm, o_hbm)

    hbm = pl.BlockSpec(memory_space=pl.ANY)
    vmem = pl.BlockSpec(memory_space=pltpu.VMEM)
    return pl.pallas_call(
        whole_call,
        in_specs=[hbm, vmem, vmem, vmem, vmem],
        out_specs=hbm,
        out_shape=jax.ShapeDtypeStruct((t, D_MODEL), F32),
        compiler_params=pltpu.CompilerParams(vmem_limit_bytes=VMEM_LIMIT_BYTES),
        name="mlp",
    )(x2d, g, w_up, w_down, g_final)


def kernel(x, mem, positions, norm_mix, w_in, ret_gn_w, ret_gn_b, rwkv_mu, rwkv_w0, rwkv_w_up, rwkv_a0,
           rwkv_a_up, rwkv_g_up, rwkv_k_k, rwkv_k_a, rwkv_r_k, rwkv_gn_w, rwkv_gn_b, w_out, norm_xattn,
           norm_mem, xattn_w_q, xattn_w_kv, xattn_w_o, norm_mlp, mlp_w_up, mlp_w_down, norm_final):
    b, s, dm = x.shape
    n_layers = w_in.shape[0]
    freq, xi_w, zeta_w, dmask_pairs, gc_w = _retention_consts()
    tri = jnp.asarray(np.tri(WKV_CHUNK), F32)
    row = lambda a: a.reshape(1, -1)
    for l in range(n_layers):
        mu = rwkv_mu[l]
        prep_consts = [row(norm_mix[l]), jnp.swapaxes(w_in[l], 0, 1), freq, xi_w, zeta_w, row(mu[:3 * RWKV_WIDTH]), row(mu[3 * RWKV_WIDTH:]),
                       row(rwkv_w0[l]), rwkv_w_up[l], row(rwkv_a0[l]), rwkv_a_up[l],
                       rwkv_g_up[l], row(rwkv_k_k[l]), row(rwkv_k_a[l]), row(rwkv_r_k[l]), tri]
        sb, sf, pc = _prep(x.reshape(b * s, dm), _packed_positions(positions, PREP_ROWS), s, prep_consts)
        y = _mixers(sb.reshape(b, s, -1), sf.reshape(b, s, -1), pc.reshape(b, s // WKV_CHUNK, -1), dmask_pairs, gc_w,
                    row(ret_gn_w[l]), row(ret_gn_b[l]), row(rwkv_gn_w[l]), row(rwkv_gn_b[l]))
        kv = _mem_kv(mem.reshape(-1, dm), norm_mem[l][None, :], xattn_w_kv[l])
        x, w_up, w_down = _post_mix(x, y, w_out[l], norm_xattn[l][None, :], xattn_w_q[l], kv.reshape(b, -1, 2 * dm),
                                    xattn_w_o[l], mlp_w_up[l], mlp_w_down[l])
        x = _mlp(x.reshape(b * s, dm), norm_mlp[l][None, :], w_up, w_down, norm_final[None, :],
                 l == n_layers - 1).reshape(b, s, dm)
    return x
```
